```python
import math
import jax, jax.numpy as jnp
from jax import lax
import numpy as np

D_MODEL = 1024
BATCH = 8
SEQ = 2048
DEPTH = 4

N_MIXERS = 2
RMS_EPS = 1e-6
GN_EPS = 1e-6
NEG_INF = -1e30

NSA_HEADS = 16
NSA_HEAD_DIM = 64
NSA_KV_GROUPS = 4
NSA_Q_PER_GROUP = NSA_HEADS // NSA_KV_GROUPS
NSA_WIDTH = NSA_HEADS * NSA_HEAD_DIM
NSA_KV_WIDTH = NSA_KV_GROUPS * NSA_HEAD_DIM
CMP_BLOCK = 32
CMP_STRIDE = 16
CMP_HIDDEN = 256
SLC_BLOCK = 64
SLC_TOPN = 16
SLC_Q_BLOCK = 32
WIN_SIZE = 512
WIN_Q_BLOCK = 128
FORCED_SCORE = 1e3
NSA_PROJ = NSA_WIDTH + 6 * NSA_KV_WIDTH + 3 * NSA_HEADS + 3 * NSA_WIDTH

REL_BUCKETS = 32
REL_MAX_DIST = 128

RET_HEADS = 4
RET_QK_DIM = 256
RET_V_DIM = 512
RET_QK_WIDTH = RET_HEADS * RET_QK_DIM
RET_WIDTH = RET_HEADS * RET_V_DIM
RET_CHUNK = 128
ROPE_BASE = 10000.0
RET_PROJ = 2 * RET_QK_WIDTH + 2 * RET_WIDTH

kernel_name = "hybrid_nsa_retention_sandwich"


def rms_norm(x, gain):
    xf = x.astype(jnp.float32)
    y = xf * lax.rsqrt(jnp.mean(xf * xf, axis=-1, keepdims=True) + RMS_EPS)
    return (y * gain.astype(jnp.float32)).astype(x.dtype)


def split_cols(a, sizes):
    return jnp.split(a, list(np.cumsum(sizes)[:-1]), axis=-1)


def masked_softmax(logits, mask):
    p = jax.nn.softmax(jnp.where(mask, logits, NEG_INF), axis=-1)
    return jnp.where(mask, p, 0.0)


def t5_bucket(dist):
    n = jnp.maximum(dist, 0)
    max_exact = REL_BUCKETS // 2
    nf = jnp.maximum(n, 1).astype(jnp.float32)
    large = max_exact + (jnp.log(nf / max_exact) / math.log(REL_MAX_DIST / max_exact)
                         * (REL_BUCKETS - max_exact)).astype(jnp.int32)
    large = jnp.minimum(large, REL_BUCKETS - 1)
    return jnp.where(n < max_exact, n, large)


def compress_kv(kv, pos, w1, w2):
    B, S, G, dh = kv.shape
    n_cmp = (S - CMP_BLOCK) // CMP_STRIDE + 1
    idx = jnp.arange(n_cmp)[:, None] * CMP_STRIDE + jnp.arange(CMP_BLOCK)[None, :]
    blocks = kv[:, idx] + pos[None, None, :, None, :].astype(kv.dtype)
    flat = blocks.transpose(0, 1, 3, 2, 4).reshape(B, n_cmp, G, CMP_BLOCK * dh)
    return jax.nn.silu(flat @ w1) @ w2


def nsa_compressed(q, kc, vc, table):
    B, S, H, dh = q.shape
    n_cmp = kc.shape[1]
    qg = q.reshape(B, S, NSA_KV_GROUPS, NSA_Q_PER_GROUP, dh)
    logits = jnp.einsum('bsgrd,bcgd->bgrsc', qg, kc).astype(jnp.float32) * (dh ** -0.5)
    t = jnp.arange(S)
    end = jnp.arange(n_cmp) * CMP_STRIDE + CMP_BLOCK - 1
    dist = t[:, None] - end[None, :]
    bias = table[t5_bucket(dist)].astype(jnp.float32)
    bias = bias.reshape(S, n_cmp, NSA_KV_GROUPS, NSA_Q_PER_GROUP).transpose(2, 3, 0, 1)
    p = masked_softmax(logits + bias, dist >= 0)
    o = jnp.einsum('bgrsc,bcgd->bsgrd', p.astype(vc.dtype), vc).reshape(B, S, H, dh)
    return o, p


def select_blocks(p_cmp, S):
    n_cmp = p_cmp.shape[-1]
    n_sel = S // SLC_BLOCK
    cs = jnp.arange(n_cmp) * CMP_STRIDE
    j = jnp.arange(n_sel)
    overlap = ((cs[:, None] < (j[None, :] + 1) * SLC_BLOCK)
               & (cs[:, None] + CMP_BLOCK > j[None, :] * SLC_BLOCK)).astype(jnp.float32)
    imp = jnp.einsum('bgrsc,cj->bgsj', p_cmp, overlap)
    t = jnp.arange(S)
    cur = t // SLC_BLOCK
    valid = j[None, :] * SLC_BLOCK <= t[:, None]
    forced = (j[None, :] == 0) | (j[None, :] == cur[:, None]) | (j[None, :] == cur[:, None] - 1)
    score = jnp.where(valid, imp + jnp.where(forced, FORCED_SCORE, 0.0), NEG_INF)
    _, sel = lax.top_k(score, min(SLC_TOPN, n_sel))
    return sel


def nsa_selected(q, k, v, sel, table):
    B, S, H, dh = q.shape
    G, R = NSA_KV_GROUPS, NSA_Q_PER_GROUP
    n_sel = S // SLC_BLOCK
    n_top = sel.shape[-1]
    n_keys = n_top * SLC_BLOCK
    kb = k.reshape(B, n_sel, SLC_BLOCK, G, dh).transpose(0, 3, 1, 2, 4)
    vb = v.reshape(B, n_sel, SLC_BLOCK, G, dh).transpose(0, 3, 1, 2, 4)
    nq = S // SLC_Q_BLOCK
    q_blk = q.reshape(B, nq, SLC_Q_BLOCK, G, R, dh).transpose(1, 0, 2, 3, 4, 5)
    sel_blk = sel.reshape(B, G, nq, SLC_Q_BLOCK, n_top).transpose(2, 0, 1, 3, 4)
    bi = jnp.arange(B)[:, None, None, None]
    gi = jnp.arange(G)[None, :, None, None]
    tbl = table.reshape(REL_BUCKETS, G, R)

    def one(args):
        qb, sb, blk = args
        t = blk * SLC_Q_BLOCK + jnp.arange(SLC_Q_BLOCK)
        kg = kb[bi, gi, sb].reshape(B, G, SLC_Q_BLOCK, n_keys, dh)
        vg = vb[bi, gi, sb].reshape(B, G, SLC_Q_BLOCK, n_keys, dh)
        pos = (sb[..., None] * SLC_BLOCK + jnp.arange(SLC_BLOCK)).reshape(B, G, SLC_Q_BLOCK, n_keys)
        dist = t[None, None, :, None] - pos
        bias = tbl[t5_bucket(dist), gi].astype(jnp.float32).transpose(0, 1, 4, 2, 3)
        logits = jnp.einsum('bqgrd,bgqkd->bgrqk', qb, kg).astype(jnp.float32) * (dh ** -0.5)
        p = masked_softmax(logits + bias, (dist >= 0)[:, :, None])
        return jnp.einsum('bgrqk,bgqkd->bqgrd', p.astype(vg.dtype), vg)

    out = lax.map(one, (q_blk, sel_blk, jnp.arange(nq)))
    return out.transpose(1, 0, 2, 3, 4, 5).reshape(B, S, H, dh)


def nsa_window(q, k, v, table):
    B, S, H, dh = q.shape
    G, R = NSA_KV_GROUPS, NSA_Q_PER_GROUP
    nq = S // WIN_Q_BLOCK
    span = WIN_SIZE + WIN_Q_BLOCK
    kp = jnp.pad(k, ((0, 0), (WIN_SIZE, 0), (0, 0), (0, 0)))
    vp = jnp.pad(v, ((0, 0), (WIN_SIZE, 0), (0, 0), (0, 0)))
    q_blk = q.reshape(B, nq, WIN_Q_BLOCK, G, R, dh).transpose(1, 0, 2, 3, 4, 5)

    def one(args):
        qb, blk = args
        start = blk * WIN_Q_BLOCK
        kw = lax.dynamic_slice_in_dim(kp, start, span, axis=1)
        vw = lax.dynamic_slice_in_dim(vp, start, span, axis=1)
        t = start + jnp.arange(WIN_Q_BLOCK)
        pos = start - WIN_SIZE + jnp.arange(span)
        dist = t[:, None] - pos[None, :]
        mask = (dist >= 0) & (dist < WIN_SIZE) & (pos[None, :] >= 0)
        bias = table[t5_bucket(dist)].astype(jnp.float32)
        bias = bias.reshape(WIN_Q_BLOCK, span, G, R).transpose(2, 3, 0, 1)
        logits = jnp.einsum('bqgrd,bkgd->bgrqk', qb, kw).astype(jnp.float32) * (dh ** -0.5)
        p = masked_softmax(logits + bias, mask)
        return jnp.einsum('bgrqk,bkgd->bqgrd', p.astype(vw.dtype), vw)

    out = lax.map(one, (q_blk, jnp.arange(nq)))
    return out.transpose(1, 0, 2, 3, 4, 5).reshape(B, S, H, dh)


def nsa_mixer(h, w_in, w_out, k_pos, k_w1, k_w2, v_pos, v_w1, v_w2, table):
    B, S, _ = h.shape
    H, dh, G = NSA_HEADS, NSA_HEAD_DIM, NSA_KV_GROUPS
    q, kv, g, z = split_cols(h @ w_in, [NSA_WIDTH, 6 * NSA_KV_WIDTH, 3 * H, 3 * NSA_WIDTH])
    q = q.reshape(B, S, H, dh)
    k_c, v_c, k_s, v_s, k_w, v_w = [a.reshape(B, S, G, dh) for a in jnp.split(kv, 6, axis=-1)]
    gates = jax.nn.sigmoid(g).reshape(B, S, 3, H, 1)
    zs = jax.nn.silu(z).reshape(B, S, 3, H, dh)
    kc = compress_kv(k_c, k_pos, k_w1, k_w2)
    vc = compress_kv(v_c, v_pos, v_w1, v_w2)
    o_cmp, p_cmp = nsa_compressed(q, kc, vc, table)
    sel = select_blocks(p_cmp, S)
    o_slc = nsa_selected(q, k_s, v_s, sel, table)
    o_win = nsa_window(q, k_w, v_w, table)
    o = jnp.stack([o_cmp, o_slc, o_win], axis=2)
    y = jnp.sum(gates * o * zs, axis=2).reshape(B, S, NSA_WIDTH)
    return y @ w_out


def rotary(x):
    S, d = x.shape[1], x.shape[-1]
    inv = 1.0 / (ROPE_BASE ** jnp.linspace(0.0, 1.0, d // 2, dtype=jnp.float32))
    ang = jnp.arange(S, dtype=jnp.float32)[:, None] * inv[None, :]
    cos = jnp.cos(ang)[None, :, None, :].astype(x.dtype)
    sin = jnp.sin(ang)[None, :, None, :].astype(x.dtype)
    x1, x2 = x[..., :d // 2], x[..., d // 2:]
    return jnp.concatenate([x1 * cos - x2 * sin, x1 * sin + x2 * cos], axis=-1)


def chunkwise_retention(q, k, v):
    B, S, H, dk = q.shape
    dv = v.shape[-1]
    C = RET_CHUNK
    nc = S // C
    dt = q.dtype
    log_g = jnp.log(1.0 - 2.0 ** (-5.0 - jnp.arange(H, dtype=jnp.float32)))
    i = jnp.arange(C, dtype=jnp.float32)
    diff = i[:, None] - i[None, :]
    inner = jnp.where(diff >= 0, jnp.exp(diff[None] * log_g[:, None, None]), 0.0).astype(dt)
    xi = jnp.exp((i + 1.0)[None, :] * log_g[:, None]).astype(dt)
    zeta = jnp.exp((C - 1.0 - i)[None, :] * log_g[:, None]).astype(dt)
    chunk_decay = jnp.exp(C * log_g).astype(dt)
    qc = q.reshape(B, nc, C, H, dk).transpose(1, 0, 3, 2, 4)
    kc = k.reshape(B, nc, C, H, dk).transpose(1, 0, 3, 2, 4)
    vc = v.reshape(B, nc, C, H, dv).transpose(1, 0, 3, 2, 4)

    def step(state, inp):
        qi, ki, vi = inp
        attn = jnp.einsum('bhid,bhjd->bhij', qi, ki) * inner[None]
        o = (jnp.einsum('bhij,bhjv->bhiv', attn, vi)
             + jnp.einsum('bhid,bhdv->bhiv', qi, state) * xi[None, :, :, None])
        state = (state * chunk_decay[None, :, None, None]
                 + jnp.einsum('bhjd,bhjv->bhdv', ki * zeta[None, :, :, None], vi))
        return state, o

    state0 = jnp.zeros((B, H, dk, dv), dt)
    _, o = lax.scan(step, state0, (qc, kc, vc))
    return o.transpose(1, 0, 3, 2, 4).reshape(B, S, H, dv)


def retention_mixer(h, w_in, w_out, gn_gain):
    B, S, _ = h.shape
    q, k, v, z = split_cols(h @ w_in, [RET_QK_WIDTH, RET_QK_WIDTH, RET_WIDTH, RET_WIDTH])
    q = rotary(q.reshape(B, S, RET_HEADS, RET_QK_DIM))
    k = rotary(k.reshape(B, S, RET_HEADS, RET_QK_DIM)) * (RET_QK_DIM ** -0.5)
    v = v.reshape(B, S, RET_HEADS, RET_V_DIM)
    o = chunkwise_retention(q, k, v).astype(jnp.float32)
    mu = jnp.mean(o, axis=-1, keepdims=True)
    var = jnp.mean(jnp.square(o - mu), axis=-1, keepdims=True)
    o = (o - mu) * lax.rsqrt(var + GN_EPS) * gn_gain.reshape(RET_HEADS, RET_V_DIM).astype(jnp.float32)
    o = o.astype(h.dtype).reshape(B, S, RET_WIDTH)
    return (o * jax.nn.silu(z)) @ w_out


def setup_inputs(seed: int = 0) -> dict:
    key = jax.random.key(seed)
    ks = jax.random.split(key, 16)
    n_nsa = (DEPTH + N_MIXERS - 1) // N_MIXERS
    n_ret = DEPTH // N_MIXERS

    def nrm(k, shape, scale):
        return jax.random.normal(k, shape, jnp.float32) * scale

    return {
        "x": nrm(ks[0], (BATCH, SEQ, D_MODEL), 1.0),
        "pre_norm_gain": 1.0 + nrm(ks[1], (DEPTH, D_MODEL), 0.1),
        "post_norm_gain": 1.0 + nrm(ks[2], (DEPTH, D_MODEL), 0.1),
        "rel_bias_table": nrm(ks[3], (REL_BUCKETS, NSA_HEADS), 0.5),
        "nsa_w_in": nrm(ks[4], (n_nsa, D_MODEL, NSA_PROJ), D_MODEL ** -0.5),
        "nsa_w_out": nrm(ks[5], (n_nsa, NSA_WIDTH, D_MODEL), NSA_WIDTH ** -0.5),
        "nsa_cmp_k_pos": nrm(ks[6], (n_nsa, CMP_BLOCK, NSA_HEAD_DIM), 0.5),
        "nsa_cmp_k_w1": nrm(ks[7], (n_nsa, CMP_BLOCK * NSA_HEAD_DIM, CMP_HIDDEN), (CMP_BLOCK * NSA_HEAD_DIM) ** -0.5),
        "nsa_cmp_k_w2": nrm(ks[8], (n_nsa, CMP_HIDDEN, NSA_HEAD_DIM), CMP_HIDDEN ** -0.5),
        "nsa_cmp_v_pos": nrm(ks[9], (n_nsa, CMP_BLOCK, NSA_HEAD_DIM), 0.5),
        "nsa_cmp_v_w1": nrm(ks[10], (n_nsa, CMP_BLOCK * NSA_HEAD_DIM, CMP_HIDDEN), (CMP_BLOCK * NSA_HEAD_DIM) ** -0.5),
        "nsa_cmp_v_w2": nrm(ks[11], (n_nsa, CMP_HIDDEN, NSA_HEAD_DIM), CMP_HIDDEN ** -0.5),
        "ret_w_in": nrm(ks[12], (n_ret, D_MODEL, RET_PROJ), D_MODEL ** -0.5),
        "ret_w_out": nrm(ks[13], (n_ret, RET_WIDTH, D_MODEL), RET_WIDTH ** -0.5),
        "ret_gn_gain": 1.0 + nrm(ks[14], (n_ret, RET_WIDTH), 0.1),
    }


def reference(x, pre_norm_gain, post_norm_gain, rel_bias_table, nsa_w_in, nsa_w_out,
              nsa_cmp_k_pos, nsa_cmp_k_w1, nsa_cmp_k_w2, nsa_cmp_v_pos, nsa_cmp_v_w1, nsa_cmp_v_w2,
              ret_w_in, ret_w_out, ret_gn_gain):
    h = x
    for layer in range(DEPTH):
        slot = layer // N_MIXERS
        u = rms_norm(h, pre_norm_gain[layer])
        if layer % N_MIXERS == 0:
            y = nsa_mixer(u, nsa_w_in[slot], nsa_w_out[slot],
                          nsa_cmp_k_pos[slot], nsa_cmp_k_w1[slot], nsa_cmp_k_w2[slot],
                          nsa_cmp_v_pos[slot], nsa_cmp_v_w1[slot], nsa_cmp_v_w2[slot],
                          rel_bias_table)
        else:
            y = retention_mixer(u, ret_w_in[slot], ret_w_out[slot], ret_gn_gain[slot])
        h = h + rms_norm(y, post_norm_gain[layer])
    return h
```

```python
import functools
import math

import numpy as np
import jax
import jax.numpy as jnp
from jax import lax
from jax.experimental import pallas as pl
from jax.experimental.pallas import tpu as pltpu

F32 = jnp.float32
BF16 = jnp.bfloat16

D_MODEL = 1024
SEQ = 2048
DEPTH = 4
RMS_EPS = 1e-6
GN_EPS = 1e-6
MASKED = -1e30

HEADS = 16
HEAD_DIM = 64
GROUPS = 4
HEADS_PER_GROUP = HEADS // GROUPS
NSA_WIDTH = HEADS * HEAD_DIM
KV_WIDTH = GROUPS * HEAD_DIM
CMP_BLOCK = 32
CMP_STRIDE = 16
CMP_HIDDEN = 256
N_CMP = (SEQ - CMP_BLOCK) // CMP_STRIDE + 1
N_CMP_PAD = 128
SLC_BLOCK = 64
N_SLC = SEQ // SLC_BLOCK
SLC_TOPN = 16
WIN_SIZE = 512
FORCED_SCORE = 1e3
REL_BUCKETS = 32
REL_MAX_DIST = 128

RET_HEADS = 4
RET_QK_DIM = 256
RET_V_DIM = 512
RET_QK_WIDTH = RET_HEADS * RET_QK_DIM
RET_WIDTH = RET_HEADS * RET_V_DIM
ROPE_BASE = 10000.0

LANES = 128
VMEM_LIMIT_BYTES = 56 * 1024 * 1024

PROJ_TM = 1024
PROJ_TN = 1024
POST_TM = 512
ATT_TQ = 256
ATT_TK = 256
RET_CHUNK = 256

NSA_Q_COL = 0
NSA_Z_COL = 1024
NSA_KV_COL = 4096
NSA_CKV_COL = 6144
NSA_G_COL = 6656
NSA_PROJ_PAD = 7168

_NT = (((1,), (1,)), ((), ()))
_TN = (((0,), (0,)), ((), ()))


def _cparams(sem):
    return pltpu.CompilerParams(dimension_semantics=sem, vmem_limit_bytes=VMEM_LIMIT_BYTES)


def _norm_proj_kernel(x_ref, g_ref, w_ref, o_ref, xn_ref):
    @pl.when(pl.program_id(1) == 0)
    def _():
        x = x_ref[...]
        ms = jnp.mean(x * x, axis=-1, keepdims=True)
        xn_ref[...] = (x * lax.rsqrt(ms + RMS_EPS) * g_ref[...]).astype(BF16)

    o_ref[...] = jnp.dot(xn_ref[...], w_ref[...], preferred_element_type=F32).astype(o_ref.dtype)


def _norm_proj(x2d, gain, w_bf16):
    m, d = x2d.shape
    n = w_bf16.shape[1]
    return pl.pallas_call(
        _norm_proj_kernel,
        grid=(m // PROJ_TM, n // PROJ_TN),
        in_specs=[
            pl.BlockSpec((PROJ_TM, d), lambda i, j: (i, 0)),
            pl.BlockSpec((1, d), lambda i, j: (0, 0)),
            pl.BlockSpec((d, PROJ_TN), lambda i, j: (0, j)),
        ],
        out_specs=pl.BlockSpec((PROJ_TM, PROJ_TN), lambda i, j: (i, j)),
        out_shape=jax.ShapeDtypeStruct((m, n), BF16),
        scratch_shapes=[pltpu.VMEM((PROJ_TM, d), BF16)],
        compiler_params=_cparams(("parallel", "arbitrary")),
        name="norm_proj",
    )(x2d, gain.reshape(1, d), w_bf16)


def _out_post_kernel(*refs, n_parts):
    y_refs = refs[:n_parts]
    w_ref, h_ref, g_ref, o_ref = refs[n_parts:]
    y = y_refs[0][...].astype(F32)
    for r in y_refs[1:]:
        y = y + r[...].astype(F32)
    t = jnp.dot(y.astype(BF16), w_ref[...], preferred_element_type=F32)
    ms = jnp.mean(t * t, axis=-1, keepdims=True)
    o_ref[...] = h_ref[...] + t * lax.rsqrt(ms + RMS_EPS) * g_ref[...]


def _out_post(parts, w_bf16, h2d, gain):
    m, d = h2d.shape
    k = w_bf16.shape[0]
    n_parts = len(parts)
    return pl.pallas_call(
        functools.partial(_out_post_kernel, n_parts=n_parts),
        grid=(m // POST_TM,),
        in_specs=[pl.BlockSpec((POST_TM, k), lambda i: (i, 0)) for _ in parts] + [
            pl.BlockSpec((k, d), lambda i: (0, 0)),
            pl.BlockSpec((POST_TM, d), lambda i: (i, 0)),
            pl.BlockSpec((1, d), lambda i: (0, 0)),
        ],
        out_specs=pl.BlockSpec((POST_TM, d), lambda i: (i, 0)),
        out_shape=jax.ShapeDtypeStruct((m, d), F32),
        compiler_params=_cparams(("parallel",)),
        name="out_post",
    )(*parts, w_bf16, h2d, gain.reshape(1, d))


def _compress_kernel(x_ref, pos_ref, w1_ref, w2_ref, o_ref):
    half = CMP_STRIDE * HEAD_DIM
    for which in range(2):
        x = x_ref[0, which, 0].astype(F32)
        xa = (x + pos_ref[which, 0:1, :]).astype(BF16)
        xb = (x + pos_ref[which, 1:2, :]).astype(BF16)
        pa = jnp.dot(xa, w1_ref[which, :half, :], preferred_element_type=F32)
        pb = jnp.dot(xb, w1_ref[which, half:, :], preferred_element_type=F32)
        hid = pa + pltpu.roll(pb, N_CMP_PAD - 1, 0)
        hid = hid * jax.nn.sigmoid(hid)
        o_ref[0, which, 0] = jnp.dot(hid.astype(BF16), w2_ref[which],
                                     preferred_element_type=F32).astype(o_ref.dtype)


def _compress(ckv_rows, pos, w1, w2d):
    b = ckv_rows.shape[0]
    row_w = CMP_STRIDE * HEAD_DIM
    return pl.pallas_call(
        _compress_kernel,
        grid=(b, GROUPS),
        in_specs=[
            pl.BlockSpec((1, 2, 1, N_CMP_PAD, row_w), lambda i, g: (i, 0, g, 0, 0)),
            pl.BlockSpec((2, 2, row_w), lambda i, g: (0, 0, 0)),
            pl.BlockSpec((2, 2 * row_w, CMP_HIDDEN), lambda i, g: (0, 0, 0)),
            pl.BlockSpec((2, CMP_HIDDEN, 2 * HEAD_DIM), lambda i, g: (0, 0, 0)),
        ],
        out_specs=pl.BlockSpec((1, 2, 1, N_CMP_PAD, 2 * HEAD_DIM), lambda i, g: (i, 0, g, 0, 0)),
        out_shape=jax.ShapeDtypeStruct((b, 2, GROUPS, N_CMP_PAD, 2 * HEAD_DIM), BF16),
        compiler_params=_cparams(("parallel", "parallel")),
        name="compress",
    )(ckv_rows, pos, w1, w2d)


def _lane_consts(rows):
    lane = np.arange(LANES)
    lo = (lane < HEAD_DIM).astype(np.float32)
    hi = (lane >= HEAD_DIM).astype(np.float32)
    last = (lane == LANES - 1).astype(np.float32)
    first = (lane == 0).astype(np.float32)
    c = np.stack([lo, hi, 1.0 - last, last, 1.0 - first, first])
    return jnp.asarray(np.broadcast_to(c[:, None, :], (6, rows, LANES)), BF16)


def _gate_column(sig, col):
    lane = lax.broadcasted_iota(jnp.int32, sig.shape, 1)
    return jnp.sum(jnp.where(lane == col, sig, 0.0), axis=1, keepdims=True)


def _split_dot(x, w_bf16):
    hi = x.astype(BF16)
    r1 = x - hi.astype(F32)
    mid = r1.astype(BF16)
    lo = (r1 - mid.astype(F32)).astype(BF16)
    dot = functools.partial(jnp.dot, preferred_element_type=F32)
    return dot(hi, w_bf16) + dot(mid, w_bf16) + dot(lo, w_bf16)


def _silu(x):
    return x * jax.nn.sigmoid(x)


def _cmp_select_kernel(q_ref, gate_ref, z_ref, ckv_ref, bias_ref, ovl_ref, lc_ref, y_ref, sel_ref):
    g = pl.program_id(0)
    qi = pl.program_id(1)
    tq = q_ref.shape[1]
    q = q_ref[0]
    kk = ckv_ref[0, 0, 0]
    vv = ckv_ref[0, 1, 0]
    sig = jax.nn.sigmoid(gate_ref[0].astype(F32))
    lane = lax.broadcasted_iota(jnp.int32, (tq, LANES), 1)

    psum = jnp.zeros((tq, N_CMP_PAD), F32)
    pairs = []
    for a in range(2):
        qp = q[:, a * LANES:(a + 1) * LANES]
        outs = []
        for e in range(2):
            r = 2 * a + e
            s = lax.dot_general(qp * lc_ref[e], kk, _NT, preferred_element_type=F32)
            bias = bias_ref[r]
            s = s + bias
            valid = bias > 0.5 * MASKED
            m = jnp.max(s, axis=1, keepdims=True)
            p = jnp.exp(s - m)
            p = jnp.where(valid, p / jnp.sum(p, axis=1, keepdims=True), 0.0)
            psum = psum + p
            o = jnp.dot(p.astype(BF16), vv, preferred_element_type=F32)
            outs.append(o * _gate_column(sig, HEADS_PER_GROUP * g + r))
        pairs.append(jnp.where(lane < HEAD_DIM, outs[0], outs[1]))
    y_ref[0] = jnp.concatenate(pairs, axis=1) * _silu(z_ref[0].astype(F32))

    imp = _split_dot(psum, ovl_ref[...])
    t = qi * tq + lax.broadcasted_iota(jnp.int32, (tq, LANES), 0)
    cur = lax.shift_right_logical(t, int(math.log2(SLC_BLOCK)))
    valid_blk = lane * SLC_BLOCK <= t
    forced = (lane == 0) | (lane == cur) | (lane == cur - 1)
    score = jnp.where(valid_blk, imp + jnp.where(forced, FORCED_SCORE, 0.0), MASKED)
    cnt = jnp.zeros((tq, LANES), jnp.int32)
    for jp in range(N_SLC):
        col = score[:, jp:jp + 1]
        beats = (col > score) | ((col == score) & (lane > jp))
        cnt = cnt + beats.astype(jnp.int32)
    chosen = (cnt < SLC_TOPN) & (lane < N_SLC)
    sel_ref[0, 0] = jnp.where(chosen, 1.0, 0.0).astype(sel_ref.dtype)


def _cmp_select(proj, ckv, bias_cmp, ovl, lane_consts):
    b = proj.shape[0]
    tq = ATT_TQ
    grp_w = HEADS_PER_GROUP * HEAD_DIM
    return pl.pallas_call(
        _cmp_select_kernel,
        grid=(GROUPS, SEQ // tq, b),
        in_specs=[
            pl.BlockSpec((1, tq, grp_w), lambda g, qi, i: (i, qi, NSA_Q_COL // grp_w + g)),
            pl.BlockSpec((1, tq, LANES), lambda g, qi, i: (i, qi, NSA_G_COL // LANES)),
            pl.BlockSpec((1, tq, grp_w), lambda g, qi, i: (i, qi, NSA_Z_COL // grp_w + g)),
            pl.BlockSpec((1, 2, 1, N_CMP_PAD, 2 * HEAD_DIM), lambda g, qi, i: (i, 0, g, 0, 0)),
            pl.BlockSpec((HEADS_PER_GROUP, tq, N_CMP_PAD), lambda g, qi, i: (g, qi, 0)),
            pl.BlockSpec((N_CMP_PAD, LANES), lambda g, qi, i: (0, 0)),
            pl.BlockSpec(memory_space=pltpu.VMEM),
        ],
        out_specs=[
            pl.BlockSpec((1, tq, grp_w), lambda g, qi, i: (i, qi, g)),
            pl.BlockSpec((1, 1, tq, LANES), lambda g, qi, i: (i, g, qi, 0)),
        ],
        out_shape=[
            jax.ShapeDtypeStruct((b, SEQ, NSA_WIDTH), F32),
            jax.ShapeDtypeStruct((b, GROUPS, SEQ, LANES), BF16),
        ],
        compiler_params=_cparams(("parallel", "parallel", "parallel")),
        name="cmp_select",
    )(proj, proj, proj, ckv, bias_cmp, ovl, lane_consts)


def _flash_kernel(*refs, selected, n_steps):
    if selected:
        (q_ref, kv_ref, gate_ref, z_ref, a_ref, lc_ref, sel_ref, exp_ref,
         y_ref, qm_ref, m_ref, acc_ref) = refs
    else:
        q_ref, kv_ref, gate_ref, z_ref, a_ref, lc_ref, y_ref, qm_ref, m_ref, acc_ref = refs
    g = pl.program_id(1)
    qi = pl.program_id(2)
    step = pl.program_id(3)
    ki = qi - step
    tq = q_ref.shape[1]

    @pl.when(step == 0)
    def _():
        m_ref[...] = jnp.full(m_ref.shape, -jnp.inf, F32)
        acc_ref[...] = jnp.zeros(acc_ref.shape, F32)
        q = q_ref[0]
        for r in range(HEADS_PER_GROUP):
            a, e = divmod(r, 2)
            qm_ref[r] = q[:, a * LANES:(a + 1) * LANES] * lc_ref[e]

    @pl.when(ki >= 0)
    def _():
        kv = kv_ref[0]
        kk = kv[:, :LANES]
        vv = kv[:, LANES:]
        v_even = vv * lc_ref[2] + lc_ref[3]
        v_odd = vv * lc_ref[4] + lc_ref[5]
        kind = jnp.minimum(step, 2)
        if selected:
            picked = jnp.dot(sel_ref[0, 0], exp_ref[ki], preferred_element_type=F32) > 0.5
        for r in range(HEADS_PER_GROUP):
            s = lax.dot_general(qm_ref[r], kk, _NT, preferred_element_type=F32)
            s = s + a_ref[kind, HEADS_PER_GROUP * g + r]
            if selected:
                s = jnp.where(picked, s, MASKED)
            m_prev = m_ref[r]
            m_new = jnp.maximum(m_prev, jnp.max(s, axis=1, keepdims=True))
            alpha = jnp.exp(m_prev - m_new)
            p = jnp.exp(s - m_new)
            pv = jnp.dot(p.astype(BF16), v_even if r % 2 == 0 else v_odd,
                         preferred_element_type=F32)
            acc_ref[r] = alpha * acc_ref[r] + pv
            m_ref[r] = m_new

    @pl.when(step == n_steps - 1)
    def _():
        sig = jax.nn.sigmoid(gate_ref[0].astype(F32))
        branch = 1 if selected else 2
        lane = lax.broadcasted_iota(jnp.int32, (tq, LANES), 1)
        pairs = []
        for a in range(2):
            acc_e = acc_ref[2 * a]
            acc_o = acc_ref[2 * a + 1]
            col = branch * HEADS + HEADS_PER_GROUP * g + 2 * a
            w_e = _gate_column(sig, col) / acc_e[:, LANES - 1:LANES]
            w_o = _gate_column(sig, col + 1) / acc_o[:, 0:1]
            pairs.append(jnp.where(lane < HEAD_DIM, acc_e * w_e, acc_o * w_o))
        y_ref[0] = jnp.concatenate(pairs, axis=1) * _silu(z_ref[0].astype(F32))


def _flash(proj, a_tiles, lane_consts, sel=None, expand=None):
    selected = sel is not None
    b = proj.shape[0]
    tq, tk = ATT_TQ, ATT_TK
    nq = SEQ // tq
    n_steps = nq if selected else WIN_SIZE // tk + 1
    branch = 1 if selected else 2
    grp_w = HEADS_PER_GROUP * HEAD_DIM
    kv_blk = NSA_KV_COL // grp_w + (branch - 1) * GROUPS
    z_blk = NSA_Z_COL // grp_w + branch * GROUPS
    in_specs = [
        pl.BlockSpec((1, tq, grp_w), lambda i, g, qi, s: (i, qi, NSA_Q_COL // grp_w + g)),
        pl.BlockSpec((1, tk, grp_w), lambda i, g, qi, s: (i, jnp.maximum(qi - s, 0), kv_blk + g)),
        pl.BlockSpec((1, tq, LANES), lambda i, g, qi, s: (i, qi, NSA_G_COL // LANES)),
        pl.BlockSpec((1, tq, grp_w), lambda i, g, qi, s: (i, qi, z_blk + g)),
        pl.BlockSpec(memory_space=pltpu.VMEM),
        pl.BlockSpec(memory_space=pltpu.VMEM),
    ]
    args = [proj, proj, proj, proj, a_tiles, lane_consts]
    if selected:
        in_specs += [
            pl.BlockSpec((1, 1, tq, LANES), lambda i, g, qi, s: (i, g, qi, 0)),
            pl.BlockSpec(memory_space=pltpu.VMEM),
        ]
        args += [sel, expand]
    return pl.pallas_call(
        functools.partial(_flash_kernel, selected=selected, n_steps=n_steps),
        grid=(b, GROUPS, nq, n_steps),
        in_specs=in_specs,
        out_specs=pl.BlockSpec((1, tq, grp_w), lambda i, g, qi, s: (i, qi, g)),
        out_shape=jax.ShapeDtypeStruct((b, SEQ, NSA_WIDTH), F32),
        scratch_shapes=[
            pltpu.VMEM((HEADS_PER_GROUP, tq, LANES), BF16),
            pltpu.VMEM((HEADS_PER_GROUP, tq, 1), F32),
            pltpu.VMEM((HEADS_PER_GROUP, tq, LANES), F32),
        ],
        compiler_params=_cparams(("parallel", "parallel", "parallel", "arbitrary")),
        name="flash_selected" if selected else "flash_window",
    )(*args)


def _retention_kernel(q_ref, k_ref, v_ref, z_ref, cos_ref, sin_ref, inner_ref, xi_ref, zeta_ref,
                      gn_ref, y_ref, state_ref, *, decays):
    @pl.when(pl.program_id(1) == 0)
    def _():
        state_ref[...] = jnp.zeros(state_ref.shape, F32)

    cos = cos_ref[...]
    sin = sin_ref[...]
    half = RET_QK_DIM // 2

    def rot(x):
        x1, x2 = x[:, :half], x[:, half:]
        return jnp.concatenate([x1 * cos - x2 * sin, x1 * sin + x2 * cos], axis=1)

    for h in range(RET_HEADS):
        qs = slice(h * RET_QK_DIM, (h + 1) * RET_QK_DIM)
        vs = slice(h * RET_V_DIM, (h + 1) * RET_V_DIM)
        qr = rot(q_ref[0, :, qs].astype(F32))
        kr = rot(k_ref[0, :, qs].astype(F32)) * (RET_QK_DIM ** -0.5)
        qb = qr.astype(BF16)
        vh = v_ref[0, :, vs]
        attn = lax.dot_general(qb, kr.astype(BF16), _NT, preferred_element_type=F32) * inner_ref[h]
        st = state_ref[h]
        o = (jnp.dot(attn.astype(BF16), vh, preferred_element_type=F32)
             + jnp.dot(qb, st.astype(BF16), preferred_element_type=F32) * xi_ref[h])
        kz = (kr * zeta_ref[h]).astype(BF16)
        state_ref[h] = st * decays[h] + lax.dot_general(kz, vh, _TN, preferred_element_type=F32)
        mu = jnp.mean(o, axis=1, keepdims=True)
        d = o - mu
        var = jnp.mean(d * d, axis=1, keepdims=True)
        on = d * lax.rsqrt(var + GN_EPS) * gn_ref[h]
        y_ref[0, :, vs] = (on * _silu(z_ref[0, :, vs].astype(F32))).astype(y_ref.dtype)


def _retention(proj, gn_gain, tables):
    b = proj.shape[0]
    c = RET_CHUNK
    cos, sin, inner, xi, zeta, decays = tables
    v_blk = 2 * RET_QK_WIDTH // RET_WIDTH
    return pl.pallas_call(
        functools.partial(_retention_kernel, decays=decays),
        grid=(b, SEQ // c),
        in_specs=[
            pl.BlockSpec((1, c, RET_QK_WIDTH), lambda i, j: (i, j, 0)),
            pl.BlockSpec((1, c, RET_QK_WIDTH), lambda i, j: (i, j, 1)),
            pl.BlockSpec((1, c, RET_WIDTH), lambda i, j: (i, j, v_blk)),
            pl.BlockSpec((1, c, RET_WIDTH), lambda i, j: (i, j, v_blk + 1)),
            pl.BlockSpec((c, RET_QK_DIM // 2), lambda i, j: (j, 0)),
            pl.BlockSpec((c, RET_QK_DIM // 2), lambda i, j: (j, 0)),
            pl.BlockSpec((RET_HEADS, c, c), lambda i, j: (0, 0, 0)),
            pl.BlockSpec((RET_HEADS, c, 1), lambda i, j: (0, 0, 0)),
            pl.BlockSpec((RET_HEADS, c, 1), lambda i, j: (0, 0, 0)),
            pl.BlockSpec((RET_HEADS, 1, RET_V_DIM), lambda i, j: (0, 0, 0)),
        ],
        out_specs=pl.BlockSpec((1, c, RET_WIDTH), lambda i, j: (i, j, 0)),
        out_shape=jax.ShapeDtypeStruct((b, SEQ, RET_WIDTH), BF16),
        scratch_shapes=[pltpu.VMEM((RET_HEADS, RET_QK_DIM, RET_V_DIM), F32)],
        compiler_params=_cparams(("parallel", "arbitrary")),
        name="retention",
    )(proj, proj, proj, proj, cos, sin, inner, xi, zeta, gn_gain.reshape(RET_HEADS, 1, RET_V_DIM))


def _t5_bucket_np(dist):
    n = np.maximum(dist, 0)
    max_exact = REL_BUCKETS // 2
    nf = np.maximum(n, 1).astype(np.float64)
    large = max_exact + (np.log(nf / max_exact) / math.log(REL_MAX_DIST / max_exact)
                         * (REL_BUCKETS - max_exact)).astype(np.int64)
    large = np.minimum(large, REL_BUCKETS - 1)
    return np.where(n < max_exact, n, large).astype(np.int32)


def _nsa_tables(table):
    tq, tk = ATT_TQ, ATT_TK
    t = np.arange(SEQ)[:, None]
    end = np.arange(N_CMP_PAD)[None, :] * CMP_STRIDE + CMP_BLOCK - 1
    dist = t - end
    bias_cmp = jnp.take(table, jnp.asarray(_t5_bucket_np(dist)), axis=0)
    bias_cmp = jnp.where(jnp.asarray(dist >= 0)[:, :, None], bias_cmp, MASKED)
    bias_cmp = bias_cmp.transpose(2, 0, 1)
    rel = table - table[REL_BUCKETS - 1][None, :]
    i = np.arange(tq)[:, None]
    j = np.arange(tk)[None, :]
    tiles = []
    for off in range(3):
        d = off * tk + i - j
        vals = jnp.take(rel, jnp.asarray(_t5_bucket_np(d)), axis=0)
        ok = (d >= 0) & (d < WIN_SIZE) if off == 2 else (d >= 0)
        tiles.append(jnp.where(jnp.asarray(ok)[:, :, None], vals, MASKED).transpose(2, 0, 1))
    a_win = jnp.stack(tiles)
    a_slc = jnp.stack([tiles[0], tiles[1], jnp.zeros_like(tiles[0])])
    cs = np.arange(N_CMP_PAD)[:, None] * CMP_STRIDE
    jj = np.arange(LANES)[None, :]
    ovl = ((cs < (jj + 1) * SLC_BLOCK) & (cs + CMP_BLOCK > jj * SLC_BLOCK)
           & (np.arange(N_CMP_PAD)[:, None] < N_CMP) & (jj < N_SLC))
    ovl = jnp.asarray(ovl.astype(np.float32), BF16)
    key = np.arange(SEQ).reshape(SEQ // tk, 1, tk)
    expand = (key // SLC_BLOCK == np.arange(LANES).reshape(1, LANES, 1))
    expand = jnp.asarray(expand.astype(np.float32), BF16)
    assert tq == tk
    return bias_cmp, a_slc, a_win, ovl, expand, _lane_consts(tq)


def _retention_tables():
    c = RET_CHUNK
    log_g = jnp.log(1.0 - 2.0 ** (-5.0 - jnp.arange(RET_HEADS, dtype=F32)))
    i = jnp.arange(c, dtype=F32)
    diff = i[:, None] - i[None, :]
    inner = jnp.where(diff >= 0, jnp.exp(diff[None] * log_g[:, None, None]), 0.0)
    xi = jnp.exp((i + 1.0)[None, :] * log_g[:, None])[:, :, None]
    zeta = jnp.exp((c - 1.0 - i)[None, :] * log_g[:, None])[:, :, None]
    decays = tuple(float((1.0 - 2.0 ** (-5.0 - h)) ** c) for h in range(RET_HEADS))
    inv = 1.0 / (ROPE_BASE ** jnp.linspace(0.0, 1.0, RET_QK_DIM // 2, dtype=F32))
    ang = jnp.arange(SEQ, dtype=F32)[:, None] * inv[None, :]
    return jnp.cos(ang), jnp.sin(ang), inner, xi, zeta, decays


def _nsa_w_in_layout(w):
    d = w.shape[0]
    q = w[:, :NSA_WIDTH] * (HEAD_DIM ** -0.5)
    kv0 = NSA_WIDTH
    g0 = kv0 + 6 * KV_WIDTH
    z0 = g0 + 3 * HEADS
    slab = lambda n: w[:, kv0 + n * KV_WIDTH: kv0 + (n + 1) * KV_WIDTH]
    cols = [q, w[:, z0:z0 + 3 * NSA_WIDTH]]
    for branch in range(2):
        k, v = slab(2 + 2 * branch), slab(3 + 2 * branch)
        for g in range(GROUPS):
            kg = k[:, g * HEAD_DIM:(g + 1) * HEAD_DIM]
            vg = v[:, g * HEAD_DIM:(g + 1) * HEAD_DIM]
            cols += [kg, kg, vg, vg]
    cols += [slab(0), slab(1), w[:, g0:z0]]
    used = NSA_G_COL + 3 * HEADS
    cols.append(jnp.zeros((d, NSA_PROJ_PAD - used), w.dtype))
    return jnp.concatenate(cols, axis=1).astype(BF16)


def _nsa_layer(h2d, b, pre_gain, post_gain, w_in, w_out, k_pos, k_w1, k_w2, v_pos, v_w1, v_w2, tabs):
    bias_cmp, a_slc, a_win, ovl, expand, lane_consts = tabs
    proj = _norm_proj(h2d, pre_gain, _nsa_w_in_layout(w_in)).reshape(b, SEQ, NSA_PROJ_PAD)
    ckv = proj[:, :, NSA_CKV_COL:NSA_CKV_COL + 2 * KV_WIDTH]
    ckv = ckv.reshape(b, N_CMP_PAD, CMP_STRIDE, 2, GROUPS, HEAD_DIM).transpose(0, 3, 4, 1, 2, 5)
    ckv = ckv.reshape(b, 2, GROUPS, N_CMP_PAD, CMP_STRIDE * HEAD_DIM)
    pos = jnp.stack([k_pos, v_pos]).reshape(2, 2, CMP_STRIDE * HEAD_DIM)
    w1 = jnp.stack([k_w1, v_w1]).astype(BF16)
    w2 = jnp.stack([k_w2, v_w2])
    w2d = jnp.concatenate([w2, w2], axis=2).astype(BF16)
    ckv_c = _compress(ckv, pos, w1, w2d)
    y_cmp, sel = _cmp_select(proj, ckv_c, bias_cmp, ovl, lane_consts)
    y_slc = _flash(proj, a_slc, lane_consts, sel, expand)
    y_win = _flash(proj, a_win, lane_consts)
    parts = [y.reshape(b * SEQ, NSA_WIDTH) for y in (y_cmp, y_slc, y_win)]
    return _out_post(parts, w_out.astype(BF16), h2d, post_gain)


def _ret_layer(h2d, b, pre_gain, post_gain, w_in, w_out, gn_gain, tabs):
    proj = _norm_proj(h2d, pre_gain, w_in.astype(BF16)).reshape(b, SEQ, -1)
    y = _retention(proj, gn_gain, tabs)
    return _out_post([y.reshape(b * SEQ, RET_WIDTH)], w_out.astype(BF16), h2d, post_gain)


def kernel(x, pre_norm_gain, post_norm_gain, rel_bias_table, nsa_w_in, nsa_w_out, nsa_cmp_k_pos, nsa_cmp_k_w1, nsa_cmp_k_w2, nsa_cmp_v_pos, nsa_cmp_v_w1, nsa_cmp_v_w2, ret_w_in, ret_w_out, ret_gn_gain):
    b, s, d = x.shape
    assert s == SEQ and d == D_MODEL
    nsa_tabs = _nsa_tables(rel_bias_table)
    ret_tabs = _retention_tables()
    h = x.reshape(b * s, d)
    for layer in range(DEPTH):
        slot = layer // 2
        if layer % 2 == 0:
            h = _nsa_layer(h, b, pre_norm_gain[layer], post_norm_gain[layer], nsa_w_in[slot],
                           nsa_w_out[slot], nsa_cmp_k_pos[slot], nsa_cmp_k_w1[slot],
                           nsa_cmp_k_w2[slot], nsa_cmp_v_pos[slot], nsa_cmp_v_w1[slot],
                           nsa_cmp_v_w2[slot], nsa_tabs)
        else:
            h = _ret_layer(h, b, pre_norm_gain[layer], post_norm_gain[layer], ret_w_in[slot],
                           ret_w_out[slot], ret_gn_gain[slot], ret_tabs)
    return h.reshape(b, s, d)
```

```python
import functools
import math

import numpy as np
import jax
import jax.numpy as jnp
from jax import lax
from jax.experimental import pallas as pl
from jax.experimental.pallas import tpu as pltpu

F32 = jnp.float32
BF16 = jnp.bfloat16

D_MODEL = 1024
SEQ = 2048
DEPTH = 4
RMS_EPS = 1e-6
GN_EPS = 1e-6
MASKED = -1e30

HEADS = 16
HEAD_DIM = 64
GROUPS = 4
HEADS_PER_GROUP = HEADS // GROUPS
NSA_WIDTH = HEADS * HEAD_DIM
KV_WIDTH = GROUPS * HEAD_DIM
CMP_BLOCK = 32
CMP_STRIDE = 16
CMP_HIDDEN = 256
N_CMP = (SEQ - CMP_BLOCK) // CMP_STRIDE + 1
N_CMP_PAD = 128
SLC_BLOCK = 64
N_SLC = SEQ // SLC_BLOCK
SLC_TOPN = 16
WIN_SIZE = 512
FORCED_SCORE = 1e3
REL_BUCKETS = 32
REL_MAX_DIST = 128

RET_HEADS = 4
RET_QK_DIM = 256
RET_V_DIM = 512
RET_QK_WIDTH = RET_HEADS * RET_QK_DIM
RET_WIDTH = RET_HEADS * RET_V_DIM
ROPE_BASE = 10000.0

LANES = 128
VMEM_LIMIT_BYTES = 56 * 1024 * 1024

PROJ_TM = 1024
PROJ_TN = 1024
POST_TM = 512
ATT_TQ = 256
ATT_TK = 256
RET_CHUNK = 256

NSA_Q_COL = 0
NSA_Z_COL = 1024
NSA_KV_COL = 4096
NSA_CKV_COL = 6144
NSA_G_COL = 6656
NSA_PROJ_PAD = 7168

_NT = (((1,), (1,)), ((), ()))
_TN = (((0,), (0,)), ((), ()))


def _cparams(sem):
    return pltpu.CompilerParams(dimension_semantics=sem, vmem_limit_bytes=VMEM_LIMIT_BYTES)


def _norm_proj_kernel(x_ref, g_ref, w_ref, o_ref, xn_ref):
    @pl.when(pl.program_id(1) == 0)
    def _():
        x = x_ref[...]
        ms = jnp.mean(x * x, axis=-1, keepdims=True)
        xn_ref[...] = (x * lax.rsqrt(ms + RMS_EPS) * g_ref[...]).astype(BF16)

    o_ref[...] = jnp.dot(xn_ref[...], w_ref[...], preferred_element_type=F32).astype(o_ref.dtype)


def _norm_proj(x2d, gain, w_bf16):
    m, d = x2d.shape
    n = w_bf16.shape[1]
    return pl.pallas_call(
        _norm_proj_kernel,
        grid=(m // PROJ_TM, n // PROJ_TN),
        in_specs=[
            pl.BlockSpec((PROJ_TM, d), lambda i, j: (i, 0)),
            pl.BlockSpec((1, d), lambda i, j: (0, 0)),
            pl.BlockSpec((d, PROJ_TN), lambda i, j: (0, j)),
        ],
        out_specs=pl.BlockSpec((PROJ_TM, PROJ_TN), lambda i, j: (i, j)),
        out_shape=jax.ShapeDtypeStruct((m, n), BF16),
        scratch_shapes=[pltpu.VMEM((PROJ_TM, d), BF16)],
        compiler_params=_cparams(("parallel", "arbitrary")),
        name="norm_proj",
    )(x2d, gain.reshape(1, d), w_bf16)


def _out_post_kernel(*refs, n_parts):
    y_refs = refs[:n_parts]
    w_ref, h_ref, g_ref, o_ref = refs[n_parts:]
    y = y_refs[0][...].astype(F32)
    for r in y_refs[1:]:
        y = y + r[...].astype(F32)
    t = jnp.dot(y.astype(BF16), w_ref[...], preferred_element_type=F32)
    ms = jnp.mean(t * t, axis=-1, keepdims=True)
    o_ref[...] = h_ref[...] + t * lax.rsqrt(ms + RMS_EPS) * g_ref[...]


def _out_post(parts, w_bf16, h2d, gain):
    m, d = h2d.shape
    k = w_bf16.shape[0]
    n_parts = len(parts)
    return pl.pallas_call(
        functools.partial(_out_post_kernel, n_parts=n_parts),
        grid=(m // POST_TM,),
        in_specs=[pl.BlockSpec((POST_TM, k), lambda i: (i, 0)) for _ in parts] + [
            pl.BlockSpec((k, d), lambda i: (0, 0)),
            pl.BlockSpec((POST_TM, d), lambda i: (i, 0)),
            pl.BlockSpec((1, d), lambda i: (0, 0)),
        ],
        out_specs=pl.BlockSpec((POST_TM, d), lambda i: (i, 0)),
        out_shape=jax.ShapeDtypeStruct((m, d), F32),
        compiler_params=_cparams(("parallel",)),
        name="out_post",
    )(*parts, w_bf16, h2d, gain.reshape(1, d))


def _compress_kernel(x_ref, pos_ref, w1_ref, w2_ref, o_ref):
    half = CMP_STRIDE * HEAD_DIM
    for which in range(2):
        x = x_ref[0, which, 0].astype(F32)
        xa = (x + pos_ref[which, 0:1, :]).astype(BF16)
        xb = (x + pos_ref[which, 1:2, :]).astype(BF16)
        pa = jnp.dot(xa, w1_ref[which, :half, :], preferred_element_type=F32)
        pb = jnp.dot(xb, w1_ref[which, half:, :], preferred_element_type=F32)
        hid = pa + pltpu.roll(pb, N_CMP_PAD - 1, 0)
        hid = hid * jax.nn.sigmoid(hid)
        o_ref[0, which, 0] = jnp.dot(hid.astype(BF16), w2_ref[which],
                                     preferred_element_type=F32).astype(o_ref.dtype)


def _compress(ckv_rows, pos, w1, w2d):
    b = ckv_rows.shape[0]
    row_w = CMP_STRIDE * HEAD_DIM
    return pl.pallas_call(
        _compress_kernel,
        grid=(b, GROUPS),
        in_specs=[
            pl.BlockSpec((1, 2, 1, N_CMP_PAD, row_w), lambda i, g: (i, 0, g, 0, 0)),
            pl.BlockSpec((2, 2, row_w), lambda i, g: (0, 0, 0)),
            pl.BlockSpec((2, 2 * row_w, CMP_HIDDEN), lambda i, g: (0, 0, 0)),
            pl.BlockSpec((2, CMP_HIDDEN, 2 * HEAD_DIM), lambda i, g: (0, 0, 0)),
        ],
        out_specs=pl.BlockSpec((1, 2, 1, N_CMP_PAD, 2 * HEAD_DIM), lambda i, g: (i, 0, g, 0, 0)),
        out_shape=jax.ShapeDtypeStruct((b, 2, GROUPS, N_CMP_PAD, 2 * HEAD_DIM), BF16),
        compiler_params=_cparams(("parallel", "parallel")),
        name="compress",
    )(ckv_rows, pos, w1, w2d)


def _lane_consts(rows):
    lane = np.arange(LANES)
    lo = (lane < HEAD_DIM).astype(np.float32)
    hi = (lane >= HEAD_DIM).astype(np.float32)
    last = (lane == LANES - 1).astype(np.float32)
    first = (lane == 0).astype(np.float32)
    c = np.stack([lo, hi, 1.0 - last, last, 1.0 - first, first])
    return jnp.asarray(np.broadcast_to(c[:, None, :], (6, rows, LANES)), BF16)


def _bf16_terms(x, n):
    terms = []
    for _ in range(n - 1):
        t = x.astype(BF16)
        terms.append(t)
        x = x - t.astype(F32)
    terms.append(x.astype(BF16))
    return terms


def _gates(gate_logits, expand):
    sig = jax.nn.sigmoid(gate_logits.astype(F32))
    return sum(jnp.dot(t, expand, preferred_element_type=F32) for t in _bf16_terms(sig, 2))


def _silu(x):
    return x * jax.nn.sigmoid(x)


def _cmp_select_kernel(q_ref, gate_ref, z_ref, ckv_ref, bias_ref, ovl_ref, lc_ref, gexp_ref,
                       y_ref, sel_ref):
    g = pl.program_id(0)
    qi = pl.program_id(1)
    tq = q_ref.shape[1]
    q = q_ref[0]
    kk = ckv_ref[0, 0, 0]
    vv = ckv_ref[0, 1, 0]
    lane = lax.broadcasted_iota(jnp.int32, (tq, LANES), 1)

    psum = jnp.zeros((N_CMP_PAD, tq), F32)
    pairs = []
    for a in range(2):
        qp = q[:, a * LANES:(a + 1) * LANES]
        outs = []
        for e in range(2):
            s = lax.dot_general(kk, qp * lc_ref[e], _NT, preferred_element_type=F32)
            bias = bias_ref[2 * a + e]
            s = s + bias
            m = jnp.max(s, axis=0, keepdims=True)
            p = jnp.exp(s - m)
            p = jnp.where(bias > 0.5 * MASKED, p / jnp.sum(p, axis=0, keepdims=True), 0.0)
            psum = psum + p
            outs.append(lax.dot_general(p.astype(BF16), vv, _TN, preferred_element_type=F32))
        pairs.append(jnp.where(lane < HEAD_DIM, outs[0], outs[1]))
    y_ref[0] = (jnp.concatenate(pairs, axis=1) * _gates(gate_ref[0], gexp_ref[g])
                * _silu(z_ref[0].astype(F32)))

    ovl = ovl_ref[...]
    imp = sum(jnp.dot(ovl, t, preferred_element_type=F32) for t in _bf16_terms(psum, 3))
    t = qi * tq + lax.broadcasted_iota(jnp.int32, (N_SLC, tq), 1)
    blk = lax.broadcasted_iota(jnp.int32, (N_SLC, tq), 0)
    cur = lax.shift_right_logical(t, int(math.log2(SLC_BLOCK)))
    forced = (blk == 0) | (blk == cur) | (blk == cur - 1)
    score = jnp.where(blk * SLC_BLOCK <= t, imp + jnp.where(forced, FORCED_SCORE, 0.0), MASKED)
    cnt = jnp.zeros((N_SLC, tq), jnp.int32)
    for jp in range(N_SLC):
        row = score[jp:jp + 1, :]
        beats = (row > score) | ((row == score) & (blk > jp))
        cnt = cnt + beats.astype(jnp.int32)
    sel_ref[0, 0] = jnp.where(cnt < SLC_TOPN, 1.0, 0.0).astype(sel_ref.dtype)


def _cmp_select(proj, ckv, bias_cmp, ovl, lane_consts, gate_expand):
    b = proj.shape[0]
    tq = ATT_TQ
    grp_w = HEADS_PER_GROUP * HEAD_DIM
    return pl.pallas_call(
        _cmp_select_kernel,
        grid=(GROUPS, SEQ // tq, b),
        in_specs=[
            pl.BlockSpec((1, tq, grp_w), lambda g, qi, i: (i, qi, NSA_Q_COL // grp_w + g)),
            pl.BlockSpec((1, tq, LANES), lambda g, qi, i: (i, qi, NSA_G_COL // LANES)),
            pl.BlockSpec((1, tq, grp_w), lambda g, qi, i: (i, qi, NSA_Z_COL // grp_w + g)),
            pl.BlockSpec((1, 2, 1, N_CMP_PAD, 2 * HEAD_DIM), lambda g, qi, i: (i, 0, g, 0, 0)),
            pl.BlockSpec((HEADS_PER_GROUP, N_CMP_PAD, tq), lambda g, qi, i: (g, 0, qi)),
            pl.BlockSpec((N_SLC, N_CMP_PAD), lambda g, qi, i: (0, 0)),
            pl.BlockSpec(memory_space=pltpu.VMEM),
            pl.BlockSpec(memory_space=pltpu.VMEM),
        ],
        out_specs=[
            pl.BlockSpec((1, tq, grp_w), lambda g, qi, i: (i, qi, g)),
            pl.BlockSpec((1, 1, N_SLC, tq), lambda g, qi, i: (i, g, 0, qi)),
        ],
        out_shape=[
            jax.ShapeDtypeStruct((b, SEQ, NSA_WIDTH), F32),
            jax.ShapeDtypeStruct((b, GROUPS, N_SLC, SEQ), BF16),
        ],
        compiler_params=_cparams(("parallel", "parallel", "parallel")),
        name="cmp_select",
    )(proj, proj, proj, ckv, bias_cmp, ovl, lane_consts, gate_expand)


def _flash_kernel(*refs, selected):
    if selected:
        (q_ref, kv_ref, gate_ref, z_ref, a_ref, lc_ref, gexp_ref, sel_ref, exp_ref,
         y_ref, qm_ref, m_ref, acc_ref) = refs
    else:
        (q_ref, kv_ref, gate_ref, z_ref, a_ref, lc_ref, gexp_ref,
         y_ref, qm_ref, m_ref, acc_ref) = refs
    g = pl.program_id(1)
    qi = pl.program_id(2)
    tq = q_ref.shape[1]
    tk = ATT_TK

    q = q_ref[0]
    for r in range(HEADS_PER_GROUP):
        a, e = divmod(r, 2)
        qm_ref[r] = q[:, a * LANES:(a + 1) * LANES] * lc_ref[e]

    def visit(ki, kind, first=False):
        kv = kv_ref[0, pl.ds(pl.multiple_of(ki * tk, tk), tk), :]
        kk = kv[:, :LANES]
        vv = kv[:, LANES:]
        v_even = vv * lc_ref[2] + lc_ref[3]
        v_odd = vv * lc_ref[4] + lc_ref[5]
        if selected:
            picked = lax.dot_general(sel_ref[0, 0], exp_ref[ki], _TN,
                                     preferred_element_type=F32) > 0.5
        for r in range(HEADS_PER_GROUP):
            s = lax.dot_general(qm_ref[r], kk, _NT, preferred_element_type=F32)
            if kind is not None:
                s = s + a_ref[kind, HEADS_PER_GROUP * g + r]
            if selected:
                s = jnp.where(picked, s, MASKED)
            m_cur = jnp.max(s, axis=1, keepdims=True)
            if first:
                m_new = jnp.broadcast_to(m_cur, (tq, LANES))
            else:
                m_prev = m_ref[r]
                m_new = jnp.maximum(m_prev, m_cur)
            p = jnp.exp(s - jnp.concatenate([m_new] * (tk // LANES), axis=1))
            pv = jnp.dot(p.astype(BF16), v_even if r % 2 == 0 else v_odd,
                         preferred_element_type=F32)
            if first:
                acc_ref[r] = pv
            else:
                acc_ref[r] = jnp.exp(m_prev - m_new) * acc_ref[r] + pv
            m_ref[r] = m_new

    visit(qi, 0, first=True)

    @pl.when(qi >= 1)
    def _():
        visit(qi - 1, 1)

    if selected:
        def far(ki, carry):
            visit(ki, None)
            return carry
        lax.fori_loop(0, jnp.maximum(qi - 1, 0), far, 0)
    else:
        @pl.when(qi >= 2)
        def _():
            visit(qi - 2, 2)

    lane = lax.broadcasted_iota(jnp.int32, (tq, LANES), 1)
    pairs = []
    for a in range(2):
        acc_e = acc_ref[2 * a]
        acc_o = acc_ref[2 * a + 1]
        pairs.append(jnp.where(lane < HEAD_DIM, acc_e / acc_e[:, LANES - 1:LANES],
                               acc_o / acc_o[:, 0:1]))
    y_ref[0] = (jnp.concatenate(pairs, axis=1) * _gates(gate_ref[0], gexp_ref[g])
                * _silu(z_ref[0].astype(F32)))


def _flash(proj, a_tiles, lane_consts, gate_expand, sel=None, expand=None):
    selected = sel is not None
    b = proj.shape[0]
    tq = ATT_TQ
    branch = 1 if selected else 2
    grp_w = HEADS_PER_GROUP * HEAD_DIM
    kv_blk = NSA_KV_COL // grp_w + (branch - 1) * GROUPS
    z_blk = NSA_Z_COL // grp_w + branch * GROUPS
    in_specs = [
        pl.BlockSpec((1, tq, grp_w), lambda i, g, qi: (i, qi, NSA_Q_COL // grp_w + g)),
        pl.BlockSpec((1, SEQ, grp_w), lambda i, g, qi: (i, 0, kv_blk + g)),
        pl.BlockSpec((1, tq, LANES), lambda i, g, qi: (i, qi, NSA_G_COL // LANES)),
        pl.BlockSpec((1, tq, grp_w), lambda i, g, qi: (i, qi, z_blk + g)),
        pl.BlockSpec(memory_space=pltpu.VMEM),
        pl.BlockSpec(memory_space=pltpu.VMEM),
        pl.BlockSpec(memory_space=pltpu.VMEM),
    ]
    args = [proj, proj, proj, proj, a_tiles, lane_consts, gate_expand]
    if selected:
        in_specs += [
            pl.BlockSpec((1, 1, N_SLC, tq), lambda i, g, qi: (i, g, 0, qi)),
            pl.BlockSpec(memory_space=pltpu.VMEM),
        ]
        args += [sel, expand]
    return pl.pallas_call(
        functools.partial(_flash_kernel, selected=selected),
        grid=(b, GROUPS, SEQ // tq),
        in_specs=in_specs,
        out_specs=pl.BlockSpec((1, tq, grp_w), lambda i, g, qi: (i, qi, g)),
        out_shape=jax.ShapeDtypeStruct((b, SEQ, NSA_WIDTH), F32),
        scratch_shapes=[
            pltpu.VMEM((HEADS_PER_GROUP, tq, LANES), BF16),
            pltpu.VMEM((HEADS_PER_GROUP, tq, LANES), F32),
            pltpu.VMEM((HEADS_PER_GROUP, tq, LANES), F32),
        ],
        compiler_params=_cparams(("parallel", "parallel", "parallel")),
        name="flash_selected" if selected else "flash_window",
    )(*args)


def _retention_kernel(q_ref, k_ref, v_ref, z_ref, cos_ref, sin_ref, inner_ref, xi_ref, zeta_ref,
                      gn_ref, y_ref, state_ref, *, decays):
    @pl.when(pl.program_id(1) == 0)
    def _():
        state_ref[...] = jnp.zeros(state_ref.shape, F32)

    cos = cos_ref[...]
    sin = sin_ref[...]
    half = RET_QK_DIM // 2

    def rot(x):
        x1, x2 = x[:, :half], x[:, half:]
        return jnp.concatenate([x1 * cos - x2 * sin, x1 * sin + x2 * cos], axis=1)

    for h in range(RET_HEADS):
        qs = slice(h * RET_QK_DIM, (h + 1) * RET_QK_DIM)
        vs = slice(h * RET_V_DIM, (h + 1) * RET_V_DIM)
        qr = rot(q_ref[0, :, qs].astype(F32))
        kr = rot(k_ref[0, :, qs].astype(F32)) * (RET_QK_DIM ** -0.5)
        qb = qr.astype(BF16)
        vh = v_ref[0, :, vs]
        attn = lax.dot_general(qb, kr.astype(BF16), _NT, preferred_element_type=F32) * inner_ref[h]
        st = state_ref[h]
        o = (jnp.dot(attn.astype(BF16), vh, preferred_element_type=F32)
             + jnp.dot(qb, st.astype(BF16), preferred_element_type=F32) * xi_ref[h])
        kz = (kr * zeta_ref[h]).astype(BF16)
        state_ref[h] = st * decays[h] + lax.dot_general(kz, vh, _TN, preferred_element_type=F32)
        mu = jnp.mean(o, axis=1, keepdims=True)
        d = o - mu
        var = jnp.mean(d * d, axis=1, keepdims=True)
        on = d * lax.rsqrt(var + GN_EPS) * gn_ref[h]
        y_ref[0, :, vs] = (on * _silu(z_ref[0, :, vs].astype(F32))).astype(y_ref.dtype)


def _retention(proj, gn_gain, tables):
    b = proj.shape[0]
    c = RET_CHUNK
    cos, sin, inner, xi, zeta, decays = tables
    v_blk = 2 * RET_QK_WIDTH // RET_WIDTH
    return pl.pallas_call(
        functools.partial(_retention_kernel, decays=decays),
        grid=(b, SEQ // c),
        in_specs=[
            pl.BlockSpec((1, c, RET_QK_WIDTH), lambda i, j: (i, j, 0)),
            pl.BlockSpec((1, c, RET_QK_WIDTH), lambda i, j: (i, j, 1)),
            pl.BlockSpec((1, c, RET_WIDTH), lambda i, j: (i, j, v_blk)),
            pl.BlockSpec((1, c, RET_WIDTH), lambda i, j: (i, j, v_blk + 1)),
            pl.BlockSpec((c, RET_QK_DIM // 2), lambda i, j: (j, 0)),
            pl.BlockSpec((c, RET_QK_DIM // 2), lambda i, j: (j, 0)),
            pl.BlockSpec((RET_HEADS, c, c), lambda i, j: (0, 0, 0)),
            pl.BlockSpec((RET_HEADS, c, 1), lambda i, j: (0, 0, 0)),
            pl.BlockSpec((RET_HEADS, c, 1), lambda i, j: (0, 0, 0)),
            pl.BlockSpec((RET_HEADS, 1, RET_V_DIM), lambda i, j: (0, 0, 0)),
        ],
        out_specs=pl.BlockSpec((1, c, RET_WIDTH), lambda i, j: (i, j, 0)),
        out_shape=jax.ShapeDtypeStruct((b, SEQ, RET_WIDTH), BF16),
        scratch_shapes=[pltpu.VMEM((RET_HEADS, RET_QK_DIM, RET_V_DIM), F32)],
        compiler_params=_cparams(("parallel", "arbitrary")),
        name="retention",
    )(proj, proj, proj, proj, cos, sin, inner, xi, zeta, gn_gain.reshape(RET_HEADS, 1, RET_V_DIM))


def _t5_bucket_np(dist):
    n = np.maximum(dist, 0)
    max_exact = REL_BUCKETS // 2
    nf = np.maximum(n, 1).astype(np.float64)
    large = max_exact + (np.log(nf / max_exact) / math.log(REL_MAX_DIST / max_exact)
                         * (REL_BUCKETS - max_exact)).astype(np.int64)
    large = np.minimum(large, REL_BUCKETS - 1)
    return np.where(n < max_exact, n, large).astype(np.int32)


def _skewed(vec, rows, stride, cols):
    p = vec.shape[-1]
    lead = vec.shape[:-1]
    flat = jnp.tile(vec, (1,) * len(lead) + (rows,))[..., :rows * (p - stride)]
    return flat.reshape(lead + (rows, p - stride))[..., :cols]


def _nsa_tables(table):
    tq, tk = ATT_TQ, ATT_TK
    assert tq == tk
    tab_t = table.T
    period = 2 * SEQ
    x = np.arange(period)
    ok = (x >= CMP_BLOCK - 1) & (x < SEQ)
    vec = jnp.take(tab_t, jnp.asarray(_t5_bucket_np(x - (CMP_BLOCK - 1))), axis=1)
    vec = jnp.where(jnp.asarray(ok)[None, :], vec, MASKED)
    bias_cmp = _skewed(vec, N_CMP_PAD, CMP_STRIDE, SEQ)
    rel_t = tab_t - tab_t[:, REL_BUCKETS - 1:]
    period = 2 * tq
    x = np.arange(period)
    x = np.where(x < tq, x, x - period)
    tiles = []
    for off in range(3):
        d = off * tk - x
        ok = (d >= 0) & (d < WIN_SIZE) if off == 2 else (d >= 0)
        vec = jnp.take(rel_t, jnp.asarray(_t5_bucket_np(d)), axis=1)
        vec = jnp.where(jnp.asarray(ok)[None, :], vec, MASKED)
        tiles.append(_skewed(vec, tq, 1, tk))
    a_win = jnp.stack(tiles)
    a_slc = a_win[:2]
    cs = np.arange(N_CMP_PAD)[None, :] * CMP_STRIDE
    jj = np.arange(N_SLC)[:, None]
    ovl = ((cs < (jj + 1) * SLC_BLOCK) & (cs + CMP_BLOCK > jj * SLC_BLOCK)
           & (np.arange(N_CMP_PAD)[None, :] < N_CMP))
    ovl = jnp.asarray(ovl.astype(np.float32), BF16)
    key = np.arange(SEQ).reshape(SEQ // tk, 1, tk)
    expand = (key // SLC_BLOCK == np.arange(N_SLC).reshape(1, N_SLC, 1))
    expand = jnp.asarray(expand.astype(np.float32), BF16)
    c = np.arange(LANES).reshape(1, 1, LANES, 1)
    col = np.arange(HEADS_PER_GROUP * HEAD_DIM).reshape(1, 1, 1, -1) // HEAD_DIM
    br = np.arange(3).reshape(3, 1, 1, 1)
    gg = np.arange(GROUPS).reshape(1, GROUPS, 1, 1)
    gate_expand = jnp.asarray((c == br * HEADS + HEADS_PER_GROUP * gg + col).astype(np.float32),
                              BF16)
    return bias_cmp, a_slc, a_win, ovl, expand, _lane_consts(tq), gate_expand


def _retention_tables():
    c = RET_CHUNK
    log_g = jnp.log(1.0 - 2.0 ** (-5.0 - jnp.arange(RET_HEADS, dtype=F32)))
    i = jnp.arange(c, dtype=F32)
    diff = i[:, None] - i[None, :]
    inner = jnp.where(diff >= 0, jnp.exp(diff[None] * log_g[:, None, None]), 0.0)
    xi = jnp.exp((i + 1.0)[None, :] * log_g[:, None])[:, :, None]
    zeta = jnp.exp((c - 1.0 - i)[None, :] * log_g[:, None])[:, :, None]
    decays = tuple(float((1.0 - 2.0 ** (-5.0 - h)) ** c) for h in range(RET_HEADS))
    inv = 1.0 / (ROPE_BASE ** jnp.linspace(0.0, 1.0, RET_QK_DIM // 2, dtype=F32))
    ang = jnp.arange(SEQ, dtype=F32)[:, None] * inv[None, :]
    return jnp.cos(ang), jnp.sin(ang), inner, xi, zeta, decays


def _nsa_w_in_layout(w):
    d = w.shape[0]
    q = w[:, :NSA_WIDTH] * (HEAD_DIM ** -0.5)
    kv0 = NSA_WIDTH
    g0 = kv0 + 6 * KV_WIDTH
    z0 = g0 + 3 * HEADS
    slab = lambda n: w[:, kv0 + n * KV_WIDTH: kv0 + (n + 1) * KV_WIDTH]
    cols = [q, w[:, z0:z0 + 3 * NSA_WIDTH]]
    for branch in range(2):
        k, v = slab(2 + 2 * branch), slab(3 + 2 * branch)
        for g in range(GROUPS):
            kg = k[:, g * HEAD_DIM:(g + 1) * HEAD_DIM]
            vg = v[:, g * HEAD_DIM:(g + 1) * HEAD_DIM]
            cols += [kg, kg, vg, vg]
    cols += [slab(0), slab(1), w[:, g0:z0]]
    used = NSA_G_COL + 3 * HEADS
    cols.append(jnp.zeros((d, NSA_PROJ_PAD - used), w.dtype))
    return jnp.concatenate(cols, axis=1).astype(BF16)


def _nsa_layer(h2d, b, pre_gain, post_gain, w_in, w_out, k_pos, k_w1, k_w2, v_pos, v_w1, v_w2, tabs):
    bias_cmp, a_slc, a_win, ovl, expand, lane_consts, gate_expand = tabs
    proj = _norm_proj(h2d, pre_gain, _nsa_w_in_layout(w_in)).reshape(b, SEQ, NSA_PROJ_PAD)
    ckv = proj[:, :, NSA_CKV_COL:NSA_CKV_COL + 2 * KV_WIDTH]
    ckv = ckv.reshape(b, N_CMP_PAD, CMP_STRIDE, 2, GROUPS, HEAD_DIM).transpose(0, 3, 4, 1, 2, 5)
    ckv = ckv.reshape(b, 2, GROUPS, N_CMP_PAD, CMP_STRIDE * HEAD_DIM)
    pos = jnp.stack([k_pos, v_pos]).reshape(2, 2, CMP_STRIDE * HEAD_DIM)
    w1 = jnp.stack([k_w1, v_w1]).astype(BF16)
    w2 = jnp.stack([k_w2, v_w2])
    w2d = jnp.concatenate([w2, w2], axis=2).astype(BF16)
    ckv_c = _compress(ckv, pos, w1, w2d)
    y_cmp, sel = _cmp_select(proj, ckv_c, bias_cmp, ovl, lane_consts, gate_expand[0])
    y_slc = _flash(proj, a_slc, lane_consts, gate_expand[1], sel, expand)
    y_win = _flash(proj, a_win, lane_consts, gate_expand[2])
    parts = [y.reshape(b * SEQ, NSA_WIDTH) for y in (y_cmp, y_slc, y_win)]
    return _out_post(parts, w_out.astype(BF16), h2d, post_gain)


def _ret_layer(h2d, b, pre_gain, post_gain, w_in, w_out, gn_gain, tabs):
    proj = _norm_proj(h2d, pre_gain, w_in.astype(BF16)).reshape(b, SEQ, -1)
    y = _retention(proj, gn_gain, tabs)
    return _out_post([y.reshape(b * SEQ, RET_WIDTH)], w_out.astype(BF16), h2d, post_gain)


def kernel(x, pre_norm_gain, post_norm_gain, rel_bias_table, nsa_w_in, nsa_w_out, nsa_cmp_k_pos, nsa_cmp_k_w1, nsa_cmp_k_w2, nsa_cmp_v_pos, nsa_cmp_v_w1, nsa_cmp_v_w2, ret_w_in, ret_w_out, ret_gn_gain):
    b, s, d = x.shape
    assert s == SEQ and d == D_MODEL
    nsa_tabs = _nsa_tables(rel_bias_table)
    ret_tabs = _retention_tables()
    h = x.reshape(b * s, d)
    for layer in range(DEPTH):
        slot = layer // 2
        if layer % 2 == 0:
            h = _nsa_layer(h, b, pre_norm_gain[layer], post_norm_gain[layer], nsa_w_in[slot],
                           nsa_w_out[slot], nsa_cmp_k_pos[slot], nsa_cmp_k_w1[slot],
                           nsa_cmp_k_w2[slot], nsa_cmp_v_pos[slot], nsa_cmp_v_w1[slot],
                           nsa_cmp_v_w2[slot], nsa_tabs)
        else:
            h = _ret_layer(h, b, pre_norm_gain[layer], post_norm_gain[layer], ret_w_in[slot],
                           ret_w_out[slot], ret_gn_gain[slot], ret_tabs)
    return h.reshape(b, s, d)
```

```python
import functools
import math

import numpy as np
import jax
import jax.numpy as jnp
from jax import lax
from jax.experimental import pallas as pl
from jax.experimental.pallas import tpu as pltpu

F32 = jnp.float32
BF16 = jnp.bfloat16

D_MODEL = 1024
SEQ = 2048
DEPTH = 4
RMS_EPS = 1e-6
GN_EPS = 1e-6
MASKED = -1e30

HEADS = 16
HEAD_DIM = 64
GROUPS = 4
HEADS_PER_GROUP = HEADS // GROUPS
NSA_WIDTH = HEADS * HEAD_DIM
KV_WIDTH = GROUPS * HEAD_DIM
CMP_BLOCK = 32
CMP_STRIDE = 16
CMP_HIDDEN = 256
N_CMP = (SEQ - CMP_BLOCK) // CMP_STRIDE + 1
N_CMP_PAD = 128
SLC_BLOCK = 64
N_SLC = SEQ // SLC_BLOCK
SLC_TOPN = 16
WIN_SIZE = 512
FORCED_SCORE = 1e3
REL_BUCKETS = 32
REL_MAX_DIST = 128

RET_HEADS = 4
RET_QK_DIM = 256
RET_V_DIM = 512
RET_QK_WIDTH = RET_HEADS * RET_QK_DIM
RET_WIDTH = RET_HEADS * RET_V_DIM
ROPE_BASE = 10000.0

LANES = 128
VMEM_LIMIT_BYTES = 56 * 1024 * 1024

PROJ_TM = 1024
RET_PROJ_TN = 1024
NSA_PROJ_TN = 2304
POST_TM = 512
ATT_TQ = 256
ATT_TK = 256
RET_CHUNK = 256

NSA_Q_COL = 0
NSA_Z_COL = 1024
NSA_K_COL = 4096
NSA_V_COL = 5120
NSA_CKV_COL = 6144
NSA_G_COL = 6656
NSA_PROJ_PAD = 6912

_NT = (((1,), (1,)), ((), ()))
_TN = (((0,), (0,)), ((), ()))


def _cparams(sem, flags=None):
    return pltpu.CompilerParams(dimension_semantics=sem, vmem_limit_bytes=VMEM_LIMIT_BYTES,
                                flags=flags)


def _norm_proj_kernel(x_ref, g_ref, w_ref, o_ref, xn_ref):
    @pl.when(pl.program_id(1) == 0)
    def _():
        x = x_ref[...]
        ms = jnp.mean(x * x, axis=-1, keepdims=True)
        xn_ref[...] = (x * lax.rsqrt(ms + RMS_EPS) * g_ref[...]).astype(BF16)

    o_ref[...] = jnp.dot(xn_ref[...], w_ref[...], preferred_element_type=F32).astype(o_ref.dtype)


def _norm_proj(x2d, gain, w_bf16, tn):
    m, d = x2d.shape
    n = w_bf16.shape[1]
    assert n % tn == 0
    return pl.pallas_call(
        _norm_proj_kernel,
        grid=(m // PROJ_TM, n // tn),
        in_specs=[
            pl.BlockSpec((PROJ_TM, d), lambda i, j: (i, 0)),
            pl.BlockSpec((1, d), lambda i, j: (0, 0)),
            pl.BlockSpec((d, tn), lambda i, j: (0, j)),
        ],
        out_specs=pl.BlockSpec((PROJ_TM, tn), lambda i, j: (i, j)),
        out_shape=jax.ShapeDtypeStruct((m, n), BF16),
        scratch_shapes=[pltpu.VMEM((PROJ_TM, d), BF16)],
        compiler_params=_cparams(("parallel", "arbitrary")),
        name="norm_proj",
    )(x2d, gain.reshape(1, d), w_bf16)


def _out_post_kernel(*refs, n_parts):
    y_refs = refs[:n_parts]
    w_ref, h_ref, g_ref, o_ref = refs[n_parts:]
    y = y_refs[0][...].astype(F32)
    for r in y_refs[1:]:
        y = y + r[...].astype(F32)
    t = jnp.dot(y.astype(BF16), w_ref[...], preferred_element_type=F32)
    ms = jnp.mean(t * t, axis=-1, keepdims=True)
    o_ref[...] = h_ref[...] + t * lax.rsqrt(ms + RMS_EPS) * g_ref[...]


def _out_post(parts, w_bf16, h2d, gain):
    m, d = h2d.shape
    k = w_bf16.shape[0]
    n_parts = len(parts)
    return pl.pallas_call(
        functools.partial(_out_post_kernel, n_parts=n_parts),
        grid=(m // POST_TM,),
        in_specs=[pl.BlockSpec((POST_TM, k), lambda i: (i, 0)) for _ in parts] + [
            pl.BlockSpec((k, d), lambda i: (0, 0)),
            pl.BlockSpec((POST_TM, d), lambda i: (i, 0)),
            pl.BlockSpec((1, d), lambda i: (0, 0)),
        ],
        out_specs=pl.BlockSpec((POST_TM, d), lambda i: (i, 0)),
        out_shape=jax.ShapeDtypeStruct((m, d), F32),
        compiler_params=_cparams(("parallel",)),
        name="out_post",
    )(*parts, w_bf16, h2d, gain.reshape(1, d))


def _compress_kernel(x_ref, pos_ref, w1_ref, w2_ref, o_ref):
    half = CMP_STRIDE * HEAD_DIM
    for which in range(2):
        x = x_ref[0, which, 0].astype(F32)
        xa = (x + pos_ref[which, 0:1, :]).astype(BF16)
        xb = (x + pos_ref[which, 1:2, :]).astype(BF16)
        pa = jnp.dot(xa, w1_ref[which, :half, :], preferred_element_type=F32)
        pb = jnp.dot(xb, w1_ref[which, half:, :], preferred_element_type=F32)
        hid = pa + pltpu.roll(pb, N_CMP_PAD - 1, 0)
        hid = hid * jax.nn.sigmoid(hid)
        o_ref[0, which, 0] = jnp.dot(hid.astype(BF16), w2_ref[which],
                                     preferred_element_type=F32).astype(o_ref.dtype)


def _compress(ckv_rows, pos, w1, w2d):
    b = ckv_rows.shape[0]
    row_w = CMP_STRIDE * HEAD_DIM
    return pl.pallas_call(
        _compress_kernel,
        grid=(b, GROUPS),
        in_specs=[
            pl.BlockSpec((1, 2, 1, N_CMP_PAD, row_w), lambda i, g: (i, 0, g, 0, 0)),
            pl.BlockSpec((2, 2, row_w), lambda i, g: (0, 0, 0)),
            pl.BlockSpec((2, 2 * row_w, CMP_HIDDEN), lambda i, g: (0, 0, 0)),
            pl.BlockSpec((2, CMP_HIDDEN, 2 * HEAD_DIM), lambda i, g: (0, 0, 0)),
        ],
        out_specs=pl.BlockSpec((1, 2, 1, N_CMP_PAD, 2 * HEAD_DIM), lambda i, g: (i, 0, g, 0, 0)),
        out_shape=jax.ShapeDtypeStruct((b, 2, GROUPS, N_CMP_PAD, 2 * HEAD_DIM), BF16),
        compiler_params=_cparams(("parallel", "parallel")),
        name="compress",
    )(ckv_rows, pos, w1, w2d)


def _lane_consts(rows):
    lane = np.arange(LANES)
    lo = (lane < HEAD_DIM).astype(np.float32)
    hi = (lane >= HEAD_DIM).astype(np.float32)
    last = (lane == LANES - 1).astype(np.float32)
    first = (lane == 0).astype(np.float32)
    c = np.stack([lo, hi, 1.0 - last, last, 1.0 - first, first])
    return jnp.asarray(np.broadcast_to(c[:, None, :], (6, rows, LANES)), BF16)


def _bf16_terms(x, n):
    terms = []
    for _ in range(n - 1):
        t = x.astype(BF16)
        terms.append(t)
        x = x - t.astype(F32)
    terms.append(x.astype(BF16))
    return terms


def _gates(gate_logits, expand):
    sig = jax.nn.sigmoid(gate_logits.astype(F32))
    return sum(jnp.dot(t, expand, preferred_element_type=F32) for t in _bf16_terms(sig, 2))


def _silu(x):
    return x * jax.nn.sigmoid(x)


def _cmp_select_kernel(q_ref, gate_ref, z_ref, ckv_ref, bias_ref, ovl_ref, lc_ref, gexp_ref,
                       y_ref, sel_ref):
    g = pl.program_id(0)
    qi = pl.program_id(1)
    tq = q_ref.shape[1]
    q = q_ref[0]
    kk = ckv_ref[0, 0, 0]
    vv = ckv_ref[0, 1, 0]
    lane = lax.broadcasted_iota(jnp.int32, (tq, LANES), 1)

    psum = jnp.zeros((N_CMP_PAD, tq), F32)
    pairs = []
    for a in range(2):
        qp = q[:, a * LANES:(a + 1) * LANES]
        outs = []
        for e in range(2):
            s = lax.dot_general(kk, qp * lc_ref[e], _NT, preferred_element_type=F32)
            bias = bias_ref[2 * a + e]
            s = s + bias
            m = jnp.max(s, axis=0, keepdims=True)
            p = jnp.exp(s - m)
            p = jnp.where(bias > 0.5 * MASKED, p / jnp.sum(p, axis=0, keepdims=True), 0.0)
            psum = psum + p
            outs.append(lax.dot_general(p.astype(BF16), vv, _TN, preferred_element_type=F32))
        pairs.append(jnp.where(lane < HEAD_DIM, outs[0], outs[1]))
    y_ref[0] = (jnp.concatenate(pairs, axis=1) * _gates(gate_ref[0], gexp_ref[g])
                * _silu(z_ref[0].astype(F32)))

    ovl = ovl_ref[...]
    imp = sum(jnp.dot(ovl, t, preferred_element_type=F32) for t in _bf16_terms(psum, 3))
    t = qi * tq + lax.broadcasted_iota(jnp.int32, (N_SLC, tq), 1)
    blk = lax.broadcasted_iota(jnp.int32, (N_SLC, tq), 0)
    cur = lax.shift_right_logical(t, int(math.log2(SLC_BLOCK)))
    forced = (blk == 0) | (blk == cur) | (blk == cur - 1)
    score = jnp.where(blk * SLC_BLOCK <= t, imp + jnp.where(forced, FORCED_SCORE, 0.0), MASKED)
    cnt = jnp.zeros((N_SLC, tq), jnp.int32)
    for jp in range(N_SLC):
        row = score[jp:jp + 1, :]
        beats = (row > score) | ((row == score) & (blk > jp))
        cnt = cnt + beats.astype(jnp.int32)
    sel_ref[0, 0] = jnp.where(cnt < SLC_TOPN, 1.0, 0.0).astype(sel_ref.dtype)


def _cmp_select(proj, ckv, bias_cmp, ovl, lane_consts, gate_expand):
    b = proj.shape[0]
    tq = ATT_TQ
    grp_w = HEADS_PER_GROUP * HEAD_DIM
    return pl.pallas_call(
        _cmp_select_kernel,
        grid=(GROUPS, SEQ // tq, b),
        in_specs=[
            pl.BlockSpec((1, tq, grp_w), lambda g, qi, i: (i, qi, NSA_Q_COL // grp_w + g)),
            pl.BlockSpec((1, tq, LANES), lambda g, qi, i: (i, qi, NSA_G_COL // LANES)),
            pl.BlockSpec((1, tq, grp_w), lambda g, qi, i: (i, qi, NSA_Z_COL // grp_w + g)),
            pl.BlockSpec((1, 2, 1, N_CMP_PAD, 2 * HEAD_DIM), lambda g, qi, i: (i, 0, g, 0, 0)),
            pl.BlockSpec((HEADS_PER_GROUP, N_CMP_PAD, tq), lambda g, qi, i: (g, 0, qi)),
            pl.BlockSpec((N_SLC, N_CMP_PAD), lambda g, qi, i: (0, 0)),
            pl.BlockSpec(memory_space=pltpu.VMEM),
            pl.BlockSpec(memory_space=pltpu.VMEM),
        ],
        out_specs=[
            pl.BlockSpec((1, tq, grp_w), lambda g, qi, i: (i, qi, g)),
            pl.BlockSpec((1, 1, N_SLC, tq), lambda g, qi, i: (i, g, 0, qi)),
        ],
        out_shape=[
            jax.ShapeDtypeStruct((b, SEQ, NSA_WIDTH), F32),
            jax.ShapeDtypeStruct((b, GROUPS, N_SLC, SEQ), BF16),
        ],
        compiler_params=_cparams(("parallel", "parallel", "parallel")),
        name="cmp_select",
    )(proj, proj, proj, ckv, bias_cmp, ovl, lane_consts, gate_expand)


def _flash_kernel(*refs, selected):
    if selected:
        (q_ref, k_ref, v_ref, gate_ref, z_ref, a_ref, lc_ref, gexp_ref, sel_ref, kblk_ref,
         place_ref, placed_ref, y_ref, qm_ref, m_ref, acc_ref, s_bufs, p_bufs, c_bufs, vx_ref,
         kx_ref) = refs
    else:
        (q_ref, k_ref, v_ref, gate_ref, z_ref, a_ref, lc_ref, gexp_ref,
         y_ref, qm_ref, m_ref, acc_ref, s_bufs, p_bufs, c_bufs, vx_ref) = refs
    g = pl.program_id(1)
    qi = pl.program_id(2)
    tq = q_ref.shape[1]
    tk = ATT_TK
    n_tiles = qi + 1 if selected else jnp.minimum(qi, WIN_SIZE // tk) + 1

    q = q_ref[0]
    if selected:
        flags = lax.dot_general(sel_ref[0, 0], place_ref[...], _TN, preferred_element_type=F32)
        fill = ((placed_ref[...] - flags) * MASKED).astype(BF16)
    for r in range(HEADS_PER_GROUP):
        a, e = divmod(r, 2)
        qm = q[:, a * LANES:(a + 1) * LANES] * lc_ref[e]
        if selected:
            qm = qm + fill[:, e * LANES:(e + 1) * LANES]
        qm_ref[r * tq:(r + 1) * tq, :] = qm

    @pl.when(qi == 0)
    def _():
        for c in range(SEQ // tk):
            rows = slice(c * tk, (c + 1) * tk)
            vv = v_ref[0, rows, :]
            vx_ref[0, rows, :] = vv * lc_ref[2] + lc_ref[3]
            vx_ref[1, rows, :] = vv * lc_ref[4] + lc_ref[5]
            if selected:
                kk = k_ref[0, rows, :]
                for e in range(2):
                    kx_ref[e, rows, :] = kk * lc_ref[e] + kblk_ref[e, c]

    def tile_rows(t):
        return pl.ds(pl.multiple_of(jnp.clip(qi - t, 0, qi) * tk, tk), tk)

    def scores(t, s_ref):
        rows = tile_rows(t)
        if selected:
            for e in range(2):
                ke = kx_ref[e, rows, :]
                for a in range(2):
                    r = 2 * a + e
                    s_ref[r] = lax.dot_general(qm_ref[r * tq:(r + 1) * tq, :], ke, _NT,
                                               preferred_element_type=F32)
        else:
            s = lax.dot_general(qm_ref[...], k_ref[0, rows, :], _NT, preferred_element_type=F32)
            for r in range(HEADS_PER_GROUP):
                s_ref[r] = s[r * tq:(r + 1) * tq]

    def softmax(t, s_ref, p_ref, c_ref):
        kind = jnp.minimum(t, 2)
        for r in range(HEADS_PER_GROUP):
            s = s_ref[r] + a_ref[kind, HEADS_PER_GROUP * g + r]
            m_prev = m_ref[r]
            m_new = jnp.maximum(m_prev, jnp.max(s, axis=1, keepdims=True))
            c_ref[r] = jnp.exp(m_prev - m_new)
            p = jnp.exp(s - jnp.concatenate([m_new] * (tk // LANES), axis=1))
            p_ref[r] = p.astype(BF16)
            m_ref[r] = m_new

    def values(t, p_ref, c_ref):
        rows = tile_rows(t)
        for r in range(HEADS_PER_GROUP):
            pv = jnp.dot(p_ref[r], vx_ref[r % 2, rows, :], preferred_element_type=F32)
            acc_ref[r] = c_ref[r] * acc_ref[r] + pv

    m_ref[...] = jnp.full(m_ref.shape, -jnp.inf, F32)
    acc_ref[...] = jnp.zeros(acc_ref.shape, F32)
    bufs = [(s_bufs.at[i], p_bufs.at[i], c_bufs.at[i]) for i in range(2)]

    scores(0, bufs[0][0])
    softmax(0, *bufs[0])
    scores(1, bufs[1][0])

    def step(j, carry):
        for parity in range(2):
            cur, nxt = bufs[parity], bufs[1 - parity]

            @pl.when(lax.bitwise_and(j, 1) == parity)
            def _():
                softmax(j, *cur)
                scores(j + 1, nxt[0])
                values(j - 1, nxt[1], nxt[2])
        return carry
    lax.fori_loop(1, n_tiles, step, 0)

    for parity in range(2):
        @pl.when(lax.bitwise_and(n_tiles - 1, 1) == parity)
        def _():
            values(n_tiles - 1, bufs[parity][1], bufs[parity][2])

    lane = lax.broadcasted_iota(jnp.int32, (tq, LANES), 1)
    pairs = []
    for a in range(2):
        acc_e = acc_ref[2 * a]
        acc_o = acc_ref[2 * a + 1]
        pairs.append(jnp.where(lane < HEAD_DIM, acc_e / acc_e[:, LANES - 1:LANES],
                               acc_o / acc_o[:, 0:1]))
    y_ref[0] = (jnp.concatenate(pairs, axis=1) * _gates(gate_ref[0], gexp_ref[g])
                * _silu(z_ref[0].astype(F32)))


def _flash(proj, a_tiles, lane_consts, gate_expand, sel=None, sel_consts=None):
    selected = sel is not None
    b = proj.shape[0]
    tq, tk = ATT_TQ, ATT_TK
    branch = 1 if selected else 2
    grp_w = HEADS_PER_GROUP * HEAD_DIM
    k_blk = NSA_K_COL // LANES + (branch - 1) * GROUPS
    v_blk = NSA_V_COL // LANES + (branch - 1) * GROUPS
    z_blk = NSA_Z_COL // grp_w + branch * GROUPS
    in_specs = [
        pl.BlockSpec((1, tq, grp_w), lambda i, g, qi: (i, qi, NSA_Q_COL // grp_w + g)),
        pl.BlockSpec((1, SEQ, LANES), lambda i, g, qi: (i, 0, k_blk + g)),
        pl.BlockSpec((1, SEQ, LANES), lambda i, g, qi: (i, 0, v_blk + g)),
        pl.BlockSpec((1, tq, LANES), lambda i, g, qi: (i, qi, NSA_G_COL // LANES)),
        pl.BlockSpec((1, tq, grp_w), lambda i, g, qi: (i, qi, z_blk + g)),
        pl.BlockSpec(memory_space=pltpu.VMEM),
        pl.BlockSpec(memory_space=pltpu.VMEM),
        pl.BlockSpec(memory_space=pltpu.VMEM),
    ]
    args = [proj, proj, proj, proj, proj, a_tiles, lane_consts, gate_expand]
    if selected:
        in_specs += [pl.BlockSpec((1, 1, N_SLC, tq), lambda i, g, qi: (i, g, 0, qi))]
        in_specs += [pl.BlockSpec(memory_space=pltpu.VMEM)] * len(sel_consts)
        args += [sel, *sel_consts]
    return pl.pallas_call(
        functools.partial(_flash_kernel, selected=selected),
        grid=(b, GROUPS, SEQ // tq),
        in_specs=in_specs,
        out_specs=pl.BlockSpec((1, tq, grp_w), lambda i, g, qi: (i, qi, g)),
        out_shape=jax.ShapeDtypeStruct((b, SEQ, NSA_WIDTH), F32),
        scratch_shapes=[
            pltpu.VMEM((HEADS_PER_GROUP * tq, LANES), BF16),
            pltpu.VMEM((HEADS_PER_GROUP, tq, LANES), F32),
            pltpu.VMEM((HEADS_PER_GROUP, tq, LANES), F32),
            pltpu.VMEM((2, HEADS_PER_GROUP, tq, tk), F32),
            pltpu.VMEM((2, HEADS_PER_GROUP, tq, tk), BF16),
            pltpu.VMEM((2, HEADS_PER_GROUP, tq, LANES), F32),
            pltpu.VMEM((2, SEQ, LANES), BF16),
        ] + ([pltpu.VMEM((2, SEQ, LANES), BF16)] if selected else []),
        compiler_params=_cparams(("parallel", "parallel", "arbitrary")),
        name="flash_selected" if selected else "flash_window",
    )(*args)


def _retention_kernel(q_ref, k_ref, v_ref, z_ref, cos_ref, sin_ref, inner_ref, xi_ref, zeta_ref,
                      gn_ref, y_ref, state_ref, *, decays):
    @pl.when(pl.program_id(1) == 0)
    def _():
        state_ref[...] = jnp.zeros(state_ref.shape, F32)

    cos = cos_ref[...]
    sin = sin_ref[...]
    half = RET_QK_DIM // 2

    def rot(x):
        x1, x2 = x[:, :half], x[:, half:]
        return jnp.concatenate([x1 * cos - x2 * sin, x1 * sin + x2 * cos], axis=1)

    for h in range(RET_HEADS):
        qs = slice(h * RET_QK_DIM, (h + 1) * RET_QK_DIM)
        vs = slice(h * RET_V_DIM, (h + 1) * RET_V_DIM)
        qr = rot(q_ref[0, :, qs].astype(F32))
        kr = rot(k_ref[0, :, qs].astype(F32)) * (RET_QK_DIM ** -0.5)
        qb = qr.astype(BF16)
        vh = v_ref[0, :, vs]
        attn = lax.dot_general(qb, kr.astype(BF16), _NT, preferred_element_type=F32) * inner_ref[h]
        st = state_ref[h]
        o = (jnp.dot(attn.astype(BF16), vh, preferred_element_type=F32)
             + jnp.dot(qb, st.astype(BF16), preferred_element_type=F32) * xi_ref[h])
        kz = (kr * zeta_ref[h]).astype(BF16)
        state_ref[h] = st * decays[h] + lax.dot_general(kz, vh, _TN, preferred_element_type=F32)
        mu = jnp.mean(o, axis=1, keepdims=True)
        d = o - mu
        var = jnp.mean(d * d, axis=1, keepdims=True)
        on = d * lax.rsqrt(var + GN_EPS) * gn_ref[h]
        y_ref[0, :, vs] = (on * _silu(z_ref[0, :, vs].astype(F32))).astype(y_ref.dtype)


def _retention(proj, gn_gain, tables):
    b = proj.shape[0]
    c = RET_CHUNK
    cos, sin, inner, xi, zeta, decays = tables
    v_blk = 2 * RET_QK_WIDTH // RET_WIDTH
    return pl.pallas_call(
        functools.partial(_retention_kernel, decays=decays),
        grid=(b, SEQ // c),
        in_specs=[
            pl.BlockSpec((1, c, RET_QK_WIDTH), lambda i, j: (i, j, 0)),
            pl.BlockSpec((1, c, RET_QK_WIDTH), lambda i, j: (i, j, 1)),
            pl.BlockSpec((1, c, RET_WIDTH), lambda i, j: (i, j, v_blk)),
            pl.BlockSpec((1, c, RET_WIDTH), lambda i, j: (i, j, v_blk + 1)),
            pl.BlockSpec((c, RET_QK_DIM // 2), lambda i, j: (j, 0)),
            pl.BlockSpec((c, RET_QK_DIM // 2), lambda i, j: (j, 0)),
            pl.BlockSpec((RET_HEADS, c, c), lambda i, j: (0, 0, 0)),
            pl.BlockSpec((RET_HEADS, c, 1), lambda i, j: (0, 0, 0)),
            pl.BlockSpec((RET_HEADS, c, 1), lambda i, j: (0, 0, 0)),
            pl.BlockSpec((RET_HEADS, 1, RET_V_DIM), lambda i, j: (0, 0, 0)),
        ],
        out_specs=pl.BlockSpec((1, c, RET_WIDTH), lambda i, j: (i, j, 0)),
        out_shape=jax.ShapeDtypeStruct((b, SEQ, RET_WIDTH), BF16),
        scratch_shapes=[pltpu.VMEM((RET_HEADS, RET_QK_DIM, RET_V_DIM), F32)],
        compiler_params=_cparams(("parallel", "arbitrary")),
        name="retention",
    )(proj, proj, proj, proj, cos, sin, inner, xi, zeta, gn_gain.reshape(RET_HEADS, 1, RET_V_DIM))


def _t5_bucket_np(dist):
    n = np.maximum(dist, 0)
    max_exact = REL_BUCKETS // 2
    nf = np.maximum(n, 1).astype(np.float64)
    large = max_exact + (np.log(nf / max_exact) / math.log(REL_MAX_DIST / max_exact)
                         * (REL_BUCKETS - max_exact)).astype(np.int64)
    large = np.minimum(large, REL_BUCKETS - 1)
    return np.where(n < max_exact, n, large).astype(np.int32)


def _skewed(vec, rows, stride, cols):
    p = vec.shape[-1]
    lead = vec.shape[:-1]
    flat = jnp.tile(vec, (1,) * len(lead) + (rows,))[..., :rows * (p - stride)]
    return flat.reshape(lead + (rows, p - stride))[..., :cols]


def _nsa_tables(table):
    tq, tk = ATT_TQ, ATT_TK
    assert tq == tk
    tab_t = table.T
    period = 2 * SEQ
    x = np.arange(period)
    ok = (x >= CMP_BLOCK - 1) & (x < SEQ)
    vec = jnp.take(tab_t, jnp.asarray(_t5_bucket_np(x - (CMP_BLOCK - 1))), axis=1)
    vec = jnp.where(jnp.asarray(ok)[None, :], vec, MASKED)
    bias_cmp = _skewed(vec, N_CMP_PAD, CMP_STRIDE, SEQ)
    rel_t = tab_t - tab_t[:, REL_BUCKETS - 1:]
    period = 2 * tq
    x = np.arange(period)
    x = np.where(x < tq, x, x - period)
    tiles = []
    for off in range(3):
        d = off * tk - x
        ok = (d >= 0) & (d < WIN_SIZE) if off == 2 else (d >= 0)
        vec = jnp.take(rel_t, jnp.asarray(_t5_bucket_np(d)), axis=1)
        vec = jnp.where(jnp.asarray(ok)[None, :], vec, MASKED)
        tiles.append(_skewed(vec, tq, 1, tk))
    a_win = jnp.stack(tiles)
    a_slc = jnp.stack([tiles[0], tiles[1], jnp.zeros_like(tiles[0])])
    cs = np.arange(N_CMP_PAD)[None, :] * CMP_STRIDE
    jj = np.arange(N_SLC)[:, None]
    ovl = ((cs < (jj + 1) * SLC_BLOCK) & (cs + CMP_BLOCK > jj * SLC_BLOCK)
           & (np.arange(N_CMP_PAD)[None, :] < N_CMP))
    ovl = jnp.asarray(ovl.astype(np.float32), BF16)
    lane = np.arange(LANES).reshape(1, 1, 1, LANES)
    key = np.arange(SEQ).reshape(1, SEQ // tk, tk, 1)
    base = np.array([HEAD_DIM, 0]).reshape(2, 1, 1, 1)
    key_blocks = jnp.asarray((lane == base + key // SLC_BLOCK).astype(np.float32), BF16)
    col = np.arange(2 * LANES).reshape(1, -1)
    jb = np.arange(N_SLC).reshape(-1, 1)
    place = (col == HEAD_DIM + jb) | (col == LANES + jb)
    sel_consts = (key_blocks, jnp.asarray(place.astype(np.float32), BF16),
                  jnp.asarray(place.any(axis=0, keepdims=True).astype(np.float32)))
    c = np.arange(LANES).reshape(1, 1, LANES, 1)
    col = np.arange(HEADS_PER_GROUP * HEAD_DIM).reshape(1, 1, 1, -1) // HEAD_DIM
    br = np.arange(3).reshape(3, 1, 1, 1)
    gg = np.arange(GROUPS).reshape(1, GROUPS, 1, 1)
    gate_expand = jnp.asarray((c == br * HEADS + HEADS_PER_GROUP * gg + col).astype(np.float32),
                              BF16)
    return bias_cmp, a_slc, a_win, ovl, sel_consts, _lane_consts(tq), gate_expand


def _retention_tables():
    c = RET_CHUNK
    log_g = jnp.log(1.0 - 2.0 ** (-5.0 - jnp.arange(RET_HEADS, dtype=F32)))
    i = jnp.arange(c, dtype=F32)
    diff = i[:, None] - i[None, :]
    inner = jnp.where(diff >= 0, jnp.exp(diff[None] * log_g[:, None, None]), 0.0)
    xi = jnp.exp((i + 1.0)[None, :] * log_g[:, None])[:, :, None]
    zeta = jnp.exp((c - 1.0 - i)[None, :] * log_g[:, None])[:, :, None]
    decays = tuple(float((1.0 - 2.0 ** (-5.0 - h)) ** c) for h in range(RET_HEADS))
    inv = 1.0 / (ROPE_BASE ** jnp.linspace(0.0, 1.0, RET_QK_DIM // 2, dtype=F32))
    ang = jnp.arange(SEQ, dtype=F32)[:, None] * inv[None, :]
    return jnp.cos(ang), jnp.sin(ang), inner, xi, zeta, decays


def _nsa_w_in_layout(w):
    d = w.shape[0]
    q = w[:, :NSA_WIDTH] * (HEAD_DIM ** -0.5)
    kv0 = NSA_WIDTH
    g0 = kv0 + 6 * KV_WIDTH
    z0 = g0 + 3 * HEADS
    slab = lambda n: w[:, kv0 + n * KV_WIDTH: kv0 + (n + 1) * KV_WIDTH]
    cols = [q, w[:, z0:z0 + 3 * NSA_WIDTH]]
    for n in (2, 4, 3, 5):
        for g in range(GROUPS):
            part = slab(n)[:, g * HEAD_DIM:(g + 1) * HEAD_DIM]
            cols += [part, part]
    cols += [slab(0), slab(1), w[:, g0:z0]]
    used = NSA_G_COL + 3 * HEADS
    cols.append(jnp.zeros((d, NSA_PROJ_PAD - used), w.dtype))
    return jnp.concatenate(cols, axis=1).astype(BF16)


def _nsa_layer(h2d, b, pre_gain, post_gain, w_in, w_out, k_pos, k_w1, k_w2, v_pos, v_w1, v_w2, tabs):
    bias_cmp, a_slc, a_win, ovl, sel_consts, lane_consts, gate_expand = tabs
    proj = _norm_proj(h2d, pre_gain, _nsa_w_in_layout(w_in), NSA_PROJ_TN)
    proj = proj.reshape(b, SEQ, NSA_PROJ_PAD)
    ckv = proj[:, :, NSA_CKV_COL:NSA_CKV_COL + 2 * KV_WIDTH]
    ckv = ckv.reshape(b, N_CMP_PAD, CMP_STRIDE, 2, GROUPS, HEAD_DIM).transpose(0, 3, 4, 1, 2, 5)
    ckv = ckv.reshape(b, 2, GROUPS, N_CMP_PAD, CMP_STRIDE * HEAD_DIM)
    pos = jnp.stack([k_pos, v_pos]).reshape(2, 2, CMP_STRIDE * HEAD_DIM)
    w1 = jnp.stack([k_w1, v_w1]).astype(BF16)
    w2 = jnp.stack([k_w2, v_w2])
    w2d = jnp.concatenate([w2, w2], axis=2).astype(BF16)
    ckv_c = _compress(ckv, pos, w1, w2d)
    y_cmp, sel = _cmp_select(proj, ckv_c, bias_cmp, ovl, lane_consts, gate_expand[0])
    y_slc = _flash(proj, a_slc, lane_consts, gate_expand[1], sel, sel_consts)
    y_win = _flash(proj, a_win, lane_consts, gate_expand[2])
    parts = [y.reshape(b * SEQ, NSA_WIDTH) for y in (y_cmp, y_slc, y_win)]
    return _out_post(parts, w_out.astype(BF16), h2d, post_gain)


def _ret_layer(h2d, b, pre_gain, post_gain, w_in, w_out, gn_gain, tabs):
    proj = _norm_proj(h2d, pre_gain, w_in.astype(BF16), RET_PROJ_TN).reshape(b, SEQ, -1)
    y = _retention(proj, gn_gain, tabs)
    return _out_post([y.reshape(b * SEQ, RET_WIDTH)], w_out.astype(BF16), h2d, post_gain)


def kernel(x, pre_norm_gain, post_norm_gain, rel_bias_table, nsa_w_in, nsa_w_out, nsa_cmp_k_pos, nsa_cmp_k_w1, nsa_cmp_k_w2, nsa_cmp_v_pos, nsa_cmp_v_w1, nsa_cmp_v_w2, ret_w_in, ret_w_out, ret_gn_gain):
    b, s, d = x.shape
    assert s == SEQ and d == D_MODEL
    nsa_tabs = _nsa_tables(rel_bias_table)
    ret_tabs = _retention_tables()
    h = x.reshape(b * s, d)
    for layer in range(DEPTH):
        slot = layer // 2
        if layer % 2 == 0:
            h = _nsa_layer(h, b, pre_norm_gain[layer], post_norm_gain[layer], nsa_w_in[slot],
                           nsa_w_out[slot], nsa_cmp_k_pos[slot], nsa_cmp_k_w1[slot],
                           nsa_cmp_k_w2[slot], nsa_cmp_v_pos[slot], nsa_cmp_v_w1[slot],
                           nsa_cmp_v_w2[slot], nsa_tabs)
        else:
            h = _ret_layer(h, b, pre_norm_gain[layer], post_norm_gain[layer], ret_w_in[slot],
                           ret_w_out[slot], ret_gn_gain[slot], ret_tabs)
    return h.reshape(b, s, d)
```

```python
import functools
import math

import numpy as np
import jax
import jax.numpy as jnp
from jax import lax
from jax.experimental import pallas as pl
from jax.experimental.pallas import tpu as pltpu

F32 = jnp.float32
BF16 = jnp.bfloat16

D_MODEL = 1024
SEQ = 2048
DEPTH = 4
RMS_EPS = 1e-6
GN_EPS = 1e-6
MASKED = -1e30

HEADS = 16
HEAD_DIM = 64
GROUPS = 4
HEADS_PER_GROUP = HEADS // GROUPS
NSA_WIDTH = HEADS * HEAD_DIM
KV_WIDTH = GROUPS * HEAD_DIM
CMP_BLOCK = 32
CMP_STRIDE = 16
CMP_HIDDEN = 256
N_CMP = (SEQ - CMP_BLOCK) // CMP_STRIDE + 1
N_CMP_PAD = 128
SLC_BLOCK = 64
N_SLC = SEQ // SLC_BLOCK
SLC_TOPN = 16
WIN_SIZE = 512
FORCED_SCORE = 1e3
REL_BUCKETS = 32
REL_MAX_DIST = 128

RET_HEADS = 4
RET_QK_DIM = 256
RET_V_DIM = 512
RET_QK_WIDTH = RET_HEADS * RET_QK_DIM
RET_WIDTH = RET_HEADS * RET_V_DIM
ROPE_BASE = 10000.0

LANES = 128
VMEM_LIMIT_BYTES = 56 * 1024 * 1024

PROJ_TM = 1024
RET_PROJ_TN = 1024
NSA_PROJ_TN = 2304
POST_TM = 512
ATT_TQ = 256
ATT_TK = 256
RET_CHUNK = 256

NSA_Q_COL = 0
NSA_Z_COL = 1024
NSA_K_COL = 4096
NSA_V_COL = 5120
NSA_CKV_COL = 6144
NSA_G_COL = 6656
NSA_PROJ_PAD = 6912

_NT = (((1,), (1,)), ((), ()))
_TN = (((0,), (0,)), ((), ()))


def _cparams(sem, flags=None):
    return pltpu.CompilerParams(dimension_semantics=sem, vmem_limit_bytes=VMEM_LIMIT_BYTES,
                                flags=flags)


def _norm_proj_kernel(x_ref, g_ref, w_ref, o_ref, xn_ref):
    @pl.when(pl.program_id(1) == 0)
    def _():
        x = x_ref[...]
        ms = jnp.mean(x * x, axis=-1, keepdims=True)
        xn_ref[...] = (x * lax.rsqrt(ms + RMS_EPS) * g_ref[...]).astype(BF16)

    o_ref[...] = jnp.dot(xn_ref[...], w_ref[...], preferred_element_type=F32).astype(o_ref.dtype)


def _norm_proj(x2d, gain, w_bf16, tn):
    m, d = x2d.shape
    n = w_bf16.shape[1]
    assert n % tn == 0
    return pl.pallas_call(
        _norm_proj_kernel,
        grid=(m // PROJ_TM, n // tn),
        in_specs=[
            pl.BlockSpec((PROJ_TM, d), lambda i, j: (i, 0)),
            pl.BlockSpec((1, d), lambda i, j: (0, 0)),
            pl.BlockSpec((d, tn), lambda i, j: (0, j)),
        ],
        out_specs=pl.BlockSpec((PROJ_TM, tn), lambda i, j: (i, j)),
        out_shape=jax.ShapeDtypeStruct((m, n), BF16),
        scratch_shapes=[pltpu.VMEM((PROJ_TM, d), BF16)],
        compiler_params=_cparams(("parallel", "arbitrary")),
        name="norm_proj",
    )(x2d, gain.reshape(1, d), w_bf16)


def _out_post_kernel(*refs, n_parts):
    y_refs = refs[:n_parts]
    w_ref, h_ref, g_ref, o_ref = refs[n_parts:]
    y = y_refs[0][...].astype(F32)
    for r in y_refs[1:]:
        y = y + r[...].astype(F32)
    t = jnp.dot(y.astype(BF16), w_ref[...], preferred_element_type=F32)
    ms = jnp.mean(t * t, axis=-1, keepdims=True)
    o_ref[...] = h_ref[...] + t * lax.rsqrt(ms + RMS_EPS) * g_ref[...]


def _out_post(parts, w_bf16, h2d, gain):
    m, d = h2d.shape
    k = w_bf16.shape[0]
    n_parts = len(parts)
    return pl.pallas_call(
        functools.partial(_out_post_kernel, n_parts=n_parts),
        grid=(m // POST_TM,),
        in_specs=[pl.BlockSpec((POST_TM, k), lambda i: (i, 0)) for _ in parts] + [
            pl.BlockSpec((k, d), lambda i: (0, 0)),
            pl.BlockSpec((POST_TM, d), lambda i: (i, 0)),
            pl.BlockSpec((1, d), lambda i: (0, 0)),
        ],
        out_specs=pl.BlockSpec((POST_TM, d), lambda i: (i, 0)),
        out_shape=jax.ShapeDtypeStruct((m, d), F32),
        compiler_params=_cparams(("parallel",)),
        name="out_post",
    )(*parts, w_bf16, h2d, gain.reshape(1, d))


def _compress_kernel(x_ref, pos_ref, w1_ref, w2_ref, o_ref):
    half = CMP_STRIDE * HEAD_DIM
    for which in range(2):
        x = x_ref[0, which, 0].astype(F32)
        xa = (x + pos_ref[which, 0:1, :]).astype(BF16)
        xb = (x + pos_ref[which, 1:2, :]).astype(BF16)
        pa = jnp.dot(xa, w1_ref[which, :half, :], preferred_element_type=F32)
        pb = jnp.dot(xb, w1_ref[which, half:, :], preferred_element_type=F32)
        hid = pa + pltpu.roll(pb, N_CMP_PAD - 1, 0)
        hid = hid * jax.nn.sigmoid(hid)
        o_ref[0, which, 0] = jnp.dot(hid.astype(BF16), w2_ref[which],
                                     preferred_element_type=F32).astype(o_ref.dtype)


def _compress(ckv_rows, pos, w1, w2d):
    b = ckv_rows.shape[0]
    row_w = CMP_STRIDE * HEAD_DIM
    return pl.pallas_call(
        _compress_kernel,
        grid=(b, GROUPS),
        in_specs=[
            pl.BlockSpec((1, 2, 1, N_CMP_PAD, row_w), lambda i, g: (i, 0, g, 0, 0)),
            pl.BlockSpec((2, 2, row_w), lambda i, g: (0, 0, 0)),
            pl.BlockSpec((2, 2 * row_w, CMP_HIDDEN), lambda i, g: (0, 0, 0)),
            pl.BlockSpec((2, CMP_HIDDEN, 2 * HEAD_DIM), lambda i, g: (0, 0, 0)),
        ],
        out_specs=pl.BlockSpec((1, 2, 1, N_CMP_PAD, 2 * HEAD_DIM), lambda i, g: (i, 0, g, 0, 0)),
        out_shape=jax.ShapeDtypeStruct((b, 2, GROUPS, N_CMP_PAD, 2 * HEAD_DIM), BF16),
        compiler_params=_cparams(("parallel", "parallel")),
        name="compress",
    )(ckv_rows, pos, w1, w2d)


def _lane_consts(rows):
    lane = np.arange(LANES)
    lo = (lane < HEAD_DIM).astype(np.float32)
    hi = (lane >= HEAD_DIM).astype(np.float32)
    last = (lane == LANES - 1).astype(np.float32)
    first = (lane == 0).astype(np.float32)
    c = np.stack([lo, hi, 1.0 - last, last, 1.0 - first, first])
    return jnp.asarray(np.broadcast_to(c[:, None, :], (6, rows, LANES)), BF16)


def _bf16_terms(x, n):
    terms = []
    for _ in range(n - 1):
        t = x.astype(BF16)
        terms.append(t)
        x = x - t.astype(F32)
    terms.append(x.astype(BF16))
    return terms


def _gates(gate_logits, expand):
    sig = jax.nn.sigmoid(gate_logits.astype(F32))
    return sum(jnp.dot(t, expand, preferred_element_type=F32) for t in _bf16_terms(sig, 2))


def _silu(x):
    return x * jax.nn.sigmoid(x)


def _cmp_select_kernel(q_ref, gate_ref, z_ref, ckv_ref, bias_ref, ovl_ref, lc_ref, gexp_ref,
                       y_ref, sel_ref):
    g = pl.program_id(0)
    qi = pl.program_id(1)
    tq = q_ref.shape[1]
    q = q_ref[0]
    kk = ckv_ref[0, 0, 0]
    vv = ckv_ref[0, 1, 0]
    lane = lax.broadcasted_iota(jnp.int32, (tq, LANES), 1)

    psum = jnp.zeros((N_CMP_PAD, tq), F32)
    pairs = []
    for a in range(2):
        qp = q[:, a * LANES:(a + 1) * LANES]
        outs = []
        for e in range(2):
            s = lax.dot_general(kk, qp * lc_ref[e], _NT, preferred_element_type=F32)
            bias = bias_ref[2 * a + e]
            s = s + bias
            m = jnp.max(s, axis=0, keepdims=True)
            p = jnp.exp(s - m)
            p = jnp.where(bias > 0.5 * MASKED, p / jnp.sum(p, axis=0, keepdims=True), 0.0)
            psum = psum + p
            outs.append(lax.dot_general(p.astype(BF16), vv, _TN, preferred_element_type=F32))
        pairs.append(jnp.where(lane < HEAD_DIM, outs[0], outs[1]))
    y_ref[0] = (jnp.concatenate(pairs, axis=1) * _gates(gate_ref[0], gexp_ref[g])
                * _silu(z_ref[0].astype(F32)))

    ovl = ovl_ref[...]
    imp = sum(jnp.dot(ovl, t, preferred_element_type=F32) for t in _bf16_terms(psum, 3))
    t = qi * tq + lax.broadcasted_iota(jnp.int32, (N_SLC, tq), 1)
    blk = lax.broadcasted_iota(jnp.int32, (N_SLC, tq), 0)
    cur = lax.shift_right_logical(t, int(math.log2(SLC_BLOCK)))
    forced = (blk == 0) | (blk == cur) | (blk == cur - 1)
    score = jnp.where(blk * SLC_BLOCK <= t, imp + jnp.where(forced, FORCED_SCORE, 0.0), MASKED)
    cnt = jnp.zeros((N_SLC, tq), jnp.int32)
    for jp in range(N_SLC):
        row = score[jp:jp + 1, :]
        beats = (row > score) | ((row == score) & (blk > jp))
        cnt = cnt + beats.astype(jnp.int32)
    sel_ref[0, 0] = jnp.where(cnt < SLC_TOPN, 1.0, 0.0).astype(sel_ref.dtype)


def _cmp_select(proj, ckv, bias_cmp, ovl, lane_consts, gate_expand):
    b = proj.shape[0]
    tq = ATT_TQ
    grp_w = HEADS_PER_GROUP * HEAD_DIM
    return pl.pallas_call(
        _cmp_select_kernel,
        grid=(GROUPS, SEQ // tq, b),
        in_specs=[
            pl.BlockSpec((1, tq, grp_w), lambda g, qi, i: (i, qi, NSA_Q_COL // grp_w + g)),
            pl.BlockSpec((1, tq, LANES), lambda g, qi, i: (i, qi, NSA_G_COL // LANES)),
            pl.BlockSpec((1, tq, grp_w), lambda g, qi, i: (i, qi, NSA_Z_COL // grp_w + g)),
            pl.BlockSpec((1, 2, 1, N_CMP_PAD, 2 * HEAD_DIM), lambda g, qi, i: (i, 0, g, 0, 0)),
            pl.BlockSpec((HEADS_PER_GROUP, N_CMP_PAD, tq), lambda g, qi, i: (g, 0, qi)),
            pl.BlockSpec((N_SLC, N_CMP_PAD), lambda g, qi, i: (0, 0)),
            pl.BlockSpec(memory_space=pltpu.VMEM),
            pl.BlockSpec(memory_space=pltpu.VMEM),
        ],
        out_specs=[
            pl.BlockSpec((1, tq, grp_w), lambda g, qi, i: (i, qi, g)),
            pl.BlockSpec((1, 1, N_SLC, tq), lambda g, qi, i: (i, g, 0, qi)),
        ],
        out_shape=[
            jax.ShapeDtypeStruct((b, SEQ, NSA_WIDTH), F32),
            jax.ShapeDtypeStruct((b, GROUPS, N_SLC, SEQ), BF16),
        ],
        compiler_params=_cparams(("parallel", "parallel", "parallel")),
        name="cmp_select",
    )(proj, proj, proj, ckv, bias_cmp, ovl, lane_consts, gate_expand)


def _flash_kernel(*refs, selected):
    if selected:
        (q_ref, k_ref, v_ref, gate_ref, z_ref, a_ref, lc_ref, gexp_ref, sel_ref, kblk_ref,
         place_ref, placed_ref, y_ref, qm_ref, m_ref, acc_ref, s_bufs, p_bufs, c_bufs, vx_ref,
         kx_ref) = refs
    else:
        (q_ref, k_ref, v_ref, gate_ref, z_ref, a_ref, lc_ref, gexp_ref,
         y_ref, qm_ref, m_ref, acc_ref, s_bufs, p_bufs, c_bufs, vx_ref) = refs
    g = pl.program_id(1)
    qi = pl.program_id(2)
    tq = q_ref.shape[1]
    tk = ATT_TK
    n_tiles = qi + 1 if selected else jnp.minimum(qi, WIN_SIZE // tk) + 1

    q = q_ref[0]
    if selected:
        flags = lax.dot_general(sel_ref[0, 0], place_ref[...], _TN, preferred_element_type=F32)
        fill = ((placed_ref[...] - flags) * MASKED).astype(BF16)
    for r in range(HEADS_PER_GROUP):
        a, e = divmod(r, 2)
        qm = q[:, a * LANES:(a + 1) * LANES] * lc_ref[e]
        if selected:
            qm = qm + fill[:, e * LANES:(e + 1) * LANES]
        qm_ref[r * tq:(r + 1) * tq, :] = qm

    @pl.when(qi == 0)
    def _():
        for c in range(SEQ // tk):
            rows = slice(c * tk, (c + 1) * tk)
            vv = v_ref[0, rows, :]
            vx_ref[0, rows, :] = vv * lc_ref[2] + lc_ref[3]
            vx_ref[1, rows, :] = vv * lc_ref[4] + lc_ref[5]
            if selected:
                kk = k_ref[0, rows, :]
                for e in range(2):
                    kx_ref[e, rows, :] = kk * lc_ref[e] + kblk_ref[e, c]

    def tile_rows(t):
        return pl.ds(pl.multiple_of(jnp.clip(qi - t, 0, qi) * tk, tk), tk)

    def scores(t, s_ref):
        rows = tile_rows(t)
        if selected:
            for e in range(2):
                ke = kx_ref[e, rows, :]
                for a in range(2):
                    r = 2 * a + e
                    s_ref[r] = lax.dot_general(qm_ref[r * tq:(r + 1) * tq, :], ke, _NT,
                                               preferred_element_type=F32)
        else:
            s = lax.dot_general(qm_ref[...], k_ref[0, rows, :], _NT, preferred_element_type=F32)
            for r in range(HEADS_PER_GROUP):
                s_ref[r] = s[r * tq:(r + 1) * tq]

    def softmax(t, s_ref, p_ref, c_ref):
        kind = jnp.minimum(t, 2)
        for r in range(HEADS_PER_GROUP):
            s = s_ref[r] + a_ref[kind, HEADS_PER_GROUP * g + r]
            m_prev = m_ref[r]
            m_new = jnp.maximum(m_prev, jnp.max(s, axis=1, keepdims=True))
            c_ref[r] = jnp.exp(m_prev - m_new)
            p = jnp.exp(s - jnp.concatenate([m_new] * (tk // LANES), axis=1))
            p_ref[r] = p.astype(BF16)
            m_ref[r] = m_new

    def values(t, p_ref, c_ref):
        rows = tile_rows(t)
        for r in range(HEADS_PER_GROUP):
            pv = jnp.dot(p_ref[r], vx_ref[r % 2, rows, :], preferred_element_type=F32)
            acc_ref[r] = c_ref[r] * acc_ref[r] + pv

    m_ref[...] = jnp.full(m_ref.shape, -jnp.inf, F32)
    acc_ref[...] = jnp.zeros(acc_ref.shape, F32)
    bufs = [(s_bufs.at[i], p_bufs.at[i], c_bufs.at[i]) for i in range(2)]

    scores(0, bufs[0][0])
    softmax(0, *bufs[0])
    scores(1, bufs[1][0])

    def step(j, carry):
        for parity in range(2):
            cur, nxt = bufs[parity], bufs[1 - parity]

            @pl.when(lax.bitwise_and(j, 1) == parity)
            def _():
                softmax(j, *cur)
                scores(j + 1, nxt[0])
                values(j - 1, nxt[1], nxt[2])
        return carry
    lax.fori_loop(1, n_tiles, step, 0)

    for parity in range(2):
        @pl.when(lax.bitwise_and(n_tiles - 1, 1) == parity)
        def _():
            values(n_tiles - 1, bufs[parity][1], bufs[parity][2])

    lane = lax.broadcasted_iota(jnp.int32, (tq, LANES), 1)
    pairs = []
    for a in range(2):
        acc_e = acc_ref[2 * a]
        acc_o = acc_ref[2 * a + 1]
        pairs.append(jnp.where(lane < HEAD_DIM, acc_e / acc_e[:, LANES - 1:LANES],
                               acc_o / acc_o[:, 0:1]))
    y_ref[0] = (jnp.concatenate(pairs, axis=1) * _gates(gate_ref[0], gexp_ref[g])
                * _silu(z_ref[0].astype(F32)))


def _flash(proj, a_tiles, lane_consts, gate_expand, sel=None, sel_consts=None):
    selected = sel is not None
    b = proj.shape[0]
    tq, tk = ATT_TQ, ATT_TK
    branch = 1 if selected else 2
    grp_w = HEADS_PER_GROUP * HEAD_DIM
    k_blk = NSA_K_COL // LANES + (branch - 1) * GROUPS
    v_blk = NSA_V_COL // LANES + (branch - 1) * GROUPS
    z_blk = NSA_Z_COL // grp_w + branch * GROUPS
    in_specs = [
        pl.BlockSpec((1, tq, grp_w), lambda i, g, qi: (i, qi, NSA_Q_COL // grp_w + g)),
        pl.BlockSpec((1, SEQ, LANES), lambda i, g, qi: (i, 0, k_blk + g)),
        pl.BlockSpec((1, SEQ, LANES), lambda i, g, qi: (i, 0, v_blk + g)),
        pl.BlockSpec((1, tq, LANES), lambda i, g, qi: (i, qi, NSA_G_COL // LANES)),
        pl.BlockSpec((1, tq, grp_w), lambda i, g, qi: (i, qi, z_blk + g)),
        pl.BlockSpec(memory_space=pltpu.VMEM),
        pl.BlockSpec(memory_space=pltpu.VMEM),
        pl.BlockSpec(memory_space=pltpu.VMEM),
    ]
    args = [proj, proj, proj, proj, proj, a_tiles, lane_consts, gate_expand]
    if selected:
        in_specs += [pl.BlockSpec((1, 1, N_SLC, tq), lambda i, g, qi: (i, g, 0, qi))]
        in_specs += [pl.BlockSpec(memory_space=pltpu.VMEM)] * len(sel_consts)
        args += [sel, *sel_consts]
    return pl.pallas_call(
        functools.partial(_flash_kernel, selected=selected),
        grid=(b, GROUPS, SEQ // tq),
        in_specs=in_specs,
        out_specs=pl.BlockSpec((1, tq, grp_w), lambda i, g, qi: (i, qi, g)),
        out_shape=jax.ShapeDtypeStruct((b, SEQ, NSA_WIDTH), F32),
        scratch_shapes=[
            pltpu.VMEM((HEADS_PER_GROUP * tq, LANES), BF16),
            pltpu.VMEM((HEADS_PER_GROUP, tq, LANES), F32),
            pltpu.VMEM((HEADS_PER_GROUP, tq, LANES), F32),
            pltpu.VMEM((2, HEADS_PER_GROUP, tq, tk), F32),
            pltpu.VMEM((2, HEADS_PER_GROUP, tq, tk), BF16),
            pltpu.VMEM((2, HEADS_PER_GROUP, tq, LANES), F32),
            pltpu.VMEM((2, SEQ, LANES), BF16),
        ] + ([pltpu.VMEM((2, SEQ, LANES), BF16)] if selected else []),
        compiler_params=_cparams(("parallel", "parallel", "arbitrary")),
        name="flash_selected" if selected else "flash_window",
    )(*args)


def _token_kernel(q_ref, ks_ref, vs_ref, kw_ref, vw_ref, gate_ref, zs_ref, zw_ref, a_ref, lc_ref,
                  gexp_ref, sel_ref, kblk_ref, place_ref, placed_ref, y_ref,
                  qm_ref, m_ref, acc_ref, s_bufs, p_bufs, c_bufs, kx_ref, vx_ref):
    g = pl.program_id(1)
    qi = pl.program_id(2)
    tq = q_ref.shape[1]
    tk = ATT_TK
    n_slc = qi + 1
    n_tiles = n_slc + jnp.minimum(qi, WIN_SIZE // tk) + 1

    q = q_ref[0]
    flags = lax.dot_general(sel_ref[0, 0], place_ref[...], _TN, preferred_element_type=F32)
    fill = ((placed_ref[...] - flags) * MASKED).astype(BF16)
    for r in range(HEADS_PER_GROUP):
        a, e = divmod(r, 2)
        qm = q[:, a * LANES:(a + 1) * LANES] * lc_ref[e]
        qm_ref[0, r * tq:(r + 1) * tq, :] = qm + fill[:, e * LANES:(e + 1) * LANES]
        qm_ref[1, r * tq:(r + 1) * tq, :] = qm

    @pl.when(qi == 0)
    def _():
        for c in range(SEQ // tk):
            rows = slice(c * tk, (c + 1) * tk)
            for branch, v_ref in enumerate((vs_ref, vw_ref)):
                vv = v_ref[0, rows, :]
                vx_ref[2 * branch, rows, :] = vv * lc_ref[2] + lc_ref[3]
                vx_ref[2 * branch + 1, rows, :] = vv * lc_ref[4] + lc_ref[5]
            kk = ks_ref[0, rows, :]
            for e in range(2):
                kx_ref[e, rows, :] = kk * lc_ref[e] + kblk_ref[e, c]
            kx_ref[2, rows, :] = kw_ref[0, rows, :]

    def tile(j):
        win = (j >= n_slc).astype(jnp.int32)
        t = j - win * n_slc
        rows = pl.ds(pl.multiple_of(jnp.clip(qi - t, 0, qi) * tk, tk), tk)
        return win, rows, jnp.where(win == 1, t, jnp.where(t < 2, t, 3))

    def scores(j, s_ref):
        win, rows, _ = tile(j)
        for e in range(2):
            ke = kx_ref[jnp.where(win == 1, 2, e), rows, :]
            for a in range(2):
                r = 2 * a + e
                s_ref[r] = lax.dot_general(qm_ref[win, r * tq:(r + 1) * tq, :], ke, _NT,
                                           preferred_element_type=F32)

    def softmax(j, s_ref, p_ref, c_ref):
        win, _, kind = tile(j)
        for r in range(HEADS_PER_GROUP):
            s = s_ref[r] + a_ref[kind, HEADS_PER_GROUP * g + r]
            m_prev = m_ref[win, r]
            m_new = jnp.maximum(m_prev, jnp.max(s, axis=1, keepdims=True))
            c_ref[r] = jnp.exp(m_prev - m_new)
            p = jnp.exp(s - jnp.concatenate([m_new] * (tk // LANES), axis=1))
            p_ref[r] = p.astype(BF16)
            m_ref[win, r] = m_new

    def values(j, p_ref, c_ref):
        win, rows, _ = tile(j)
        for r in range(HEADS_PER_GROUP):
            pv = jnp.dot(p_ref[r], vx_ref[2 * win + r % 2, rows, :], preferred_element_type=F32)
            acc_ref[win, r] = c_ref[r] * acc_ref[win, r] + pv

    m_ref[...] = jnp.full(m_ref.shape, -jnp.inf, F32)
    acc_ref[...] = jnp.zeros(acc_ref.shape, F32)
    bufs = [(s_bufs.at[i], p_bufs.at[i], c_bufs.at[i]) for i in range(2)]

    scores(0, bufs[0][0])
    softmax(0, *bufs[0])
    scores(1, bufs[1][0])

    def step(j, carry):
        for parity in range(2):
            cur, nxt = bufs[parity], bufs[1 - parity]

            @pl.when(lax.bitwise_and(j, 1) == parity)
            def _():
                softmax(j, *cur)
                scores(j + 1, nxt[0])
                values(j - 1, nxt[1], nxt[2])
        return carry
    lax.fori_loop(1, n_tiles, step, 0)

    for parity in range(2):
        @pl.when(lax.bitwise_and(n_tiles - 1, 1) == parity)
        def _():
            values(n_tiles - 1, bufs[parity][1], bufs[parity][2])

    lane = lax.broadcasted_iota(jnp.int32, (tq, LANES), 1)
    y = None
    for branch, z_ref in enumerate((zs_ref, zw_ref)):
        pairs = []
        for a in range(2):
            acc_e = acc_ref[branch, 2 * a]
            acc_o = acc_ref[branch, 2 * a + 1]
            pairs.append(jnp.where(lane < HEAD_DIM, acc_e / acc_e[:, LANES - 1:LANES],
                                   acc_o / acc_o[:, 0:1]))
        part = (jnp.concatenate(pairs, axis=1) * _gates(gate_ref[0], gexp_ref[branch, g])
                * _silu(z_ref[0].astype(F32)))
        y = part if y is None else y + part
    y_ref[0] = y


def _token_attention(proj, a_tiles, lane_consts, gate_expand, sel, sel_consts):
    b = proj.shape[0]
    tq, tk = ATT_TQ, ATT_TK
    grp_w = HEADS_PER_GROUP * HEAD_DIM
    k_blk = NSA_K_COL // LANES
    v_blk = NSA_V_COL // LANES
    z_blk = NSA_Z_COL // grp_w
    kv_spec = lambda blk: pl.BlockSpec((1, SEQ, LANES), lambda i, g, qi: (i, 0, blk + g))
    z_spec = lambda blk: pl.BlockSpec((1, tq, grp_w), lambda i, g, qi: (i, qi, blk + g))
    whole = pl.BlockSpec(memory_space=pltpu.VMEM)
    return pl.pallas_call(
        _token_kernel,
        grid=(b, GROUPS, SEQ // tq),
        in_specs=[
            pl.BlockSpec((1, tq, grp_w), lambda i, g, qi: (i, qi, NSA_Q_COL // grp_w + g)),
            kv_spec(k_blk), kv_spec(v_blk), kv_spec(k_blk + GROUPS), kv_spec(v_blk + GROUPS),
            pl.BlockSpec((1, tq, LANES), lambda i, g, qi: (i, qi, NSA_G_COL // LANES)),
            z_spec(z_blk + GROUPS), z_spec(z_blk + 2 * GROUPS),
            whole, whole, whole,
            pl.BlockSpec((1, 1, N_SLC, tq), lambda i, g, qi: (i, g, 0, qi)),
            whole, whole, whole,
        ],
        out_specs=pl.BlockSpec((1, tq, grp_w), lambda i, g, qi: (i, qi, g)),
        out_shape=jax.ShapeDtypeStruct((b, SEQ, NSA_WIDTH), F32),
        scratch_shapes=[
            pltpu.VMEM((2, HEADS_PER_GROUP * tq, LANES), BF16),
            pltpu.VMEM((2, HEADS_PER_GROUP, tq, LANES), F32),
            pltpu.VMEM((2, HEADS_PER_GROUP, tq, LANES), F32),
            pltpu.VMEM((2, HEADS_PER_GROUP, tq, tk), F32),
            pltpu.VMEM((2, HEADS_PER_GROUP, tq, tk), BF16),
            pltpu.VMEM((2, HEADS_PER_GROUP, tq, LANES), F32),
            pltpu.VMEM((3, SEQ, LANES), BF16),
            pltpu.VMEM((4, SEQ, LANES), BF16),
        ],
        compiler_params=_cparams(("parallel", "parallel", "arbitrary")),
        name="token_attention",
    )(proj, proj, proj, proj, proj, proj, proj, proj, a_tiles, lane_consts, gate_expand,
      sel, *sel_consts)


def _retention_kernel(q_ref, k_ref, v_ref, z_ref, cos_ref, sin_ref, inner_ref, xi_ref, zeta_ref,
                      gn_ref, y_ref, state_ref, *, decays):
    @pl.when(pl.program_id(1) == 0)
    def _():
        state_ref[...] = jnp.zeros(state_ref.shape, F32)

    cos = cos_ref[...]
    sin = sin_ref[...]
    half = RET_QK_DIM // 2

    def rot(x):
        x1, x2 = x[:, :half], x[:, half:]
        return jnp.concatenate([x1 * cos - x2 * sin, x1 * sin + x2 * cos], axis=1)

    for h in range(RET_HEADS):
        qs = slice(h * RET_QK_DIM, (h + 1) * RET_QK_DIM)
        vs = slice(h * RET_V_DIM, (h + 1) * RET_V_DIM)
        qr = rot(q_ref[0, :, qs].astype(F32))
        kr = rot(k_ref[0, :, qs].astype(F32)) * (RET_QK_DIM ** -0.5)
        qb = qr.astype(BF16)
        vh = v_ref[0, :, vs]
        attn = lax.dot_general(qb, kr.astype(BF16), _NT, preferred_element_type=F32) * inner_ref[h]
        st = state_ref[h]
        o = (jnp.dot(attn.astype(BF16), vh, preferred_element_type=F32)
             + jnp.dot(qb, st.astype(BF16), preferred_element_type=F32) * xi_ref[h])
        kz = (kr * zeta_ref[h]).astype(BF16)
        state_ref[h] = st * decays[h] + lax.dot_general(kz, vh, _TN, preferred_element_type=F32)
        mu = jnp.mean(o, axis=1, keepdims=True)
        d = o - mu
        var = jnp.mean(d * d, axis=1, keepdims=True)
        on = d * lax.rsqrt(var + GN_EPS) * gn_ref[h]
        y_ref[0, :, vs] = (on * _silu(z_ref[0, :, vs].astype(F32))).astype(y_ref.dtype)


def _retention(proj, gn_gain, tables):
    b = proj.shape[0]
    c = RET_CHUNK
    cos, sin, inner, xi, zeta, decays = tables
    v_blk = 2 * RET_QK_WIDTH // RET_WIDTH
    return pl.pallas_call(
        functools.partial(_retention_kernel, decays=decays),
        grid=(b, SEQ // c),
        in_specs=[
            pl.BlockSpec((1, c, RET_QK_WIDTH), lambda i, j: (i, j, 0)),
            pl.BlockSpec((1, c, RET_QK_WIDTH), lambda i, j: (i, j, 1)),
            pl.BlockSpec((1, c, RET_WIDTH), lambda i, j: (i, j, v_blk)),
            pl.BlockSpec((1, c, RET_WIDTH), lambda i, j: (i, j, v_blk + 1)),
            pl.BlockSpec((c, RET_QK_DIM // 2), lambda i, j: (j, 0)),
            pl.BlockSpec((c, RET_QK_DIM // 2), lambda i, j: (j, 0)),
            pl.BlockSpec((RET_HEADS, c, c), lambda i, j: (0, 0, 0)),
            pl.BlockSpec((RET_HEADS, c, 1), lambda i, j: (0, 0, 0)),
            pl.BlockSpec((RET_HEADS, c, 1), lambda i, j: (0, 0, 0)),
            pl.BlockSpec((RET_HEADS, 1, RET_V_DIM), lambda i, j: (0, 0, 0)),
        ],
        out_specs=pl.BlockSpec((1, c, RET_WIDTH), lambda i, j: (i, j, 0)),
        out_shape=jax.ShapeDtypeStruct((b, SEQ, RET_WIDTH), BF16),
        scratch_shapes=[pltpu.VMEM((RET_HEADS, RET_QK_DIM, RET_V_DIM), F32)],
        compiler_params=_cparams(("parallel", "arbitrary")),
        name="retention",
    )(proj, proj, proj, proj, cos, sin, inner, xi, zeta, gn_gain.reshape(RET_HEADS, 1, RET_V_DIM))


def _t5_bucket_np(dist):
    n = np.maximum(dist, 0)
    max_exact = REL_BUCKETS // 2
    nf = np.maximum(n, 1).astype(np.float64)
    large = max_exact + (np.log(nf / max_exact) / math.log(REL_MAX_DIST / max_exact)
                         * (REL_BUCKETS - max_exact)).astype(np.int64)
    large = np.minimum(large, REL_BUCKETS - 1)
    return np.where(n < max_exact, n, large).astype(np.int32)


def _skewed(vec, rows, stride, cols):
    p = vec.shape[-1]
    lead = vec.shape[:-1]
    flat = jnp.tile(vec, (1,) * len(lead) + (rows,))[..., :rows * (p - stride)]
    return flat.reshape(lead + (rows, p - stride))[..., :cols]


def _nsa_tables(table):
    tq, tk = ATT_TQ, ATT_TK
    assert tq == tk
    tab_t = table.T
    period = 2 * SEQ
    x = np.arange(period)
    ok = (x >= CMP_BLOCK - 1) & (x < SEQ)
    vec = jnp.take(tab_t, jnp.asarray(_t5_bucket_np(x - (CMP_BLOCK - 1))), axis=1)
    vec = jnp.where(jnp.asarray(ok)[None, :], vec, MASKED)
    bias_cmp = _skewed(vec, N_CMP_PAD, CMP_STRIDE, SEQ)
    rel_t = tab_t - tab_t[:, REL_BUCKETS - 1:]
    period = 2 * tq
    x = np.arange(period)
    x = np.where(x < tq, x, x - period)
    tiles = []
    for off in range(3):
        d = off * tk - x
        ok = (d >= 0) & (d < WIN_SIZE) if off == 2 else (d >= 0)
        vec = jnp.take(rel_t, jnp.asarray(_t5_bucket_np(d)), axis=1)
        vec = jnp.where(jnp.asarray(ok)[None, :], vec, MASKED)
        tiles.append(_skewed(vec, tq, 1, tk))
    a_tiles = jnp.stack(tiles + [jnp.zeros_like(tiles[0])])
    cs = np.arange(N_CMP_PAD)[None, :] * CMP_STRIDE
    jj = np.arange(N_SLC)[:, None]
    ovl = ((cs < (jj + 1) * SLC_BLOCK) & (cs + CMP_BLOCK > jj * SLC_BLOCK)
           & (np.arange(N_CMP_PAD)[None, :] < N_CMP))
    ovl = jnp.asarray(ovl.astype(np.float32), BF16)
    lane = np.arange(LANES).reshape(1, 1, 1, LANES)
    key = np.arange(SEQ).reshape(1, SEQ // tk, tk, 1)
    base = np.array([HEAD_DIM, 0]).reshape(2, 1, 1, 1)
    key_blocks = jnp.asarray((lane == base + key // SLC_BLOCK).astype(np.float32), BF16)
    col = np.arange(2 * LANES).reshape(1, -1)
    jb = np.arange(N_SLC).reshape(-1, 1)
    place = (col == HEAD_DIM + jb) | (col == LANES + jb)
    sel_consts = (key_blocks, jnp.asarray(place.astype(np.float32), BF16),
                  jnp.asarray(place.any(axis=0, keepdims=True).astype(np.float32)))
    c = np.arange(LANES).reshape(1, 1, LANES, 1)
    col = np.arange(HEADS_PER_GROUP * HEAD_DIM).reshape(1, 1, 1, -1) // HEAD_DIM
    br = np.arange(3).reshape(3, 1, 1, 1)
    gg = np.arange(GROUPS).reshape(1, GROUPS, 1, 1)
    gate_expand = jnp.asarray((c == br * HEADS + HEADS_PER_GROUP * gg + col).astype(np.float32),
                              BF16)
    return bias_cmp, a_tiles, ovl, sel_consts, _lane_consts(tq), gate_expand


def _retention_tables():
    c = RET_CHUNK
    log_g = jnp.log(1.0 - 2.0 ** (-5.0 - jnp.arange(RET_HEADS, dtype=F32)))
    i = jnp.arange(c, dtype=F32)
    diff = i[:, None] - i[None, :]
    inner = jnp.where(diff >= 0, jnp.exp(diff[None] * log_g[:, None, None]), 0.0)
    xi = jnp.exp((i + 1.0)[None, :] * log_g[:, None])[:, :, None]
    zeta = jnp.exp((c - 1.0 - i)[None, :] * log_g[:, None])[:, :, None]
    decays = tuple(float((1.0 - 2.0 ** (-5.0 - h)) ** c) for h in range(RET_HEADS))
    inv = 1.0 / (ROPE_BASE ** jnp.linspace(0.0, 1.0, RET_QK_DIM // 2, dtype=F32))
    ang = jnp.arange(SEQ, dtype=F32)[:, None] * inv[None, :]
    return jnp.cos(ang), jnp.sin(ang), inner, xi, zeta, decays


def _nsa_w_in_layout(w):
    d = w.shape[0]
    q = w[:, :NSA_WIDTH] * (HEAD_DIM ** -0.5)
    kv0 = NSA_WIDTH
    g0 = kv0 + 6 * KV_WIDTH
    z0 = g0 + 3 * HEADS
    slab = lambda n: w[:, kv0 + n * KV_WIDTH: kv0 + (n + 1) * KV_WIDTH]
    cols = [q, w[:, z0:z0 + 3 * NSA_WIDTH]]
    for n in (2, 4, 3, 5):
        for g in range(GROUPS):
            part = slab(n)[:, g * HEAD_DIM:(g + 1) * HEAD_DIM]
            cols += [part, part]
    cols += [slab(0), slab(1), w[:, g0:z0]]
    used = NSA_G_COL + 3 * HEADS
    cols.append(jnp.zeros((d, NSA_PROJ_PAD - used), w.dtype))
    return jnp.concatenate(cols, axis=1).astype(BF16)


def _nsa_layer(h2d, b, pre_gain, post_gain, w_in, w_out, k_pos, k_w1, k_w2, v_pos, v_w1, v_w2, tabs):
    bias_cmp, a_tiles, ovl, sel_consts, lane_consts, gate_expand = tabs
    proj = _norm_proj(h2d, pre_gain, _nsa_w_in_layout(w_in), NSA_PROJ_TN)
    proj = proj.reshape(b, SEQ, NSA_PROJ_PAD)
    ckv = proj[:, :, NSA_CKV_COL:NSA_CKV_COL + 2 * KV_WIDTH]
    ckv = ckv.reshape(b, N_CMP_PAD, CMP_STRIDE, 2, GROUPS, HEAD_DIM).transpose(0, 3, 4, 1, 2, 5)
    ckv = ckv.reshape(b, 2, GROUPS, N_CMP_PAD, CMP_STRIDE * HEAD_DIM)
    pos = jnp.stack([k_pos, v_pos]).reshape(2, 2, CMP_STRIDE * HEAD_DIM)
    w1 = jnp.stack([k_w1, v_w1]).astype(BF16)
    w2 = jnp.stack([k_w2, v_w2])
    w2d = jnp.concatenate([w2, w2], axis=2).astype(BF16)
    ckv_c = _compress(ckv, pos, w1, w2d)
    y_cmp, sel = _cmp_select(proj, ckv_c, bias_cmp, ovl, lane_consts, gate_expand[0])
    y_tok = _token_attention(proj, a_tiles, lane_consts, gate_expand[1:], sel, sel_consts)
    parts = [y.reshape(b * SEQ, NSA_WIDTH) for y in (y_cmp, y_tok)]
    return _out_post(parts, w_out.astype(BF16), h2d, post_gain)


def _ret_layer(h2d, b, pre_gain, post_gain, w_in, w_out, gn_gain, tabs):
    proj = _norm_proj(h2d, pre_gain, w_in.astype(BF16), RET_PROJ_TN).reshape(b, SEQ, -1)
    y = _retention(proj, gn_gain, tabs)
    return _out_post([y.reshape(b * SEQ, RET_WIDTH)], w_out.astype(BF16), h2d, post_gain)


def kernel(x, pre_norm_gain, post_norm_gain, rel_bias_table, nsa_w_in, nsa_w_out, nsa_cmp_k_pos, nsa_cmp_k_w1, nsa_cmp_k_w2, nsa_cmp_v_pos, nsa_cmp_v_w1, nsa_cmp_v_w2, ret_w_in, ret_w_out, ret_gn_gain):
    b, s, d = x.shape
    assert s == SEQ and d == D_MODEL
    nsa_tabs = _nsa_tables(rel_bias_table)
    ret_tabs = _retention_tables()
    h = x.reshape(b * s, d)
    for layer in range(DEPTH):
        slot = layer // 2
        if layer % 2 == 0:
            h = _nsa_layer(h, b, pre_norm_gain[layer], post_norm_gain[layer], nsa_w_in[slot],
                           nsa_w_out[slot], nsa_cmp_k_pos[slot], nsa_cmp_k_w1[slot],
                           nsa_cmp_k_w2[slot], nsa_cmp_v_pos[slot], nsa_cmp_v_w1[slot],
                           nsa_cmp_v_w2[slot], nsa_tabs)
        else:
            h = _ret_layer(h, b, pre_norm_gain[layer], post_norm_gain[layer], ret_w_in[slot],
                           ret_w_out[slot], ret_gn_gain[slot], ret_tabs)
    return h.reshape(b, s, d)
```

```python
import functools
import math

import numpy as np
import jax
import jax.numpy as jnp
from jax import lax
from jax.experimental import pallas as pl
from jax.experimental.pallas import tpu as pltpu

F32 = jnp.float32
BF16 = jnp.bfloat16

D_MODEL = 1024
SEQ = 2048
DEPTH = 4
RMS_EPS = 1e-6
GN_EPS = 1e-6
MASKED = -1e30
LOG2E = math.log2(math.e)

HEADS = 16
HEAD_DIM = 64
GROUPS = 4
HEADS_PER_GROUP = HEADS // GROUPS
NSA_WIDTH = HEADS * HEAD_DIM
KV_WIDTH = GROUPS * HEAD_DIM
CMP_BLOCK = 32
CMP_STRIDE = 16
CMP_HIDDEN = 256
N_CMP = (SEQ - CMP_BLOCK) // CMP_STRIDE + 1
N_CMP_PAD = 128
SLC_BLOCK = 64
N_SLC = SEQ // SLC_BLOCK
SLC_TOPN = 16
WIN_SIZE = 512
FORCED_SCORE = 1e3
REL_BUCKETS = 32
REL_MAX_DIST = 128

RET_HEADS = 4
RET_QK_DIM = 256
RET_V_DIM = 512
RET_QK_WIDTH = RET_HEADS * RET_QK_DIM
RET_WIDTH = RET_HEADS * RET_V_DIM
ROPE_BASE = 10000.0

LANES = 128
VMEM_LIMIT_BYTES = 56 * 1024 * 1024

PROJ_TM = 1024
RET_PROJ_TN = 1024
NSA_PROJ_TN = 2304
POST_TM = 512
ATT_TQ = 256
ATT_TK = 256
RET_CHUNK = 256

NSA_Q_COL = 0
NSA_Z_COL = 1024
NSA_K_COL = 4096
NSA_V_COL = 5120
NSA_CKV_COL = 6144
NSA_G_COL = 6656
NSA_PROJ_PAD = 6912

_NT = (((1,), (1,)), ((), ()))
_TN = (((0,), (0,)), ((), ()))


def _cparams(sem):
    return pltpu.CompilerParams(dimension_semantics=sem, vmem_limit_bytes=VMEM_LIMIT_BYTES)


def _norm_proj_kernel(x_ref, g_ref, w_ref, o_ref, xn_ref):
    @pl.when(pl.program_id(1) == 0)
    def _():
        x = x_ref[...]
        ms = jnp.mean(x * x, axis=-1, keepdims=True)
        xn_ref[...] = (x * lax.rsqrt(ms + RMS_EPS) * g_ref[...]).astype(BF16)

    o_ref[...] = jnp.dot(xn_ref[...], w_ref[...], preferred_element_type=F32).astype(o_ref.dtype)


def _norm_proj(x2d, gain, w_bf16, tn):
    m, d = x2d.shape
    n = w_bf16.shape[1]
    assert n % tn == 0
    return pl.pallas_call(
        _norm_proj_kernel,
        grid=(m // PROJ_TM, n // tn),
        in_specs=[
            pl.BlockSpec((PROJ_TM, d), lambda i, j: (i, 0)),
            pl.BlockSpec((1, d), lambda i, j: (0, 0)),
            pl.BlockSpec((d, tn), lambda i, j: (0, j)),
        ],
        out_specs=pl.BlockSpec((PROJ_TM, tn), lambda i, j: (i, j)),
        out_shape=jax.ShapeDtypeStruct((m, n), BF16),
        scratch_shapes=[pltpu.VMEM((PROJ_TM, d), BF16)],
        compiler_params=_cparams(("parallel", "arbitrary")),
        name="norm_proj",
    )(x2d, gain.reshape(1, d), w_bf16)


def _out_post_kernel(*refs, n_parts):
    y_refs = refs[:n_parts]
    w_ref, h_ref, g_ref, o_ref = refs[n_parts:]
    y = y_refs[0][...].astype(F32)
    for r in y_refs[1:]:
        y = y + r[...].astype(F32)
    t = jnp.dot(y.astype(BF16), w_ref[...], preferred_element_type=F32)
    ms = jnp.mean(t * t, axis=-1, keepdims=True)
    o_ref[...] = h_ref[...] + t * lax.rsqrt(ms + RMS_EPS) * g_ref[...]


def _out_post(parts, w_bf16, h2d, gain):
    m, d = h2d.shape
    k = w_bf16.shape[0]
    n_parts = len(parts)
    return pl.pallas_call(
        functools.partial(_out_post_kernel, n_parts=n_parts),
        grid=(m // POST_TM,),
        in_specs=[pl.BlockSpec((POST_TM, k), lambda i: (i, 0)) for _ in parts] + [
            pl.BlockSpec((k, d), lambda i: (0, 0)),
            pl.BlockSpec((POST_TM, d), lambda i: (i, 0)),
            pl.BlockSpec((1, d), lambda i: (0, 0)),
        ],
        out_specs=pl.BlockSpec((POST_TM, d), lambda i: (i, 0)),
        out_shape=jax.ShapeDtypeStruct((m, d), F32),
        compiler_params=_cparams(("parallel",)),
        name="out_post",
    )(*parts, w_bf16, h2d, gain.reshape(1, d))


def _compress_kernel(x_ref, pos_ref, w1_ref, w2_ref, o_ref):
    half = CMP_STRIDE * HEAD_DIM
    for which in range(2):
        x = x_ref[0, which, 0].astype(F32)
        xa = (x + pos_ref[which, 0:1, :]).astype(BF16)
        xb = (x + pos_ref[which, 1:2, :]).astype(BF16)
        pa = jnp.dot(xa, w1_ref[which, :half, :], preferred_element_type=F32)
        pb = jnp.dot(xb, w1_ref[which, half:, :], preferred_element_type=F32)
        hid = pa + pltpu.roll(pb, N_CMP_PAD - 1, 0)
        hid = hid * jax.nn.sigmoid(hid)
        o_ref[0, which, 0] = jnp.dot(hid.astype(BF16), w2_ref[which],
                                     preferred_element_type=F32).astype(o_ref.dtype)


def _compress(ckv_rows, pos, w1, w2d):
    b = ckv_rows.shape[0]
    row_w = CMP_STRIDE * HEAD_DIM
    return pl.pallas_call(
        _compress_kernel,
        grid=(b, GROUPS),
        in_specs=[
            pl.BlockSpec((1, 2, 1, N_CMP_PAD, row_w), lambda i, g: (i, 0, g, 0, 0)),
            pl.BlockSpec((2, 2, row_w), lambda i, g: (0, 0, 0)),
            pl.BlockSpec((2, 2 * row_w, CMP_HIDDEN), lambda i, g: (0, 0, 0)),
            pl.BlockSpec((2, CMP_HIDDEN, 2 * HEAD_DIM), lambda i, g: (0, 0, 0)),
        ],
        out_specs=pl.BlockSpec((1, 2, 1, N_CMP_PAD, 2 * HEAD_DIM), lambda i, g: (i, 0, g, 0, 0)),
        out_shape=jax.ShapeDtypeStruct((b, 2, GROUPS, N_CMP_PAD, 2 * HEAD_DIM), BF16),
        compiler_params=_cparams(("parallel", "parallel")),
        name="compress",
    )(ckv_rows, pos, w1, w2d)


def _lane_consts(rows):
    lane = np.arange(LANES)
    lo = (lane < HEAD_DIM).astype(np.float32)
    hi = (lane >= HEAD_DIM).astype(np.float32)
    last = (lane == LANES - 1).astype(np.float32)
    first = (lane == 0).astype(np.float32)
    c = np.stack([lo, hi, 1.0 - last, last, 1.0 - first, first])
    return jnp.asarray(np.broadcast_to(c[:, None, :], (6, rows, LANES)), BF16)


def _bf16_terms(x, n):
    terms = []
    for _ in range(n - 1):
        t = x.astype(BF16)
        terms.append(t)
        x = x - t.astype(F32)
    terms.append(x.astype(BF16))
    return terms


def _gates(gate_logits, expand):
    sig = jax.nn.sigmoid(gate_logits.astype(F32))
    return sum(jnp.dot(t, expand, preferred_element_type=F32) for t in _bf16_terms(sig, 2))


def _silu(x):
    return x * jax.nn.sigmoid(x)


def _cmp_select_kernel(q_ref, gate_ref, z_ref, ckv_ref, bias_ref, ovl_ref, lc_ref, gexp_ref,
                       y_ref, sel_ref):
    g = pl.program_id(0)
    qi = pl.program_id(1)
    tq = q_ref.shape[1]
    q = q_ref[0]
    kk = ckv_ref[0, 0, 0]
    vv = ckv_ref[0, 1, 0]
    lane = lax.broadcasted_iota(jnp.int32, (tq, LANES), 1)

    psum = jnp.zeros((N_CMP_PAD, tq), F32)
    pairs = []
    for a in range(2):
        qp = q[:, a * LANES:(a + 1) * LANES]
        outs = []
        for e in range(2):
            s = lax.dot_general(kk, qp * lc_ref[e], _NT, preferred_element_type=F32)
            bias = bias_ref[2 * a + e]
            s = s + bias
            m = jnp.max(s, axis=0, keepdims=True)
            p = jnp.exp2(s - m)
            p = jnp.where(bias > 0.5 * MASKED, p / jnp.sum(p, axis=0, keepdims=True), 0.0)
            psum = psum + p
            outs.append(lax.dot_general(p.astype(BF16), vv, _TN, preferred_element_type=F32))
        pairs.append(jnp.where(lane < HEAD_DIM, outs[0], outs[1]))
    y_ref[0] = (jnp.concatenate(pairs, axis=1) * _gates(gate_ref[0], gexp_ref[g])
                * _silu(z_ref[0].astype(F32)))

    ovl = ovl_ref[...]
    imp = sum(jnp.dot(ovl, t, preferred_element_type=F32) for t in _bf16_terms(psum, 3))
    t = qi * tq + lax.broadcasted_iota(jnp.int32, (N_SLC, tq), 1)
    blk = lax.broadcasted_iota(jnp.int32, (N_SLC, tq), 0)
    cur = lax.shift_right_logical(t, int(math.log2(SLC_BLOCK)))
    forced = (blk == 0) | (blk == cur) | (blk == cur - 1)
    score = jnp.where(blk * SLC_BLOCK <= t, imp + jnp.where(forced, FORCED_SCORE, 0.0), MASKED)
    cnt = jnp.zeros((N_SLC, tq), jnp.int32)
    for jp in range(N_SLC):
        row = score[jp:jp + 1, :]
        beats = (row > score) | ((row == score) & (blk > jp))
        cnt = cnt + beats.astype(jnp.int32)
    sel_ref[0, 0] = jnp.where(cnt < SLC_TOPN, 1.0, 0.0).astype(sel_ref.dtype)


def _cmp_select(proj, ckv, bias_cmp, ovl, lane_consts, gate_expand):
    b = proj.shape[0]
    tq = ATT_TQ
    grp_w = HEADS_PER_GROUP * HEAD_DIM
    return pl.pallas_call(
        _cmp_select_kernel,
        grid=(GROUPS, SEQ // tq, b),
        in_specs=[
            pl.BlockSpec((1, tq, grp_w), lambda g, qi, i: (i, qi, NSA_Q_COL // grp_w + g)),
            pl.BlockSpec((1, tq, LANES), lambda g, qi, i: (i, qi, NSA_G_COL // LANES)),
            pl.BlockSpec((1, tq, grp_w), lambda g, qi, i: (i, qi, NSA_Z_COL // grp_w + g)),
            pl.BlockSpec((1, 2, 1, N_CMP_PAD, 2 * HEAD_DIM), lambda g, qi, i: (i, 0, g, 0, 0)),
            pl.BlockSpec((HEADS_PER_GROUP, N_CMP_PAD, tq), lambda g, qi, i: (g, 0, qi)),
            pl.BlockSpec((N_SLC, N_CMP_PAD), lambda g, qi, i: (0, 0)),
            pl.BlockSpec(memory_space=pltpu.VMEM),
            pl.BlockSpec(memory_space=pltpu.VMEM),
        ],
        out_specs=[
            pl.BlockSpec((1, tq, grp_w), lambda g, qi, i: (i, qi, g)),
            pl.BlockSpec((1, 1, N_SLC, tq), lambda g, qi, i: (i, g, 0, qi)),
        ],
        out_shape=[
            jax.ShapeDtypeStruct((b, SEQ, NSA_WIDTH), F32),
            jax.ShapeDtypeStruct((b, GROUPS, N_SLC, SEQ), BF16),
        ],
        compiler_params=_cparams(("parallel", "parallel", "parallel")),
        name="cmp_select",
    )(proj, proj, proj, ckv, bias_cmp, ovl, lane_consts, gate_expand)


def _token_kernel(q_ref, ks_ref, vs_ref, kw_ref, vw_ref, gate_ref, zs_ref, zw_ref, a_ref, lc_ref,
                  gexp_ref, sel_ref, kblk_ref, place_ref, placed_ref, y_ref,
                  qm_ref, m_ref, acc_ref, s_bufs, p_bufs, c_bufs, kx_ref, vx_ref):
    g = pl.program_id(1)
    qi = pl.program_id(2)
    tq = q_ref.shape[1]
    tk = ATT_TK
    n_slc = qi + 1
    n_tiles = n_slc + jnp.minimum(qi, WIN_SIZE // tk) + 1

    q = q_ref[0]
    flags = lax.dot_general(sel_ref[0, 0], place_ref[...], _TN, preferred_element_type=F32)
    fill = ((placed_ref[...] - flags) * MASKED).astype(BF16)
    for r in range(HEADS_PER_GROUP):
        a, e = divmod(r, 2)
        qm = q[:, a * LANES:(a + 1) * LANES] * lc_ref[e]
        qm_ref[0, r * tq:(r + 1) * tq, :] = qm + fill[:, e * LANES:(e + 1) * LANES]
        qm_ref[1, r * tq:(r + 1) * tq, :] = qm

    @pl.when(qi == 0)
    def _():
        for c in range(SEQ // tk):
            rows = slice(c * tk, (c + 1) * tk)
            for branch, v_ref in enumerate((vs_ref, vw_ref)):
                vv = v_ref[0, rows, :]
                vx_ref[2 * branch, rows, :] = vv * lc_ref[2] + lc_ref[3]
                vx_ref[2 * branch + 1, rows, :] = vv * lc_ref[4] + lc_ref[5]
            kk = ks_ref[0, rows, :]
            for e in range(2):
                kx_ref[e, rows, :] = kk * lc_ref[e] + kblk_ref[e, c]
            kx_ref[2, rows, :] = kw_ref[0, rows, :]

    def tile(j):
        win = (j >= n_slc).astype(jnp.int32)
        t = j - win * n_slc
        rows = pl.ds(pl.multiple_of(jnp.clip(qi - t, 0, qi) * tk, tk), tk)
        return win, rows, jnp.where(win == 1, t, jnp.where(t < 2, t, 3))

    def scores(j, s_ref):
        win, rows, _ = tile(j)
        for e in range(2):
            ke = kx_ref[jnp.where(win == 1, 2, e), rows, :]
            for a in range(2):
                r = 2 * a + e
                s_ref[r] = lax.dot_general(qm_ref[win, r * tq:(r + 1) * tq, :], ke, _NT,
                                           preferred_element_type=F32)

    def softmax(j, s_ref, p_ref, c_ref):
        win, _, kind = tile(j)
        for r in range(HEADS_PER_GROUP):
            s = s_ref[r] + a_ref[kind, HEADS_PER_GROUP * g + r]
            m_prev = m_ref[win, r]
            m_new = jnp.maximum(m_prev, jnp.max(s, axis=1, keepdims=True))
            c_ref[r] = jnp.exp2(m_prev - m_new)
            x = s - jnp.concatenate([m_new] * (tk // LANES), axis=1)
            p_ref[r] = jnp.exp2(x.astype(BF16))
            m_ref[win, r] = m_new

    def values(j, p_ref, c_ref):
        win, rows, _ = tile(j)
        for r in range(HEADS_PER_GROUP):
            pv = jnp.dot(p_ref[r], vx_ref[2 * win + r % 2, rows, :], preferred_element_type=F32)
            acc_ref[win, r] = c_ref[r] * acc_ref[win, r] + pv

    m_ref[...] = jnp.full(m_ref.shape, -jnp.inf, F32)
    acc_ref[...] = jnp.zeros(acc_ref.shape, F32)
    bufs = [(s_bufs.at[i], p_bufs.at[i], c_bufs.at[i]) for i in range(2)]

    scores(0, bufs[0][0])
    softmax(0, *bufs[0])
    scores(1, bufs[1][0])

    def step(j, carry):
        for parity in range(2):
            cur, nxt = bufs[parity], bufs[1 - parity]

            @pl.when(lax.bitwise_and(j, 1) == parity)
            def _():
                softmax(j, *cur)
                scores(j + 1, nxt[0])
                values(j - 1, nxt[1], nxt[2])
        return carry
    lax.fori_loop(1, n_tiles, step, 0)

    for parity in range(2):
        @pl.when(lax.bitwise_and(n_tiles - 1, 1) == parity)
        def _():
            values(n_tiles - 1, bufs[parity][1], bufs[parity][2])

    lane = lax.broadcasted_iota(jnp.int32, (tq, LANES), 1)
    y = None
    for branch, z_ref in enumerate((zs_ref, zw_ref)):
        pairs = []
        for a in range(2):
            acc_e = acc_ref[branch, 2 * a]
            acc_o = acc_ref[branch, 2 * a + 1]
            pairs.append(jnp.where(lane < HEAD_DIM, acc_e / acc_e[:, LANES - 1:LANES],
                                   acc_o / acc_o[:, 0:1]))
        part = (jnp.concatenate(pairs, axis=1) * _gates(gate_ref[0], gexp_ref[branch, g])
                * _silu(z_ref[0].astype(F32)))
        y = part if y is None else y + part
    y_ref[0] = y


def _token_attention(proj, a_tiles, lane_consts, gate_expand, sel, sel_consts):
    b = proj.shape[0]
    tq, tk = ATT_TQ, ATT_TK
    grp_w = HEADS_PER_GROUP * HEAD_DIM
    k_blk = NSA_K_COL // LANES
    v_blk = NSA_V_COL // LANES
    z_blk = NSA_Z_COL // grp_w
    kv_spec = lambda blk: pl.BlockSpec((1, SEQ, LANES), lambda i, g, qi: (i, 0, blk + g))
    z_spec = lambda blk: pl.BlockSpec((1, tq, grp_w), lambda i, g, qi: (i, qi, blk + g))
    whole = pl.BlockSpec(memory_space=pltpu.VMEM)
    return pl.pallas_call(
        _token_kernel,
        grid=(b, GROUPS, SEQ // tq),
        in_specs=[
            pl.BlockSpec((1, tq, grp_w), lambda i, g, qi: (i, qi, NSA_Q_COL // grp_w + g)),
            kv_spec(k_blk), kv_spec(v_blk), kv_spec(k_blk + GROUPS), kv_spec(v_blk + GROUPS),
            pl.BlockSpec((1, tq, LANES), lambda i, g, qi: (i, qi, NSA_G_COL // LANES)),
            z_spec(z_blk + GROUPS), z_spec(z_blk + 2 * GROUPS),
            whole, whole, whole,
            pl.BlockSpec((1, 1, N_SLC, tq), lambda i, g, qi: (i, g, 0, qi)),
            whole, whole, whole,
        ],
        out_specs=pl.BlockSpec((1, tq, grp_w), lambda i, g, qi: (i, qi, g)),
        out_shape=jax.ShapeDtypeStruct((b, SEQ, NSA_WIDTH), F32),
        scratch_shapes=[
            pltpu.VMEM((2, HEADS_PER_GROUP * tq, LANES), BF16),
            pltpu.VMEM((2, HEADS_PER_GROUP, tq, LANES), F32),
            pltpu.VMEM((2, HEADS_PER_GROUP, tq, LANES), F32),
            pltpu.VMEM((2, HEADS_PER_GROUP, tq, tk), F32),
            pltpu.VMEM((2, HEADS_PER_GROUP, tq, tk), BF16),
            pltpu.VMEM((2, HEADS_PER_GROUP, tq, LANES), F32),
            pltpu.VMEM((3, SEQ, LANES), BF16),
            pltpu.VMEM((4, SEQ, LANES), BF16),
        ],
        compiler_params=_cparams(("parallel", "parallel", "arbitrary")),
        name="token_attention",
    )(proj, proj, proj, proj, proj, proj, proj, proj, a_tiles, lane_consts, gate_expand,
      sel, *sel_consts)


def _retention_kernel(q_ref, k_ref, v_ref, z_ref, cos_ref, sin_ref, inner_ref, xi_ref, zeta_ref,
                      gn_ref, y_ref, state_ref, *, decays):
    @pl.when(pl.program_id(1) == 0)
    def _():
        state_ref[...] = jnp.zeros(state_ref.shape, F32)

    cos = cos_ref[...]
    sin = sin_ref[...]
    half = RET_QK_DIM // 2

    def rot(x):
        x1, x2 = x[:, :half], x[:, half:]
        return jnp.concatenate([x1 * cos - x2 * sin, x1 * sin + x2 * cos], axis=1)

    for h in range(RET_HEADS):
        qs = slice(h * RET_QK_DIM, (h + 1) * RET_QK_DIM)
        vs = slice(h * RET_V_DIM, (h + 1) * RET_V_DIM)
        qr = rot(q_ref[0, :, qs].astype(F32))
        kr = rot(k_ref[0, :, qs].astype(F32)) * (RET_QK_DIM ** -0.5)
        qb = qr.astype(BF16)
        vh = v_ref[0, :, vs]
        attn = lax.dot_general(qb, kr.astype(BF16), _NT, preferred_element_type=F32) * inner_ref[h]
        st = state_ref[h]
        o = (jnp.dot(attn.astype(BF16), vh, preferred_element_type=F32)
             + jnp.dot(qb, st.astype(BF16), preferred_element_type=F32) * xi_ref[h])
        kz = (kr * zeta_ref[h]).astype(BF16)
        state_ref[h] = st * decays[h] + lax.dot_general(kz, vh, _TN, preferred_element_type=F32)
        mu = jnp.mean(o, axis=1, keepdims=True)
        d = o - mu
        var = jnp.mean(d * d, axis=1, keepdims=True)
        on = d * lax.rsqrt(var + GN_EPS) * gn_ref[h]
        y_ref[0, :, vs] = (on * _silu(z_ref[0, :, vs].astype(F32))).astype(y_ref.dtype)


def _retention(proj, gn_gain, tables):
    b = proj.shape[0]
    c = RET_CHUNK
    cos, sin, inner, xi, zeta, decays = tables
    v_blk = 2 * RET_QK_WIDTH // RET_WIDTH
    return pl.pallas_call(
        functools.partial(_retention_kernel, decays=decays),
        grid=(b, SEQ // c),
        in_specs=[
            pl.BlockSpec((1, c, RET_QK_WIDTH), lambda i, j: (i, j, 0)),
            pl.BlockSpec((1, c, RET_QK_WIDTH), lambda i, j: (i, j, 1)),
            pl.BlockSpec((1, c, RET_WIDTH), lambda i, j: (i, j, v_blk)),
            pl.BlockSpec((1, c, RET_WIDTH), lambda i, j: (i, j, v_blk + 1)),
            pl.BlockSpec((c, RET_QK_DIM // 2), lambda i, j: (j, 0)),
            pl.BlockSpec((c, RET_QK_DIM // 2), lambda i, j: (j, 0)),
            pl.BlockSpec((RET_HEADS, c, c), lambda i, j: (0, 0, 0)),
            pl.BlockSpec((RET_HEADS, c, 1), lambda i, j: (0, 0, 0)),
            pl.BlockSpec((RET_HEADS, c, 1), lambda i, j: (0, 0, 0)),
            pl.BlockSpec((RET_HEADS, 1, RET_V_DIM), lambda i, j: (0, 0, 0)),
        ],
        out_specs=pl.BlockSpec((1, c, RET_WIDTH), lambda i, j: (i, j, 0)),
        out_shape=jax.ShapeDtypeStruct((b, SEQ, RET_WIDTH), BF16),
        scratch_shapes=[pltpu.VMEM((RET_HEADS, RET_QK_DIM, RET_V_DIM), F32)],
        compiler_params=_cparams(("parallel", "arbitrary")),
        name="retention",
    )(proj, proj, proj, proj, cos, sin, inner, xi, zeta, gn_gain.reshape(RET_HEADS, 1, RET_V_DIM))


def _t5_bucket_np(dist):
    n = np.maximum(dist, 0)
    max_exact = REL_BUCKETS // 2
    nf = np.maximum(n, 1).astype(np.float64)
    large = max_exact + (np.log(nf / max_exact) / math.log(REL_MAX_DIST / max_exact)
                         * (REL_BUCKETS - max_exact)).astype(np.int64)
    large = np.minimum(large, REL_BUCKETS - 1)
    return np.where(n < max_exact, n, large).astype(np.int32)


def _skewed(vec, rows, stride, cols):
    p = vec.shape[-1]
    lead = vec.shape[:-1]
    flat = jnp.tile(vec, (1,) * len(lead) + (rows,))[..., :rows * (p - stride)]
    return flat.reshape(lead + (rows, p - stride))[..., :cols]


def _nsa_tables(table):
    tq, tk = ATT_TQ, ATT_TK
    assert tq == tk
    tab_t = table.T * LOG2E
    period = 2 * SEQ
    x = np.arange(period)
    ok = (x >= CMP_BLOCK - 1) & (x < SEQ)
    vec = jnp.take(tab_t, jnp.asarray(_t5_bucket_np(x - (CMP_BLOCK - 1))), axis=1)
    vec = jnp.where(jnp.asarray(ok)[None, :], vec, MASKED)
    bias_cmp = _skewed(vec, N_CMP_PAD, CMP_STRIDE, SEQ)
    rel_t = tab_t - tab_t[:, REL_BUCKETS - 1:]
    period = 2 * tq
    x = np.arange(period)
    x = np.where(x < tq, x, x - period)
    tiles = []
    for off in range(3):
        d = off * tk - x
        ok = (d >= 0) & (d < WIN_SIZE) if off == 2 else (d >= 0)
        vec = jnp.take(rel_t, jnp.asarray(_t5_bucket_np(d)), axis=1)
        vec = jnp.where(jnp.asarray(ok)[None, :], vec, MASKED)
        tiles.append(_skewed(vec, tq, 1, tk))
    a_tiles = jnp.stack(tiles + [jnp.zeros_like(tiles[0])])
    cs = np.arange(N_CMP_PAD)[None, :] * CMP_STRIDE
    jj = np.arange(N_SLC)[:, None]
    ovl = ((cs < (jj + 1) * SLC_BLOCK) & (cs + CMP_BLOCK > jj * SLC_BLOCK)
           & (np.arange(N_CMP_PAD)[None, :] < N_CMP))
    ovl = jnp.asarray(ovl.astype(np.float32), BF16)
    lane = np.arange(LANES).reshape(1, 1, 1, LANES)
    key = np.arange(SEQ).reshape(1, SEQ // tk, tk, 1)
    base = np.array([HEAD_DIM, 0]).reshape(2, 1, 1, 1)
    key_blocks = jnp.asarray((lane == base + key // SLC_BLOCK).astype(np.float32), BF16)
    col = np.arange(2 * LANES).reshape(1, -1)
    jb = np.arange(N_SLC).reshape(-1, 1)
    place = (col == HEAD_DIM + jb) | (col == LANES + jb)
    sel_consts = (key_blocks, jnp.asarray(place.astype(np.float32), BF16),
                  jnp.asarray(place.any(axis=0, keepdims=True).astype(np.float32)))
    c = np.arange(LANES).reshape(1, 1, LANES, 1)
    col = np.arange(HEADS_PER_GROUP * HEAD_DIM).reshape(1, 1, 1, -1) // HEAD_DIM
    br = np.arange(3).reshape(3, 1, 1, 1)
    gg = np.arange(GROUPS).reshape(1, GROUPS, 1, 1)
    gate_expand = jnp.asarray((c == br * HEADS + HEADS_PER_GROUP * gg + col).astype(np.float32),
                              BF16)
    return bias_cmp, a_tiles, ovl, sel_consts, _lane_consts(tq), gate_expand


def _retention_tables():
    c = RET_CHUNK
    log_g = jnp.log(1.0 - 2.0 ** (-5.0 - jnp.arange(RET_HEADS, dtype=F32)))
    i = jnp.arange(c, dtype=F32)
    diff = i[:, None] - i[None, :]
    inner = jnp.where(diff >= 0, jnp.exp(diff[None] * log_g[:, None, None]), 0.0)
    xi = jnp.exp((i + 1.0)[None, :] * log_g[:, None])[:, :, None]
    zeta = jnp.exp((c - 1.0 - i)[None, :] * log_g[:, None])[:, :, None]
    decays = tuple(float((1.0 - 2.0 ** (-5.0 - h)) ** c) for h in range(RET_HEADS))
    inv = 1.0 / (ROPE_BASE ** jnp.linspace(0.0, 1.0, RET_QK_DIM // 2, dtype=F32))
    ang = jnp.arange(SEQ, dtype=F32)[:, None] * inv[None, :]
    return jnp.cos(ang), jnp.sin(ang), inner, xi, zeta, decays


def _nsa_w_in_layout(w):
    d = w.shape[0]
    q = w[:, :NSA_WIDTH] * (HEAD_DIM ** -0.5 * LOG2E)
    kv0 = NSA_WIDTH
    g0 = kv0 + 6 * KV_WIDTH
    z0 = g0 + 3 * HEADS
    slab = lambda n: w[:, kv0 + n * KV_WIDTH: kv0 + (n + 1) * KV_WIDTH]
    cols = [q, w[:, z0:z0 + 3 * NSA_WIDTH]]
    for n in (2, 4, 3, 5):
        for g in range(GROUPS):
            part = slab(n)[:, g * HEAD_DIM:(g + 1) * HEAD_DIM]
            cols += [part, part]
    cols += [slab(0), slab(1), w[:, g0:z0]]
    used = NSA_G_COL + 3 * HEADS
    cols.append(jnp.zeros((d, NSA_PROJ_PAD - used), w.dtype))
    return jnp.concatenate(cols, axis=1).astype(BF16)


def _nsa_layer(h2d, b, pre_gain, post_gain, w_in, w_out, k_pos, k_w1, k_w2, v_pos, v_w1, v_w2, tabs):
    bias_cmp, a_tiles, ovl, sel_consts, lane_consts, gate_expand = tabs
    proj = _norm_proj(h2d, pre_gain, _nsa_w_in_layout(w_in), NSA_PROJ_TN)
    proj = proj.reshape(b, SEQ, NSA_PROJ_PAD)
    ckv = proj[:, :, NSA_CKV_COL:NSA_CKV_COL + 2 * KV_WIDTH]
    ckv = ckv.reshape(b, N_CMP_PAD, CMP_STRIDE, 2, GROUPS, HEAD_DIM).transpose(0, 3, 4, 1, 2, 5)
    ckv = ckv.reshape(b, 2, GROUPS, N_CMP_PAD, CMP_STRIDE * HEAD_DIM)
    pos = jnp.stack([k_pos, v_pos]).reshape(2, 2, CMP_STRIDE * HEAD_DIM)
    w1 = jnp.stack([k_w1, v_w1]).astype(BF16)
    w2 = jnp.stack([k_w2, v_w2])
    w2d = jnp.concatenate([w2, w2], axis=2).astype(BF16)
    ckv_c = _compress(ckv, pos, w1, w2d)
    y_cmp, sel = _cmp_select(proj, ckv_c, bias_cmp, ovl, lane_consts, gate_expand[0])
    y_tok = _token_attention(proj, a_tiles, lane_consts, gate_expand[1:], sel, sel_consts)
    parts = [y.reshape(b * SEQ, NSA_WIDTH) for y in (y_cmp, y_tok)]
    return _out_post(parts, w_out.astype(BF16), h2d, post_gain)


def _ret_layer(h2d, b, pre_gain, post_gain, w_in, w_out, gn_gain, tabs):
    proj = _norm_proj(h2d, pre_gain, w_in.astype(BF16), RET_PROJ_TN).reshape(b, SEQ, -1)
    y = _retention(proj, gn_gain, tabs)
    return _out_post([y.reshape(b * SEQ, RET_WIDTH)], w_out.astype(BF16), h2d, post_gain)


def kernel(x, pre_norm_gain, post_norm_gain, rel_bias_table, nsa_w_in, nsa_w_out, nsa_cmp_k_pos, nsa_cmp_k_w1, nsa_cmp_k_w2, nsa_cmp_v_pos, nsa_cmp_v_w1, nsa_cmp_v_w2, ret_w_in, ret_w_out, ret_gn_gain):
    b, s, d = x.shape
    assert s == SEQ and d == D_MODEL
    nsa_tabs = _nsa_tables(rel_bias_table)
    ret_tabs = _retention_tables()
    h = x.reshape(b * s, d)
    for layer in range(DEPTH):
        slot = layer // 2
        if layer % 2 == 0:
            h = _nsa_layer(h, b, pre_norm_gain[layer], post_norm_gain[layer], nsa_w_in[slot],
                           nsa_w_out[slot], nsa_cmp_k_pos[slot], nsa_cmp_k_w1[slot],
                           nsa_cmp_k_w2[slot], nsa_cmp_v_pos[slot], nsa_cmp_v_w1[slot],
                           nsa_cmp_v_w2[slot], nsa_tabs)
        else:
            h = _ret_layer(h, b, pre_norm_gain[layer], post_norm_gain[layer], ret_w_in[slot],
                           ret_w_out[slot], ret_gn_gain[slot], ret_tabs)
    return h.reshape(b, s, d)
```

```python
import functools
import math

import numpy as np
import jax
import jax.numpy as jnp
from jax import lax
from jax.experimental import pallas as pl
from jax.experimental.pallas import tpu as pltpu

F32 = jnp.float32
BF16 = jnp.bfloat16

D_MODEL = 1024
SEQ = 2048
DEPTH = 4
RMS_EPS = 1e-6
GN_EPS = 1e-6
MASKED = -1e30
LOG2E = math.log2(math.e)

HEADS = 16
HEAD_DIM = 64
GROUPS = 4
HEADS_PER_GROUP = HEADS // GROUPS
NSA_WIDTH = HEADS * HEAD_DIM
KV_WIDTH = GROUPS * HEAD_DIM
CMP_BLOCK = 32
CMP_STRIDE = 16
CMP_HIDDEN = 256
N_CMP = (SEQ - CMP_BLOCK) // CMP_STRIDE + 1
N_CMP_PAD = 128
SLC_BLOCK = 64
N_SLC = SEQ // SLC_BLOCK
SLC_TOPN = 16
WIN_SIZE = 512
FORCED_SCORE = 1e3
REL_BUCKETS = 32
REL_MAX_DIST = 128

RET_HEADS = 4
RET_QK_DIM = 256
RET_V_DIM = 512
RET_QK_WIDTH = RET_HEADS * RET_QK_DIM
RET_WIDTH = RET_HEADS * RET_V_DIM
ROPE_BASE = 10000.0

LANES = 128
VMEM_LIMIT_BYTES = 56 * 1024 * 1024

PROJ_TM = 1024
RET_PROJ_TN = 1024
NSA_PROJ_TN = 2304
POST_TM = 512
ATT_TQ = 256
ATT_TK = 256
RET_CHUNK = 256

NSA_Q_COL = 0
NSA_Z_COL = 1024
NSA_K_COL = 4096
NSA_V_COL = 5120
NSA_CKV_COL = 6144
NSA_G_COL = 6656
NSA_PROJ_PAD = 6912

_NT = (((1,), (1,)), ((), ()))
_TN = (((0,), (0,)), ((), ()))


def _cparams(sem):
    return pltpu.CompilerParams(dimension_semantics=sem, vmem_limit_bytes=VMEM_LIMIT_BYTES)


def _norm_proj_kernel(x_ref, g_ref, w_ref, o_ref, xn_ref):
    @pl.when(pl.program_id(1) == 0)
    def _():
        x = x_ref[...]
        ms = jnp.mean(x * x, axis=-1, keepdims=True)
        xn_ref[...] = (x * lax.rsqrt(ms + RMS_EPS) * g_ref[...]).astype(BF16)

    o_ref[...] = jnp.dot(xn_ref[...], w_ref[...], preferred_element_type=F32).astype(o_ref.dtype)


def _norm_proj(x2d, gain, w_bf16, tn):
    m, d = x2d.shape
    n = w_bf16.shape[1]
    assert n % tn == 0
    return pl.pallas_call(
        _norm_proj_kernel,
        grid=(m // PROJ_TM, n // tn),
        in_specs=[
            pl.BlockSpec((PROJ_TM, d), lambda i, j: (i, 0)),
            pl.BlockSpec((1, d), lambda i, j: (0, 0)),
            pl.BlockSpec((d, tn), lambda i, j: (0, j)),
        ],
        out_specs=pl.BlockSpec((PROJ_TM, tn), lambda i, j: (i, j)),
        out_shape=jax.ShapeDtypeStruct((m, n), BF16),
        scratch_shapes=[pltpu.VMEM((PROJ_TM, d), BF16)],
        compiler_params=_cparams(("parallel", "arbitrary")),
        name="norm_proj",
    )(x2d, gain.reshape(1, d), w_bf16)


def _out_post_kernel(*refs, n_parts):
    y_refs = refs[:n_parts]
    w_ref, h_ref, g_ref, o_ref = refs[n_parts:]
    y = y_refs[0][...].astype(F32)
    for r in y_refs[1:]:
        y = y + r[...].astype(F32)
    t = jnp.dot(y.astype(BF16), w_ref[...], preferred_element_type=F32)
    ms = jnp.mean(t * t, axis=-1, keepdims=True)
    o_ref[...] = h_ref[...] + t * lax.rsqrt(ms + RMS_EPS) * g_ref[...]


def _out_post(parts, w_bf16, h2d, gain):
    m, d = h2d.shape
    k = w_bf16.shape[0]
    n_parts = len(parts)
    return pl.pallas_call(
        functools.partial(_out_post_kernel, n_parts=n_parts),
        grid=(m // POST_TM,),
        in_specs=[pl.BlockSpec((POST_TM, k), lambda i: (i, 0)) for _ in parts] + [
            pl.BlockSpec((k, d), lambda i: (0, 0)),
            pl.BlockSpec((POST_TM, d), lambda i: (i, 0)),
            pl.BlockSpec((1, d), lambda i: (0, 0)),
        ],
        out_specs=pl.BlockSpec((POST_TM, d), lambda i: (i, 0)),
        out_shape=jax.ShapeDtypeStruct((m, d), F32),
        compiler_params=_cparams(("parallel",)),
        name="out_post",
    )(*parts, w_bf16, h2d, gain.reshape(1, d))


def _compress_kernel(x_ref, pos_ref, w1_ref, w2_ref, o_ref):
    half = CMP_STRIDE * HEAD_DIM
    for which in range(2):
        x = x_ref[0, which, 0].astype(F32)
        xa = (x + pos_ref[which, 0:1, :]).astype(BF16)
        xb = (x + pos_ref[which, 1:2, :]).astype(BF16)
        pa = jnp.dot(xa, w1_ref[which, :half, :], preferred_element_type=F32)
        pb = jnp.dot(xb, w1_ref[which, half:, :], preferred_element_type=F32)
        hid = pa + pltpu.roll(pb, N_CMP_PAD - 1, 0)
        hid = hid * jax.nn.sigmoid(hid)
        o_ref[0, which, 0] = jnp.dot(hid.astype(BF16), w2_ref[which],
                                     preferred_element_type=F32).astype(o_ref.dtype)


def _compress(ckv_rows, pos, w1, w2d):
    b = ckv_rows.shape[0]
    row_w = CMP_STRIDE * HEAD_DIM
    return pl.pallas_call(
        _compress_kernel,
        grid=(b, GROUPS),
        in_specs=[
            pl.BlockSpec((1, 2, 1, N_CMP_PAD, row_w), lambda i, g: (i, 0, g, 0, 0)),
            pl.BlockSpec((2, 2, row_w), lambda i, g: (0, 0, 0)),
            pl.BlockSpec((2, 2 * row_w, CMP_HIDDEN), lambda i, g: (0, 0, 0)),
            pl.BlockSpec((2, CMP_HIDDEN, 2 * HEAD_DIM), lambda i, g: (0, 0, 0)),
        ],
        out_specs=pl.BlockSpec((1, 2, 1, N_CMP_PAD, 2 * HEAD_DIM), lambda i, g: (i, 0, g, 0, 0)),
        out_shape=jax.ShapeDtypeStruct((b, 2, GROUPS, N_CMP_PAD, 2 * HEAD_DIM), BF16),
        compiler_params=_cparams(("parallel", "parallel")),
        name="compress",
    )(ckv_rows, pos, w1, w2d)


def _lane_consts(rows):
    lane = np.arange(LANES)
    lo = (lane < HEAD_DIM).astype(np.float32)
    hi = (lane >= HEAD_DIM).astype(np.float32)
    last = (lane == LANES - 1).astype(np.float32)
    first = (lane == 0).astype(np.float32)
    c = np.stack([lo, hi, 1.0 - last, last, 1.0 - first, first])
    return jnp.asarray(np.broadcast_to(c[:, None, :], (6, rows, LANES)), BF16)


def _bf16_terms(x, n):
    terms = []
    for _ in range(n - 1):
        t = x.astype(BF16)
        terms.append(t)
        x = x - t.astype(F32)
    terms.append(x.astype(BF16))
    return terms


def _gates(gate_logits, expand):
    sig = jax.nn.sigmoid(gate_logits.astype(F32))
    return sum(jnp.dot(t, expand, preferred_element_type=F32) for t in _bf16_terms(sig, 2))


def _silu(x):
    return x * jax.nn.sigmoid(x)


def _cmp_select_kernel(q_ref, gate_ref, z_ref, ckv_ref, bias_ref, ovl_ref, lc_ref, gexp_ref,
                       y_ref, sel_ref):
    g = pl.program_id(0)
    qi = pl.program_id(1)
    tq = q_ref.shape[1]
    q = q_ref[0]
    kk = ckv_ref[0, 0, 0]
    vv = ckv_ref[0, 1, 0]
    lane = lax.broadcasted_iota(jnp.int32, (tq, LANES), 1)

    psum = jnp.zeros((N_CMP_PAD, tq), F32)
    pairs = []
    for a in range(2):
        qp = q[:, a * LANES:(a + 1) * LANES]
        outs = []
        for e in range(2):
            s = lax.dot_general(kk, qp * lc_ref[e], _NT, preferred_element_type=F32)
            bias = bias_ref[2 * a + e]
            s = s + bias
            m = jnp.max(s, axis=0, keepdims=True)
            p = jnp.exp2(s - m)
            p = jnp.where(bias > 0.5 * MASKED, p / jnp.sum(p, axis=0, keepdims=True), 0.0)
            psum = psum + p
            outs.append(lax.dot_general(p.astype(BF16), vv, _TN, preferred_element_type=F32))
        pairs.append(jnp.where(lane < HEAD_DIM, outs[0], outs[1]))
    y_ref[0] = (jnp.concatenate(pairs, axis=1) * _gates(gate_ref[0], gexp_ref[g])
                * _silu(z_ref[0].astype(F32)))

    ovl = ovl_ref[...]
    imp = sum(jnp.dot(ovl, t, preferred_element_type=F32) for t in _bf16_terms(psum, 3))
    t = qi * tq + lax.broadcasted_iota(jnp.int32, (N_SLC, tq), 1)
    blk = lax.broadcasted_iota(jnp.int32, (N_SLC, tq), 0)
    cur = lax.shift_right_logical(t, int(math.log2(SLC_BLOCK)))
    forced = (blk == 0) | (blk == cur) | (blk == cur - 1)
    score = jnp.where(blk * SLC_BLOCK <= t, imp + jnp.where(forced, FORCED_SCORE, 0.0), MASKED)
    cnt = jnp.zeros((N_SLC, tq), jnp.int32)
    for jp in range(N_SLC):
        row = score[jp:jp + 1, :]
        beats = (row > score) | ((row == score) & (blk > jp))
        cnt = cnt + beats.astype(jnp.int32)
    sel_ref[0, 0] = jnp.where(cnt < SLC_TOPN, 1.0, 0.0).astype(sel_ref.dtype)


def _cmp_select(proj, ckv, bias_cmp, ovl, lane_consts, gate_expand):
    b = proj.shape[0]
    tq = ATT_TQ
    grp_w = HEADS_PER_GROUP * HEAD_DIM
    return pl.pallas_call(
        _cmp_select_kernel,
        grid=(GROUPS, SEQ // tq, b),
        in_specs=[
            pl.BlockSpec((1, tq, grp_w), lambda g, qi, i: (i, qi, NSA_Q_COL // grp_w + g)),
            pl.BlockSpec((1, tq, LANES), lambda g, qi, i: (i, qi, NSA_G_COL // LANES)),
            pl.BlockSpec((1, tq, grp_w), lambda g, qi, i: (i, qi, NSA_Z_COL // grp_w + g)),
            pl.BlockSpec((1, 2, 1, N_CMP_PAD, 2 * HEAD_DIM), lambda g, qi, i: (i, 0, g, 0, 0)),
            pl.BlockSpec((HEADS_PER_GROUP, N_CMP_PAD, tq), lambda g, qi, i: (g, 0, qi)),
            pl.BlockSpec((N_SLC, N_CMP_PAD), lambda g, qi, i: (0, 0)),
            pl.BlockSpec(memory_space=pltpu.VMEM),
            pl.BlockSpec(memory_space=pltpu.VMEM),
        ],
        out_specs=[
            pl.BlockSpec((1, tq, grp_w), lambda g, qi, i: (i, qi, g)),
            pl.BlockSpec((1, 1, N_SLC, tq), lambda g, qi, i: (i, g, 0, qi)),
        ],
        out_shape=[
            jax.ShapeDtypeStruct((b, SEQ, NSA_WIDTH), F32),
            jax.ShapeDtypeStruct((b, GROUPS, N_SLC, SEQ), BF16),
        ],
        compiler_params=_cparams(("parallel", "parallel", "parallel")),
        name="cmp_select",
    )(proj, proj, proj, ckv, bias_cmp, ovl, lane_consts, gate_expand)


def _token_kernel(q_ref, ks_ref, vs_ref, kw_ref, vw_ref, gate_ref, zs_ref, zw_ref, a_ref, lc_ref,
                  gexp_ref, sel_ref, kblk_ref, place_ref, placed_ref, y_ref,
                  qm_ref, m_ref, acc_ref, s_bufs, p_bufs, c_bufs, kx_ref, vx_ref):
    g = pl.program_id(1)
    qi = pl.program_id(2)
    tq = q_ref.shape[1]
    tk = ATT_TK
    n_slc = qi + 1
    n_tiles = n_slc + jnp.minimum(qi, WIN_SIZE // tk) + 1

    q = q_ref[0]
    flags = lax.dot_general(sel_ref[0, 0], place_ref[...], _TN, preferred_element_type=F32)
    fill = ((placed_ref[...] - flags) * MASKED).astype(BF16)
    for slot in range(HEADS_PER_GROUP):
        e, a = divmod(slot, 2)
        qm = q[:, a * LANES:(a + 1) * LANES] * lc_ref[e]
        qm_ref[0, slot * tq:(slot + 1) * tq, :] = qm + fill[:, e * LANES:(e + 1) * LANES]
        qm_ref[1, slot * tq:(slot + 1) * tq, :] = qm

    @pl.when(qi == 0)
    def _():
        for c in range(SEQ // tk):
            rows = slice(c * tk, (c + 1) * tk)
            for branch, v_ref in enumerate((vs_ref, vw_ref)):
                vv = v_ref[0, rows, :]
                vx_ref[2 * branch, rows, :] = vv * lc_ref[2] + lc_ref[3]
                vx_ref[2 * branch + 1, rows, :] = vv * lc_ref[4] + lc_ref[5]
            kk = ks_ref[0, rows, :]
            for e in range(2):
                kx_ref[e, c] = (kk * lc_ref[e] + kblk_ref[e, c]).astype(F32).T.astype(BF16)
            kx_ref[2, c] = kw_ref[0, rows, :].astype(F32).T.astype(BF16)

    def tile(j):
        win = (j >= n_slc).astype(jnp.int32)
        t = j - win * n_slc
        return win, jnp.clip(qi - t, 0, qi), jnp.where(win == 1, t, jnp.where(t < 2, t, 3))

    def scores(j, s_ref):
        win, ki, _ = tile(j)
        for e in range(2):
            pair = slice(2 * e * tq, 2 * (e + 1) * tq)
            s_ref[pair, :] = jnp.dot(qm_ref[win, pair, :], kx_ref[jnp.where(win == 1, 2, e), ki],
                                     preferred_element_type=F32)

    def softmax(j, s_ref, p_ref, c_ref):
        win, _, kind = tile(j)
        for slot in range(HEADS_PER_GROUP):
            e, a = divmod(slot, 2)
            rows = slice(slot * tq, (slot + 1) * tq)
            s = s_ref[rows, :] + a_ref[kind, HEADS_PER_GROUP * g + 2 * a + e]
            m_prev = m_ref[win, rows, :]
            m_new = jnp.maximum(m_prev, jnp.max(s, axis=1, keepdims=True))
            c_ref[rows, :] = jnp.exp2(m_prev - m_new)
            x = s - jnp.concatenate([m_new] * (tk // LANES), axis=1)
            p_ref[rows, :] = jnp.exp2(x.astype(BF16))
            m_ref[win, rows, :] = m_new

    def values(j, p_ref, c_ref):
        win, ki, _ = tile(j)
        rows = pl.ds(pl.multiple_of(ki * tk, tk), tk)
        for e in range(2):
            pair = slice(2 * e * tq, 2 * (e + 1) * tq)
            pv = jnp.dot(p_ref[pair, :], vx_ref[2 * win + e, rows, :], preferred_element_type=F32)
            acc_ref[win, pair, :] = c_ref[pair, :] * acc_ref[win, pair, :] + pv

    m_ref[...] = jnp.full(m_ref.shape, -jnp.inf, F32)
    acc_ref[...] = jnp.zeros(acc_ref.shape, F32)
    bufs = [(s_bufs.at[i], p_bufs.at[i], c_bufs.at[i]) for i in range(2)]

    scores(0, bufs[0][0])
    softmax(0, *bufs[0])
    scores(1, bufs[1][0])

    def step(j, carry):
        for parity in range(2):
            cur, nxt = bufs[parity], bufs[1 - parity]

            @pl.when(lax.bitwise_and(j, 1) == parity)
            def _():
                softmax(j, *cur)
                scores(j + 1, nxt[0])
                values(j - 1, nxt[1], nxt[2])
        return carry
    lax.fori_loop(1, n_tiles, step, 0)

    for parity in range(2):
        @pl.when(lax.bitwise_and(n_tiles - 1, 1) == parity)
        def _():
            values(n_tiles - 1, bufs[parity][1], bufs[parity][2])

    lane = lax.broadcasted_iota(jnp.int32, (tq, LANES), 1)
    y = None
    for branch, z_ref in enumerate((zs_ref, zw_ref)):
        pairs = []
        for a in range(2):
            acc_e = acc_ref[branch, a * tq:(a + 1) * tq, :]
            acc_o = acc_ref[branch, (2 + a) * tq:(3 + a) * tq, :]
            pairs.append(jnp.where(lane < HEAD_DIM, acc_e / acc_e[:, LANES - 1:LANES],
                                   acc_o / acc_o[:, 0:1]))
        part = (jnp.concatenate(pairs, axis=1) * _gates(gate_ref[0], gexp_ref[branch, g])
                * _silu(z_ref[0].astype(F32)))
        y = part if y is None else y + part
    y_ref[0] = y


def _token_attention(proj, a_tiles, lane_consts, gate_expand, sel, sel_consts):
    b = proj.shape[0]
    tq, tk = ATT_TQ, ATT_TK
    grp_w = HEADS_PER_GROUP * HEAD_DIM
    k_blk = NSA_K_COL // LANES
    v_blk = NSA_V_COL // LANES
    z_blk = NSA_Z_COL // grp_w
    kv_spec = lambda blk: pl.BlockSpec((1, SEQ, LANES), lambda i, g, qi: (i, 0, blk + g))
    z_spec = lambda blk: pl.BlockSpec((1, tq, grp_w), lambda i, g, qi: (i, qi, blk + g))
    whole = pl.BlockSpec(memory_space=pltpu.VMEM)
    return pl.pallas_call(
        _token_kernel,
        grid=(b, GROUPS, SEQ // tq),
        in_specs=[
            pl.BlockSpec((1, tq, grp_w), lambda i, g, qi: (i, qi, NSA_Q_COL // grp_w + g)),
            kv_spec(k_blk), kv_spec(v_blk), kv_spec(k_blk + GROUPS), kv_spec(v_blk + GROUPS),
            pl.BlockSpec((1, tq, LANES), lambda i, g, qi: (i, qi, NSA_G_COL // LANES)),
            z_spec(z_blk + GROUPS), z_spec(z_blk + 2 * GROUPS),
            whole, whole, whole,
            pl.BlockSpec((1, 1, N_SLC, tq), lambda i, g, qi: (i, g, 0, qi)),
            whole, whole, whole,
        ],
        out_specs=pl.BlockSpec((1, tq, grp_w), lambda i, g, qi: (i, qi, g)),
        out_shape=jax.ShapeDtypeStruct((b, SEQ, NSA_WIDTH), F32),
        scratch_shapes=[
            pltpu.VMEM((2, HEADS_PER_GROUP * tq, LANES), BF16),
            pltpu.VMEM((2, HEADS_PER_GROUP * tq, LANES), F32),
            pltpu.VMEM((2, HEADS_PER_GROUP * tq, LANES), F32),
            pltpu.VMEM((2, HEADS_PER_GROUP * tq, tk), F32),
            pltpu.VMEM((2, HEADS_PER_GROUP * tq, tk), BF16),
            pltpu.VMEM((2, HEADS_PER_GROUP * tq, LANES), F32),
            pltpu.VMEM((3, SEQ // tk, LANES, tk), BF16),
            pltpu.VMEM((4, SEQ, LANES), BF16),
        ],
        compiler_params=_cparams(("parallel", "parallel", "arbitrary")),
        name="token_attention",
    )(proj, proj, proj, proj, proj, proj, proj, proj, a_tiles, lane_consts, gate_expand,
      sel, *sel_consts)


def _retention_kernel(q_ref, k_ref, v_ref, z_ref, cos_ref, sin_ref, inner_ref, xi_ref, zeta_ref,
                      gn_ref, y_ref, state_ref, *, decays):
    @pl.when(pl.program_id(1) == 0)
    def _():
        state_ref[...] = jnp.zeros(state_ref.shape, F32)

    cos = cos_ref[...]
    sin = sin_ref[...]
    half = RET_QK_DIM // 2

    def rot(x):
        x1, x2 = x[:, :half], x[:, half:]
        return jnp.concatenate([x1 * cos - x2 * sin, x1 * sin + x2 * cos], axis=1)

    for h in range(RET_HEADS):
        qs = slice(h * RET_QK_DIM, (h + 1) * RET_QK_DIM)
        vs = slice(h * RET_V_DIM, (h + 1) * RET_V_DIM)
        qr = rot(q_ref[0, :, qs].astype(F32))
        kr = rot(k_ref[0, :, qs].astype(F32)) * (RET_QK_DIM ** -0.5)
        qb = qr.astype(BF16)
        vh = v_ref[0, :, vs]
        attn = lax.dot_general(qb, kr.astype(BF16), _NT, preferred_element_type=F32) * inner_ref[h]
        st = state_ref[h]
        o = (jnp.dot(attn.astype(BF16), vh, preferred_element_type=F32)
             + jnp.dot(qb, st.astype(BF16), preferred_element_type=F32) * xi_ref[h])
        kz = (kr * zeta_ref[h]).astype(BF16)
        state_ref[h] = st * decays[h] + lax.dot_general(kz, vh, _TN, preferred_element_type=F32)
        mu = jnp.mean(o, axis=1, keepdims=True)
        d = o - mu
        var = jnp.mean(d * d, axis=1, keepdims=True)
        on = d * lax.rsqrt(var + GN_EPS) * gn_ref[h]
        y_ref[0, :, vs] = (on * _silu(z_ref[0, :, vs].astype(F32))).astype(y_ref.dtype)


def _retention(proj, gn_gain, tables):
    b = proj.shape[0]
    c = RET_CHUNK
    cos, sin, inner, xi, zeta, decays = tables
    v_blk = 2 * RET_QK_WIDTH // RET_WIDTH
    return pl.pallas_call(
        functools.partial(_retention_kernel, decays=decays),
        grid=(b, SEQ // c),
        in_specs=[
            pl.BlockSpec((1, c, RET_QK_WIDTH), lambda i, j: (i, j, 0)),
            pl.BlockSpec((1, c, RET_QK_WIDTH), lambda i, j: (i, j, 1)),
            pl.BlockSpec((1, c, RET_WIDTH), lambda i, j: (i, j, v_blk)),
            pl.BlockSpec((1, c, RET_WIDTH), lambda i, j: (i, j, v_blk + 1)),
            pl.BlockSpec((c, RET_QK_DIM // 2), lambda i, j: (j, 0)),
            pl.BlockSpec((c, RET_QK_DIM // 2), lambda i, j: (j, 0)),
            pl.BlockSpec((RET_HEADS, c, c), lambda i, j: (0, 0, 0)),
            pl.BlockSpec((RET_HEADS, c, 1), lambda i, j: (0, 0, 0)),
            pl.BlockSpec((RET_HEADS, c, 1), lambda i, j: (0, 0, 0)),
            pl.BlockSpec((RET_HEADS, 1, RET_V_DIM), lambda i, j: (0, 0, 0)),
        ],
        out_specs=pl.BlockSpec((1, c, RET_WIDTH), lambda i, j: (i, j, 0)),
        out_shape=jax.ShapeDtypeStruct((b, SEQ, RET_WIDTH), BF16),
        scratch_shapes=[pltpu.VMEM((RET_HEADS, RET_QK_DIM, RET_V_DIM), F32)],
        compiler_params=_cparams(("parallel", "arbitrary")),
        name="retention",
    )(proj, proj, proj, proj, cos, sin, inner, xi, zeta, gn_gain.reshape(RET_HEADS, 1, RET_V_DIM))


def _t5_bucket_np(dist):
    n = np.maximum(dist, 0)
    max_exact = REL_BUCKETS // 2
    nf = np.maximum(n, 1).astype(np.float64)
    large = max_exact + (np.log(nf / max_exact) / math.log(REL_MAX_DIST / max_exact)
                         * (REL_BUCKETS - max_exact)).astype(np.int64)
    large = np.minimum(large, REL_BUCKETS - 1)
    return np.where(n < max_exact, n, large).astype(np.int32)


def _skewed(vec, rows, stride, cols):
    p = vec.shape[-1]
    lead = vec.shape[:-1]
    flat = jnp.tile(vec, (1,) * len(lead) + (rows,))[..., :rows * (p - stride)]
    return flat.reshape(lead + (rows, p - stride))[..., :cols]


def _nsa_tables(table):
    tq, tk = ATT_TQ, ATT_TK
    assert tq == tk
    tab_t = table.T * LOG2E
    period = 2 * SEQ
    x = np.arange(period)
    ok = (x >= CMP_BLOCK - 1) & (x < SEQ)
    vec = jnp.take(tab_t, jnp.asarray(_t5_bucket_np(x - (CMP_BLOCK - 1))), axis=1)
    vec = jnp.where(jnp.asarray(ok)[None, :], vec, MASKED)
    bias_cmp = _skewed(vec, N_CMP_PAD, CMP_STRIDE, SEQ)
    rel_t = tab_t - tab_t[:, REL_BUCKETS - 1:]
    period = 2 * tq
    x = np.arange(period)
    x = np.where(x < tq, x, x - period)
    tiles = []
    for off in range(3):
        d = off * tk - x
        ok = (d >= 0) & (d < WIN_SIZE) if off == 2 else (d >= 0)
        vec = jnp.take(rel_t, jnp.asarray(_t5_bucket_np(d)), axis=1)
        vec = jnp.where(jnp.asarray(ok)[None, :], vec, MASKED)
        tiles.append(_skewed(vec, tq, 1, tk))
    a_tiles = jnp.stack(tiles + [jnp.zeros_like(tiles[0])])
    cs = np.arange(N_CMP_PAD)[None, :] * CMP_STRIDE
    jj = np.arange(N_SLC)[:, None]
    ovl = ((cs < (jj + 1) * SLC_BLOCK) & (cs + CMP_BLOCK > jj * SLC_BLOCK)
           & (np.arange(N_CMP_PAD)[None, :] < N_CMP))
    ovl = jnp.asarray(ovl.astype(np.float32), BF16)
    lane = np.arange(LANES).reshape(1, 1, 1, LANES)
    key = np.arange(SEQ).reshape(1, SEQ // tk, tk, 1)
    base = np.array([HEAD_DIM, 0]).reshape(2, 1, 1, 1)
    key_blocks = jnp.asarray((lane == base + key // SLC_BLOCK).astype(np.float32), BF16)
    col = np.arange(2 * LANES).reshape(1, -1)
    jb = np.arange(N_SLC).reshape(-1, 1)
    place = (col == HEAD_DIM + jb) | (col == LANES + jb)
    sel_consts = (key_blocks, jnp.asarray(place.astype(np.float32), BF16),
                  jnp.asarray(place.any(axis=0, keepdims=True).astype(np.float32)))
    c = np.arange(LANES).reshape(1, 1, LANES, 1)
    col = np.arange(HEADS_PER_GROUP * HEAD_DIM).reshape(1, 1, 1, -1) // HEAD_DIM
    br = np.arange(3).reshape(3, 1, 1, 1)
    gg = np.arange(GROUPS).reshape(1, GROUPS, 1, 1)
    gate_expand = jnp.asarray((c == br * HEADS + HEADS_PER_GROUP * gg + col).astype(np.float32),
                              BF16)
    return bias_cmp, a_tiles, ovl, sel_consts, _lane_consts(tq), gate_expand


def _retention_tables():
    c = RET_CHUNK
    log_g = jnp.log(1.0 - 2.0 ** (-5.0 - jnp.arange(RET_HEADS, dtype=F32)))
    i = jnp.arange(c, dtype=F32)
    diff = i[:, None] - i[None, :]
    inner = jnp.where(diff >= 0, jnp.exp(diff[None] * log_g[:, None, None]), 0.0)
    xi = jnp.exp((i + 1.0)[None, :] * log_g[:, None])[:, :, None]
    zeta = jnp.exp((c - 1.0 - i)[None, :] * log_g[:, None])[:, :, None]
    decays = tuple(float((1.0 - 2.0 ** (-5.0 - h)) ** c) for h in range(RET_HEADS))
    inv = 1.0 / (ROPE_BASE ** jnp.linspace(0.0, 1.0, RET_QK_DIM // 2, dtype=F32))
    ang = jnp.arange(SEQ, dtype=F32)[:, None] * inv[None, :]
    return jnp.cos(ang), jnp.sin(ang), inner, xi, zeta, decays


def _nsa_w_in_layout(w):
    d = w.shape[0]
    q = w[:, :NSA_WIDTH] * (HEAD_DIM ** -0.5 * LOG2E)
    kv0 = NSA_WIDTH
    g0 = kv0 + 6 * KV_WIDTH
    z0 = g0 + 3 * HEADS
    slab = lambda n: w[:, kv0 + n * KV_WIDTH: kv0 + (n + 1) * KV_WIDTH]
    cols = [q, w[:, z0:z0 + 3 * NSA_WIDTH]]
    for n in (2, 4, 3, 5):
        for g in range(GROUPS):
            part = slab(n)[:, g * HEAD_DIM:(g + 1) * HEAD_DIM]
            cols += [part, part]
    cols += [slab(0), slab(1), w[:, g0:z0]]
    used = NSA_G_COL + 3 * HEADS
    cols.append(jnp.zeros((d, NSA_PROJ_PAD - used), w.dtype))
    return jnp.concatenate(cols, axis=1).astype(BF16)


def _nsa_layer(h2d, b, pre_gain, post_gain, w_in, w_out, k_pos, k_w1, k_w2, v_pos, v_w1, v_w2, tabs):
    bias_cmp, a_tiles, ovl, sel_consts, lane_consts, gate_expand = tabs
    proj = _norm_proj(h2d, pre_gain, _nsa_w_in_layout(w_in), NSA_PROJ_TN)
    proj = proj.reshape(b, SEQ, NSA_PROJ_PAD)
    ckv = proj[:, :, NSA_CKV_COL:NSA_CKV_COL + 2 * KV_WIDTH]
    ckv = ckv.reshape(b, N_CMP_PAD, CMP_STRIDE, 2, GROUPS, HEAD_DIM).transpose(0, 3, 4, 1, 2, 5)
    ckv = ckv.reshape(b, 2, GROUPS, N_CMP_PAD, CMP_STRIDE * HEAD_DIM)
    pos = jnp.stack([k_pos, v_pos]).reshape(2, 2, CMP_STRIDE * HEAD_DIM)
    w1 = jnp.stack([k_w1, v_w1]).astype(BF16)
    w2 = jnp.stack([k_w2, v_w2])
    w2d = jnp.concatenate([w2, w2], axis=2).astype(BF16)
    ckv_c = _compress(ckv, pos, w1, w2d)
    y_cmp, sel = _cmp_select(proj, ckv_c, bias_cmp, ovl, lane_consts, gate_expand[0])
    y_tok = _token_attention(proj, a_tiles, lane_consts, gate_expand[1:], sel, sel_consts)
    parts = [y.reshape(b * SEQ, NSA_WIDTH) for y in (y_cmp, y_tok)]
    return _out_post(parts, w_out.astype(BF16), h2d, post_gain)


def _ret_layer(h2d, b, pre_gain, post_gain, w_in, w_out, gn_gain, tabs):
    proj = _norm_proj(h2d, pre_gain, w_in.astype(BF16), RET_PROJ_TN).reshape(b, SEQ, -1)
    y = _retention(proj, gn_gain, tabs)
    return _out_post([y.reshape(b * SEQ, RET_WIDTH)], w_out.astype(BF16), h2d, post_gain)


def kernel(x, pre_norm_gain, post_norm_gain, rel_bias_table, nsa_w_in, nsa_w_out, nsa_cmp_k_pos, nsa_cmp_k_w1, nsa_cmp_k_w2, nsa_cmp_v_pos, nsa_cmp_v_w1, nsa_cmp_v_w2, ret_w_in, ret_w_out, ret_gn_gain):
    b, s, d = x.shape
    assert s == SEQ and d == D_MODEL
    nsa_tabs = _nsa_tables(rel_bias_table)
    ret_tabs = _retention_tables()
    h = x.reshape(b * s, d)
    for layer in range(DEPTH):
        slot = layer // 2
        if layer % 2 == 0:
            h = _nsa_layer(h, b, pre_norm_gain[layer], post_norm_gain[layer], nsa_w_in[slot],
                           nsa_w_out[slot], nsa_cmp_k_pos[slot], nsa_cmp_k_w1[slot],
                           nsa_cmp_k_w2[slot], nsa_cmp_v_pos[slot], nsa_cmp_v_w1[slot],
                           nsa_cmp_v_w2[slot], nsa_tabs)
        else:
            h = _ret_layer(h, b, pre_norm_gain[layer], post_norm_gain[layer], ret_w_in[slot],
                           ret_w_out[slot], ret_gn_gain[slot], ret_tabs)
    return h.reshape(b, s, d)
```

```python
import functools
import math

import numpy as np
import jax
import jax.numpy as jnp
from jax import lax
from jax.experimental import pallas as pl
from jax.experimental.pallas import tpu as pltpu

F32 = jnp.float32
BF16 = jnp.bfloat16

D_MODEL = 1024
SEQ = 2048
DEPTH = 4
RMS_EPS = 1e-6
GN_EPS = 1e-6
MASKED = -1e30
LOG2E = math.log2(math.e)

HEADS = 16
HEAD_DIM = 64
GROUPS = 4
HEADS_PER_GROUP = HEADS // GROUPS
NSA_WIDTH = HEADS * HEAD_DIM
KV_WIDTH = GROUPS * HEAD_DIM
CMP_BLOCK = 32
CMP_STRIDE = 16
CMP_HIDDEN = 256
N_CMP = (SEQ - CMP_BLOCK) // CMP_STRIDE + 1
N_CMP_PAD = 128
SLC_BLOCK = 64
N_SLC = SEQ // SLC_BLOCK
SLC_TOPN = 16
WIN_SIZE = 512
FORCED_SCORE = 1e3
REL_BUCKETS = 32
REL_MAX_DIST = 128

RET_HEADS = 4
RET_QK_DIM = 256
RET_V_DIM = 512
RET_QK_WIDTH = RET_HEADS * RET_QK_DIM
RET_WIDTH = RET_HEADS * RET_V_DIM
ROPE_BASE = 10000.0

LANES = 128
VMEM_LIMIT_BYTES = 56 * 1024 * 1024

PROJ_TM = 1024
RET_PROJ_TN = 2048
NSA_PROJ_TN = 3456
POST_TM = 512
ATT_TQ = 256
ATT_TK = 256
CMP_TQ = 512
RET_CHUNK = 256

NSA_Q_COL = 0
NSA_Z_COL = 1024
NSA_K_COL = 4096
NSA_V_COL = 5120
NSA_CKV_COL = 6144
NSA_G_COL = 6656
NSA_PROJ_PAD = 6912

_NT = (((1,), (1,)), ((), ()))
_TN = (((0,), (0,)), ((), ()))


def _cparams(sem):
    return pltpu.CompilerParams(dimension_semantics=sem, vmem_limit_bytes=VMEM_LIMIT_BYTES)


def _norm_proj_kernel(x_ref, g_ref, w_ref, o_ref, xn_ref):
    @pl.when(pl.program_id(1) == 0)
    def _():
        x = x_ref[...]
        ms = jnp.mean(x * x, axis=-1, keepdims=True)
        xn_ref[...] = (x * lax.rsqrt(ms + RMS_EPS) * g_ref[...]).astype(BF16)

    o_ref[...] = jnp.dot(xn_ref[...], w_ref[...], preferred_element_type=F32).astype(o_ref.dtype)


def _norm_proj(x2d, gain, w_bf16, tn):
    m, d = x2d.shape
    n = w_bf16.shape[1]
    assert n % tn == 0
    return pl.pallas_call(
        _norm_proj_kernel,
        grid=(m // PROJ_TM, n // tn),
        in_specs=[
            pl.BlockSpec((PROJ_TM, d), lambda i, j: (i, 0)),
            pl.BlockSpec((1, d), lambda i, j: (0, 0)),
            pl.BlockSpec((d, tn), lambda i, j: (0, j)),
        ],
        out_specs=pl.BlockSpec((PROJ_TM, tn), lambda i, j: (i, j)),
        out_shape=jax.ShapeDtypeStruct((m, n), BF16),
        scratch_shapes=[pltpu.VMEM((PROJ_TM, d), BF16)],
        compiler_params=_cparams(("parallel", "arbitrary")),
        name="norm_proj",
    )(x2d, gain.reshape(1, d), w_bf16)


def _out_post_kernel(*refs, n_parts):
    y_refs = refs[:n_parts]
    w_ref, h_ref, g_ref, o_ref = refs[n_parts:]
    y = y_refs[0][...].astype(F32)
    for r in y_refs[1:]:
        y = y + r[...].astype(F32)
    t = jnp.dot(y.astype(BF16), w_ref[...], preferred_element_type=F32)
    ms = jnp.mean(t * t, axis=-1, keepdims=True)
    o_ref[...] = h_ref[...] + t * lax.rsqrt(ms + RMS_EPS) * g_ref[...]


def _out_post(parts, w_bf16, h2d, gain):
    m, d = h2d.shape
    k = w_bf16.shape[0]
    n_parts = len(parts)
    return pl.pallas_call(
        functools.partial(_out_post_kernel, n_parts=n_parts),
        grid=(m // POST_TM,),
        in_specs=[pl.BlockSpec((POST_TM, k), lambda i: (i, 0)) for _ in parts] + [
            pl.BlockSpec((k, d), lambda i: (0, 0)),
            pl.BlockSpec((POST_TM, d), lambda i: (i, 0)),
            pl.BlockSpec((1, d), lambda i: (0, 0)),
        ],
        out_specs=pl.BlockSpec((POST_TM, d), lambda i: (i, 0)),
        out_shape=jax.ShapeDtypeStruct((m, d), F32),
        compiler_params=_cparams(("parallel",)),
        name="out_post",
    )(*parts, w_bf16, h2d, gain.reshape(1, d))


def _compress_kernel(x_ref, pos_ref, w1_ref, w2_ref, o_ref):
    half = CMP_STRIDE * HEAD_DIM
    for which in range(2):
        x = x_ref[0, which, 0].astype(F32)
        xa = (x + pos_ref[which, 0:1, :]).astype(BF16)
        xb = (x + pos_ref[which, 1:2, :]).astype(BF16)
        pa = jnp.dot(xa, w1_ref[which, :half, :], preferred_element_type=F32)
        pb = jnp.dot(xb, w1_ref[which, half:, :], preferred_element_type=F32)
        hid = pa + pltpu.roll(pb, N_CMP_PAD - 1, 0)
        hid = hid * jax.nn.sigmoid(hid)
        o_ref[0, which, 0] = jnp.dot(hid.astype(BF16), w2_ref[which],
                                     preferred_element_type=F32).astype(o_ref.dtype)


def _compress(ckv_rows, pos, w1, w2d):
    b = ckv_rows.shape[0]
    row_w = CMP_STRIDE * HEAD_DIM
    return pl.pallas_call(
        _compress_kernel,
        grid=(b, GROUPS),
        in_specs=[
            pl.BlockSpec((1, 2, 1, N_CMP_PAD, row_w), lambda i, g: (i, 0, g, 0, 0)),
            pl.BlockSpec((2, 2, row_w), lambda i, g: (0, 0, 0)),
            pl.BlockSpec((2, 2 * row_w, CMP_HIDDEN), lambda i, g: (0, 0, 0)),
            pl.BlockSpec((2, CMP_HIDDEN, 2 * HEAD_DIM), lambda i, g: (0, 0, 0)),
        ],
        out_specs=pl.BlockSpec((1, 2, 1, N_CMP_PAD, 2 * HEAD_DIM), lambda i, g: (i, 0, g, 0, 0)),
        out_shape=jax.ShapeDtypeStruct((b, 2, GROUPS, N_CMP_PAD, 2 * HEAD_DIM), BF16),
        compiler_params=_cparams(("parallel", "parallel")),
        name="compress",
    )(ckv_rows, pos, w1, w2d)


def _lane_consts(rows):
    lane = np.arange(LANES)
    lo = (lane < HEAD_DIM).astype(np.float32)
    hi = (lane >= HEAD_DIM).astype(np.float32)
    last = (lane == LANES - 1).astype(np.float32)
    first = (lane == 0).astype(np.float32)
    c = np.stack([lo, hi, 1.0 - last, last, 1.0 - first, first])
    return jnp.asarray(np.broadcast_to(c[:, None, :], (6, rows, LANES)), BF16)


def _bf16_terms(x, n):
    terms = []
    for _ in range(n - 1):
        t = x.astype(BF16)
        terms.append(t)
        x = x - t.astype(F32)
    terms.append(x.astype(BF16))
    return terms


def _gates(gate_logits, expands):
    terms = _bf16_terms(jax.nn.sigmoid(gate_logits.astype(F32)), 2)
    return [sum(jnp.dot(t, ex, preferred_element_type=F32) for t in terms) for ex in expands]


def _silu(x):
    return x * jax.nn.sigmoid(x)


def _cmp_select_kernel(q_ref, gate_ref, z_ref, ckv_ref, bias_ref, ovl_ref, lc_ref, gexp_ref,
                       y_ref, sel_ref):
    g = pl.program_id(0)
    qi = pl.program_id(1)
    tq = q_ref.shape[1]
    q = q_ref[0]
    kk = ckv_ref[0, 0, 0]
    vv = ckv_ref[0, 1, 0]
    lane = lax.broadcasted_iota(jnp.int32, (tq, LANES), 1)

    psum = jnp.zeros((N_CMP_PAD, tq), F32)
    pairs = []
    for a in range(2):
        qp = q[:, a * LANES:(a + 1) * LANES]
        outs = []
        for e in range(2):
            s = lax.dot_general(kk, qp * lc_ref[e, :tq], _NT, preferred_element_type=F32)
            bias = bias_ref[2 * a + e]
            s = s + bias
            m = jnp.max(s, axis=0, keepdims=True)
            p = jnp.exp2(s - m)
            p = jnp.where(bias > 0.5 * MASKED, p / jnp.sum(p, axis=0, keepdims=True), 0.0)
            psum = psum + p
            outs.append(lax.dot_general(p.astype(BF16), vv, _TN, preferred_element_type=F32))
        pairs.append(jnp.where(lane < HEAD_DIM, outs[0], outs[1]))
    y_ref[0] = (jnp.concatenate(pairs, axis=1) * _gates(gate_ref[0], [gexp_ref[g]])[0]
                * _silu(z_ref[0].astype(F32))).astype(y_ref.dtype)

    ovl = ovl_ref[...]
    imp = sum(jnp.dot(ovl, t, preferred_element_type=F32) for t in _bf16_terms(psum, 3))
    t = qi * tq + lax.broadcasted_iota(jnp.int32, (N_SLC, tq), 1)
    blk = lax.broadcasted_iota(jnp.int32, (N_SLC, tq), 0)
    cur = lax.shift_right_logical(t, int(math.log2(SLC_BLOCK)))
    forced = (blk == 0) | (blk == cur) | (blk == cur - 1)
    score = jnp.where(blk * SLC_BLOCK <= t, imp + jnp.where(forced, FORCED_SCORE, 0.0), MASKED)
    cnt = jnp.zeros((N_SLC, tq), jnp.int32)
    for jp in range(N_SLC):
        row = score[jp:jp + 1, :]
        beats = (row > score) | ((row == score) & (blk > jp))
        cnt = cnt + beats.astype(jnp.int32)
    sel_ref[0, 0] = jnp.where(cnt < SLC_TOPN, 1.0, 0.0).astype(sel_ref.dtype)


def _cmp_select(proj, ckv, bias_cmp, ovl, lane_consts, gate_expand):
    b = proj.shape[0]
    tq = CMP_TQ
    grp_w = HEADS_PER_GROUP * HEAD_DIM
    return pl.pallas_call(
        _cmp_select_kernel,
        grid=(GROUPS, SEQ // tq, b),
        in_specs=[
            pl.BlockSpec((1, tq, grp_w), lambda g, qi, i: (i, qi, NSA_Q_COL // grp_w + g)),
            pl.BlockSpec((1, tq, LANES), lambda g, qi, i: (i, qi, NSA_G_COL // LANES)),
            pl.BlockSpec((1, tq, grp_w), lambda g, qi, i: (i, qi, NSA_Z_COL // grp_w + g)),
            pl.BlockSpec((1, 2, 1, N_CMP_PAD, 2 * HEAD_DIM), lambda g, qi, i: (i, 0, g, 0, 0)),
            pl.BlockSpec((HEADS_PER_GROUP, N_CMP_PAD, tq), lambda g, qi, i: (g, 0, qi)),
            pl.BlockSpec((N_SLC, N_CMP_PAD), lambda g, qi, i: (0, 0)),
            pl.BlockSpec(memory_space=pltpu.VMEM),
            pl.BlockSpec(memory_space=pltpu.VMEM),
        ],
        out_specs=[
            pl.BlockSpec((1, tq, grp_w), lambda g, qi, i: (i, qi, g)),
            pl.BlockSpec((1, 1, N_SLC, tq), lambda g, qi, i: (i, g, 0, qi)),
        ],
        out_shape=[
            jax.ShapeDtypeStruct((b, SEQ, NSA_WIDTH), BF16),
            jax.ShapeDtypeStruct((b, GROUPS, N_SLC, SEQ), BF16),
        ],
        compiler_params=_cparams(("parallel", "parallel", "parallel")),
        name="cmp_select",
    )(proj, proj, proj, ckv, bias_cmp, ovl, lane_consts, gate_expand)


def _token_kernel(q_ref, ks_ref, vs_ref, kw_ref, vw_ref, gate_ref, zs_ref, zw_ref, a_ref, lc_ref,
                  gexp_ref, sel_ref, kblk_ref, place_ref, placed_ref, y_ref,
                  qm_ref, m_ref, acc_ref, s_bufs, p_bufs, c_bufs, kx_ref, vx_ref):
    g = pl.program_id(1)
    qi = pl.program_id(2)
    tq = q_ref.shape[1]
    tk = ATT_TK
    n_slc = qi + 1
    n_tiles = n_slc + jnp.minimum(qi, WIN_SIZE // tk) + 1

    q = q_ref[0]
    flags = lax.dot_general(sel_ref[0, 0], place_ref[...], _TN, preferred_element_type=F32)
    fill = ((placed_ref[...] - flags) * MASKED).astype(BF16)
    for slot in range(HEADS_PER_GROUP):
        e, a = divmod(slot, 2)
        qm = q[:, a * LANES:(a + 1) * LANES] * lc_ref[e, :tq]
        qm_ref[0, slot * tq:(slot + 1) * tq, :] = qm + fill[:, e * LANES:(e + 1) * LANES]
        qm_ref[1, slot * tq:(slot + 1) * tq, :] = qm

    @pl.when(qi == 0)
    def _():
        for c in range(SEQ // tk):
            rows = slice(c * tk, (c + 1) * tk)
            for branch, v_ref in enumerate((vs_ref, vw_ref)):
                vv = v_ref[0, rows, :]
                vx_ref[2 * branch, rows, :] = vv * lc_ref[2, :tk] + lc_ref[3, :tk]
                vx_ref[2 * branch + 1, rows, :] = vv * lc_ref[4, :tk] + lc_ref[5, :tk]
            kk = ks_ref[0, rows, :]
            for e in range(2):
                kx_ref[e, c] = (kk * lc_ref[e, :tk] + kblk_ref[e, c]).astype(F32).T.astype(BF16)
            kx_ref[2, c] = kw_ref[0, rows, :].astype(F32).T.astype(BF16)

    def tile(j):
        win = (j >= n_slc).astype(jnp.int32)
        t = j - win * n_slc
        return win, jnp.clip(qi - t, 0, qi), jnp.where(win == 1, t, jnp.where(t < 2, t, 3))

    def scores(j, s_ref):
        win, ki, _ = tile(j)
        for e in range(2):
            pair = slice(2 * e * tq, 2 * (e + 1) * tq)
            s_ref[pair, :] = jnp.dot(qm_ref[win, pair, :], kx_ref[jnp.where(win == 1, 2, e), ki],
                                     preferred_element_type=F32)

    def softmax(j, s_ref, p_ref, c_ref):
        win, _, kind = tile(j)
        for slot in range(HEADS_PER_GROUP):
            e, a = divmod(slot, 2)
            rows = slice(slot * tq, (slot + 1) * tq)
            s = s_ref[rows, :] + a_ref[kind, HEADS_PER_GROUP * g + 2 * a + e]
            m_prev = m_ref[win, rows, :]
            m_new = jnp.maximum(m_prev, jnp.max(s, axis=1, keepdims=True))
            c_ref[rows, :] = jnp.exp2(m_prev - m_new)
            x = s - jnp.concatenate([m_new] * (tk // LANES), axis=1)
            p_ref[rows, :] = jnp.exp2(x.astype(BF16))
            m_ref[win, rows, :] = m_new

    def values(j, p_ref, c_ref):
        win, ki, _ = tile(j)
        rows = pl.ds(pl.multiple_of(ki * tk, tk), tk)
        for e in range(2):
            pair = slice(2 * e * tq, 2 * (e + 1) * tq)
            pv = jnp.dot(p_ref[pair, :], vx_ref[2 * win + e, rows, :], preferred_element_type=F32)
            acc_ref[win, pair, :] = c_ref[pair, :] * acc_ref[win, pair, :] + pv

    m_ref[...] = jnp.full(m_ref.shape, -jnp.inf, F32)
    acc_ref[...] = jnp.zeros(acc_ref.shape, F32)
    bufs = [(s_bufs.at[i], p_bufs.at[i], c_bufs.at[i]) for i in range(2)]

    scores(0, bufs[0][0])
    softmax(0, *bufs[0])
    scores(1, bufs[1][0])

    def step(j, carry):
        for parity in range(2):
            cur, nxt = bufs[parity], bufs[1 - parity]

            @pl.when(lax.bitwise_and(j, 1) == parity)
            def _():
                softmax(j, *cur)
                scores(j + 1, nxt[0])
                values(j - 1, nxt[1], nxt[2])
        return carry
    lax.fori_loop(1, n_tiles, step, 0)

    for parity in range(2):
        @pl.when(lax.bitwise_and(n_tiles - 1, 1) == parity)
        def _():
            values(n_tiles - 1, bufs[parity][1], bufs[parity][2])

    lane = lax.broadcasted_iota(jnp.int32, (tq, LANES), 1)
    gates = _gates(gate_ref[0], [gexp_ref[0, g], gexp_ref[1, g]])
    y = None
    for branch, z_ref in enumerate((zs_ref, zw_ref)):
        pairs = []
        for a in range(2):
            acc_e = acc_ref[branch, a * tq:(a + 1) * tq, :]
            acc_o = acc_ref[branch, (2 + a) * tq:(3 + a) * tq, :]
            pairs.append(jnp.where(lane < HEAD_DIM, acc_e / acc_e[:, LANES - 1:LANES],
                                   acc_o / acc_o[:, 0:1]))
        part = jnp.concatenate(pairs, axis=1) * gates[branch] * _silu(z_ref[0].astype(F32))
        y = part if y is None else y + part
    y_ref[0] = y.astype(y_ref.dtype)


def _token_attention(proj, a_tiles, lane_consts, gate_expand, sel, sel_consts):
    b = proj.shape[0]
    tq, tk = ATT_TQ, ATT_TK
    grp_w = HEADS_PER_GROUP * HEAD_DIM
    k_blk = NSA_K_COL // LANES
    v_blk = NSA_V_COL // LANES
    z_blk = NSA_Z_COL // grp_w
    kv_spec = lambda blk: pl.BlockSpec((1, SEQ, LANES), lambda i, g, qi: (i, 0, blk + g))
    z_spec = lambda blk: pl.BlockSpec((1, tq, grp_w), lambda i, g, qi: (i, qi, blk + g))
    whole = pl.BlockSpec(memory_space=pltpu.VMEM)
    return pl.pallas_call(
        _token_kernel,
        grid=(b, GROUPS, SEQ // tq),
        in_specs=[
            pl.BlockSpec((1, tq, grp_w), lambda i, g, qi: (i, qi, NSA_Q_COL // grp_w + g)),
            kv_spec(k_blk), kv_spec(v_blk), kv_spec(k_blk + GROUPS), kv_spec(v_blk + GROUPS),
            pl.BlockSpec((1, tq, LANES), lambda i, g, qi: (i, qi, NSA_G_COL // LANES)),
            z_spec(z_blk + GROUPS), z_spec(z_blk + 2 * GROUPS),
            whole, whole, whole,
            pl.BlockSpec((1, 1, N_SLC, tq), lambda i, g, qi: (i, g, 0, qi)),
            whole, whole, whole,
        ],
        out_specs=pl.BlockSpec((1, tq, grp_w), lambda i, g, qi: (i, qi, g)),
        out_shape=jax.ShapeDtypeStruct((b, SEQ, NSA_WIDTH), BF16),
        scratch_shapes=[
            pltpu.VMEM((2, HEADS_PER_GROUP * tq, LANES), BF16),
            pltpu.VMEM((2, HEADS_PER_GROUP * tq, LANES), F32),
            pltpu.VMEM((2, HEADS_PER_GROUP * tq, LANES), F32),
            pltpu.VMEM((2, HEADS_PER_GROUP * tq, tk), F32),
            pltpu.VMEM((2, HEADS_PER_GROUP * tq, tk), BF16),
            pltpu.VMEM((2, HEADS_PER_GROUP * tq, LANES), F32),
            pltpu.VMEM((3, SEQ // tk, LANES, tk), BF16),
            pltpu.VMEM((4, SEQ, LANES), BF16),
        ],
        compiler_params=_cparams(("parallel", "parallel", "arbitrary")),
        name="token_attention",
    )(proj, proj, proj, proj, proj, proj, proj, proj, a_tiles, lane_consts, gate_expand,
      sel, *sel_consts)


def _retention_kernel(q_ref, k_ref, v_ref, z_ref, cos_ref, sin_ref, inner_ref, xi_ref, zeta_ref,
                      gn_ref, y_ref, state_ref, *, decays):
    @pl.when(pl.program_id(1) == 0)
    def _():
        state_ref[...] = jnp.zeros(state_ref.shape, F32)

    cos = cos_ref[...]
    sin = sin_ref[...]
    half = RET_QK_DIM // 2

    def rot(x):
        x1, x2 = x[:, :half], x[:, half:]
        return jnp.concatenate([x1 * cos - x2 * sin, x1 * sin + x2 * cos], axis=1)

    for h in range(RET_HEADS):
        qs = slice(h * RET_QK_DIM, (h + 1) * RET_QK_DIM)
        vs = slice(h * RET_V_DIM, (h + 1) * RET_V_DIM)
        qr = rot(q_ref[0, :, qs].astype(F32))
        kr = rot(k_ref[0, :, qs].astype(F32)) * (RET_QK_DIM ** -0.5)
        qb = qr.astype(BF16)
        vh = v_ref[0, :, vs]
        attn = lax.dot_general(qb, kr.astype(BF16), _NT, preferred_element_type=F32) * inner_ref[h]
        st = state_ref[h]
        o = (jnp.dot(attn.astype(BF16), vh, preferred_element_type=F32)
             + jnp.dot(qb, st.astype(BF16), preferred_element_type=F32) * xi_ref[h])
        kz = (kr * zeta_ref[h]).astype(BF16)
        state_ref[h] = st * decays[h] + lax.dot_general(kz, vh, _TN, preferred_element_type=F32)
        mu = jnp.mean(o, axis=1, keepdims=True)
        d = o - mu
        var = jnp.mean(d * d, axis=1, keepdims=True)
        on = d * lax.rsqrt(var + GN_EPS) * gn_ref[h]
        y_ref[0, :, vs] = (on * _silu(z_ref[0, :, vs].astype(F32))).astype(y_ref.dtype)


def _retention(proj, gn_gain, tables):
    b = proj.shape[0]
    c = RET_CHUNK
    cos, sin, inner, xi, zeta, decays = tables
    v_blk = 2 * RET_QK_WIDTH // RET_WIDTH
    return pl.pallas_call(
        functools.partial(_retention_kernel, decays=decays),
        grid=(b, SEQ // c),
        in_specs=[
            pl.BlockSpec((1, c, RET_QK_WIDTH), lambda i, j: (i, j, 0)),
            pl.BlockSpec((1, c, RET_QK_WIDTH), lambda i, j: (i, j, 1)),
            pl.BlockSpec((1, c, RET_WIDTH), lambda i, j: (i, j, v_blk)),
            pl.BlockSpec((1, c, RET_WIDTH), lambda i, j: (i, j, v_blk + 1)),
            pl.BlockSpec((c, RET_QK_DIM // 2), lambda i, j: (j, 0)),
            pl.BlockSpec((c, RET_QK_DIM // 2), lambda i, j: (j, 0)),
            pl.BlockSpec((RET_HEADS, c, c), lambda i, j: (0, 0, 0)),
            pl.BlockSpec((RET_HEADS, c, 1), lambda i, j: (0, 0, 0)),
            pl.BlockSpec((RET_HEADS, c, 1), lambda i, j: (0, 0, 0)),
            pl.BlockSpec((RET_HEADS, 1, RET_V_DIM), lambda i, j: (0, 0, 0)),
        ],
        out_specs=pl.BlockSpec((1, c, RET_WIDTH), lambda i, j: (i, j, 0)),
        out_shape=jax.ShapeDtypeStruct((b, SEQ, RET_WIDTH), BF16),
        scratch_shapes=[pltpu.VMEM((RET_HEADS, RET_QK_DIM, RET_V_DIM), F32)],
        compiler_params=_cparams(("parallel", "arbitrary")),
        name="retention",
    )(proj, proj, proj, proj, cos, sin, inner, xi, zeta, gn_gain.reshape(RET_HEADS, 1, RET_V_DIM))


def _t5_bucket_np(dist):
    n = np.maximum(dist, 0)
    max_exact = REL_BUCKETS // 2
    nf = np.maximum(n, 1).astype(np.float64)
    large = max_exact + (np.log(nf / max_exact) / math.log(REL_MAX_DIST / max_exact)
                         * (REL_BUCKETS - max_exact)).astype(np.int64)
    large = np.minimum(large, REL_BUCKETS - 1)
    return np.where(n < max_exact, n, large).astype(np.int32)


def _skewed(vec, rows, stride, cols):
    p = vec.shape[-1]
    lead = vec.shape[:-1]
    flat = jnp.tile(vec, (1,) * len(lead) + (rows,))[..., :rows * (p - stride)]
    return flat.reshape(lead + (rows, p - stride))[..., :cols]


def _nsa_tables(table):
    tq, tk = ATT_TQ, ATT_TK
    assert tq == tk
    tab_t = table.T * LOG2E
    period = 2 * SEQ
    x = np.arange(period)
    ok = (x >= CMP_BLOCK - 1) & (x < SEQ)
    vec = jnp.take(tab_t, jnp.asarray(_t5_bucket_np(x - (CMP_BLOCK - 1))), axis=1)
    vec = jnp.where(jnp.asarray(ok)[None, :], vec, MASKED)
    bias_cmp = _skewed(vec, N_CMP_PAD, CMP_STRIDE, SEQ)
    rel_t = tab_t - tab_t[:, REL_BUCKETS - 1:]
    period = 2 * tq
    x = np.arange(period)
    x = np.where(x < tq, x, x - period)
    tiles = []
    for off in range(3):
        d = off * tk - x
        ok = (d >= 0) & (d < WIN_SIZE) if off == 2 else (d >= 0)
        vec = jnp.take(rel_t, jnp.asarray(_t5_bucket_np(d)), axis=1)
        vec = jnp.where(jnp.asarray(ok)[None, :], vec, MASKED)
        tiles.append(_skewed(vec, tq, 1, tk))
    a_tiles = jnp.stack(tiles + [jnp.zeros_like(tiles[0])])
    cs = np.arange(N_CMP_PAD)[None, :] * CMP_STRIDE
    jj = np.arange(N_SLC)[:, None]
    ovl = ((cs < (jj + 1) * SLC_BLOCK) & (cs + CMP_BLOCK > jj * SLC_BLOCK)
           & (np.arange(N_CMP_PAD)[None, :] < N_CMP))
    ovl = jnp.asarray(ovl.astype(np.float32), BF16)
    lane = np.arange(LANES).reshape(1, 1, 1, LANES)
    key = np.arange(SEQ).reshape(1, SEQ // tk, tk, 1)
    base = np.array([HEAD_DIM, 0]).reshape(2, 1, 1, 1)
    key_blocks = jnp.asarray((lane == base + key // SLC_BLOCK).astype(np.float32), BF16)
    col = np.arange(2 * LANES).reshape(1, -1)
    jb = np.arange(N_SLC).reshape(-1, 1)
    place = (col == HEAD_DIM + jb) | (col == LANES + jb)
    sel_consts = (key_blocks, jnp.asarray(place.astype(np.float32), BF16),
                  jnp.asarray(place.any(axis=0, keepdims=True).astype(np.float32)))
    c = np.arange(LANES).reshape(1, 1, LANES, 1)
    col = np.arange(HEADS_PER_GROUP * HEAD_DIM).reshape(1, 1, 1, -1) // HEAD_DIM
    br = np.arange(3).reshape(3, 1, 1, 1)
    gg = np.arange(GROUPS).reshape(1, GROUPS, 1, 1)
    gate_expand = jnp.asarray((c == br * HEADS + HEADS_PER_GROUP * gg + col).astype(np.float32),
                              BF16)
    return bias_cmp, a_tiles, ovl, sel_consts, _lane_consts(max(tq, CMP_TQ)), gate_expand


def _retention_tables():
    c = RET_CHUNK
    log_g = jnp.log(1.0 - 2.0 ** (-5.0 - jnp.arange(RET_HEADS, dtype=F32)))
    i = jnp.arange(c, dtype=F32)
    diff = i[:, None] - i[None, :]
    inner = jnp.where(diff >= 0, jnp.exp(diff[None] * log_g[:, None, None]), 0.0)
    xi = jnp.exp((i + 1.0)[None, :] * log_g[:, None])[:, :, None]
    zeta = jnp.exp((c - 1.0 - i)[None, :] * log_g[:, None])[:, :, None]
    decays = tuple(float((1.0 - 2.0 ** (-5.0 - h)) ** c) for h in range(RET_HEADS))
    inv = 1.0 / (ROPE_BASE ** jnp.linspace(0.0, 1.0, RET_QK_DIM // 2, dtype=F32))
    ang = jnp.arange(SEQ, dtype=F32)[:, None] * inv[None, :]
    return jnp.cos(ang), jnp.sin(ang), inner, xi, zeta, decays


def _nsa_w_in_layout(w):
    d = w.shape[0]
    q = w[:, :NSA_WIDTH] * (HEAD_DIM ** -0.5 * LOG2E)
    kv0 = NSA_WIDTH
    g0 = kv0 + 6 * KV_WIDTH
    z0 = g0 + 3 * HEADS
    slab = lambda n: w[:, kv0 + n * KV_WIDTH: kv0 + (n + 1) * KV_WIDTH]
    cols = [q, w[:, z0:z0 + 3 * NSA_WIDTH]]
    for n in (2, 4, 3, 5):
        for g in range(GROUPS):
            part = slab(n)[:, g * HEAD_DIM:(g + 1) * HEAD_DIM]
            cols += [part, part]
    cols += [slab(0), slab(1), w[:, g0:z0]]
    used = NSA_G_COL + 3 * HEADS
    cols.append(jnp.zeros((d, NSA_PROJ_PAD - used), w.dtype))
    return jnp.concatenate(cols, axis=1).astype(BF16)


def _nsa_layer(h2d, b, pre_gain, post_gain, w_in, w_out, k_pos, k_w1, k_w2, v_pos, v_w1, v_w2, tabs):
    bias_cmp, a_tiles, ovl, sel_consts, lane_consts, gate_expand = tabs
    proj = _norm_proj(h2d, pre_gain, _nsa_w_in_layout(w_in), NSA_PROJ_TN)
    proj = proj.reshape(b, SEQ, NSA_PROJ_PAD)
    ckv = proj[:, :, NSA_CKV_COL:NSA_CKV_COL + 2 * KV_WIDTH]
    ckv = ckv.reshape(b, N_CMP_PAD, CMP_STRIDE, 2, GROUPS, HEAD_DIM).transpose(0, 3, 4, 1, 2, 5)
    ckv = ckv.reshape(b, 2, GROUPS, N_CMP_PAD, CMP_STRIDE * HEAD_DIM)
    pos = jnp.stack([k_pos, v_pos]).reshape(2, 2, CMP_STRIDE * HEAD_DIM)
    w1 = jnp.stack([k_w1, v_w1]).astype(BF16)
    w2 = jnp.stack([k_w2, v_w2])
    w2d = jnp.concatenate([w2, w2], axis=2).astype(BF16)
    ckv_c = _compress(ckv, pos, w1, w2d)
    y_cmp, sel = _cmp_select(proj, ckv_c, bias_cmp, ovl, lane_consts, gate_expand[0])
    y_tok = _token_attention(proj, a_tiles, lane_consts, gate_expand[1:], sel, sel_consts)
    parts = [y.reshape(b * SEQ, NSA_WIDTH) for y in (y_cmp, y_tok)]
    return _out_post(parts, w_out.astype(BF16), h2d, post_gain)


def _ret_layer(h2d, b, pre_gain, post_gain, w_in, w_out, gn_gain, tabs):
    proj = _norm_proj(h2d, pre_gain, w_in.astype(BF16), RET_PROJ_TN).reshape(b, SEQ, -1)
    y = _retention(proj, gn_gain, tabs)
    return _out_post([y.reshape(b * SEQ, RET_WIDTH)], w_out.astype(BF16), h2d, post_gain)


def kernel(x, pre_norm_gain, post_norm_gain, rel_bias_table, nsa_w_in, nsa_w_out, nsa_cmp_k_pos, nsa_cmp_k_w1, nsa_cmp_k_w2, nsa_cmp_v_pos, nsa_cmp_v_w1, nsa_cmp_v_w2, ret_w_in, ret_w_out, ret_gn_gain):
    b, s, d = x.shape
    assert s == SEQ and d == D_MODEL
    nsa_tabs = _nsa_tables(rel_bias_table)
    ret_tabs = _retention_tables()
    h = x.reshape(b * s, d)
    for layer in range(DEPTH):
        slot = layer // 2
        if layer % 2 == 0:
            h = _nsa_layer(h, b, pre_norm_gain[layer], post_norm_gain[layer], nsa_w_in[slot],
                           nsa_w_out[slot], nsa_cmp_k_pos[slot], nsa_cmp_k_w1[slot],
                           nsa_cmp_k_w2[slot], nsa_cmp_v_pos[slot], nsa_cmp_v_w1[slot],
                           nsa_cmp_v_w2[slot], nsa_tabs)
        else:
            h = _ret_layer(h, b, pre_norm_gain[layer], post_norm_gain[layer], ret_w_in[slot],
                           ret_w_out[slot], ret_gn_gain[slot], ret_tabs)
    return h.reshape(b, s, d)
```

```python
import functools
import math

import numpy as np
import jax
import jax.numpy as jnp
from jax import lax
from jax.experimental import pallas as pl
from jax.experimental.pallas import tpu as pltpu

F32 = jnp.float32
BF16 = jnp.bfloat16

D_MODEL = 1024
SEQ = 2048
DEPTH = 4
RMS_EPS = 1e-6
GN_EPS = 1e-6
MASKED = -1e30
LOG2E = math.log2(math.e)

HEADS = 16
HEAD_DIM = 64
GROUPS = 4
HEADS_PER_GROUP = HEADS // GROUPS
NSA_WIDTH = HEADS * HEAD_DIM
KV_WIDTH = GROUPS * HEAD_DIM
CMP_BLOCK = 32
CMP_STRIDE = 16
CMP_HIDDEN = 256
N_CMP = (SEQ - CMP_BLOCK) // CMP_STRIDE + 1
N_CMP_PAD = 128
SLC_BLOCK = 64
N_SLC = SEQ // SLC_BLOCK
SLC_TOPN = 16
WIN_SIZE = 512
FORCED_SCORE = 1e3
REL_BUCKETS = 32
REL_MAX_DIST = 128

RET_HEADS = 4
RET_QK_DIM = 256
RET_V_DIM = 512
RET_QK_WIDTH = RET_HEADS * RET_QK_DIM
RET_WIDTH = RET_HEADS * RET_V_DIM
ROPE_BASE = 10000.0

LANES = 128
VMEM_LIMIT_BYTES = 56 * 1024 * 1024

PROJ_TM = 1024
RET_PROJ_TN = 2048
NSA_PROJ_TN = 3456
POST_TM = 512
ATT_TQ = 256
ATT_TK = 256
CMP_TQ = 1024
RET_CHUNK = 256

NSA_Q_COL = 0
NSA_Z_COL = 1024
NSA_K_COL = 4096
NSA_V_COL = 5120
NSA_CKV_COL = 6144
NSA_G_COL = 6656
NSA_PROJ_PAD = 6912

_NT = (((1,), (1,)), ((), ()))
_TN = (((0,), (0,)), ((), ()))


def _cparams(sem):
    return pltpu.CompilerParams(dimension_semantics=sem, vmem_limit_bytes=VMEM_LIMIT_BYTES)


def _norm_proj_kernel(x_ref, g_ref, w_ref, o_ref, xn_ref):
    @pl.when(pl.program_id(1) == 0)
    def _():
        x = x_ref[...]
        ms = jnp.mean(x * x, axis=-1, keepdims=True)
        xn_ref[...] = (x * lax.rsqrt(ms + RMS_EPS) * g_ref[...]).astype(BF16)

    o_ref[...] = jnp.dot(xn_ref[...], w_ref[...], preferred_element_type=F32).astype(o_ref.dtype)


def _norm_proj(x2d, gain, w_bf16, tn):
    m, d = x2d.shape
    n = w_bf16.shape[1]
    assert n % tn == 0
    return pl.pallas_call(
        _norm_proj_kernel,
        grid=(m // PROJ_TM, n // tn),
        in_specs=[
            pl.BlockSpec((PROJ_TM, d), lambda i, j: (i, 0)),
            pl.BlockSpec((1, d), lambda i, j: (0, 0)),
            pl.BlockSpec((d, tn), lambda i, j: (0, j)),
        ],
        out_specs=pl.BlockSpec((PROJ_TM, tn), lambda i, j: (i, j)),
        out_shape=jax.ShapeDtypeStruct((m, n), BF16),
        scratch_shapes=[pltpu.VMEM((PROJ_TM, d), BF16)],
        compiler_params=_cparams(("parallel", "arbitrary")),
        name="norm_proj",
    )(x2d, gain.reshape(1, d), w_bf16)


def _out_post_kernel(*refs, n_parts):
    y_refs = refs[:n_parts]
    w_ref, h_ref, g_ref, o_ref = refs[n_parts:]
    y = y_refs[0][...].astype(F32)
    for r in y_refs[1:]:
        y = y + r[...].astype(F32)
    t = jnp.dot(y.astype(BF16), w_ref[...], preferred_element_type=F32)
    ms = jnp.mean(t * t, axis=-1, keepdims=True)
    o_ref[...] = h_ref[...] + t * lax.rsqrt(ms + RMS_EPS) * g_ref[...]


def _out_post(parts, w_bf16, h2d, gain):
    m, d = h2d.shape
    k = w_bf16.shape[0]
    n_parts = len(parts)
    return pl.pallas_call(
        functools.partial(_out_post_kernel, n_parts=n_parts),
        grid=(m // POST_TM,),
        in_specs=[pl.BlockSpec((POST_TM, k), lambda i: (i, 0)) for _ in parts] + [
            pl.BlockSpec((k, d), lambda i: (0, 0)),
            pl.BlockSpec((POST_TM, d), lambda i: (i, 0)),
            pl.BlockSpec((1, d), lambda i: (0, 0)),
        ],
        out_specs=pl.BlockSpec((POST_TM, d), lambda i: (i, 0)),
        out_shape=jax.ShapeDtypeStruct((m, d), F32),
        compiler_params=_cparams(("parallel",)),
        name="out_post",
    )(*parts, w_bf16, h2d, gain.reshape(1, d))


def _compress_kernel(x_ref, pos_ref, w1_ref, w2_ref, o_ref):
    half = CMP_STRIDE * HEAD_DIM
    for which in range(2):
        x = x_ref[0, which, 0].astype(F32)
        xa = (x + pos_ref[which, 0:1, :]).astype(BF16)
        xb = (x + pos_ref[which, 1:2, :]).astype(BF16)
        pa = jnp.dot(xa, w1_ref[which, :half, :], preferred_element_type=F32)
        pb = jnp.dot(xb, w1_ref[which, half:, :], preferred_element_type=F32)
        hid = pa + pltpu.roll(pb, N_CMP_PAD - 1, 0)
        hid = hid * jax.nn.sigmoid(hid)
        o_ref[0, which, 0] = jnp.dot(hid.astype(BF16), w2_ref[which],
                                     preferred_element_type=F32).astype(o_ref.dtype)


def _compress(ckv_rows, pos, w1, w2d):
    b = ckv_rows.shape[0]
    row_w = CMP_STRIDE * HEAD_DIM
    return pl.pallas_call(
        _compress_kernel,
        grid=(b, GROUPS),
        in_specs=[
            pl.BlockSpec((1, 2, 1, N_CMP_PAD, row_w), lambda i, g: (i, 0, g, 0, 0)),
            pl.BlockSpec((2, 2, row_w), lambda i, g: (0, 0, 0)),
            pl.BlockSpec((2, 2 * row_w, CMP_HIDDEN), lambda i, g: (0, 0, 0)),
            pl.BlockSpec((2, CMP_HIDDEN, 2 * HEAD_DIM), lambda i, g: (0, 0, 0)),
        ],
        out_specs=pl.BlockSpec((1, 2, 1, N_CMP_PAD, 2 * HEAD_DIM), lambda i, g: (i, 0, g, 0, 0)),
        out_shape=jax.ShapeDtypeStruct((b, 2, GROUPS, N_CMP_PAD, 2 * HEAD_DIM), BF16),
        compiler_params=_cparams(("parallel", "parallel")),
        name="compress",
    )(ckv_rows, pos, w1, w2d)


def _lane_consts(rows):
    lane = np.arange(LANES)
    lo = (lane < HEAD_DIM).astype(np.float32)
    hi = (lane >= HEAD_DIM).astype(np.float32)
    last = (lane == LANES - 1).astype(np.float32)
    first = (lane == 0).astype(np.float32)
    c = np.stack([lo, hi, 1.0 - last, last, 1.0 - first, first])
    return jnp.asarray(np.broadcast_to(c[:, None, :], (6, rows, LANES)), BF16)


def _bf16_terms(x, n):
    terms = []
    for _ in range(n - 1):
        t = x.astype(BF16)
        terms.append(t)
        x = x - t.astype(F32)
    terms.append(x.astype(BF16))
    return terms


def _gates(gate_logits, expands):
    terms = _bf16_terms(jax.nn.sigmoid(gate_logits.astype(F32)), 2)
    return [sum(jnp.dot(t, ex, preferred_element_type=F32) for t in terms) for ex in expands]


def _silu(x):
    return x * jax.nn.sigmoid(x)


def _cmp_select_kernel(q_ref, gate_ref, z_ref, ckv_ref, bias_ref, ovl_ref, lc_ref, gexp_ref,
                       y_ref, sel_ref):
    g = pl.program_id(0)
    qi = pl.program_id(1)
    tq = q_ref.shape[1]
    q = q_ref[0]
    kk = ckv_ref[0, 0, 0]
    vv = ckv_ref[0, 1, 0]
    lane = lax.broadcasted_iota(jnp.int32, (tq, LANES), 1)

    psum = jnp.zeros((N_CMP_PAD, tq), F32)
    pairs = []
    for a in range(2):
        qp = q[:, a * LANES:(a + 1) * LANES]
        outs = []
        for e in range(2):
            s = lax.dot_general(kk, qp * lc_ref[e, :tq], _NT, preferred_element_type=F32)
            bias = bias_ref[2 * a + e]
            s = s + bias
            m = jnp.max(s, axis=0, keepdims=True)
            p = jnp.exp2(s - m)
            p = jnp.where(bias > 0.5 * MASKED, p / jnp.sum(p, axis=0, keepdims=True), 0.0)
            psum = psum + p
            outs.append(lax.dot_general(p.astype(BF16), vv, _TN, preferred_element_type=F32))
        pairs.append(jnp.where(lane < HEAD_DIM, outs[0], outs[1]))
    y_ref[0] = (jnp.concatenate(pairs, axis=1) * _gates(gate_ref[0], [gexp_ref[g]])[0]
                * _silu(z_ref[0].astype(F32))).astype(y_ref.dtype)

    ovl = ovl_ref[...]
    imp = sum(jnp.dot(ovl, t, preferred_element_type=F32) for t in _bf16_terms(psum, 3))
    t = qi * tq + lax.broadcasted_iota(jnp.int32, (N_SLC, tq), 1)
    blk = lax.broadcasted_iota(jnp.int32, (N_SLC, tq), 0)
    cur = lax.shift_right_logical(t, int(math.log2(SLC_BLOCK)))
    forced = (blk == 0) | (blk == cur) | (blk == cur - 1)
    score = jnp.where(blk * SLC_BLOCK <= t, imp + jnp.where(forced, FORCED_SCORE, 0.0), MASKED)
    cnt = jnp.zeros((N_SLC, tq), jnp.int32)
    for jp in range(N_SLC):
        row = score[jp:jp + 1, :]
        beats = (row > score) | ((row == score) & (blk > jp))
        cnt = cnt + beats.astype(jnp.int32)
    sel_ref[0, 0] = jnp.where(cnt < SLC_TOPN, 1.0, 0.0).astype(sel_ref.dtype)


def _cmp_select(proj, ckv, bias_cmp, ovl, lane_consts, gate_expand):
    b = proj.shape[0]
    tq = CMP_TQ
    grp_w = HEADS_PER_GROUP * HEAD_DIM
    return pl.pallas_call(
        _cmp_select_kernel,
        grid=(GROUPS, SEQ // tq, b),
        in_specs=[
            pl.BlockSpec((1, tq, grp_w), lambda g, qi, i: (i, qi, NSA_Q_COL // grp_w + g)),
            pl.BlockSpec((1, tq, LANES), lambda g, qi, i: (i, qi, NSA_G_COL // LANES)),
            pl.BlockSpec((1, tq, grp_w), lambda g, qi, i: (i, qi, NSA_Z_COL // grp_w + g)),
            pl.BlockSpec((1, 2, 1, N_CMP_PAD, 2 * HEAD_DIM), lambda g, qi, i: (i, 0, g, 0, 0)),
            pl.BlockSpec((HEADS_PER_GROUP, N_CMP_PAD, tq), lambda g, qi, i: (g, 0, qi)),
            pl.BlockSpec((N_SLC, N_CMP_PAD), lambda g, qi, i: (0, 0)),
            pl.BlockSpec(memory_space=pltpu.VMEM),
            pl.BlockSpec(memory_space=pltpu.VMEM),
        ],
        out_specs=[
            pl.BlockSpec((1, tq, grp_w), lambda g, qi, i: (i, qi, g)),
            pl.BlockSpec((1, 1, N_SLC, tq), lambda g, qi, i: (i, g, 0, qi)),
        ],
        out_shape=[
            jax.ShapeDtypeStruct((b, SEQ, NSA_WIDTH), BF16),
            jax.ShapeDtypeStruct((b, GROUPS, N_SLC, SEQ), BF16),
        ],
        compiler_params=_cparams(("parallel", "parallel", "parallel")),
        name="cmp_select",
    )(proj, proj, proj, ckv, bias_cmp, ovl, lane_consts, gate_expand)


def _token_kernel(q_ref, ks_ref, vs_ref, kw_ref, vw_ref, gate_ref, zs_ref, zw_ref, a_ref, lc_ref,
                  gexp_ref, sel_ref, kblk_ref, place_ref, placed_ref, y_ref,
                  qm_ref, m_ref, acc_ref, s_bufs, p_bufs, c_bufs, kx_ref, vx_ref):
    g = pl.program_id(1)
    qi = pl.program_id(2)
    tq = q_ref.shape[1]
    tk = ATT_TK
    n_slc = qi + 1
    n_tiles = n_slc + jnp.minimum(qi, WIN_SIZE // tk) + 1

    q = q_ref[0]
    flags = lax.dot_general(sel_ref[0, 0], place_ref[...], _TN, preferred_element_type=F32)
    fill = ((placed_ref[...] - flags) * MASKED).astype(BF16)
    for slot in range(HEADS_PER_GROUP):
        e, a = divmod(slot, 2)
        qm = q[:, a * LANES:(a + 1) * LANES] * lc_ref[e, :tq]
        qm_ref[0, slot * tq:(slot + 1) * tq, :] = qm + fill[:, e * LANES:(e + 1) * LANES]
        qm_ref[1, slot * tq:(slot + 1) * tq, :] = qm

    @pl.when(qi == 0)
    def _():
        for c in range(SEQ // tk):
            rows = slice(c * tk, (c + 1) * tk)
            for branch, v_ref in enumerate((vs_ref, vw_ref)):
                vv = v_ref[0, rows, :]
                vx_ref[2 * branch, rows, :] = vv * lc_ref[2, :tk] + lc_ref[3, :tk]
                vx_ref[2 * branch + 1, rows, :] = vv * lc_ref[4, :tk] + lc_ref[5, :tk]
            kk = ks_ref[0, rows, :]
            for e in range(2):
                kx_ref[e, c] = (kk * lc_ref[e, :tk] + kblk_ref[e, c]).astype(F32).T.astype(BF16)
            kx_ref[2, c] = kw_ref[0, rows, :].astype(F32).T.astype(BF16)

    def tile(j):
        win = (j >= n_slc).astype(jnp.int32)
        t = j - win * n_slc
        return win, jnp.clip(qi - t, 0, qi), jnp.where(win == 1, t, jnp.where(t < 2, t, 3))

    def scores(j, s_ref):
        win, ki, _ = tile(j)
        for e in range(2):
            pair = slice(2 * e * tq, 2 * (e + 1) * tq)
            s_ref[pair, :] = jnp.dot(qm_ref[win, pair, :], kx_ref[jnp.where(win == 1, 2, e), ki],
                                     preferred_element_type=F32)

    def softmax(j, s_ref, p_ref, c_ref):
        win, _, kind = tile(j)
        for slot in range(HEADS_PER_GROUP):
            e, a = divmod(slot, 2)
            rows = slice(slot * tq, (slot + 1) * tq)
            s = s_ref[rows, :] + a_ref[kind, HEADS_PER_GROUP * g + 2 * a + e]
            m_prev = m_ref[win, rows, :]
            m_new = jnp.maximum(m_prev, jnp.max(s, axis=1, keepdims=True))
            c_ref[rows, :] = jnp.exp2(m_prev - m_new)
            x = s - jnp.concatenate([m_new] * (tk // LANES), axis=1)
            p_ref[rows, :] = jnp.exp2(x.astype(BF16))
            m_ref[win, rows, :] = m_new

    def values(j, p_ref, c_ref):
        win, ki, _ = tile(j)
        rows = pl.ds(pl.multiple_of(ki * tk, tk), tk)
        for e in range(2):
            pair = slice(2 * e * tq, 2 * (e + 1) * tq)
            pv = jnp.dot(p_ref[pair, :], vx_ref[2 * win + e, rows, :], preferred_element_type=F32)
            acc_ref[win, pair, :] = c_ref[pair, :] * acc_ref[win, pair, :] + pv

    m_ref[...] = jnp.full(m_ref.shape, -jnp.inf, F32)
    acc_ref[...] = jnp.zeros(acc_ref.shape, F32)
    bufs = [(s_bufs.at[i], p_bufs.at[i], c_bufs.at[i]) for i in range(2)]

    scores(0, bufs[0][0])
    softmax(0, *bufs[0])
    scores(1, bufs[1][0])

    def step(j, parity):
        cur, nxt = bufs[parity], bufs[1 - parity]
        softmax(j, *cur)
        scores(j + 1, nxt[0])
        values(j - 1, nxt[1], nxt[2])

    def two_steps(i, carry):
        step(2 * i + 1, 1)
        step(2 * i + 2, 0)
        return carry
    lax.fori_loop(0, lax.shift_right_logical(n_tiles - 1, 1), two_steps, 0)

    @pl.when(lax.bitwise_and(n_tiles - 1, 1) == 1)
    def _():
        step(n_tiles - 1, 1)

    for parity in range(2):
        @pl.when(lax.bitwise_and(n_tiles - 1, 1) == parity)
        def _():
            values(n_tiles - 1, bufs[parity][1], bufs[parity][2])

    lane = lax.broadcasted_iota(jnp.int32, (tq, LANES), 1)
    gates = _gates(gate_ref[0], [gexp_ref[0, g], gexp_ref[1, g]])
    y = None
    for branch, z_ref in enumerate((zs_ref, zw_ref)):
        pairs = []
        for a in range(2):
            acc_e = acc_ref[branch, a * tq:(a + 1) * tq, :]
            acc_o = acc_ref[branch, (2 + a) * tq:(3 + a) * tq, :]
            pairs.append(jnp.where(lane < HEAD_DIM, acc_e / acc_e[:, LANES - 1:LANES],
                                   acc_o / acc_o[:, 0:1]))
        part = jnp.concatenate(pairs, axis=1) * gates[branch] * _silu(z_ref[0].astype(F32))
        y = part if y is None else y + part
    y_ref[0] = y.astype(y_ref.dtype)


def _token_attention(proj, a_tiles, lane_consts, gate_expand, sel, sel_consts):
    b = proj.shape[0]
    tq, tk = ATT_TQ, ATT_TK
    grp_w = HEADS_PER_GROUP * HEAD_DIM
    k_blk = NSA_K_COL // LANES
    v_blk = NSA_V_COL // LANES
    z_blk = NSA_Z_COL // grp_w
    kv_spec = lambda blk: pl.BlockSpec((1, SEQ, LANES), lambda i, g, qi: (i, 0, blk + g))
    z_spec = lambda blk: pl.BlockSpec((1, tq, grp_w), lambda i, g, qi: (i, qi, blk + g))
    whole = pl.BlockSpec(memory_space=pltpu.VMEM)
    return pl.pallas_call(
        _token_kernel,
        grid=(b, GROUPS, SEQ // tq),
        in_specs=[
            pl.BlockSpec((1, tq, grp_w), lambda i, g, qi: (i, qi, NSA_Q_COL // grp_w + g)),
            kv_spec(k_blk), kv_spec(v_blk), kv_spec(k_blk + GROUPS), kv_spec(v_blk + GROUPS),
            pl.BlockSpec((1, tq, LANES), lambda i, g, qi: (i, qi, NSA_G_COL // LANES)),
            z_spec(z_blk + GROUPS), z_spec(z_blk + 2 * GROUPS),
            whole, whole, whole,
            pl.BlockSpec((1, 1, N_SLC, tq), lambda i, g, qi: (i, g, 0, qi)),
            whole, whole, whole,
        ],
        out_specs=pl.BlockSpec((1, tq, grp_w), lambda i, g, qi: (i, qi, g)),
        out_shape=jax.ShapeDtypeStruct((b, SEQ, NSA_WIDTH), BF16),
        scratch_shapes=[
            pltpu.VMEM((2, HEADS_PER_GROUP * tq, LANES), BF16),
            pltpu.VMEM((2, HEADS_PER_GROUP * tq, LANES), F32),
            pltpu.VMEM((2, HEADS_PER_GROUP * tq, LANES), F32),
            pltpu.VMEM((2, HEADS_PER_GROUP * tq, tk), F32),
            pltpu.VMEM((2, HEADS_PER_GROUP * tq, tk), BF16),
            pltpu.VMEM((2, HEADS_PER_GROUP * tq, LANES), F32),
            pltpu.VMEM((3, SEQ // tk, LANES, tk), BF16),
            pltpu.VMEM((4, SEQ, LANES), BF16),
        ],
        compiler_params=_cparams(("parallel", "parallel", "arbitrary")),
        name="token_attention",
    )(proj, proj, proj, proj, proj, proj, proj, proj, a_tiles, lane_consts, gate_expand,
      sel, *sel_consts)


def _retention_kernel(q_ref, k_ref, v_ref, z_ref, cos_ref, sin_ref, inner_ref, xi_ref, zeta_ref,
                      gn_ref, y_ref, state_ref, *, decays):
    @pl.when(pl.program_id(1) == 0)
    def _():
        state_ref[...] = jnp.zeros(state_ref.shape, F32)

    cos = cos_ref[...]
    sin = sin_ref[...]
    half = RET_QK_DIM // 2

    def rot(x):
        x1, x2 = x[:, :half], x[:, half:]
        return jnp.concatenate([x1 * cos - x2 * sin, x1 * sin + x2 * cos], axis=1)

    for h in range(RET_HEADS):
        qs = slice(h * RET_QK_DIM, (h + 1) * RET_QK_DIM)
        vs = slice(h * RET_V_DIM, (h + 1) * RET_V_DIM)
        qr = rot(q_ref[0, :, qs].astype(F32))
        kr = rot(k_ref[0, :, qs].astype(F32)) * (RET_QK_DIM ** -0.5)
        qb = qr.astype(BF16)
        vh = v_ref[0, :, vs]
        attn = lax.dot_general(qb, kr.astype(BF16), _NT, preferred_element_type=F32) * inner_ref[h]
        st = state_ref[h]
        o = (jnp.dot(attn.astype(BF16), vh, preferred_element_type=F32)
             + jnp.dot(qb, st.astype(BF16), preferred_element_type=F32) * xi_ref[h])
        kz = (kr * zeta_ref[h]).astype(BF16)
        state_ref[h] = st * decays[h] + lax.dot_general(kz, vh, _TN, preferred_element_type=F32)
        mu = jnp.mean(o, axis=1, keepdims=True)
        d = o - mu
        var = jnp.mean(d * d, axis=1, keepdims=True)
        on = d * lax.rsqrt(var + GN_EPS) * gn_ref[h]
        y_ref[0, :, vs] = (on * _silu(z_ref[0, :, vs].astype(F32))).astype(y_ref.dtype)


def _retention(proj, gn_gain, tables):
    b = proj.shape[0]
    c = RET_CHUNK
    cos, sin, inner, xi, zeta, decays = tables
    v_blk = 2 * RET_QK_WIDTH // RET_WIDTH
    return pl.pallas_call(
        functools.partial(_retention_kernel, decays=decays),
        grid=(b, SEQ // c),
        in_specs=[
            pl.BlockSpec((1, c, RET_QK_WIDTH), lambda i, j: (i, j, 0)),
            pl.BlockSpec((1, c, RET_QK_WIDTH), lambda i, j: (i, j, 1)),
            pl.BlockSpec((1, c, RET_WIDTH), lambda i, j: (i, j, v_blk)),
            pl.BlockSpec((1, c, RET_WIDTH), lambda i, j: (i, j, v_blk + 1)),
            pl.BlockSpec((c, RET_QK_DIM // 2), lambda i, j: (j, 0)),
            pl.BlockSpec((c, RET_QK_DIM // 2), lambda i, j: (j, 0)),
            pl.BlockSpec((RET_HEADS, c, c), lambda i, j: (0, 0, 0)),
            pl.BlockSpec((RET_HEADS, c, 1), lambda i, j: (0, 0, 0)),
            pl.BlockSpec((RET_HEADS, c, 1), lambda i, j: (0, 0, 0)),
            pl.BlockSpec((RET_HEADS, 1, RET_V_DIM), lambda i, j: (0, 0, 0)),
        ],
        out_specs=pl.BlockSpec((1, c, RET_WIDTH), lambda i, j: (i, j, 0)),
        out_shape=jax.ShapeDtypeStruct((b, SEQ, RET_WIDTH), BF16),
        scratch_shapes=[pltpu.VMEM((RET_HEADS, RET_QK_DIM, RET_V_DIM), F32)],
        compiler_params=_cparams(("parallel", "arbitrary")),
        name="retention",
    )(proj, proj, proj, proj, cos, sin, inner, xi, zeta, gn_gain.reshape(RET_HEADS, 1, RET_V_DIM))


def _t5_bucket_np(dist):
    n = np.maximum(dist, 0)
    max_exact = REL_BUCKETS // 2
    nf = np.maximum(n, 1).astype(np.float64)
    large = max_exact + (np.log(nf / max_exact) / math.log(REL_MAX_DIST / max_exact)
                         * (REL_BUCKETS - max_exact)).astype(np.int64)
    large = np.minimum(large, REL_BUCKETS - 1)
    return np.where(n < max_exact, n, large).astype(np.int32)


def _skewed(vec, rows, stride, cols):
    p = vec.shape[-1]
    lead = vec.shape[:-1]
    flat = jnp.tile(vec, (1,) * len(lead) + (rows,))[..., :rows * (p - stride)]
    return flat.reshape(lead + (rows, p - stride))[..., :cols]


def _nsa_tables(table):
    tq, tk = ATT_TQ, ATT_TK
    assert tq == tk
    tab_t = table.T * LOG2E
    period = 2 * SEQ
    x = np.arange(period)
    ok = (x >= CMP_BLOCK - 1) & (x < SEQ)
    vec = jnp.take(tab_t, jnp.asarray(_t5_bucket_np(x - (CMP_BLOCK - 1))), axis=1)
    vec = jnp.where(jnp.asarray(ok)[None, :], vec, MASKED)
    bias_cmp = _skewed(vec, N_CMP_PAD, CMP_STRIDE, SEQ)
    rel_t = tab_t - tab_t[:, REL_BUCKETS - 1:]
    period = 2 * tq
    x = np.arange(period)
    x = np.where(x < tq, x, x - period)
    tiles = []
    for off in range(3):
        d = off * tk - x
        ok = (d >= 0) & (d < WIN_SIZE) if off == 2 else (d >= 0)
        vec = jnp.take(rel_t, jnp.asarray(_t5_bucket_np(d)), axis=1)
        vec = jnp.where(jnp.asarray(ok)[None, :], vec, MASKED)
        tiles.append(_skewed(vec, tq, 1, tk))
    a_tiles = jnp.stack(tiles + [jnp.zeros_like(tiles[0])])
    cs = np.arange(N_CMP_PAD)[None, :] * CMP_STRIDE
    jj = np.arange(N_SLC)[:, None]
    ovl = ((cs < (jj + 1) * SLC_BLOCK) & (cs + CMP_BLOCK > jj * SLC_BLOCK)
           & (np.arange(N_CMP_PAD)[None, :] < N_CMP))
    ovl = jnp.asarray(ovl.astype(np.float32), BF16)
    lane = np.arange(LANES).reshape(1, 1, 1, LANES)
    key = np.arange(SEQ).reshape(1, SEQ // tk, tk, 1)
    base = np.array([HEAD_DIM, 0]).reshape(2, 1, 1, 1)
    key_blocks = jnp.asarray((lane == base + key // SLC_BLOCK).astype(np.float32), BF16)
    col = np.arange(2 * LANES).reshape(1, -1)
    jb = np.arange(N_SLC).reshape(-1, 1)
    place = (col == HEAD_DIM + jb) | (col == LANES + jb)
    sel_consts = (key_blocks, jnp.asarray(place.astype(np.float32), BF16),
                  jnp.asarray(place.any(axis=0, keepdims=True).astype(np.float32)))
    c = np.arange(LANES).reshape(1, 1, LANES, 1)
    col = np.arange(HEADS_PER_GROUP * HEAD_DIM).reshape(1, 1, 1, -1) // HEAD_DIM
    br = np.arange(3).reshape(3, 1, 1, 1)
    gg = np.arange(GROUPS).reshape(1, GROUPS, 1, 1)
    gate_expand = jnp.asarray((c == br * HEADS + HEADS_PER_GROUP * gg + col).astype(np.float32),
                              BF16)
    return bias_cmp, a_tiles, ovl, sel_consts, _lane_consts(max(tq, CMP_TQ)), gate_expand


def _retention_tables():
    c = RET_CHUNK
    log_g = jnp.log(1.0 - 2.0 ** (-5.0 - jnp.arange(RET_HEADS, dtype=F32)))
    i = jnp.arange(c, dtype=F32)
    diff = i[:, None] - i[None, :]
    inner = jnp.where(diff >= 0, jnp.exp(diff[None] * log_g[:, None, None]), 0.0)
    xi = jnp.exp((i + 1.0)[None, :] * log_g[:, None])[:, :, None]
    zeta = jnp.exp((c - 1.0 - i)[None, :] * log_g[:, None])[:, :, None]
    decays = tuple(float((1.0 - 2.0 ** (-5.0 - h)) ** c) for h in range(RET_HEADS))
    inv = 1.0 / (ROPE_BASE ** jnp.linspace(0.0, 1.0, RET_QK_DIM // 2, dtype=F32))
    ang = jnp.arange(SEQ, dtype=F32)[:, None] * inv[None, :]
    return jnp.cos(ang), jnp.sin(ang), inner, xi, zeta, decays


def _nsa_w_in_layout(w):
    d = w.shape[0]
    q = w[:, :NSA_WIDTH] * (HEAD_DIM ** -0.5 * LOG2E)
    kv0 = NSA_WIDTH
    g0 = kv0 + 6 * KV_WIDTH
    z0 = g0 + 3 * HEADS
    slab = lambda n: w[:, kv0 + n * KV_WIDTH: kv0 + (n + 1) * KV_WIDTH]
    cols = [q, w[:, z0:z0 + 3 * NSA_WIDTH]]
    for n in (2, 4, 3, 5):
        for g in range(GROUPS):
            part = slab(n)[:, g * HEAD_DIM:(g + 1) * HEAD_DIM]
            cols += [part, part]
    cols += [slab(0), slab(1), w[:, g0:z0]]
    used = NSA_G_COL + 3 * HEADS
    cols.append(jnp.zeros((d, NSA_PROJ_PAD - used), w.dtype))
    return jnp.concatenate(cols, axis=1).astype(BF16)


def _nsa_layer(h2d, b, pre_gain, post_gain, w_in, w_out, k_pos, k_w1, k_w2, v_pos, v_w1, v_w2, tabs):
    bias_cmp, a_tiles, ovl, sel_consts, lane_consts, gate_expand = tabs
    proj = _norm_proj(h2d, pre_gain, _nsa_w_in_layout(w_in), NSA_PROJ_TN)
    proj = proj.reshape(b, SEQ, NSA_PROJ_PAD)
    ckv = proj[:, :, NSA_CKV_COL:NSA_CKV_COL + 2 * KV_WIDTH]
    ckv = ckv.reshape(b, N_CMP_PAD, CMP_STRIDE, 2, GROUPS, HEAD_DIM).transpose(0, 3, 4, 1, 2, 5)
    ckv = ckv.reshape(b, 2, GROUPS, N_CMP_PAD, CMP_STRIDE * HEAD_DIM)
    pos = jnp.stack([k_pos, v_pos]).reshape(2, 2, CMP_STRIDE * HEAD_DIM)
    w1 = jnp.stack([k_w1, v_w1]).astype(BF16)
    w2 = jnp.stack([k_w2, v_w2])
    w2d = jnp.concatenate([w2, w2], axis=2).astype(BF16)
    ckv_c = _compress(ckv, pos, w1, w2d)
    y_cmp, sel = _cmp_select(proj, ckv_c, bias_cmp, ovl, lane_consts, gate_expand[0])
    y_tok = _token_attention(proj, a_tiles, lane_consts, gate_expand[1:], sel, sel_consts)
    parts = [y.reshape(b * SEQ, NSA_WIDTH) for y in (y_cmp, y_tok)]
    return _out_post(parts, w_out.astype(BF16), h2d, post_gain)


def _ret_layer(h2d, b, pre_gain, post_gain, w_in, w_out, gn_gain, tabs):
    proj = _norm_proj(h2d, pre_gain, w_in.astype(BF16), RET_PROJ_TN).reshape(b, SEQ, -1)
    y = _retention(proj, gn_gain, tabs)
    return _out_post([y.reshape(b * SEQ, RET_WIDTH)], w_out.astype(BF16), h2d, post_gain)


def kernel(x, pre_norm_gain, post_norm_gain, rel_bias_table, nsa_w_in, nsa_w_out, nsa_cmp_k_pos, nsa_cmp_k_w1, nsa_cmp_k_w2, nsa_cmp_v_pos, nsa_cmp_v_w1, nsa_cmp_v_w2, ret_w_in, ret_w_out, ret_gn_gain):
    b, s, d = x.shape
    assert s == SEQ and d == D_MODEL
    nsa_tabs = _nsa_tables(rel_bias_table)
    ret_tabs = _retention_tables()
    h = x.reshape(b * s, d)
    for layer in range(DEPTH):
        slot = layer // 2
        if layer % 2 == 0:
            h = _nsa_layer(h, b, pre_norm_gain[layer], post_norm_gain[layer], nsa_w_in[slot],
                           nsa_w_out[slot], nsa_cmp_k_pos[slot], nsa_cmp_k_w1[slot],
                           nsa_cmp_k_w2[slot], nsa_cmp_v_pos[slot], nsa_cmp_v_w1[slot],
                           nsa_cmp_v_w2[slot], nsa_tabs)
        else:
            h = _ret_layer(h, b, pre_norm_gain[layer], post_norm_gain[layer], ret_w_in[slot],
                           ret_w_out[slot], ret_gn_gain[slot], ret_tabs)
    return h.reshape(b, s, d)
```

```python
import functools
import math

import numpy as np
import jax
import jax.numpy as jnp
from jax import lax
from jax.experimental import pallas as pl
from jax.experimental.pallas import tpu as pltpu

F32 = jnp.float32
BF16 = jnp.bfloat16

D_MODEL = 1024
SEQ = 2048
DEPTH = 4
RMS_EPS = 1e-6
GN_EPS = 1e-6
MASKED = -1e30
LOG2E = math.log2(math.e)

HEADS = 16
HEAD_DIM = 64
GROUPS = 4
HEADS_PER_GROUP = HEADS // GROUPS
NSA_WIDTH = HEADS * HEAD_DIM
KV_WIDTH = GROUPS * HEAD_DIM
CMP_BLOCK = 32
CMP_STRIDE = 16
CMP_HIDDEN = 256
N_CMP = (SEQ - CMP_BLOCK) // CMP_STRIDE + 1
N_CMP_PAD = 128
SLC_BLOCK = 64
N_SLC = SEQ // SLC_BLOCK
SLC_TOPN = 16
WIN_SIZE = 512
FORCED_SCORE = 1e3
REL_BUCKETS = 32
REL_MAX_DIST = 128

RET_HEADS = 4
RET_QK_DIM = 256
RET_V_DIM = 512
RET_QK_WIDTH = RET_HEADS * RET_QK_DIM
RET_WIDTH = RET_HEADS * RET_V_DIM
ROPE_BASE = 10000.0

LANES = 128
VMEM_LIMIT_BYTES = 56 * 1024 * 1024

PROJ_TM = 1024
RET_PROJ_TN = 2048
NSA_PROJ_TN = 3456
POST_TM = 512
ATT_TQ = 256
ATT_TK = 256
CMP_TQ = 1024
RET_CHUNK = 256

NSA_Q_COL = 0
NSA_Z_COL = 1024
NSA_K_COL = 4096
NSA_V_COL = 5120
NSA_CKV_COL = 6144
NSA_G_COL = 6656
NSA_PROJ_PAD = 6912

_NT = (((1,), (1,)), ((), ()))
_TN = (((0,), (0,)), ((), ()))


def _cparams(sem):
    return pltpu.CompilerParams(dimension_semantics=sem, vmem_limit_bytes=VMEM_LIMIT_BYTES)


def _norm_proj_kernel(x_ref, g_ref, w_ref, o_ref, xn_ref):
    @pl.when(pl.program_id(1) == 0)
    def _():
        x = x_ref[...]
        ms = jnp.mean(x * x, axis=-1, keepdims=True)
        xn_ref[...] = (x * lax.rsqrt(ms + RMS_EPS) * g_ref[...]).astype(BF16)

    o_ref[...] = jnp.dot(xn_ref[...], w_ref[...], preferred_element_type=F32).astype(o_ref.dtype)


def _norm_proj(x2d, gain, w_bf16, tn):
    m, d = x2d.shape
    n = w_bf16.shape[1]
    assert n % tn == 0
    return pl.pallas_call(
        _norm_proj_kernel,
        grid=(m // PROJ_TM, n // tn),
        in_specs=[
            pl.BlockSpec((PROJ_TM, d), lambda i, j: (i, 0)),
            pl.BlockSpec((1, d), lambda i, j: (0, 0)),
            pl.BlockSpec((d, tn), lambda i, j: (0, j)),
        ],
        out_specs=pl.BlockSpec((PROJ_TM, tn), lambda i, j: (i, j)),
        out_shape=jax.ShapeDtypeStruct((m, n), BF16),
        scratch_shapes=[pltpu.VMEM((PROJ_TM, d), BF16)],
        compiler_params=_cparams(("parallel", "arbitrary")),
        name="norm_proj",
    )(x2d, gain.reshape(1, d), w_bf16)


def _out_post_kernel(*refs, n_parts):
    y_refs = refs[:n_parts]
    w_ref, h_ref, g_ref, o_ref = refs[n_parts:]
    y = y_refs[0][...].astype(F32)
    for r in y_refs[1:]:
        y = y + r[...].astype(F32)
    t = jnp.dot(y.astype(BF16), w_ref[...], preferred_element_type=F32)
    ms = jnp.mean(t * t, axis=-1, keepdims=True)
    o_ref[...] = h_ref[...] + t * lax.rsqrt(ms + RMS_EPS) * g_ref[...]


def _out_post(parts, w_bf16, h2d, gain):
    m, d = h2d.shape
    k = w_bf16.shape[0]
    n_parts = len(parts)
    return pl.pallas_call(
        functools.partial(_out_post_kernel, n_parts=n_parts),
        grid=(m // POST_TM,),
        in_specs=[pl.BlockSpec((POST_TM, k), lambda i: (i, 0)) for _ in parts] + [
            pl.BlockSpec((k, d), lambda i: (0, 0)),
            pl.BlockSpec((POST_TM, d), lambda i: (i, 0)),
            pl.BlockSpec((1, d), lambda i: (0, 0)),
        ],
        out_specs=pl.BlockSpec((POST_TM, d), lambda i: (i, 0)),
        out_shape=jax.ShapeDtypeStruct((m, d), F32),
        compiler_params=_cparams(("parallel",)),
        name="out_post",
    )(*parts, w_bf16, h2d, gain.reshape(1, d))


def _compress_kernel(x_ref, pos_ref, w1_ref, w2_ref, o_ref):
    half = CMP_STRIDE * HEAD_DIM
    for which in range(2):
        x = x_ref[0, which, 0].astype(F32)
        xa = (x + pos_ref[which, 0:1, :]).astype(BF16)
        xb = (x + pos_ref[which, 1:2, :]).astype(BF16)
        pa = jnp.dot(xa, w1_ref[which, :half, :], preferred_element_type=F32)
        pb = jnp.dot(xb, w1_ref[which, half:, :], preferred_element_type=F32)
        hid = pa + pltpu.roll(pb, N_CMP_PAD - 1, 0)
        hid = hid * jax.nn.sigmoid(hid)
        o_ref[0, which, 0] = jnp.dot(hid.astype(BF16), w2_ref[which],
                                     preferred_element_type=F32).astype(o_ref.dtype)


def _compress(ckv_rows, pos, w1, w2d):
    b = ckv_rows.shape[0]
    row_w = CMP_STRIDE * HEAD_DIM
    return pl.pallas_call(
        _compress_kernel,
        grid=(b, GROUPS),
        in_specs=[
            pl.BlockSpec((1, 2, 1, N_CMP_PAD, row_w), lambda i, g: (i, 0, g, 0, 0)),
            pl.BlockSpec((2, 2, row_w), lambda i, g: (0, 0, 0)),
            pl.BlockSpec((2, 2 * row_w, CMP_HIDDEN), lambda i, g: (0, 0, 0)),
            pl.BlockSpec((2, CMP_HIDDEN, 2 * HEAD_DIM), lambda i, g: (0, 0, 0)),
        ],
        out_specs=pl.BlockSpec((1, 2, 1, N_CMP_PAD, 2 * HEAD_DIM), lambda i, g: (i, 0, g, 0, 0)),
        out_shape=jax.ShapeDtypeStruct((b, 2, GROUPS, N_CMP_PAD, 2 * HEAD_DIM), BF16),
        compiler_params=_cparams(("parallel", "parallel")),
        name="compress",
    )(ckv_rows, pos, w1, w2d)


def _lane_consts(rows):
    lane = np.arange(LANES)
    lo = (lane < HEAD_DIM).astype(np.float32)
    hi = (lane >= HEAD_DIM).astype(np.float32)
    last = (lane == LANES - 1).astype(np.float32)
    first = (lane == 0).astype(np.float32)
    c = np.stack([lo, hi, 1.0 - last, last, 1.0 - first, first])
    return jnp.asarray(np.broadcast_to(c[:, None, :], (6, rows, LANES)), BF16)


def _bf16_terms(x, n):
    terms = []
    for _ in range(n - 1):
        t = x.astype(BF16)
        terms.append(t)
        x = x - t.astype(F32)
    terms.append(x.astype(BF16))
    return terms


def _gates(gate_logits, expands):
    terms = _bf16_terms(jax.nn.sigmoid(gate_logits.astype(F32)), 2)
    return [sum(jnp.dot(t, ex, preferred_element_type=F32) for t in terms) for ex in expands]


def _silu(x):
    return x * jax.nn.sigmoid(x)


def _cmp_select_kernel(q_ref, gate_ref, z_ref, ckv_ref, bias_ref, ovl_ref, lc_ref, gexp_ref,
                       y_ref, sel_ref):
    g = pl.program_id(0)
    qi = pl.program_id(1)
    tq = q_ref.shape[1]
    q = q_ref[0]
    kk = ckv_ref[0, 0, 0]
    vv = ckv_ref[0, 1, 0]
    lane = lax.broadcasted_iota(jnp.int32, (tq, LANES), 1)

    psum = jnp.zeros((N_CMP_PAD, tq), F32)
    pairs = []
    for a in range(2):
        qp = q[:, a * LANES:(a + 1) * LANES]
        outs = []
        for e in range(2):
            s = lax.dot_general(kk, qp * lc_ref[e, :tq], _NT, preferred_element_type=F32)
            bias = bias_ref[2 * a + e]
            s = s + bias
            m = jnp.max(s, axis=0, keepdims=True)
            p = jnp.exp2(s - m)
            p = jnp.where(bias > 0.5 * MASKED, p / jnp.sum(p, axis=0, keepdims=True), 0.0)
            psum = psum + p
            outs.append(lax.dot_general(p.astype(BF16), vv, _TN, preferred_element_type=F32))
        pairs.append(jnp.where(lane < HEAD_DIM, outs[0], outs[1]))
    y_ref[0] = (jnp.concatenate(pairs, axis=1) * _gates(gate_ref[0], [gexp_ref[g]])[0]
                * _silu(z_ref[0].astype(F32))).astype(y_ref.dtype)

    ovl = ovl_ref[...]
    imp = sum(jnp.dot(ovl, t, preferred_element_type=F32) for t in _bf16_terms(psum, 3))
    t = qi * tq + lax.broadcasted_iota(jnp.int32, (N_SLC, tq), 1)
    blk = lax.broadcasted_iota(jnp.int32, (N_SLC, tq), 0)
    cur = lax.shift_right_logical(t, int(math.log2(SLC_BLOCK)))
    forced = (blk == 0) | (blk == cur) | (blk == cur - 1)
    score = jnp.where(blk * SLC_BLOCK <= t, imp + jnp.where(forced, FORCED_SCORE, 0.0), MASKED)
    cnt = jnp.zeros((N_SLC, tq), jnp.int32)
    for jp in range(N_SLC):
        row = score[jp:jp + 1, :]
        beats = (row > score) | ((row == score) & (blk > jp))
        cnt = cnt + beats.astype(jnp.int32)
    sel_ref[0, 0] = jnp.where(cnt < SLC_TOPN, 1.0, 0.0).astype(sel_ref.dtype)


def _cmp_select(proj, ckv, bias_cmp, ovl, lane_consts, gate_expand):
    b = proj.shape[0]
    tq = CMP_TQ
    grp_w = HEADS_PER_GROUP * HEAD_DIM
    return pl.pallas_call(
        _cmp_select_kernel,
        grid=(GROUPS, SEQ // tq, b),
        in_specs=[
            pl.BlockSpec((1, tq, grp_w), lambda g, qi, i: (i, qi, NSA_Q_COL // grp_w + g)),
            pl.BlockSpec((1, tq, LANES), lambda g, qi, i: (i, qi, NSA_G_COL // LANES)),
            pl.BlockSpec((1, tq, grp_w), lambda g, qi, i: (i, qi, NSA_Z_COL // grp_w + g)),
            pl.BlockSpec((1, 2, 1, N_CMP_PAD, 2 * HEAD_DIM), lambda g, qi, i: (i, 0, g, 0, 0)),
            pl.BlockSpec((HEADS_PER_GROUP, N_CMP_PAD, tq), lambda g, qi, i: (g, 0, qi)),
            pl.BlockSpec((N_SLC, N_CMP_PAD), lambda g, qi, i: (0, 0)),
            pl.BlockSpec(memory_space=pltpu.VMEM),
            pl.BlockSpec(memory_space=pltpu.VMEM),
        ],
        out_specs=[
            pl.BlockSpec((1, tq, grp_w), lambda g, qi, i: (i, qi, g)),
            pl.BlockSpec((1, 1, N_SLC, tq), lambda g, qi, i: (i, g, 0, qi)),
        ],
        out_shape=[
            jax.ShapeDtypeStruct((b, SEQ, NSA_WIDTH), BF16),
            jax.ShapeDtypeStruct((b, GROUPS, N_SLC, SEQ), BF16),
        ],
        compiler_params=_cparams(("parallel", "parallel", "parallel")),
        name="cmp_select",
    )(proj, proj, proj, ckv, bias_cmp, ovl, lane_consts, gate_expand)


def _token_kernel(q_ref, ks_ref, vs_ref, kw_ref, vw_ref, gate_ref, zs_ref, zw_ref, a_ref, lc_ref,
                  gexp_ref, sel_ref, kblk_ref, place_ref, placed_ref, y_ref,
                  qm_ref, m_ref, acc_ref, s_bufs, p_bufs, c_bufs, kx_ref, vx_ref):
    g = pl.program_id(1)
    qi = pl.program_id(2)
    tq = q_ref.shape[1]
    tk = ATT_TK
    n_slc = qi + 1
    n_tiles = n_slc + jnp.minimum(qi, WIN_SIZE // tk) + 1

    q = q_ref[0]
    flags = lax.dot_general(sel_ref[0, 0], place_ref[...], _TN, preferred_element_type=F32)
    fill = ((placed_ref[...] - flags) * MASKED).astype(BF16)
    for slot in range(HEADS_PER_GROUP):
        e, a = divmod(slot, 2)
        qm = q[:, a * LANES:(a + 1) * LANES] * lc_ref[e, :tq]
        qm_ref[0, slot * tq:(slot + 1) * tq, :] = qm + fill[:, e * LANES:(e + 1) * LANES]
        qm_ref[1, slot * tq:(slot + 1) * tq, :] = qm

    @pl.when(qi == 0)
    def _():
        for c in range(SEQ // tk):
            rows = slice(c * tk, (c + 1) * tk)
            for branch, v_ref in enumerate((vs_ref, vw_ref)):
                vv = v_ref[0, rows, :]
                vx_ref[2 * branch, rows, :] = vv * lc_ref[2, :tk] + lc_ref[3, :tk]
                vx_ref[2 * branch + 1, rows, :] = vv * lc_ref[4, :tk] + lc_ref[5, :tk]
            kk = ks_ref[0, rows, :]
            for e in range(2):
                kx_ref[e, c] = (kk * lc_ref[e, :tk] + kblk_ref[e, c]).astype(F32).T.astype(BF16)
            kx_ref[2, c] = kw_ref[0, rows, :].astype(F32).T.astype(BF16)

    def tile(j):
        win = (j >= n_slc).astype(jnp.int32)
        t = j - win * n_slc
        return win, jnp.clip(qi - t, 0, qi), jnp.where(win == 1, t, jnp.where(t < 2, t, 3))

    def scores(j, s_ref):
        win, ki, _ = tile(j)
        for e in range(2):
            pair = slice(2 * e * tq, 2 * (e + 1) * tq)
            s_ref[pair, :] = jnp.dot(qm_ref[win, pair, :], kx_ref[jnp.where(win == 1, 2, e), ki],
                                     preferred_element_type=F32)

    def softmax(j, s_ref, p_ref, c_ref):
        win, _, kind = tile(j)
        for slot in range(HEADS_PER_GROUP):
            e, a = divmod(slot, 2)
            rows = slice(slot * tq, (slot + 1) * tq)
            s = s_ref[rows, :] + a_ref[kind, HEADS_PER_GROUP * g + 2 * a + e]
            m_prev = m_ref[win, rows, :]
            m_new = jnp.maximum(m_prev, jnp.max(s, axis=1, keepdims=True))
            c_ref[rows, :] = jnp.exp2(m_prev - m_new)
            x = s - jnp.concatenate([m_new] * (tk // LANES), axis=1)
            p_ref[rows, :] = jnp.exp2(x.astype(BF16))
            m_ref[win, rows, :] = m_new

    def values(j, p_ref, c_ref):
        win, ki, _ = tile(j)
        rows = pl.ds(pl.multiple_of(ki * tk, tk), tk)
        for e in range(2):
            pair = slice(2 * e * tq, 2 * (e + 1) * tq)
            pv = jnp.dot(p_ref[pair, :], vx_ref[2 * win + e, rows, :], preferred_element_type=F32)
            acc_ref[win, pair, :] = c_ref[pair, :] * acc_ref[win, pair, :] + pv

    m_ref[...] = jnp.full(m_ref.shape, -jnp.inf, F32)
    acc_ref[...] = jnp.zeros(acc_ref.shape, F32)
    bufs = [(s_bufs.at[i], p_bufs.at[i], c_bufs.at[i]) for i in range(2)]

    scores(0, bufs[0][0])
    softmax(0, *bufs[0])
    scores(1, bufs[1][0])

    def step(j, parity):
        cur, nxt = bufs[parity], bufs[1 - parity]
        values(j - 1, nxt[1], nxt[2])
        softmax(j, *cur)
        scores(j + 1, nxt[0])

    def steps(first, count):
        for k in range(count):
            step(first + k, (k + 1) % 2)

    n_steps = n_tiles - 1
    n_quads = lax.shift_right_logical(n_steps, 2)

    def four_steps(i, carry):
        steps(4 * i + 1, 4)
        return carry
    lax.fori_loop(0, n_quads, four_steps, 0)

    @pl.when(lax.bitwise_and(n_steps, 2) == 2)
    def _():
        steps(4 * n_quads + 1, 2)

    @pl.when(lax.bitwise_and(n_steps, 1) == 1)
    def _():
        steps(n_steps, 1)

    for parity in range(2):
        @pl.when(lax.bitwise_and(n_tiles - 1, 1) == parity)
        def _():
            values(n_tiles - 1, bufs[parity][1], bufs[parity][2])

    lane = lax.broadcasted_iota(jnp.int32, (tq, LANES), 1)
    gates = _gates(gate_ref[0], [gexp_ref[0, g], gexp_ref[1, g]])
    y = None
    for branch, z_ref in enumerate((zs_ref, zw_ref)):
        pairs = []
        for a in range(2):
            acc_e = acc_ref[branch, a * tq:(a + 1) * tq, :]
            acc_o = acc_ref[branch, (2 + a) * tq:(3 + a) * tq, :]
            pairs.append(jnp.where(lane < HEAD_DIM, acc_e / acc_e[:, LANES - 1:LANES],
                                   acc_o / acc_o[:, 0:1]))
        part = jnp.concatenate(pairs, axis=1) * gates[branch] * _silu(z_ref[0].astype(F32))
        y = part if y is None else y + part
    y_ref[0] = y.astype(y_ref.dtype)


def _token_attention(proj, a_tiles, lane_consts, gate_expand, sel, sel_consts):
    b = proj.shape[0]
    tq, tk = ATT_TQ, ATT_TK
    grp_w = HEADS_PER_GROUP * HEAD_DIM
    k_blk = NSA_K_COL // LANES
    v_blk = NSA_V_COL // LANES
    z_blk = NSA_Z_COL // grp_w
    kv_spec = lambda blk: pl.BlockSpec((1, SEQ, LANES), lambda i, g, qi: (i, 0, blk + g))
    z_spec = lambda blk: pl.BlockSpec((1, tq, grp_w), lambda i, g, qi: (i, qi, blk + g))
    whole = pl.BlockSpec(memory_space=pltpu.VMEM)
    return pl.pallas_call(
        _token_kernel,
        grid=(b, GROUPS, SEQ // tq),
        in_specs=[
            pl.BlockSpec((1, tq, grp_w), lambda i, g, qi: (i, qi, NSA_Q_COL // grp_w + g)),
            kv_spec(k_blk), kv_spec(v_blk), kv_spec(k_blk + GROUPS), kv_spec(v_blk + GROUPS),
            pl.BlockSpec((1, tq, LANES), lambda i, g, qi: (i, qi, NSA_G_COL // LANES)),
            z_spec(z_blk + GROUPS), z_spec(z_blk + 2 * GROUPS),
            whole, whole, whole,
            pl.BlockSpec((1, 1, N_SLC, tq), lambda i, g, qi: (i, g, 0, qi)),
            whole, whole, whole,
        ],
        out_specs=pl.BlockSpec((1, tq, grp_w), lambda i, g, qi: (i, qi, g)),
        out_shape=jax.ShapeDtypeStruct((b, SEQ, NSA_WIDTH), BF16),
        scratch_shapes=[
            pltpu.VMEM((2, HEADS_PER_GROUP * tq, LANES), BF16),
            pltpu.VMEM((2, HEADS_PER_GROUP * tq, LANES), F32),
            pltpu.VMEM((2, HEADS_PER_GROUP * tq, LANES), F32),
            pltpu.VMEM((2, HEADS_PER_GROUP * tq, tk), F32),
            pltpu.VMEM((2, HEADS_PER_GROUP * tq, tk), BF16),
            pltpu.VMEM((2, HEADS_PER_GROUP * tq, LANES), F32),
            pltpu.VMEM((3, SEQ // tk, LANES, tk), BF16),
            pltpu.VMEM((4, SEQ, LANES), BF16),
        ],
        compiler_params=_cparams(("parallel", "parallel", "arbitrary")),
        name="token_attention",
    )(proj, proj, proj, proj, proj, proj, proj, proj, a_tiles, lane_consts, gate_expand,
      sel, *sel_consts)


def _retention_kernel(q_ref, k_ref, v_ref, z_ref, cos_ref, sin_ref, inner_ref, xi_ref, zeta_ref,
                      gn_ref, y_ref, state_ref, *, decays):
    @pl.when(pl.program_id(1) == 0)
    def _():
        state_ref[...] = jnp.zeros(state_ref.shape, F32)

    cos = cos_ref[...]
    sin = sin_ref[...]
    half = RET_QK_DIM // 2

    def rot(x):
        x1, x2 = x[:, :half], x[:, half:]
        return jnp.concatenate([x1 * cos - x2 * sin, x1 * sin + x2 * cos], axis=1)

    for h in range(RET_HEADS):
        qs = slice(h * RET_QK_DIM, (h + 1) * RET_QK_DIM)
        vs = slice(h * RET_V_DIM, (h + 1) * RET_V_DIM)
        qr = rot(q_ref[0, :, qs].astype(F32))
        kr = rot(k_ref[0, :, qs].astype(F32)) * (RET_QK_DIM ** -0.5)
        qb = qr.astype(BF16)
        vh = v_ref[0, :, vs]
        attn = lax.dot_general(qb, kr.astype(BF16), _NT, preferred_element_type=F32) * inner_ref[h]
        st = state_ref[h]
        o = (jnp.dot(attn.astype(BF16), vh, preferred_element_type=F32)
             + jnp.dot(qb, st.astype(BF16), preferred_element_type=F32) * xi_ref[h])
        kz = (kr * zeta_ref[h]).astype(BF16)
        state_ref[h] = st * decays[h] + lax.dot_general(kz, vh, _TN, preferred_element_type=F32)
        mu = jnp.mean(o, axis=1, keepdims=True)
        d = o - mu
        var = jnp.mean(d * d, axis=1, keepdims=True)
        on = d * lax.rsqrt(var + GN_EPS) * gn_ref[h]
        y_ref[0, :, vs] = (on * _silu(z_ref[0, :, vs].astype(F32))).astype(y_ref.dtype)


def _retention(proj, gn_gain, tables):
    b = proj.shape[0]
    c = RET_CHUNK
    cos, sin, inner, xi, zeta, decays = tables
    v_blk = 2 * RET_QK_WIDTH // RET_WIDTH
    return pl.pallas_call(
        functools.partial(_retention_kernel, decays=decays),
        grid=(b, SEQ // c),
        in_specs=[
            pl.BlockSpec((1, c, RET_QK_WIDTH), lambda i, j: (i, j, 0)),
            pl.BlockSpec((1, c, RET_QK_WIDTH), lambda i, j: (i, j, 1)),
            pl.BlockSpec((1, c, RET_WIDTH), lambda i, j: (i, j, v_blk)),
            pl.BlockSpec((1, c, RET_WIDTH), lambda i, j: (i, j, v_blk + 1)),
            pl.BlockSpec((c, RET_QK_DIM // 2), lambda i, j: (j, 0)),
            pl.BlockSpec((c, RET_QK_DIM // 2), lambda i, j: (j, 0)),
            pl.BlockSpec((RET_HEADS, c, c), lambda i, j: (0, 0, 0)),
            pl.BlockSpec((RET_HEADS, c, 1), lambda i, j: (0, 0, 0)),
            pl.BlockSpec((RET_HEADS, c, 1), lambda i, j: (0, 0, 0)),
            pl.BlockSpec((RET_HEADS, 1, RET_V_DIM), lambda i, j: (0, 0, 0)),
        ],
        out_specs=pl.BlockSpec((1, c, RET_WIDTH), lambda i, j: (i, j, 0)),
        out_shape=jax.ShapeDtypeStruct((b, SEQ, RET_WIDTH), BF16),
        scratch_shapes=[pltpu.VMEM((RET_HEADS, RET_QK_DIM, RET_V_DIM), F32)],
        compiler_params=_cparams(("parallel", "arbitrary")),
        name="retention",
    )(proj, proj, proj, proj, cos, sin, inner, xi, zeta, gn_gain.reshape(RET_HEADS, 1, RET_V_DIM))


def _t5_bucket_np(dist):
    n = np.maximum(dist, 0)
    max_exact = REL_BUCKETS // 2
    nf = np.maximum(n, 1).astype(np.float64)
    large = max_exact + (np.log(nf / max_exact) / math.log(REL_MAX_DIST / max_exact)
                         * (REL_BUCKETS - max_exact)).astype(np.int64)
    large = np.minimum(large, REL_BUCKETS - 1)
    return np.where(n < max_exact, n, large).astype(np.int32)


def _skewed(vec, rows, stride, cols):
    p = vec.shape[-1]
    lead = vec.shape[:-1]
    flat = jnp.tile(vec, (1,) * len(lead) + (rows,))[..., :rows * (p - stride)]
    return flat.reshape(lead + (rows, p - stride))[..., :cols]


def _nsa_tables(table):
    tq, tk = ATT_TQ, ATT_TK
    assert tq == tk
    tab_t = table.T * LOG2E
    period = 2 * SEQ
    x = np.arange(period)
    ok = (x >= CMP_BLOCK - 1) & (x < SEQ)
    vec = jnp.take(tab_t, jnp.asarray(_t5_bucket_np(x - (CMP_BLOCK - 1))), axis=1)
    vec = jnp.where(jnp.asarray(ok)[None, :], vec, MASKED)
    bias_cmp = _skewed(vec, N_CMP_PAD, CMP_STRIDE, SEQ)
    rel_t = tab_t - tab_t[:, REL_BUCKETS - 1:]
    period = 2 * tq
    x = np.arange(period)
    x = np.where(x < tq, x, x - period)
    tiles = []
    for off in range(3):
        d = off * tk - x
        ok = (d >= 0) & (d < WIN_SIZE) if off == 2 else (d >= 0)
        vec = jnp.take(rel_t, jnp.asarray(_t5_bucket_np(d)), axis=1)
        vec = jnp.where(jnp.asarray(ok)[None, :], vec, MASKED)
        tiles.append(_skewed(vec, tq, 1, tk))
    a_tiles = jnp.stack(tiles + [jnp.zeros_like(tiles[0])])
    cs = np.arange(N_CMP_PAD)[None, :] * CMP_STRIDE
    jj = np.arange(N_SLC)[:, None]
    ovl = ((cs < (jj + 1) * SLC_BLOCK) & (cs + CMP_BLOCK > jj * SLC_BLOCK)
           & (np.arange(N_CMP_PAD)[None, :] < N_CMP))
    ovl = jnp.asarray(ovl.astype(np.float32), BF16)
    lane = np.arange(LANES).reshape(1, 1, 1, LANES)
    key = np.arange(SEQ).reshape(1, SEQ // tk, tk, 1)
    base = np.array([HEAD_DIM, 0]).reshape(2, 1, 1, 1)
    key_blocks = jnp.asarray((lane == base + key // SLC_BLOCK).astype(np.float32), BF16)
    col = np.arange(2 * LANES).reshape(1, -1)
    jb = np.arange(N_SLC).reshape(-1, 1)
    place = (col == HEAD_DIM + jb) | (col == LANES + jb)
    sel_consts = (key_blocks, jnp.asarray(place.astype(np.float32), BF16),
                  jnp.asarray(place.any(axis=0, keepdims=True).astype(np.float32)))
    c = np.arange(LANES).reshape(1, 1, LANES, 1)
    col = np.arange(HEADS_PER_GROUP * HEAD_DIM).reshape(1, 1, 1, -1) // HEAD_DIM
    br = np.arange(3).reshape(3, 1, 1, 1)
    gg = np.arange(GROUPS).reshape(1, GROUPS, 1, 1)
    gate_expand = jnp.asarray((c == br * HEADS + HEADS_PER_GROUP * gg + col).astype(np.float32),
                              BF16)
    return bias_cmp, a_tiles, ovl, sel_consts, _lane_consts(max(tq, CMP_TQ)), gate_expand


def _retention_tables():
    c = RET_CHUNK
    log_g = jnp.log(1.0 - 2.0 ** (-5.0 - jnp.arange(RET_HEADS, dtype=F32)))
    i = jnp.arange(c, dtype=F32)
    diff = i[:, None] - i[None, :]
    inner = jnp.where(diff >= 0, jnp.exp(diff[None] * log_g[:, None, None]), 0.0)
    xi = jnp.exp((i + 1.0)[None, :] * log_g[:, None])[:, :, None]
    zeta = jnp.exp((c - 1.0 - i)[None, :] * log_g[:, None])[:, :, None]
    decays = tuple(float((1.0 - 2.0 ** (-5.0 - h)) ** c) for h in range(RET_HEADS))
    inv = 1.0 / (ROPE_BASE ** jnp.linspace(0.0, 1.0, RET_QK_DIM // 2, dtype=F32))
    ang = jnp.arange(SEQ, dtype=F32)[:, None] * inv[None, :]
    return jnp.cos(ang), jnp.sin(ang), inner, xi, zeta, decays


def _nsa_w_in_layout(w):
    d = w.shape[0]
    q = w[:, :NSA_WIDTH] * (HEAD_DIM ** -0.5 * LOG2E)
    kv0 = NSA_WIDTH
    g0 = kv0 + 6 * KV_WIDTH
    z0 = g0 + 3 * HEADS
    slab = lambda n: w[:, kv0 + n * KV_WIDTH: kv0 + (n + 1) * KV_WIDTH]
    cols = [q, w[:, z0:z0 + 3 * NSA_WIDTH]]
    for n in (2, 4, 3, 5):
        for g in range(GROUPS):
            part = slab(n)[:, g * HEAD_DIM:(g + 1) * HEAD_DIM]
            cols += [part, part]
    cols += [slab(0), slab(1), w[:, g0:z0]]
    used = NSA_G_COL + 3 * HEADS
    cols.append(jnp.zeros((d, NSA_PROJ_PAD - used), w.dtype))
    return jnp.concatenate(cols, axis=1).astype(BF16)


def _nsa_layer(h2d, b, pre_gain, post_gain, w_in, w_out, k_pos, k_w1, k_w2, v_pos, v_w1, v_w2, tabs):
    bias_cmp, a_tiles, ovl, sel_consts, lane_consts, gate_expand = tabs
    proj = _norm_proj(h2d, pre_gain, _nsa_w_in_layout(w_in), NSA_PROJ_TN)
    proj = proj.reshape(b, SEQ, NSA_PROJ_PAD)
    ckv = proj[:, :, NSA_CKV_COL:NSA_CKV_COL + 2 * KV_WIDTH]
    ckv = ckv.reshape(b, N_CMP_PAD, CMP_STRIDE, 2, GROUPS, HEAD_DIM).transpose(0, 3, 4, 1, 2, 5)
    ckv = ckv.reshape(b, 2, GROUPS, N_CMP_PAD, CMP_STRIDE * HEAD_DIM)
    pos = jnp.stack([k_pos, v_pos]).reshape(2, 2, CMP_STRIDE * HEAD_DIM)
    w1 = jnp.stack([k_w1, v_w1]).astype(BF16)
    w2 = jnp.stack([k_w2, v_w2])
    w2d = jnp.concatenate([w2, w2], axis=2).astype(BF16)
    ckv_c = _compress(ckv, pos, w1, w2d)
    y_cmp, sel = _cmp_select(proj, ckv_c, bias_cmp, ovl, lane_consts, gate_expand[0])
    y_tok = _token_attention(proj, a_tiles, lane_consts, gate_expand[1:], sel, sel_consts)
    parts = [y.reshape(b * SEQ, NSA_WIDTH) for y in (y_cmp, y_tok)]
    return _out_post(parts, w_out.astype(BF16), h2d, post_gain)


def _ret_layer(h2d, b, pre_gain, post_gain, w_in, w_out, gn_gain, tabs):
    proj = _norm_proj(h2d, pre_gain, w_in.astype(BF16), RET_PROJ_TN).reshape(b, SEQ, -1)
    y = _retention(proj, gn_gain, tabs)
    return _out_post([y.reshape(b * SEQ, RET_WIDTH)], w_out.astype(BF16), h2d, post_gain)


def kernel(x, pre_norm_gain, post_norm_gain, rel_bias_table, nsa_w_in, nsa_w_out, nsa_cmp_k_pos, nsa_cmp_k_w1, nsa_cmp_k_w2, nsa_cmp_v_pos, nsa_cmp_v_w1, nsa_cmp_v_w2, ret_w_in, ret_w_out, ret_gn_gain):
    b, s, d = x.shape
    assert s == SEQ and d == D_MODEL
    nsa_tabs = _nsa_tables(rel_bias_table)
    ret_tabs = _retention_tables()
    h = x.reshape(b * s, d)
    for layer in range(DEPTH):
        slot = layer // 2
        if layer % 2 == 0:
            h = _nsa_layer(h, b, pre_norm_gain[layer], post_norm_gain[layer], nsa_w_in[slot],
                           nsa_w_out[slot], nsa_cmp_k_pos[slot], nsa_cmp_k_w1[slot],
                           nsa_cmp_k_w2[slot], nsa_cmp_v_pos[slot], nsa_cmp_v_w1[slot],
                           nsa_cmp_v_w2[slot], nsa_tabs)
        else:
            h = _ret_layer(h, b, pre_norm_gain[layer], post_norm_gain[layer], ret_w_in[slot],
                           ret_w_out[slot], ret_gn_gain[slot], ret_tabs)
    return h.reshape(b, s, d)
```

```python
import functools
import math

import numpy as np
import jax
import jax.numpy as jnp
from jax import lax
from jax.experimental import pallas as pl
from jax.experimental.pallas import tpu as pltpu

F32 = jnp.float32
BF16 = jnp.bfloat16

D_MODEL = 1024
SEQ = 2048
DEPTH = 4
RMS_EPS = 1e-6
GN_EPS = 1e-6
MASKED = -1e30
LOG2E = math.log2(math.e)

HEADS = 16
HEAD_DIM = 64
GROUPS = 4
HEADS_PER_GROUP = HEADS // GROUPS
NSA_WIDTH = HEADS * HEAD_DIM
KV_WIDTH = GROUPS * HEAD_DIM
CMP_BLOCK = 32
CMP_STRIDE = 16
CMP_HIDDEN = 256
N_CMP = (SEQ - CMP_BLOCK) // CMP_STRIDE + 1
N_CMP_PAD = 128
SLC_BLOCK = 64
N_SLC = SEQ // SLC_BLOCK
SLC_TOPN = 16
WIN_SIZE = 512
FORCED_SCORE = 1e3
REL_BUCKETS = 32
REL_MAX_DIST = 128

RET_HEADS = 4
RET_QK_DIM = 256
RET_V_DIM = 512
RET_QK_WIDTH = RET_HEADS * RET_QK_DIM
RET_WIDTH = RET_HEADS * RET_V_DIM
ROPE_BASE = 10000.0

LANES = 128
VMEM_LIMIT_BYTES = 56 * 1024 * 1024

PROJ_TM = 1024
RET_PROJ_TN = 2048
NSA_PROJ_TN = 3456
POST_TM = 512
ATT_TQ = 256
ATT_TK = 256
CMP_TQ = 1024
RET_CHUNK = 256

NSA_Q_COL = 0
NSA_Z_COL = 1024
NSA_K_COL = 4096
NSA_V_COL = 5120
NSA_CKV_COL = 6144
NSA_G_COL = 6656
NSA_PROJ_PAD = 6912

_NT = (((1,), (1,)), ((), ()))
_TN = (((0,), (0,)), ((), ()))


def _cparams(sem):
    return pltpu.CompilerParams(dimension_semantics=sem, vmem_limit_bytes=VMEM_LIMIT_BYTES)


def _norm_proj_kernel(x_ref, g_ref, w_ref, o_ref, xn_ref):
    @pl.when(pl.program_id(1) == 0)
    def _():
        x = x_ref[...]
        ms = jnp.mean(x * x, axis=-1, keepdims=True)
        xn_ref[...] = (x * lax.rsqrt(ms + RMS_EPS) * g_ref[...]).astype(BF16)

    o_ref[...] = jnp.dot(xn_ref[...], w_ref[...], preferred_element_type=F32).astype(o_ref.dtype)


def _norm_proj(x2d, gain, w_bf16, tn):
    m, d = x2d.shape
    n = w_bf16.shape[1]
    assert n % tn == 0
    return pl.pallas_call(
        _norm_proj_kernel,
        grid=(m // PROJ_TM, n // tn),
        in_specs=[
            pl.BlockSpec((PROJ_TM, d), lambda i, j: (i, 0)),
            pl.BlockSpec((1, d), lambda i, j: (0, 0)),
            pl.BlockSpec((d, tn), lambda i, j: (0, j)),
        ],
        out_specs=pl.BlockSpec((PROJ_TM, tn), lambda i, j: (i, j)),
        out_shape=jax.ShapeDtypeStruct((m, n), BF16),
        scratch_shapes=[pltpu.VMEM((PROJ_TM, d), BF16)],
        compiler_params=_cparams(("parallel", "arbitrary")),
        name="norm_proj",
    )(x2d, gain.reshape(1, d), w_bf16)


def _out_post_kernel(*refs, n_parts):
    y_refs = refs[:n_parts]
    w_ref, h_ref, g_ref, o_ref = refs[n_parts:]
    y = y_refs[0][...].astype(F32)
    for r in y_refs[1:]:
        y = y + r[...].astype(F32)
    t = jnp.dot(y.astype(BF16), w_ref[...], preferred_element_type=F32)
    ms = jnp.mean(t * t, axis=-1, keepdims=True)
    o_ref[...] = h_ref[...] + t * lax.rsqrt(ms + RMS_EPS) * g_ref[...]


def _out_post(parts, w_bf16, h2d, gain):
    m, d = h2d.shape
    k = w_bf16.shape[0]
    n_parts = len(parts)
    return pl.pallas_call(
        functools.partial(_out_post_kernel, n_parts=n_parts),
        grid=(m // POST_TM,),
        in_specs=[pl.BlockSpec((POST_TM, k), lambda i: (i, 0)) for _ in parts] + [
            pl.BlockSpec((k, d), lambda i: (0, 0)),
            pl.BlockSpec((POST_TM, d), lambda i: (i, 0)),
            pl.BlockSpec((1, d), lambda i: (0, 0)),
        ],
        out_specs=pl.BlockSpec((POST_TM, d), lambda i: (i, 0)),
        out_shape=jax.ShapeDtypeStruct((m, d), F32),
        compiler_params=_cparams(("parallel",)),
        name="out_post",
    )(*parts, w_bf16, h2d, gain.reshape(1, d))


def _compress_kernel(x_ref, pos_ref, w1_ref, w2_ref, o_ref):
    half = CMP_STRIDE * HEAD_DIM
    for which in range(2):
        x = x_ref[0, which, 0].astype(F32)
        xa = (x + pos_ref[which, 0:1, :]).astype(BF16)
        xb = (x + pos_ref[which, 1:2, :]).astype(BF16)
        pa = jnp.dot(xa, w1_ref[which, :half, :], preferred_element_type=F32)
        pb = jnp.dot(xb, w1_ref[which, half:, :], preferred_element_type=F32)
        hid = pa + pltpu.roll(pb, N_CMP_PAD - 1, 0)
        hid = hid * jax.nn.sigmoid(hid)
        o_ref[0, which, 0] = jnp.dot(hid.astype(BF16), w2_ref[which],
                                     preferred_element_type=F32).astype(o_ref.dtype)


def _compress(ckv_rows, pos, w1, w2d):
    b = ckv_rows.shape[0]
    row_w = CMP_STRIDE * HEAD_DIM
    return pl.pallas_call(
        _compress_kernel,
        grid=(b, GROUPS),
        in_specs=[
            pl.BlockSpec((1, 2, 1, N_CMP_PAD, row_w), lambda i, g: (i, 0, g, 0, 0)),
            pl.BlockSpec((2, 2, row_w), lambda i, g: (0, 0, 0)),
            pl.BlockSpec((2, 2 * row_w, CMP_HIDDEN), lambda i, g: (0, 0, 0)),
            pl.BlockSpec((2, CMP_HIDDEN, 2 * HEAD_DIM), lambda i, g: (0, 0, 0)),
        ],
        out_specs=pl.BlockSpec((1, 2, 1, N_CMP_PAD, 2 * HEAD_DIM), lambda i, g: (i, 0, g, 0, 0)),
        out_shape=jax.ShapeDtypeStruct((b, 2, GROUPS, N_CMP_PAD, 2 * HEAD_DIM), BF16),
        compiler_params=_cparams(("parallel", "parallel")),
        name="compress",
    )(ckv_rows, pos, w1, w2d)


def _lane_consts(rows):
    lane = np.arange(LANES)
    lo = (lane < HEAD_DIM).astype(np.float32)
    hi = (lane >= HEAD_DIM).astype(np.float32)
    last = (lane == LANES - 1).astype(np.float32)
    first = (lane == 0).astype(np.float32)
    c = np.stack([lo, hi, 1.0 - last, last, 1.0 - first, first])
    return jnp.asarray(np.broadcast_to(c[:, None, :], (6, rows, LANES)), BF16)


def _bf16_terms(x, n):
    terms = []
    for _ in range(n - 1):
        t = x.astype(BF16)
        terms.append(t)
        x = x - t.astype(F32)
    terms.append(x.astype(BF16))
    return terms


def _gates(gate_logits, expands):
    terms = _bf16_terms(jax.nn.sigmoid(gate_logits.astype(F32)), 2)
    return [sum(jnp.dot(t, ex, preferred_element_type=F32) for t in terms) for ex in expands]


def _silu(x):
    return x * jax.nn.sigmoid(x)


def _cmp_select_kernel(q_ref, gate_ref, z_ref, ckv_ref, bias_ref, ovl_ref, lc_ref, gexp_ref,
                       y_ref, sel_ref):
    g = pl.program_id(0)
    qi = pl.program_id(1)
    tq = q_ref.shape[1]
    q = q_ref[0]
    kk = ckv_ref[0, 0, 0]
    vv = ckv_ref[0, 1, 0]
    lane = lax.broadcasted_iota(jnp.int32, (tq, LANES), 1)

    psum = jnp.zeros((N_CMP_PAD, tq), F32)
    pairs = []
    for a in range(2):
        qp = q[:, a * LANES:(a + 1) * LANES]
        outs = []
        for e in range(2):
            s = lax.dot_general(kk, qp * lc_ref[e, :tq], _NT, preferred_element_type=F32)
            bias = bias_ref[2 * a + e]
            s = s + bias
            m = jnp.max(s, axis=0, keepdims=True)
            p = jnp.exp2(s - m)
            p = jnp.where(bias > 0.5 * MASKED, p / jnp.sum(p, axis=0, keepdims=True), 0.0)
            psum = psum + p
            outs.append(lax.dot_general(p.astype(BF16), vv, _TN, preferred_element_type=F32))
        pairs.append(jnp.where(lane < HEAD_DIM, outs[0], outs[1]))
    y_ref[0] = (jnp.concatenate(pairs, axis=1) * _gates(gate_ref[0], [gexp_ref[g]])[0]
                * _silu(z_ref[0].astype(F32))).astype(y_ref.dtype)

    ovl = ovl_ref[...]
    imp = sum(jnp.dot(ovl, t, preferred_element_type=F32) for t in _bf16_terms(psum, 3))
    t = qi * tq + lax.broadcasted_iota(jnp.int32, (N_SLC, tq), 1)
    blk = lax.broadcasted_iota(jnp.int32, (N_SLC, tq), 0)
    cur = lax.shift_right_logical(t, int(math.log2(SLC_BLOCK)))
    forced = (blk == 0) | (blk == cur) | (blk == cur - 1)
    score = jnp.where(blk * SLC_BLOCK <= t, imp + jnp.where(forced, FORCED_SCORE, 0.0), MASKED)
    cnt = jnp.zeros((N_SLC, tq), jnp.int32)
    for jp in range(N_SLC):
        row = score[jp:jp + 1, :]
        beats = (row > score) | ((row == score) & (blk > jp))
        cnt = cnt + beats.astype(jnp.int32)
    sel_ref[0, 0] = jnp.where(cnt < SLC_TOPN, 1.0, 0.0).astype(sel_ref.dtype)


def _cmp_select(proj, ckv, bias_cmp, ovl, lane_consts, gate_expand):
    b = proj.shape[0]
    tq = CMP_TQ
    grp_w = HEADS_PER_GROUP * HEAD_DIM
    return pl.pallas_call(
        _cmp_select_kernel,
        grid=(GROUPS, SEQ // tq, b),
        in_specs=[
            pl.BlockSpec((1, tq, grp_w), lambda g, qi, i: (i, qi, NSA_Q_COL // grp_w + g)),
            pl.BlockSpec((1, tq, LANES), lambda g, qi, i: (i, qi, NSA_G_COL // LANES)),
            pl.BlockSpec((1, tq, grp_w), lambda g, qi, i: (i, qi, NSA_Z_COL // grp_w + g)),
            pl.BlockSpec((1, 2, 1, N_CMP_PAD, 2 * HEAD_DIM), lambda g, qi, i: (i, 0, g, 0, 0)),
            pl.BlockSpec((HEADS_PER_GROUP, N_CMP_PAD, tq), lambda g, qi, i: (g, 0, qi)),
            pl.BlockSpec((N_SLC, N_CMP_PAD), lambda g, qi, i: (0, 0)),
            pl.BlockSpec(memory_space=pltpu.VMEM),
            pl.BlockSpec(memory_space=pltpu.VMEM),
        ],
        out_specs=[
            pl.BlockSpec((1, tq, grp_w), lambda g, qi, i: (i, qi, g)),
            pl.BlockSpec((1, 1, N_SLC, tq), lambda g, qi, i: (i, g, 0, qi)),
        ],
        out_shape=[
            jax.ShapeDtypeStruct((b, SEQ, NSA_WIDTH), BF16),
            jax.ShapeDtypeStruct((b, GROUPS, N_SLC, SEQ), BF16),
        ],
        compiler_params=_cparams(("parallel", "parallel", "parallel")),
        name="cmp_select",
    )(proj, proj, proj, ckv, bias_cmp, ovl, lane_consts, gate_expand)


def _token_kernel(q_ref, ks_ref, vs_ref, kw_ref, vw_ref, gate_ref, zs_ref, zw_ref, a_ref, lc_ref,
                  gexp_ref, sel_ref, kblk_ref, place_ref, placed_ref, dspread_ref, y_ref,
                  qm_ref, m_ref, acc_ref, s_bufs, p_bufs, c_bufs, kx_ref, vx_ref, gz_ref):
    g = pl.program_id(1)
    qi = pl.program_id(2)
    tq = q_ref.shape[1]
    tk = ATT_TK
    n_win = jnp.minimum(qi, WIN_SIZE // tk) + 1
    n_tiles = n_win + qi + 1

    @pl.when(qi == 0)
    def _():
        for c in range(SEQ // tk):
            rows = slice(c * tk, (c + 1) * tk)
            for branch, v_ref in enumerate((vs_ref, vw_ref)):
                vv = v_ref[0, rows, :]
                vx_ref[2 * branch, rows, :] = vv * lc_ref[2, :tk] + lc_ref[3, :tk]
                vx_ref[2 * branch + 1, rows, :] = vv * lc_ref[4, :tk] + lc_ref[5, :tk]
            kk = ks_ref[0, rows, :]
            for e in range(2):
                kx_ref[e, c] = (kk * lc_ref[e, :tk] + kblk_ref[e, c]).astype(F32).T.astype(BF16)
            kx_ref[2, c] = kw_ref[0, rows, :].astype(F32).T.astype(BF16)

    q = q_ref[0]
    qms = []
    for slot in range(HEADS_PER_GROUP):
        e, a = divmod(slot, 2)
        qms.append(q[:, a * LANES:(a + 1) * LANES] * lc_ref[e, :tq])
        qm_ref[1, slot * tq:(slot + 1) * tq, :] = qms[slot]
    flags = lax.dot_general(sel_ref[0, 0], place_ref[...], _TN, preferred_element_type=F32)
    fill = ((placed_ref[...] - flags) * MASKED).astype(BF16)
    for slot in range(HEADS_PER_GROUP):
        e = slot // 2
        qm_ref[0, slot * tq:(slot + 1) * tq, :] = qms[slot] + fill[:, e * LANES:(e + 1) * LANES]

    gates = _gates(gate_ref[0], [gexp_ref[0, g], gexp_ref[1, g]])
    for branch, z_ref in enumerate((zs_ref, zw_ref)):
        gz_ref[branch] = gates[branch] * _silu(z_ref[0].astype(F32))

    def tile(j):
        win = (j < n_win).astype(jnp.int32)
        t = j - (1 - win) * n_win
        return win, jnp.clip(qi - t, 0, qi), jnp.where(win == 1, t, jnp.where(t < 2, t, 3))

    def scores(j, s_ref):
        win, ki, _ = tile(j)
        for e in range(2):
            pair = slice(2 * e * tq, 2 * (e + 1) * tq)
            s_ref[pair, :] = jnp.dot(qm_ref[win, pair, :], kx_ref[jnp.where(win == 1, 2, e), ki],
                                     preferred_element_type=F32)

    def softmax(j, s_ref, p_ref, c_ref):
        win, _, kind = tile(j)
        for slot in range(HEADS_PER_GROUP):
            e, a = divmod(slot, 2)
            rows = slice(slot * tq, (slot + 1) * tq)
            s = s_ref[rows, :] + a_ref[kind, HEADS_PER_GROUP * g + 2 * a + e]
            m_prev = m_ref[win, rows, :]
            m_new = jnp.maximum(m_prev, jnp.max(s, axis=1, keepdims=True))
            c_ref[rows, :] = jnp.exp2(m_prev - m_new)
            x = s - jnp.concatenate([m_new] * (tk // LANES), axis=1)
            p_ref[rows, :] = jnp.exp2(x.astype(BF16))
            m_ref[win, rows, :] = m_new

    def values(j, p_ref, c_ref):
        win, ki, _ = tile(j)
        rows = pl.ds(pl.multiple_of(ki * tk, tk), tk)
        for e in range(2):
            pair = slice(2 * e * tq, 2 * (e + 1) * tq)
            pv = jnp.dot(p_ref[pair, :], vx_ref[2 * win + e, rows, :], preferred_element_type=F32)
            acc_ref[win, pair, :] = c_ref[pair, :] * acc_ref[win, pair, :] + pv

    m_ref[...] = jnp.full(m_ref.shape, -jnp.inf, F32)
    acc_ref[...] = jnp.zeros(acc_ref.shape, F32)
    bufs = [(s_bufs.at[i], p_bufs.at[i], c_bufs.at[i]) for i in range(2)]

    def step(j, parity):
        cur, nxt = bufs[parity], bufs[1 - parity]
        values(j - 1, nxt[1], nxt[2])
        softmax(j, *cur)
        scores(j + 1, nxt[0])

    scores(0, bufs[0][0])
    softmax(0, *bufs[0])
    scores(1, bufs[1][0])
    step(1, 1)

    def steps(first, count):
        for k in range(count):
            step(first + k, k % 2)

    n_steps = n_tiles - 2
    n_quads = lax.shift_right_logical(n_steps, 2)

    def four_steps(i, carry):
        steps(4 * i + 2, 4)
        return carry
    lax.fori_loop(0, n_quads, four_steps, 0)

    @pl.when(lax.bitwise_and(n_steps, 2) == 2)
    def _():
        steps(4 * n_quads + 2, 2)

    @pl.when(lax.bitwise_and(n_steps, 1) == 1)
    def _():
        steps(n_tiles - 1, 1)

    lane = lax.broadcasted_iota(jnp.int32, (tq, LANES), 1)

    def gated_output(branch):
        pairs = []
        for a in range(2):
            acc_e = acc_ref[branch, a * tq:(a + 1) * tq, :]
            acc_o = acc_ref[branch, (2 + a) * tq:(3 + a) * tq, :]
            both = jnp.concatenate([acc_e, acc_o], axis=1)
            den = sum(jnp.dot(t, dspread_ref[...], preferred_element_type=F32)
                      for t in _bf16_terms(both, 2))
            pairs.append(jnp.where(lane < HEAD_DIM, acc_e, acc_o) / den)
        return jnp.concatenate(pairs, axis=1) * gz_ref[branch]

    y_win = gated_output(1)
    last = lax.bitwise_and(n_tiles - 1, 1)
    values(n_tiles - 1, p_bufs.at[last], c_bufs.at[last])
    y_ref[0] = (y_win + gated_output(0)).astype(y_ref.dtype)


def _token_attention(proj, a_tiles, lane_consts, gate_expand, sel, sel_consts):
    b = proj.shape[0]
    tq, tk = ATT_TQ, ATT_TK
    grp_w = HEADS_PER_GROUP * HEAD_DIM
    k_blk = NSA_K_COL // LANES
    v_blk = NSA_V_COL // LANES
    z_blk = NSA_Z_COL // grp_w
    kv_spec = lambda blk: pl.BlockSpec((1, SEQ, LANES), lambda i, g, qi: (i, 0, blk + g))
    z_spec = lambda blk: pl.BlockSpec((1, tq, grp_w), lambda i, g, qi: (i, qi, blk + g))
    whole = pl.BlockSpec(memory_space=pltpu.VMEM)
    return pl.pallas_call(
        _token_kernel,
        grid=(b, GROUPS, SEQ // tq),
        in_specs=[
            pl.BlockSpec((1, tq, grp_w), lambda i, g, qi: (i, qi, NSA_Q_COL // grp_w + g)),
            kv_spec(k_blk), kv_spec(v_blk), kv_spec(k_blk + GROUPS), kv_spec(v_blk + GROUPS),
            pl.BlockSpec((1, tq, LANES), lambda i, g, qi: (i, qi, NSA_G_COL // LANES)),
            z_spec(z_blk + GROUPS), z_spec(z_blk + 2 * GROUPS),
            whole, whole, whole,
            pl.BlockSpec((1, 1, N_SLC, tq), lambda i, g, qi: (i, g, 0, qi)),
            whole, whole, whole, whole,
        ],
        out_specs=pl.BlockSpec((1, tq, grp_w), lambda i, g, qi: (i, qi, g)),
        out_shape=jax.ShapeDtypeStruct((b, SEQ, NSA_WIDTH), BF16),
        scratch_shapes=[
            pltpu.VMEM((2, HEADS_PER_GROUP * tq, LANES), BF16),
            pltpu.VMEM((2, HEADS_PER_GROUP * tq, LANES), F32),
            pltpu.VMEM((2, HEADS_PER_GROUP * tq, LANES), F32),
            pltpu.VMEM((2, HEADS_PER_GROUP * tq, tk), F32),
            pltpu.VMEM((2, HEADS_PER_GROUP * tq, tk), BF16),
            pltpu.VMEM((2, HEADS_PER_GROUP * tq, LANES), F32),
            pltpu.VMEM((3, SEQ // tk, LANES, tk), BF16),
            pltpu.VMEM((4, SEQ, LANES), BF16),
            pltpu.VMEM((2, tq, grp_w), F32),
        ],
        compiler_params=_cparams(("parallel", "parallel", "arbitrary")),
        name="token_attention",
    )(proj, proj, proj, proj, proj, proj, proj, proj, a_tiles, lane_consts, gate_expand,
      sel, *sel_consts)


def _retention_kernel(q_ref, k_ref, v_ref, z_ref, cos_ref, sin_ref, inner_ref, xi_ref, zeta_ref,
                      gn_ref, y_ref, state_ref, *, decays):
    @pl.when(pl.program_id(1) == 0)
    def _():
        state_ref[...] = jnp.zeros(state_ref.shape, F32)

    cos = cos_ref[...]
    sin = sin_ref[...]
    half = RET_QK_DIM // 2

    def rot(x):
        x1, x2 = x[:, :half], x[:, half:]
        return jnp.concatenate([x1 * cos - x2 * sin, x1 * sin + x2 * cos], axis=1)

    for h in range(RET_HEADS):
        qs = slice(h * RET_QK_DIM, (h + 1) * RET_QK_DIM)
        vs = slice(h * RET_V_DIM, (h + 1) * RET_V_DIM)
        qr = rot(q_ref[0, :, qs].astype(F32))
        kr = rot(k_ref[0, :, qs].astype(F32)) * (RET_QK_DIM ** -0.5)
        qb = qr.astype(BF16)
        vh = v_ref[0, :, vs]
        attn = lax.dot_general(qb, kr.astype(BF16), _NT, preferred_element_type=F32) * inner_ref[h]
        st = state_ref[h]
        o = (jnp.dot(attn.astype(BF16), vh, preferred_element_type=F32)
             + jnp.dot(qb, st.astype(BF16), preferred_element_type=F32) * xi_ref[h])
        kz = (kr * zeta_ref[h]).astype(BF16)
        state_ref[h] = st * decays[h] + lax.dot_general(kz, vh, _TN, preferred_element_type=F32)
        mu = jnp.mean(o, axis=1, keepdims=True)
        d = o - mu
        var = jnp.mean(d * d, axis=1, keepdims=True)
        on = d * lax.rsqrt(var + GN_EPS) * gn_ref[h]
        y_ref[0, :, vs] = (on * _silu(z_ref[0, :, vs].astype(F32))).astype(y_ref.dtype)


def _retention(proj, gn_gain, tables):
    b = proj.shape[0]
    c = RET_CHUNK
    cos, sin, inner, xi, zeta, decays = tables
    v_blk = 2 * RET_QK_WIDTH // RET_WIDTH
    return pl.pallas_call(
        functools.partial(_retention_kernel, decays=decays),
        grid=(b, SEQ // c),
        in_specs=[
            pl.BlockSpec((1, c, RET_QK_WIDTH), lambda i, j: (i, j, 0)),
            pl.BlockSpec((1, c, RET_QK_WIDTH), lambda i, j: (i, j, 1)),
            pl.BlockSpec((1, c, RET_WIDTH), lambda i, j: (i, j, v_blk)),
            pl.BlockSpec((1, c, RET_WIDTH), lambda i, j: (i, j, v_blk + 1)),
            pl.BlockSpec((c, RET_QK_DIM // 2), lambda i, j: (j, 0)),
            pl.BlockSpec((c, RET_QK_DIM // 2), lambda i, j: (j, 0)),
            pl.BlockSpec((RET_HEADS, c, c), lambda i, j: (0, 0, 0)),
            pl.BlockSpec((RET_HEADS, c, 1), lambda i, j: (0, 0, 0)),
            pl.BlockSpec((RET_HEADS, c, 1), lambda i, j: (0, 0, 0)),
            pl.BlockSpec((RET_HEADS, 1, RET_V_DIM), lambda i, j: (0, 0, 0)),
        ],
        out_specs=pl.BlockSpec((1, c, RET_WIDTH), lambda i, j: (i, j, 0)),
        out_shape=jax.ShapeDtypeStruct((b, SEQ, RET_WIDTH), BF16),
        scratch_shapes=[pltpu.VMEM((RET_HEADS, RET_QK_DIM, RET_V_DIM), F32)],
        compiler_params=_cparams(("parallel", "arbitrary")),
        name="retention",
    )(proj, proj, proj, proj, cos, sin, inner, xi, zeta, gn_gain.reshape(RET_HEADS, 1, RET_V_DIM))


def _t5_bucket_np(dist):
    n = np.maximum(dist, 0)
    max_exact = REL_BUCKETS // 2
    nf = np.maximum(n, 1).astype(np.float64)
    large = max_exact + (np.log(nf / max_exact) / math.log(REL_MAX_DIST / max_exact)
                         * (REL_BUCKETS - max_exact)).astype(np.int64)
    large = np.minimum(large, REL_BUCKETS - 1)
    return np.where(n < max_exact, n, large).astype(np.int32)


def _skewed(vec, rows, stride, cols):
    p = vec.shape[-1]
    lead = vec.shape[:-1]
    flat = jnp.tile(vec, (1,) * len(lead) + (rows,))[..., :rows * (p - stride)]
    return flat.reshape(lead + (rows, p - stride))[..., :cols]


def _nsa_tables(table):
    tq, tk = ATT_TQ, ATT_TK
    assert tq == tk
    tab_t = table.T * LOG2E
    period = 2 * SEQ
    x = np.arange(period)
    ok = (x >= CMP_BLOCK - 1) & (x < SEQ)
    vec = jnp.take(tab_t, jnp.asarray(_t5_bucket_np(x - (CMP_BLOCK - 1))), axis=1)
    vec = jnp.where(jnp.asarray(ok)[None, :], vec, MASKED)
    bias_cmp = _skewed(vec, N_CMP_PAD, CMP_STRIDE, SEQ)
    rel_t = tab_t - tab_t[:, REL_BUCKETS - 1:]
    period = 2 * tq
    x = np.arange(period)
    x = np.where(x < tq, x, x - period)
    tiles = []
    for off in range(3):
        d = off * tk - x
        ok = (d >= 0) & (d < WIN_SIZE) if off == 2 else (d >= 0)
        vec = jnp.take(rel_t, jnp.asarray(_t5_bucket_np(d)), axis=1)
        vec = jnp.where(jnp.asarray(ok)[None, :], vec, MASKED)
        tiles.append(_skewed(vec, tq, 1, tk))
    a_tiles = jnp.stack(tiles + [jnp.zeros_like(tiles[0])])
    cs = np.arange(N_CMP_PAD)[None, :] * CMP_STRIDE
    jj = np.arange(N_SLC)[:, None]
    ovl = ((cs < (jj + 1) * SLC_BLOCK) & (cs + CMP_BLOCK > jj * SLC_BLOCK)
           & (np.arange(N_CMP_PAD)[None, :] < N_CMP))
    ovl = jnp.asarray(ovl.astype(np.float32), BF16)
    lane = np.arange(LANES).reshape(1, 1, 1, LANES)
    key = np.arange(SEQ).reshape(1, SEQ // tk, tk, 1)
    base = np.array([HEAD_DIM, 0]).reshape(2, 1, 1, 1)
    key_blocks = jnp.asarray((lane == base + key // SLC_BLOCK).astype(np.float32), BF16)
    col = np.arange(2 * LANES).reshape(1, -1)
    jb = np.arange(N_SLC).reshape(-1, 1)
    place = (col == HEAD_DIM + jb) | (col == LANES + jb)
    row = np.arange(2 * LANES).reshape(-1, 1)
    out = np.arange(LANES).reshape(1, -1)
    den_spread = ((row == LANES - 1) & (out < HEAD_DIM)) | ((row == LANES) & (out >= HEAD_DIM))
    sel_consts = (key_blocks, jnp.asarray(place.astype(np.float32), BF16),
                  jnp.asarray(place.any(axis=0, keepdims=True).astype(np.float32)),
                  jnp.asarray(den_spread.astype(np.float32), BF16))
    c = np.arange(LANES).reshape(1, 1, LANES, 1)
    col = np.arange(HEADS_PER_GROUP * HEAD_DIM).reshape(1, 1, 1, -1) // HEAD_DIM
    br = np.arange(3).reshape(3, 1, 1, 1)
    gg = np.arange(GROUPS).reshape(1, GROUPS, 1, 1)
    gate_expand = jnp.asarray((c == br * HEADS + HEADS_PER_GROUP * gg + col).astype(np.float32),
                              BF16)
    return bias_cmp, a_tiles, ovl, sel_consts, _lane_consts(max(tq, CMP_TQ)), gate_expand


def _retention_tables():
    c = RET_CHUNK
    log_g = jnp.log(1.0 - 2.0 ** (-5.0 - jnp.arange(RET_HEADS, dtype=F32)))
    i = jnp.arange(c, dtype=F32)
    diff = i[:, None] - i[None, :]
    inner = jnp.where(diff >= 0, jnp.exp(diff[None] * log_g[:, None, None]), 0.0)
    xi = jnp.exp((i + 1.0)[None, :] * log_g[:, None])[:, :, None]
    zeta = jnp.exp((c - 1.0 - i)[None, :] * log_g[:, None])[:, :, None]
    decays = tuple(float((1.0 - 2.0 ** (-5.0 - h)) ** c) for h in range(RET_HEADS))
    inv = 1.0 / (ROPE_BASE ** jnp.linspace(0.0, 1.0, RET_QK_DIM // 2, dtype=F32))
    ang = jnp.arange(SEQ, dtype=F32)[:, None] * inv[None, :]
    return jnp.cos(ang), jnp.sin(ang), inner, xi, zeta, decays


def _nsa_w_in_layout(w):
    d = w.shape[0]
    q = w[:, :NSA_WIDTH] * (HEAD_DIM ** -0.5 * LOG2E)
    kv0 = NSA_WIDTH
    g0 = kv0 + 6 * KV_WIDTH
    z0 = g0 + 3 * HEADS
    slab = lambda n: w[:, kv0 + n * KV_WIDTH: kv0 + (n + 1) * KV_WIDTH]
    cols = [q, w[:, z0:z0 + 3 * NSA_WIDTH]]
    for n in (2, 4, 3, 5):
        for g in range(GROUPS):
            part = slab(n)[:, g * HEAD_DIM:(g + 1) * HEAD_DIM]
            cols += [part, part]
    cols += [slab(0), slab(1), w[:, g0:z0]]
    used = NSA_G_COL + 3 * HEADS
    cols.append(jnp.zeros((d, NSA_PROJ_PAD - used), w.dtype))
    return jnp.concatenate(cols, axis=1).astype(BF16)


def _nsa_layer(h2d, b, pre_gain, post_gain, w_in, w_out, k_pos, k_w1, k_w2, v_pos, v_w1, v_w2, tabs):
    bias_cmp, a_tiles, ovl, sel_consts, lane_consts, gate_expand = tabs
    proj = _norm_proj(h2d, pre_gain, _nsa_w_in_layout(w_in), NSA_PROJ_TN)
    proj = proj.reshape(b, SEQ, NSA_PROJ_PAD)
    ckv = proj[:, :, NSA_CKV_COL:NSA_CKV_COL + 2 * KV_WIDTH]
    ckv = ckv.reshape(b, N_CMP_PAD, CMP_STRIDE, 2, GROUPS, HEAD_DIM).transpose(0, 3, 4, 1, 2, 5)
    ckv = ckv.reshape(b, 2, GROUPS, N_CMP_PAD, CMP_STRIDE * HEAD_DIM)
    pos = jnp.stack([k_pos, v_pos]).reshape(2, 2, CMP_STRIDE * HEAD_DIM)
    w1 = jnp.stack([k_w1, v_w1]).astype(BF16)
    w2 = jnp.stack([k_w2, v_w2])
    w2d = jnp.concatenate([w2, w2], axis=2).astype(BF16)
    ckv_c = _compress(ckv, pos, w1, w2d)
    y_cmp, sel = _cmp_select(proj, ckv_c, bias_cmp, ovl, lane_consts, gate_expand[0])
    y_tok = _token_attention(proj, a_tiles, lane_consts, gate_expand[1:], sel, sel_consts)
    parts = [y.reshape(b * SEQ, NSA_WIDTH) for y in (y_cmp, y_tok)]
    return _out_post(parts, w_out.astype(BF16), h2d, post_gain)


def _ret_layer(h2d, b, pre_gain, post_gain, w_in, w_out, gn_gain, tabs):
    proj = _norm_proj(h2d, pre_gain, w_in.astype(BF16), RET_PROJ_TN).reshape(b, SEQ, -1)
    y = _retention(proj, gn_gain, tabs)
    return _out_post([y.reshape(b * SEQ, RET_WIDTH)], w_out.astype(BF16), h2d, post_gain)


def kernel(x, pre_norm_gain, post_norm_gain, rel_bias_table, nsa_w_in, nsa_w_out, nsa_cmp_k_pos, nsa_cmp_k_w1, nsa_cmp_k_w2, nsa_cmp_v_pos, nsa_cmp_v_w1, nsa_cmp_v_w2, ret_w_in, ret_w_out, ret_gn_gain):
    b, s, d = x.shape
    assert s == SEQ and d == D_MODEL
    nsa_tabs = _nsa_tables(rel_bias_table)
    ret_tabs = _retention_tables()
    h = x.reshape(b * s, d)
    for layer in range(DEPTH):
        slot = layer // 2
        if layer % 2 == 0:
            h = _nsa_layer(h, b, pre_norm_gain[layer], post_norm_gain[layer], nsa_w_in[slot],
                           nsa_w_out[slot], nsa_cmp_k_pos[slot], nsa_cmp_k_w1[slot],
                           nsa_cmp_k_w2[slot], nsa_cmp_v_pos[slot], nsa_cmp_v_w1[slot],
                           nsa_cmp_v_w2[slot], nsa_tabs)
        else:
            h = _ret_layer(h, b, pre_norm_gain[layer], post_norm_gain[layer], ret_w_in[slot],
                           ret_w_out[slot], ret_gn_gain[slot], ret_tabs)
    return h.reshape(b, s, d)
```

```python
import functools
import math

import numpy as np
import jax
import jax.numpy as jnp
from jax import lax
from jax.experimental import pallas as pl
from jax.experimental.pallas import tpu as pltpu

F32 = jnp.float32
BF16 = jnp.bfloat16

D_MODEL = 1024
SEQ = 2048
DEPTH = 4
RMS_EPS = 1e-6
GN_EPS = 1e-6
MASKED = -1e30
LOG2E = math.log2(math.e)

HEADS = 16
HEAD_DIM = 64
GROUPS = 4
HEADS_PER_GROUP = HEADS // GROUPS
NSA_WIDTH = HEADS * HEAD_DIM
KV_WIDTH = GROUPS * HEAD_DIM
CMP_BLOCK = 32
CMP_STRIDE = 16
CMP_HIDDEN = 256
N_CMP = (SEQ - CMP_BLOCK) // CMP_STRIDE + 1
N_CMP_PAD = 128
SLC_BLOCK = 64
N_SLC = SEQ // SLC_BLOCK
SLC_TOPN = 16
WIN_SIZE = 512
FORCED_SCORE = 1e3
REL_BUCKETS = 32
REL_MAX_DIST = 128

RET_HEADS = 4
RET_QK_DIM = 256
RET_V_DIM = 512
RET_QK_WIDTH = RET_HEADS * RET_QK_DIM
RET_WIDTH = RET_HEADS * RET_V_DIM
ROPE_BASE = 10000.0

LANES = 128
VMEM_LIMIT_BYTES = 56 * 1024 * 1024

PROJ_TM = 1024
RET_PROJ_TN = 2048
NSA_PROJ_TN = 3456
POST_TM = 512
ATT_TQ = 256
ATT_TK = 256
CMP_TQ = 1024
RET_CHUNK = 256

NSA_Q_COL = 0
NSA_Z_COL = 1024
NSA_K_COL = 4096
NSA_V_COL = 5120
NSA_CKV_COL = 6144
NSA_G_COL = 6656
NSA_PROJ_PAD = 6912

_NT = (((1,), (1,)), ((), ()))
_TN = (((0,), (0,)), ((), ()))


def _cparams(sem):
    return pltpu.CompilerParams(dimension_semantics=sem, vmem_limit_bytes=VMEM_LIMIT_BYTES)


def _norm_proj_kernel(x_ref, g_ref, w_ref, o_ref, xn_ref):
    @pl.when(pl.program_id(1) == 0)
    def _():
        x = x_ref[...]
        ms = jnp.mean(x * x, axis=-1, keepdims=True)
        xn_ref[...] = (x * lax.rsqrt(ms + RMS_EPS) * g_ref[...]).astype(BF16)

    o_ref[...] = jnp.dot(xn_ref[...], w_ref[...], preferred_element_type=F32).astype(o_ref.dtype)


def _norm_proj(x2d, gain, w_bf16, tn):
    m, d = x2d.shape
    n = w_bf16.shape[1]
    assert n % tn == 0
    return pl.pallas_call(
        _norm_proj_kernel,
        grid=(m // PROJ_TM, n // tn),
        in_specs=[
            pl.BlockSpec((PROJ_TM, d), lambda i, j: (i, 0)),
            pl.BlockSpec((1, d), lambda i, j: (0, 0)),
            pl.BlockSpec((d, tn), lambda i, j: (0, j)),
        ],
        out_specs=pl.BlockSpec((PROJ_TM, tn), lambda i, j: (i, j)),
        out_shape=jax.ShapeDtypeStruct((m, n), BF16),
        scratch_shapes=[pltpu.VMEM((PROJ_TM, d), BF16)],
        compiler_params=_cparams(("parallel", "arbitrary")),
        name="norm_proj",
    )(x2d, gain.reshape(1, d), w_bf16)


def _out_post_kernel(*refs, n_parts):
    y_refs = refs[:n_parts]
    w_ref, h_ref, g_ref, o_ref = refs[n_parts:]
    y = y_refs[0][...].astype(F32)
    for r in y_refs[1:]:
        y = y + r[...].astype(F32)
    t = jnp.dot(y.astype(BF16), w_ref[...], preferred_element_type=F32)
    ms = jnp.mean(t * t, axis=-1, keepdims=True)
    o_ref[...] = h_ref[...] + t * lax.rsqrt(ms + RMS_EPS) * g_ref[...]


def _out_post(parts, w_bf16, h2d, gain):
    m, d = h2d.shape
    k = w_bf16.shape[0]
    n_parts = len(parts)
    return pl.pallas_call(
        functools.partial(_out_post_kernel, n_parts=n_parts),
        grid=(m // POST_TM,),
        in_specs=[pl.BlockSpec((POST_TM, k), lambda i: (i, 0)) for _ in parts] + [
            pl.BlockSpec((k, d), lambda i: (0, 0)),
            pl.BlockSpec((POST_TM, d), lambda i: (i, 0)),
            pl.BlockSpec((1, d), lambda i: (0, 0)),
        ],
        out_specs=pl.BlockSpec((POST_TM, d), lambda i: (i, 0)),
        out_shape=jax.ShapeDtypeStruct((m, d), F32),
        compiler_params=_cparams(("parallel",)),
        name="out_post",
    )(*parts, w_bf16, h2d, gain.reshape(1, d))


def _compress_kernel(x_ref, pos_ref, w1_ref, w2_ref, o_ref):
    half = CMP_STRIDE * HEAD_DIM
    for which in range(2):
        x = x_ref[0, which, 0].astype(F32)
        xa = (x + pos_ref[which, 0:1, :]).astype(BF16)
        xb = (x + pos_ref[which, 1:2, :]).astype(BF16)
        pa = jnp.dot(xa, w1_ref[which, :half, :], preferred_element_type=F32)
        pb = jnp.dot(xb, w1_ref[which, half:, :], preferred_element_type=F32)
        hid = pa + pltpu.roll(pb, N_CMP_PAD - 1, 0)
        hid = hid * jax.nn.sigmoid(hid)
        o_ref[0, which, 0] = jnp.dot(hid.astype(BF16), w2_ref[which],
                                     preferred_element_type=F32).astype(o_ref.dtype)


def _compress(ckv_rows, pos, w1, w2d):
    b = ckv_rows.shape[0]
    row_w = CMP_STRIDE * HEAD_DIM
    return pl.pallas_call(
        _compress_kernel,
        grid=(b, GROUPS),
        in_specs=[
            pl.BlockSpec((1, 2, 1, N_CMP_PAD, row_w), lambda i, g: (i, 0, g, 0, 0)),
            pl.BlockSpec((2, 2, row_w), lambda i, g: (0, 0, 0)),
            pl.BlockSpec((2, 2 * row_w, CMP_HIDDEN), lambda i, g: (0, 0, 0)),
            pl.BlockSpec((2, CMP_HIDDEN, 2 * HEAD_DIM), lambda i, g: (0, 0, 0)),
        ],
        out_specs=pl.BlockSpec((1, 2, 1, N_CMP_PAD, 2 * HEAD_DIM), lambda i, g: (i, 0, g, 0, 0)),
        out_shape=jax.ShapeDtypeStruct((b, 2, GROUPS, N_CMP_PAD, 2 * HEAD_DIM), BF16),
        compiler_params=_cparams(("parallel", "parallel")),
        name="compress",
    )(ckv_rows, pos, w1, w2d)


def _lane_consts(rows):
    lane = np.arange(LANES)
    lo = (lane < HEAD_DIM).astype(np.float32)
    hi = (lane >= HEAD_DIM).astype(np.float32)
    last = (lane == LANES - 1).astype(np.float32)
    first = (lane == 0).astype(np.float32)
    c = np.stack([lo, hi, 1.0 - last, last, 1.0 - first, first])
    return jnp.asarray(np.broadcast_to(c[:, None, :], (6, rows, LANES)), BF16)


def _bf16_terms(x, n):
    terms = []
    for _ in range(n - 1):
        t = x.astype(BF16)
        terms.append(t)
        x = x - t.astype(F32)
    terms.append(x.astype(BF16))
    return terms


def _gates(gate_logits, expands):
    terms = _bf16_terms(jax.nn.sigmoid(gate_logits.astype(F32)), 2)
    return [sum(jnp.dot(t, ex, preferred_element_type=F32) for t in terms) for ex in expands]


def _silu(x):
    return x * jax.nn.sigmoid(x)


def _cmp_select_kernel(q_ref, gate_ref, z_ref, ckv_ref, bias_ref, ovl_ref, lc_ref, gexp_ref,
                       y_ref, sel_ref):
    g = pl.program_id(0)
    qi = pl.program_id(1)
    tq = q_ref.shape[1]
    q = q_ref[0]
    kk = ckv_ref[0, 0, 0]
    vv = ckv_ref[0, 1, 0]
    lane = lax.broadcasted_iota(jnp.int32, (tq, LANES), 1)

    psum = jnp.zeros((N_CMP_PAD, tq), F32)
    pairs = []
    for a in range(2):
        qp = q[:, a * LANES:(a + 1) * LANES]
        outs = []
        for e in range(2):
            s = lax.dot_general(kk, qp * lc_ref[e, :tq], _NT, preferred_element_type=F32)
            bias = bias_ref[2 * a + e]
            s = s + bias
            m = jnp.max(s, axis=0, keepdims=True)
            p = jnp.exp2(s - m)
            p = jnp.where(bias > 0.5 * MASKED, p / jnp.sum(p, axis=0, keepdims=True), 0.0)
            psum = psum + p
            outs.append(lax.dot_general(p.astype(BF16), vv, _TN, preferred_element_type=F32))
        pairs.append(jnp.where(lane < HEAD_DIM, outs[0], outs[1]))
    y_ref[0] = (jnp.concatenate(pairs, axis=1) * _gates(gate_ref[0], [gexp_ref[g]])[0]
                * _silu(z_ref[0].astype(F32))).astype(y_ref.dtype)

    ovl = ovl_ref[...]
    imp = sum(jnp.dot(ovl, t, preferred_element_type=F32) for t in _bf16_terms(psum, 3))
    t = qi * tq + lax.broadcasted_iota(jnp.int32, (N_SLC, tq), 1)
    blk = lax.broadcasted_iota(jnp.int32, (N_SLC, tq), 0)
    cur = lax.shift_right_logical(t, int(math.log2(SLC_BLOCK)))
    forced = (blk == 0) | (blk == cur) | (blk == cur - 1)
    score = jnp.where(blk * SLC_BLOCK <= t, imp + jnp.where(forced, FORCED_SCORE, 0.0), MASKED)
    cnt = jnp.zeros((N_SLC, tq), jnp.int32)
    for jp in range(N_SLC):
        row = score[jp:jp + 1, :]
        beats = (row > score) | ((row == score) & (blk > jp))
        cnt = cnt + beats.astype(jnp.int32)
    sel_ref[0, 0] = jnp.where(cnt < SLC_TOPN, 1.0, 0.0).astype(sel_ref.dtype)


def _cmp_select(proj, ckv, bias_cmp, ovl, lane_consts, gate_expand):
    b = proj.shape[0]
    tq = CMP_TQ
    grp_w = HEADS_PER_GROUP * HEAD_DIM
    return pl.pallas_call(
        _cmp_select_kernel,
        grid=(GROUPS, SEQ // tq, b),
        in_specs=[
            pl.BlockSpec((1, tq, grp_w), lambda g, qi, i: (i, qi, NSA_Q_COL // grp_w + g)),
            pl.BlockSpec((1, tq, LANES), lambda g, qi, i: (i, qi, NSA_G_COL // LANES)),
            pl.BlockSpec((1, tq, grp_w), lambda g, qi, i: (i, qi, NSA_Z_COL // grp_w + g)),
            pl.BlockSpec((1, 2, 1, N_CMP_PAD, 2 * HEAD_DIM), lambda g, qi, i: (i, 0, g, 0, 0)),
            pl.BlockSpec((HEADS_PER_GROUP, N_CMP_PAD, tq), lambda g, qi, i: (g, 0, qi)),
            pl.BlockSpec((N_SLC, N_CMP_PAD), lambda g, qi, i: (0, 0)),
            pl.BlockSpec(memory_space=pltpu.VMEM),
            pl.BlockSpec(memory_space=pltpu.VMEM),
        ],
        out_specs=[
            pl.BlockSpec((1, tq, grp_w), lambda g, qi, i: (i, qi, g)),
            pl.BlockSpec((1, 1, N_SLC, tq), lambda g, qi, i: (i, g, 0, qi)),
        ],
        out_shape=[
            jax.ShapeDtypeStruct((b, SEQ, NSA_WIDTH), BF16),
            jax.ShapeDtypeStruct((b, GROUPS, N_SLC, SEQ), BF16),
        ],
        compiler_params=_cparams(("parallel", "parallel", "parallel")),
        name="cmp_select",
    )(proj, proj, proj, ckv, bias_cmp, ovl, lane_consts, gate_expand)


def _token_kernel(q_ref, ks_ref, vs_ref, kw_ref, vw_ref, gate_ref, zs_ref, zw_ref, a_ref, lc_ref,
                  gexp_ref, sel_ref, kblk_ref, place_ref, placed_ref, dspread_ref, y_ref,
                  qm_ref, m_ref, acc_ref, s_bufs, p_bufs, c_bufs, kx_ref, vx_ref, gz_ref):
    g = pl.program_id(1)
    pair_id = pl.program_id(2)
    tq = ATT_TQ
    tk = ATT_TK

    def stream_len(qi_):
        return jnp.minimum(qi_, WIN_SIZE // tk) + 1 + qi_ + 1

    @pl.when(pair_id == 0)
    def _():
        for c in range(SEQ // tk):
            rows = slice(c * tk, (c + 1) * tk)
            for branch, v_ref in enumerate((vs_ref, vw_ref)):
                vv = v_ref[0, rows, :]
                vx_ref[2 * branch, rows, :] = vv * lc_ref[2, :tk] + lc_ref[3, :tk]
                vx_ref[2 * branch + 1, rows, :] = vv * lc_ref[4, :tk] + lc_ref[5, :tk]
            kk = ks_ref[0, rows, :]
            for e in range(2):
                kx_ref[e, c] = (kk * lc_ref[e, :tk] + kblk_ref[e, c]).astype(F32).T.astype(BF16)
            kx_ref[2, c] = kw_ref[0, rows, :].astype(F32).T.astype(BF16)

    def tile(qi_, j):
        n_win = jnp.minimum(qi_, WIN_SIZE // tk) + 1
        win = (j < n_win).astype(jnp.int32)
        t = j - (1 - win) * n_win
        return win, jnp.clip(qi_ - t, 0, qi_), jnp.where(win == 1, t, jnp.where(t < 2, t, 3))

    def scores(qi, j, s_ref):
        win, ki, _ = tile(qi, j)
        for e in range(2):
            pair = slice(2 * e * tq, 2 * (e + 1) * tq)
            s_ref[pair, :] = jnp.dot(qm_ref[win, pair, :], kx_ref[jnp.where(win == 1, 2, e), ki],
                                     preferred_element_type=F32)

    def softmax(qi, j, s_ref, p_ref, c_ref):
        win, _, kind = tile(qi, j)
        for slot in range(HEADS_PER_GROUP):
            e, a = divmod(slot, 2)
            rows = slice(slot * tq, (slot + 1) * tq)
            s = s_ref[rows, :] + a_ref[kind, HEADS_PER_GROUP * g + 2 * a + e]
            m_prev = m_ref[win, rows, :]
            m_new = jnp.maximum(m_prev, jnp.max(s, axis=1, keepdims=True))
            c_ref[rows, :] = jnp.exp2(m_prev - m_new)
            x = s - jnp.concatenate([m_new] * (tk // LANES), axis=1)
            p_ref[rows, :] = jnp.exp2(x.astype(BF16))
            m_ref[win, rows, :] = m_new

    def values(qi_, par_, j, p_ref, c_ref):
        win, ki, _ = tile(qi_, j)
        rows = pl.ds(pl.multiple_of(ki * tk, tk), tk)
        for e in range(2):
            pair = slice(2 * e * tq, 2 * (e + 1) * tq)
            pv = jnp.dot(p_ref[pair, :], vx_ref[2 * win + e, rows, :], preferred_element_type=F32)
            acc = acc_ref.at[2 * par_ + win]
            acc[pair, :] = c_ref[pair, :] * acc[pair, :] + pv

    lane = lax.broadcasted_iota(jnp.int32, (tq, LANES), 1)

    def gated_output(par_, branch):
        acc = acc_ref.at[2 * par_ + branch]
        pairs = []
        for a in range(2):
            acc_e = acc[a * tq:(a + 1) * tq, :]
            acc_o = acc[(2 + a) * tq:(3 + a) * tq, :]
            both = jnp.concatenate([acc_e, acc_o], axis=1)
            den = sum(jnp.dot(t, dspread_ref[...], preferred_element_type=F32)
                      for t in _bf16_terms(both, 2))
            pairs.append(jnp.where(lane < HEAD_DIM, acc_e, acc_o) / den)
        return jnp.concatenate(pairs, axis=1) * gz_ref[2 * par_ + branch]

    bufs = [(s_bufs.at[i], p_bufs.at[i], c_bufs.at[i]) for i in range(2)]

    def step(qi, half, j, parity):
        cur, nxt = bufs[parity], bufs[1 - parity]
        values(qi, half, j - 1, nxt[1], nxt[2])
        softmax(qi, j, *cur)
        scores(qi, j + 1, nxt[0])

    def enter(qi, half):
        rows = slice(half * tq, (half + 1) * tq)
        q = q_ref[0, rows, :]
        qms = []
        for slot in range(HEADS_PER_GROUP):
            e, a = divmod(slot, 2)
            qms.append(q[:, a * LANES:(a + 1) * LANES] * lc_ref[e, :tq])
            qm_ref[1, slot * tq:(slot + 1) * tq, :] = qms[slot]
        flags = lax.dot_general(sel_ref[0, 0, :, rows], place_ref[...], _TN,
                                preferred_element_type=F32)
        fill = ((placed_ref[...] - flags) * MASKED).astype(BF16)
        for slot in range(HEADS_PER_GROUP):
            e = slot // 2
            qm_ref[0, slot * tq:(slot + 1) * tq, :] = (qms[slot]
                                                       + fill[:, e * LANES:(e + 1) * LANES])
        gates = _gates(gate_ref[0, rows, :], [gexp_ref[0, g], gexp_ref[1, g]])
        for branch, z_ref in enumerate((zs_ref, zw_ref)):
            gz_ref[2 * half + branch] = gates[branch] * _silu(z_ref[0, rows, :].astype(F32))
        m_ref[...] = jnp.full(m_ref.shape, -jnp.inf, F32)
        for branch in range(2):
            acc_ref[2 * half + branch] = jnp.zeros(acc_ref.shape[1:], F32)
        scores(qi, 0, bufs[0][0])

    def fill_pipeline(qi, half):
        softmax(qi, 0, *bufs[0])
        scores(qi, 1, bufs[1][0])
        step(qi, half, 1, 1)

    def run_stream(qi, half):
        def steps(first, count):
            for k in range(count):
                step(qi, half, first + k, k % 2)

        n_tiles = stream_len(qi)
        n_steps = n_tiles - 2
        n_quads = lax.shift_right_logical(n_steps, 2)

        def four_steps(i, carry):
            steps(4 * i + 2, 4)
            return carry
        lax.fori_loop(0, n_quads, four_steps, 0)

        @pl.when(lax.bitwise_and(n_steps, 2) == 2)
        def _():
            steps(4 * n_quads + 2, 2)

        @pl.when(lax.bitwise_and(n_steps, 1) == 1)
        def _():
            steps(n_tiles - 1, 1)

    def leave(qi, half):
        y_win = gated_output(half, 1)
        last_pos = stream_len(qi) - 1
        last = lax.bitwise_and(last_pos, 1)
        values(qi, half, last_pos, p_bufs.at[last], c_bufs.at[last])
        return y_win

    def write_output(half, y_win):
        rows = slice(half * tq, (half + 1) * tq)
        y_ref[0, rows, :] = (y_win + gated_output(half, 0)).astype(y_ref.dtype)

    first, second = 2 * pair_id, 2 * pair_id + 1
    enter(first, 0)
    fill_pipeline(first, 0)
    run_stream(first, 0)
    y_win = leave(first, 0)
    enter(second, 1)
    write_output(0, y_win)
    fill_pipeline(second, 1)
    run_stream(second, 1)
    write_output(1, leave(second, 1))


def _token_attention(proj, a_tiles, lane_consts, gate_expand, sel, sel_consts):
    b = proj.shape[0]
    tq, tk = ATT_TQ, ATT_TK
    grp_w = HEADS_PER_GROUP * HEAD_DIM
    k_blk = NSA_K_COL // LANES
    v_blk = NSA_V_COL // LANES
    z_blk = NSA_Z_COL // grp_w
    rows = 2 * tq
    kv_spec = lambda blk: pl.BlockSpec((1, SEQ, LANES), lambda i, g, s: (i, 0, blk + g))
    z_spec = lambda blk: pl.BlockSpec((1, rows, grp_w), lambda i, g, s: (i, s, blk + g))
    whole = pl.BlockSpec(memory_space=pltpu.VMEM)
    return pl.pallas_call(
        _token_kernel,
        grid=(b, GROUPS, SEQ // rows),
        in_specs=[
            pl.BlockSpec((1, rows, grp_w), lambda i, g, s: (i, s, NSA_Q_COL // grp_w + g)),
            kv_spec(k_blk), kv_spec(v_blk), kv_spec(k_blk + GROUPS), kv_spec(v_blk + GROUPS),
            pl.BlockSpec((1, rows, LANES), lambda i, g, s: (i, s, NSA_G_COL // LANES)),
            z_spec(z_blk + GROUPS), z_spec(z_blk + 2 * GROUPS),
            whole, whole, whole,
            pl.BlockSpec((1, 1, N_SLC, rows), lambda i, g, s: (i, g, 0, s)),
            whole, whole, whole, whole,
        ],
        out_specs=pl.BlockSpec((1, rows, grp_w), lambda i, g, s: (i, s, g)),
        out_shape=jax.ShapeDtypeStruct((b, SEQ, NSA_WIDTH), BF16),
        scratch_shapes=[
            pltpu.VMEM((2, HEADS_PER_GROUP * tq, LANES), BF16),
            pltpu.VMEM((2, HEADS_PER_GROUP * tq, LANES), F32),
            pltpu.VMEM((4, HEADS_PER_GROUP * tq, LANES), F32),
            pltpu.VMEM((2, HEADS_PER_GROUP * tq, tk), F32),
            pltpu.VMEM((2, HEADS_PER_GROUP * tq, tk), BF16),
            pltpu.VMEM((2, HEADS_PER_GROUP * tq, LANES), F32),
            pltpu.VMEM((3, SEQ // tk, LANES, tk), BF16),
            pltpu.VMEM((4, SEQ, LANES), BF16),
            pltpu.VMEM((4, tq, grp_w), F32),
        ],
        compiler_params=_cparams(("parallel", "parallel", "arbitrary")),
        name="token_attention",
    )(proj, proj, proj, proj, proj, proj, proj, proj, a_tiles, lane_consts, gate_expand,
      sel, *sel_consts)


def _retention_kernel(q_ref, k_ref, v_ref, z_ref, cos_ref, sin_ref, inner_ref, xi_ref, zeta_ref,
                      gn_ref, y_ref, state_ref, *, decays):
    @pl.when(pl.program_id(1) == 0)
    def _():
        state_ref[...] = jnp.zeros(state_ref.shape, F32)

    cos = cos_ref[...]
    sin = sin_ref[...]
    half = RET_QK_DIM // 2

    def rot(x):
        x1, x2 = x[:, :half], x[:, half:]
        return jnp.concatenate([x1 * cos - x2 * sin, x1 * sin + x2 * cos], axis=1)

    for h in range(RET_HEADS):
        qs = slice(h * RET_QK_DIM, (h + 1) * RET_QK_DIM)
        vs = slice(h * RET_V_DIM, (h + 1) * RET_V_DIM)
        qr = rot(q_ref[0, :, qs].astype(F32))
        kr = rot(k_ref[0, :, qs].astype(F32)) * (RET_QK_DIM ** -0.5)
        qb = qr.astype(BF16)
        vh = v_ref[0, :, vs]
        attn = lax.dot_general(qb, kr.astype(BF16), _NT, preferred_element_type=F32) * inner_ref[h]
        st = state_ref[h]
        o = (jnp.dot(attn.astype(BF16), vh, preferred_element_type=F32)
             + jnp.dot(qb, st.astype(BF16), preferred_element_type=F32) * xi_ref[h])
        kz = (kr * zeta_ref[h]).astype(BF16)
        state_ref[h] = st * decays[h] + lax.dot_general(kz, vh, _TN, preferred_element_type=F32)
        mu = jnp.mean(o, axis=1, keepdims=True)
        d = o - mu
        var = jnp.mean(d * d, axis=1, keepdims=True)
        on = d * lax.rsqrt(var + GN_EPS) * gn_ref[h]
        y_ref[0, :, vs] = (on * _silu(z_ref[0, :, vs].astype(F32))).astype(y_ref.dtype)


def _retention(proj, gn_gain, tables):
    b = proj.shape[0]
    c = RET_CHUNK
    cos, sin, inner, xi, zeta, decays = tables
    v_blk = 2 * RET_QK_WIDTH // RET_WIDTH
    return pl.pallas_call(
        functools.partial(_retention_kernel, decays=decays),
        grid=(b, SEQ // c),
        in_specs=[
            pl.BlockSpec((1, c, RET_QK_WIDTH), lambda i, j: (i, j, 0)),
            pl.BlockSpec((1, c, RET_QK_WIDTH), lambda i, j: (i, j, 1)),
            pl.BlockSpec((1, c, RET_WIDTH), lambda i, j: (i, j, v_blk)),
            pl.BlockSpec((1, c, RET_WIDTH), lambda i, j: (i, j, v_blk + 1)),
            pl.BlockSpec((c, RET_QK_DIM // 2), lambda i, j: (j, 0)),
            pl.BlockSpec((c, RET_QK_DIM // 2), lambda i, j: (j, 0)),
            pl.BlockSpec((RET_HEADS, c, c), lambda i, j: (0, 0, 0)),
            pl.BlockSpec((RET_HEADS, c, 1), lambda i, j: (0, 0, 0)),
            pl.BlockSpec((RET_HEADS, c, 1), lambda i, j: (0, 0, 0)),
            pl.BlockSpec((RET_HEADS, 1, RET_V_DIM), lambda i, j: (0, 0, 0)),
        ],
        out_specs=pl.BlockSpec((1, c, RET_WIDTH), lambda i, j: (i, j, 0)),
        out_shape=jax.ShapeDtypeStruct((b, SEQ, RET_WIDTH), BF16),
        scratch_shapes=[pltpu.VMEM((RET_HEADS, RET_QK_DIM, RET_V_DIM), F32)],
        compiler_params=_cparams(("parallel", "arbitrary")),
        name="retention",
    )(proj, proj, proj, proj, cos, sin, inner, xi, zeta, gn_gain.reshape(RET_HEADS, 1, RET_V_DIM))


def _t5_bucket_np(dist):
    n = np.maximum(dist, 0)
    max_exact = REL_BUCKETS // 2
    nf = np.maximum(n, 1).astype(np.float64)
    large = max_exact + (np.log(nf / max_exact) / math.log(REL_MAX_DIST / max_exact)
                         * (REL_BUCKETS - max_exact)).astype(np.int64)
    large = np.minimum(large, REL_BUCKETS - 1)
    return np.where(n < max_exact, n, large).astype(np.int32)


def _skewed(vec, rows, stride, cols):
    p = vec.shape[-1]
    lead = vec.shape[:-1]
    flat = jnp.tile(vec, (1,) * len(lead) + (rows,))[..., :rows * (p - stride)]
    return flat.reshape(lead + (rows, p - stride))[..., :cols]


def _nsa_tables(table):
    tq, tk = ATT_TQ, ATT_TK
    assert tq == tk
    tab_t = table.T * LOG2E
    period = 2 * SEQ
    x = np.arange(period)
    ok = (x >= CMP_BLOCK - 1) & (x < SEQ)
    vec = jnp.take(tab_t, jnp.asarray(_t5_bucket_np(x - (CMP_BLOCK - 1))), axis=1)
    vec = jnp.where(jnp.asarray(ok)[None, :], vec, MASKED)
    bias_cmp = _skewed(vec, N_CMP_PAD, CMP_STRIDE, SEQ)
    rel_t = tab_t - tab_t[:, REL_BUCKETS - 1:]
    period = 2 * tq
    x = np.arange(period)
    x = np.where(x < tq, x, x - period)
    tiles = []
    for off in range(3):
        d = off * tk - x
        ok = (d >= 0) & (d < WIN_SIZE) if off == 2 else (d >= 0)
        vec = jnp.take(rel_t, jnp.asarray(_t5_bucket_np(d)), axis=1)
        vec = jnp.where(jnp.asarray(ok)[None, :], vec, MASKED)
        tiles.append(_skewed(vec, tq, 1, tk))
    a_tiles = jnp.stack(tiles + [jnp.zeros_like(tiles[0])])
    cs = np.arange(N_CMP_PAD)[None, :] * CMP_STRIDE
    jj = np.arange(N_SLC)[:, None]
    ovl = ((cs < (jj + 1) * SLC_BLOCK) & (cs + CMP_BLOCK > jj * SLC_BLOCK)
           & (np.arange(N_CMP_PAD)[None, :] < N_CMP))
    ovl = jnp.asarray(ovl.astype(np.float32), BF16)
    lane = np.arange(LANES).reshape(1, 1, 1, LANES)
    key = np.arange(SEQ).reshape(1, SEQ // tk, tk, 1)
    base = np.array([HEAD_DIM, 0]).reshape(2, 1, 1, 1)
    key_blocks = jnp.asarray((lane == base + key // SLC_BLOCK).astype(np.float32), BF16)
    col = np.arange(2 * LANES).reshape(1, -1)
    jb = np.arange(N_SLC).reshape(-1, 1)
    place = (col == HEAD_DIM + jb) | (col == LANES + jb)
    row = np.arange(2 * LANES).reshape(-1, 1)
    out = np.arange(LANES).reshape(1, -1)
    den_spread = ((row == LANES - 1) & (out < HEAD_DIM)) | ((row == LANES) & (out >= HEAD_DIM))
    sel_consts = (key_blocks, jnp.asarray(place.astype(np.float32), BF16),
                  jnp.asarray(place.any(axis=0, keepdims=True).astype(np.float32)),
                  jnp.asarray(den_spread.astype(np.float32), BF16))
    c = np.arange(LANES).reshape(1, 1, LANES, 1)
    col = np.arange(HEADS_PER_GROUP * HEAD_DIM).reshape(1, 1, 1, -1) // HEAD_DIM
    br = np.arange(3).reshape(3, 1, 1, 1)
    gg = np.arange(GROUPS).reshape(1, GROUPS, 1, 1)
    gate_expand = jnp.asarray((c == br * HEADS + HEADS_PER_GROUP * gg + col).astype(np.float32),
                              BF16)
    return bias_cmp, a_tiles, ovl, sel_consts, _lane_consts(max(tq, CMP_TQ)), gate_expand


def _retention_tables():
    c = RET_CHUNK
    log_g = jnp.log(1.0 - 2.0 ** (-5.0 - jnp.arange(RET_HEADS, dtype=F32)))
    i = jnp.arange(c, dtype=F32)
    diff = i[:, None] - i[None, :]
    inner = jnp.where(diff >= 0, jnp.exp(diff[None] * log_g[:, None, None]), 0.0)
    xi = jnp.exp((i + 1.0)[None, :] * log_g[:, None])[:, :, None]
    zeta = jnp.exp((c - 1.0 - i)[None, :] * log_g[:, None])[:, :, None]
    decays = tuple(float((1.0 - 2.0 ** (-5.0 - h)) ** c) for h in range(RET_HEADS))
    inv = 1.0 / (ROPE_BASE ** jnp.linspace(0.0, 1.0, RET_QK_DIM // 2, dtype=F32))
    ang = jnp.arange(SEQ, dtype=F32)[:, None] * inv[None, :]
    return jnp.cos(ang), jnp.sin(ang), inner, xi, zeta, decays


def _nsa_w_in_layout(w):
    d = w.shape[0]
    q = w[:, :NSA_WIDTH] * (HEAD_DIM ** -0.5 * LOG2E)
    kv0 = NSA_WIDTH
    g0 = kv0 + 6 * KV_WIDTH
    z0 = g0 + 3 * HEADS
    slab = lambda n: w[:, kv0 + n * KV_WIDTH: kv0 + (n + 1) * KV_WIDTH]
    cols = [q, w[:, z0:z0 + 3 * NSA_WIDTH]]
    for n in (2, 4, 3, 5):
        for g in range(GROUPS):
            part = slab(n)[:, g * HEAD_DIM:(g + 1) * HEAD_DIM]
            cols += [part, part]
    cols += [slab(0), slab(1), w[:, g0:z0]]
    used = NSA_G_COL + 3 * HEADS
    cols.append(jnp.zeros((d, NSA_PROJ_PAD - used), w.dtype))
    return jnp.concatenate(cols, axis=1).astype(BF16)


def _nsa_layer(h2d, b, pre_gain, post_gain, w_in, w_out, k_pos, k_w1, k_w2, v_pos, v_w1, v_w2, tabs):
    bias_cmp, a_tiles, ovl, sel_consts, lane_consts, gate_expand = tabs
    proj = _norm_proj(h2d, pre_gain, _nsa_w_in_layout(w_in), NSA_PROJ_TN)
    proj = proj.reshape(b, SEQ, NSA_PROJ_PAD)
    ckv = proj[:, :, NSA_CKV_COL:NSA_CKV_COL + 2 * KV_WIDTH]
    ckv = ckv.reshape(b, N_CMP_PAD, CMP_STRIDE, 2, GROUPS, HEAD_DIM).transpose(0, 3, 4, 1, 2, 5)
    ckv = ckv.reshape(b, 2, GROUPS, N_CMP_PAD, CMP_STRIDE * HEAD_DIM)
    pos = jnp.stack([k_pos, v_pos]).reshape(2, 2, CMP_STRIDE * HEAD_DIM)
    w1 = jnp.stack([k_w1, v_w1]).astype(BF16)
    w2 = jnp.stack([k_w2, v_w2])
    w2d = jnp.concatenate([w2, w2], axis=2).astype(BF16)
    ckv_c = _compress(ckv, pos, w1, w2d)
    y_cmp, sel = _cmp_select(proj, ckv_c, bias_cmp, ovl, lane_consts, gate_expand[0])
    y_tok = _token_attention(proj, a_tiles, lane_consts, gate_expand[1:], sel, sel_consts)
    parts = [y.reshape(b * SEQ, NSA_WIDTH) for y in (y_cmp, y_tok)]
    return _out_post(parts, w_out.astype(BF16), h2d, post_gain)


def _ret_layer(h2d, b, pre_gain, post_gain, w_in, w_out, gn_gain, tabs):
    proj = _norm_proj(h2d, pre_gain, w_in.astype(BF16), RET_PROJ_TN).reshape(b, SEQ, -1)
    y = _retention(proj, gn_gain, tabs)
    return _out_post([y.reshape(b * SEQ, RET_WIDTH)], w_out.astype(BF16), h2d, post_gain)


def kernel(x, pre_norm_gain, post_norm_gain, rel_bias_table, nsa_w_in, nsa_w_out, nsa_cmp_k_pos, nsa_cmp_k_w1, nsa_cmp_k_w2, nsa_cmp_v_pos, nsa_cmp_v_w1, nsa_cmp_v_w2, ret_w_in, ret_w_out, ret_gn_gain):
    b, s, d = x.shape
    assert s == SEQ and d == D_MODEL
    nsa_tabs = _nsa_tables(rel_bias_table)
    ret_tabs = _retention_tables()
    h = x.reshape(b * s, d)
    for layer in range(DEPTH):
        slot = layer // 2
        if layer % 2 == 0:
            h = _nsa_layer(h, b, pre_norm_gain[layer], post_norm_gain[layer], nsa_w_in[slot],
                           nsa_w_out[slot], nsa_cmp_k_pos[slot], nsa_cmp_k_w1[slot],
                           nsa_cmp_k_w2[slot], nsa_cmp_v_pos[slot], nsa_cmp_v_w1[slot],
                           nsa_cmp_v_w2[slot], nsa_tabs)
        else:
            h = _ret_layer(h, b, pre_norm_gain[layer], post_norm_gain[layer], ret_w_in[slot],
                           ret_w_out[slot], ret_gn_gain[slot], ret_tabs)
    return h.reshape(b, s, d)
```

```python
import functools
import math

import numpy as np
import jax
import jax.numpy as jnp
from jax import lax
from jax.experimental import pallas as pl
from jax.experimental.pallas import tpu as pltpu

F32 = jnp.float32
BF16 = jnp.bfloat16

D_MODEL = 1024
SEQ = 2048
DEPTH = 4
RMS_EPS = 1e-6
GN_EPS = 1e-6
MASKED = -1e30
LOG2E = math.log2(math.e)

HEADS = 16
HEAD_DIM = 64
GROUPS = 4
HEADS_PER_GROUP = HEADS // GROUPS
NSA_WIDTH = HEADS * HEAD_DIM
KV_WIDTH = GROUPS * HEAD_DIM
CMP_BLOCK = 32
CMP_STRIDE = 16
CMP_HIDDEN = 256
N_CMP = (SEQ - CMP_BLOCK) // CMP_STRIDE + 1
N_CMP_PAD = 128
SLC_BLOCK = 64
N_SLC = SEQ // SLC_BLOCK
SLC_TOPN = 16
WIN_SIZE = 512
FORCED_SCORE = 1e3
REL_BUCKETS = 32
REL_MAX_DIST = 128

RET_HEADS = 4
RET_QK_DIM = 256
RET_V_DIM = 512
RET_QK_WIDTH = RET_HEADS * RET_QK_DIM
RET_WIDTH = RET_HEADS * RET_V_DIM
ROPE_BASE = 10000.0

LANES = 128
VMEM_LIMIT_BYTES = 56 * 1024 * 1024

PROJ_TM = 1024
RET_PROJ_TN = 2048
NSA_PROJ_TN = 3456
POST_TM = 512
ATT_TQ = 256
ATT_TK = 256
ATT_TILES_PER_STEP = 4
CMP_TQ = 1024
RET_CHUNK = 256

NSA_Q_COL = 0
NSA_Z_COL = 1024
NSA_K_COL = 4096
NSA_V_COL = 5120
NSA_CKV_COL = 6144
NSA_G_COL = 6656
NSA_PROJ_PAD = 6912

_NT = (((1,), (1,)), ((), ()))
_TN = (((0,), (0,)), ((), ()))


def _cparams(sem):
    return pltpu.CompilerParams(dimension_semantics=sem, vmem_limit_bytes=VMEM_LIMIT_BYTES)


def _norm_proj_kernel(x_ref, g_ref, w_ref, o_ref, xn_ref):
    @pl.when(pl.program_id(1) == 0)
    def _():
        x = x_ref[...]
        ms = jnp.mean(x * x, axis=-1, keepdims=True)
        xn_ref[...] = (x * lax.rsqrt(ms + RMS_EPS) * g_ref[...]).astype(BF16)

    o_ref[...] = jnp.dot(xn_ref[...], w_ref[...], preferred_element_type=F32).astype(o_ref.dtype)


def _norm_proj(x2d, gain, w_bf16, tn):
    m, d = x2d.shape
    n = w_bf16.shape[1]
    assert n % tn == 0
    return pl.pallas_call(
        _norm_proj_kernel,
        grid=(m // PROJ_TM, n // tn),
        in_specs=[
            pl.BlockSpec((PROJ_TM, d), lambda i, j: (i, 0)),
            pl.BlockSpec((1, d), lambda i, j: (0, 0)),
            pl.BlockSpec((d, tn), lambda i, j: (0, j)),
        ],
        out_specs=pl.BlockSpec((PROJ_TM, tn), lambda i, j: (i, j)),
        out_shape=jax.ShapeDtypeStruct((m, n), BF16),
        scratch_shapes=[pltpu.VMEM((PROJ_TM, d), BF16)],
        compiler_params=_cparams(("parallel", "arbitrary")),
        name="norm_proj",
    )(x2d, gain.reshape(1, d), w_bf16)


def _out_post_kernel(*refs, n_parts):
    y_refs = refs[:n_parts]
    w_ref, h_ref, g_ref, o_ref = refs[n_parts:]
    y = y_refs[0][...].astype(F32)
    for r in y_refs[1:]:
        y = y + r[...].astype(F32)
    t = jnp.dot(y.astype(BF16), w_ref[...], preferred_element_type=F32)
    ms = jnp.mean(t * t, axis=-1, keepdims=True)
    o_ref[...] = h_ref[...] + t * lax.rsqrt(ms + RMS_EPS) * g_ref[...]


def _out_post(parts, w_bf16, h2d, gain):
    m, d = h2d.shape
    k = w_bf16.shape[0]
    n_parts = len(parts)
    return pl.pallas_call(
        functools.partial(_out_post_kernel, n_parts=n_parts),
        grid=(m // POST_TM,),
        in_specs=[pl.BlockSpec((POST_TM, k), lambda i: (i, 0)) for _ in parts] + [
            pl.BlockSpec((k, d), lambda i: (0, 0)),
            pl.BlockSpec((POST_TM, d), lambda i: (i, 0)),
            pl.BlockSpec((1, d), lambda i: (0, 0)),
        ],
        out_specs=pl.BlockSpec((POST_TM, d), lambda i: (i, 0)),
        out_shape=jax.ShapeDtypeStruct((m, d), F32),
        compiler_params=_cparams(("parallel",)),
        name="out_post",
    )(*parts, w_bf16, h2d, gain.reshape(1, d))


def _compress_kernel(x_ref, pos_ref, w1_ref, w2_ref, o_ref):
    half = CMP_STRIDE * HEAD_DIM
    for which in range(2):
        x = x_ref[0, which, 0].astype(F32)
        xa = (x + pos_ref[which, 0:1, :]).astype(BF16)
        xb = (x + pos_ref[which, 1:2, :]).astype(BF16)
        pa = jnp.dot(xa, w1_ref[which, :half, :], preferred_element_type=F32)
        pb = jnp.dot(xb, w1_ref[which, half:, :], preferred_element_type=F32)
        hid = pa + pltpu.roll(pb, N_CMP_PAD - 1, 0)
        hid = hid * jax.nn.sigmoid(hid)
        o_ref[0, which, 0] = jnp.dot(hid.astype(BF16), w2_ref[which],
                                     preferred_element_type=F32).astype(o_ref.dtype)


def _compress(ckv_rows, pos, w1, w2d):
    b = ckv_rows.shape[0]
    row_w = CMP_STRIDE * HEAD_DIM
    return pl.pallas_call(
        _compress_kernel,
        grid=(b, GROUPS),
        in_specs=[
            pl.BlockSpec((1, 2, 1, N_CMP_PAD, row_w), lambda i, g: (i, 0, g, 0, 0)),
            pl.BlockSpec((2, 2, row_w), lambda i, g: (0, 0, 0)),
            pl.BlockSpec((2, 2 * row_w, CMP_HIDDEN), lambda i, g: (0, 0, 0)),
            pl.BlockSpec((2, CMP_HIDDEN, 2 * HEAD_DIM), lambda i, g: (0, 0, 0)),
        ],
        out_specs=pl.BlockSpec((1, 2, 1, N_CMP_PAD, 2 * HEAD_DIM), lambda i, g: (i, 0, g, 0, 0)),
        out_shape=jax.ShapeDtypeStruct((b, 2, GROUPS, N_CMP_PAD, 2 * HEAD_DIM), BF16),
        compiler_params=_cparams(("parallel", "parallel")),
        name="compress",
    )(ckv_rows, pos, w1, w2d)


def _lane_consts(rows):
    lane = np.arange(LANES)
    lo = (lane < HEAD_DIM).astype(np.float32)
    hi = (lane >= HEAD_DIM).astype(np.float32)
    last = (lane == LANES - 1).astype(np.float32)
    first = (lane == 0).astype(np.float32)
    c = np.stack([lo, hi, 1.0 - last, last, 1.0 - first, first])
    return jnp.asarray(np.broadcast_to(c[:, None, :], (6, rows, LANES)), BF16)


def _bf16_terms(x, n):
    terms = []
    for _ in range(n - 1):
        t = x.astype(BF16)
        terms.append(t)
        x = x - t.astype(F32)
    terms.append(x.astype(BF16))
    return terms


def _gates(gate_logits, expands):
    terms = _bf16_terms(jax.nn.sigmoid(gate_logits.astype(F32)), 2)
    return [sum(jnp.dot(t, ex, preferred_element_type=F32) for t in terms) for ex in expands]


def _silu(x):
    return x * jax.nn.sigmoid(x)


def _cmp_select_kernel(q_ref, gate_ref, z_ref, ckv_ref, bias_ref, ovl_ref, lc_ref, gexp_ref,
                       y_ref, sel_ref):
    g = pl.program_id(0)
    qi = pl.program_id(1)
    tq = q_ref.shape[1]
    q = q_ref[0]
    kk = ckv_ref[0, 0, 0]
    vv = ckv_ref[0, 1, 0]
    lane = lax.broadcasted_iota(jnp.int32, (tq, LANES), 1)

    psum = jnp.zeros((N_CMP_PAD, tq), F32)
    pairs = []
    for a in range(2):
        qp = q[:, a * LANES:(a + 1) * LANES]
        outs = []
        for e in range(2):
            s = lax.dot_general(kk, qp * lc_ref[e, :tq], _NT, preferred_element_type=F32)
            bias = bias_ref[2 * a + e]
            s = s + bias
            m = jnp.max(s, axis=0, keepdims=True)
            p = jnp.exp2(s - m)
            p = jnp.where(bias > 0.5 * MASKED, p / jnp.sum(p, axis=0, keepdims=True), 0.0)
            psum = psum + p
            outs.append(lax.dot_general(p.astype(BF16), vv, _TN, preferred_element_type=F32))
        pairs.append(jnp.where(lane < HEAD_DIM, outs[0], outs[1]))
    y_ref[0] = (jnp.concatenate(pairs, axis=1) * _gates(gate_ref[0], [gexp_ref[g]])[0]
                * _silu(z_ref[0].astype(F32))).astype(y_ref.dtype)

    ovl = ovl_ref[...]
    imp = sum(jnp.dot(ovl, t, preferred_element_type=F32) for t in _bf16_terms(psum, 3))
    t = qi * tq + lax.broadcasted_iota(jnp.int32, (N_SLC, tq), 1)
    blk = lax.broadcasted_iota(jnp.int32, (N_SLC, tq), 0)
    cur = lax.shift_right_logical(t, int(math.log2(SLC_BLOCK)))
    forced = (blk == 0) | (blk == cur) | (blk == cur - 1)
    score = jnp.where(blk * SLC_BLOCK <= t, imp + jnp.where(forced, FORCED_SCORE, 0.0), MASKED)
    cnt = jnp.zeros((N_SLC, tq), jnp.int32)
    for jp in range(N_SLC):
        row = score[jp:jp + 1, :]
        beats = (row > score) | ((row == score) & (blk > jp))
        cnt = cnt + beats.astype(jnp.int32)
    sel_ref[0, 0] = jnp.where(cnt < SLC_TOPN, 1.0, 0.0).astype(sel_ref.dtype)


def _cmp_select(proj, ckv, bias_cmp, ovl, lane_consts, gate_expand):
    b = proj.shape[0]
    tq = CMP_TQ
    grp_w = HEADS_PER_GROUP * HEAD_DIM
    return pl.pallas_call(
        _cmp_select_kernel,
        grid=(GROUPS, SEQ // tq, b),
        in_specs=[
            pl.BlockSpec((1, tq, grp_w), lambda g, qi, i: (i, qi, NSA_Q_COL // grp_w + g)),
            pl.BlockSpec((1, tq, LANES), lambda g, qi, i: (i, qi, NSA_G_COL // LANES)),
            pl.BlockSpec((1, tq, grp_w), lambda g, qi, i: (i, qi, NSA_Z_COL // grp_w + g)),
            pl.BlockSpec((1, 2, 1, N_CMP_PAD, 2 * HEAD_DIM), lambda g, qi, i: (i, 0, g, 0, 0)),
            pl.BlockSpec((HEADS_PER_GROUP, N_CMP_PAD, tq), lambda g, qi, i: (g, 0, qi)),
            pl.BlockSpec((N_SLC, N_CMP_PAD), lambda g, qi, i: (0, 0)),
            pl.BlockSpec(memory_space=pltpu.VMEM),
            pl.BlockSpec(memory_space=pltpu.VMEM),
        ],
        out_specs=[
            pl.BlockSpec((1, tq, grp_w), lambda g, qi, i: (i, qi, g)),
            pl.BlockSpec((1, 1, N_SLC, tq), lambda g, qi, i: (i, g, 0, qi)),
        ],
        out_shape=[
            jax.ShapeDtypeStruct((b, SEQ, NSA_WIDTH), BF16),
            jax.ShapeDtypeStruct((b, GROUPS, N_SLC, SEQ), BF16),
        ],
        compiler_params=_cparams(("parallel", "parallel", "parallel")),
        name="cmp_select",
    )(proj, proj, proj, ckv, bias_cmp, ovl, lane_consts, gate_expand)


def _token_kernel(q_ref, ks_ref, vs_ref, kw_ref, vw_ref, gate_ref, zs_ref, zw_ref, a_ref, lc_ref,
                  gexp_ref, sel_ref, kblk_ref, place_ref, placed_ref, dspread_ref, y_ref,
                  qm_ref, m_ref, acc_ref, s_bufs, p_bufs, c_bufs, kx_ref, vx_ref, gz_ref):
    g = pl.program_id(1)
    pair_id = pl.program_id(2)
    tq = ATT_TQ
    tk = ATT_TK

    def stream_len(qi_):
        return jnp.minimum(qi_, WIN_SIZE // tk) + 1 + qi_ + 1

    @pl.when(pair_id == 0)
    def _():
        for c in range(SEQ // tk):
            rows = slice(c * tk, (c + 1) * tk)
            for branch, v_ref in enumerate((vs_ref, vw_ref)):
                vv = v_ref[0, rows, :]
                vx_ref[2 * branch, rows, :] = vv * lc_ref[2, :tk] + lc_ref[3, :tk]
                vx_ref[2 * branch + 1, rows, :] = vv * lc_ref[4, :tk] + lc_ref[5, :tk]
            kk = ks_ref[0, rows, :]
            for e in range(2):
                kx_ref[e, c] = (kk * lc_ref[e, :tk] + kblk_ref[e, c]).astype(F32).T.astype(BF16)
            kx_ref[2, c] = kw_ref[0, rows, :].astype(F32).T.astype(BF16)

    def tile(qi_, j):
        n_win = jnp.minimum(qi_, WIN_SIZE // tk) + 1
        win = (j < n_win).astype(jnp.int32)
        t = j - (1 - win) * n_win
        return win, jnp.clip(qi_ - t, 0, qi_), jnp.where(win == 1, t, jnp.where(t < 2, t, 3))

    def scores(qi, j, s_ref):
        win, ki, _ = tile(qi, j)
        for e in range(2):
            pair = slice(2 * e * tq, 2 * (e + 1) * tq)
            s_ref[pair, :] = jnp.dot(qm_ref[win, pair, :], kx_ref[jnp.where(win == 1, 2, e), ki],
                                     preferred_element_type=F32)

    def softmax(qi, j, s_ref, p_ref, c_ref):
        win, _, kind = tile(qi, j)
        for slot in range(HEADS_PER_GROUP):
            e, a = divmod(slot, 2)
            rows = slice(slot * tq, (slot + 1) * tq)
            s = s_ref[rows, :] + a_ref[kind, HEADS_PER_GROUP * g + 2 * a + e]
            m_prev = m_ref[win, rows, :]
            m_new = jnp.maximum(m_prev, jnp.max(s, axis=1, keepdims=True))
            c_ref[rows, :] = jnp.exp2(m_prev - m_new)
            x = s - jnp.concatenate([m_new] * (tk // LANES), axis=1)
            p_ref[rows, :] = jnp.exp2(x.astype(BF16))
            m_ref[win, rows, :] = m_new

    def values(qi_, par_, j, p_ref, c_ref):
        win, ki, _ = tile(qi_, j)
        rows = pl.ds(pl.multiple_of(ki * tk, tk), tk)
        for e in range(2):
            pair = slice(2 * e * tq, 2 * (e + 1) * tq)
            pv = jnp.dot(p_ref[pair, :], vx_ref[2 * win + e, rows, :], preferred_element_type=F32)
            acc = acc_ref.at[2 * par_ + win]
            acc[pair, :] = c_ref[pair, :] * acc[pair, :] + pv

    lane = lax.broadcasted_iota(jnp.int32, (tq, LANES), 1)

    def gated_output(par_, branch):
        acc = acc_ref.at[2 * par_ + branch]
        pairs = []
        for a in range(2):
            acc_e = acc[a * tq:(a + 1) * tq, :]
            acc_o = acc[(2 + a) * tq:(3 + a) * tq, :]
            both = jnp.concatenate([acc_e, acc_o], axis=1)
            den = sum(jnp.dot(t, dspread_ref[...], preferred_element_type=F32)
                      for t in _bf16_terms(both, 2))
            pairs.append(jnp.where(lane < HEAD_DIM, acc_e, acc_o) / den)
        return jnp.concatenate(pairs, axis=1) * gz_ref[2 * par_ + branch]

    bufs = [(s_bufs.at[i], p_bufs.at[i], c_bufs.at[i]) for i in range(2)]

    def step(qi, half, j, parity):
        cur, nxt = bufs[parity], bufs[1 - parity]
        values(qi, half, j - 1, nxt[1], nxt[2])
        softmax(qi, j, *cur)
        scores(qi, j + 1, nxt[0])

    def enter(qi, slot):
        rows = slice(slot * tq, (slot + 1) * tq)
        half = slot % 2
        q = q_ref[0, rows, :]
        qms = []
        for slot in range(HEADS_PER_GROUP):
            e, a = divmod(slot, 2)
            qms.append(q[:, a * LANES:(a + 1) * LANES] * lc_ref[e, :tq])
            qm_ref[1, slot * tq:(slot + 1) * tq, :] = qms[slot]
        flags = lax.dot_general(sel_ref[0, 0, :, rows], place_ref[...], _TN,
                                preferred_element_type=F32)
        fill = ((placed_ref[...] - flags) * MASKED).astype(BF16)
        for slot in range(HEADS_PER_GROUP):
            e = slot // 2
            qm_ref[0, slot * tq:(slot + 1) * tq, :] = (qms[slot]
                                                       + fill[:, e * LANES:(e + 1) * LANES])
        gates = _gates(gate_ref[0, rows, :], [gexp_ref[0, g], gexp_ref[1, g]])
        for branch, z_ref in enumerate((zs_ref, zw_ref)):
            gz_ref[2 * half + branch] = gates[branch] * _silu(z_ref[0, rows, :].astype(F32))
        m_ref[...] = jnp.full(m_ref.shape, -jnp.inf, F32)
        for branch in range(2):
            acc_ref[2 * half + branch] = jnp.zeros(acc_ref.shape[1:], F32)
        scores(qi, 0, bufs[0][0])

    def fill_pipeline(qi, half):
        softmax(qi, 0, *bufs[0])
        scores(qi, 1, bufs[1][0])
        step(qi, half, 1, 1)

    def run_stream(qi, half):
        def steps(first, count):
            for k in range(count):
                step(qi, half, first + k, k % 2)

        n_tiles = stream_len(qi)
        n_steps = n_tiles - 2
        n_quads = lax.shift_right_logical(n_steps, 2)

        def four_steps(i, carry):
            steps(4 * i + 2, 4)
            return carry
        lax.fori_loop(0, n_quads, four_steps, 0)

        @pl.when(lax.bitwise_and(n_steps, 2) == 2)
        def _():
            steps(4 * n_quads + 2, 2)

        @pl.when(lax.bitwise_and(n_steps, 1) == 1)
        def _():
            steps(n_tiles - 1, 1)

    def leave(qi, half):
        y_win = gated_output(half, 1)
        last_pos = stream_len(qi) - 1
        last = lax.bitwise_and(last_pos, 1)
        values(qi, half, last_pos, p_bufs.at[last], c_bufs.at[last])
        return y_win

    def write_output(slot, y_win):
        rows = slice(slot * tq, (slot + 1) * tq)
        y_ref[0, rows, :] = (y_win + gated_output(slot % 2, 0)).astype(y_ref.dtype)

    y_win = None
    for slot in range(ATT_TILES_PER_STEP):
        qi = ATT_TILES_PER_STEP * pair_id + slot
        enter(qi, slot)
        if slot > 0:
            write_output(slot - 1, y_win)
        fill_pipeline(qi, slot % 2)
        run_stream(qi, slot % 2)
        y_win = leave(qi, slot % 2)
    write_output(ATT_TILES_PER_STEP - 1, y_win)


def _token_attention(proj, a_tiles, lane_consts, gate_expand, sel, sel_consts):
    b = proj.shape[0]
    tq, tk = ATT_TQ, ATT_TK
    grp_w = HEADS_PER_GROUP * HEAD_DIM
    k_blk = NSA_K_COL // LANES
    v_blk = NSA_V_COL // LANES
    z_blk = NSA_Z_COL // grp_w
    rows = ATT_TILES_PER_STEP * tq
    kv_spec = lambda blk: pl.BlockSpec((1, SEQ, LANES), lambda i, g, s: (i, 0, blk + g))
    z_spec = lambda blk: pl.BlockSpec((1, rows, grp_w), lambda i, g, s: (i, s, blk + g))
    whole = pl.BlockSpec(memory_space=pltpu.VMEM)
    return pl.pallas_call(
        _token_kernel,
        grid=(b, GROUPS, SEQ // rows),
        in_specs=[
            pl.BlockSpec((1, rows, grp_w), lambda i, g, s: (i, s, NSA_Q_COL // grp_w + g)),
            kv_spec(k_blk), kv_spec(v_blk), kv_spec(k_blk + GROUPS), kv_spec(v_blk + GROUPS),
            pl.BlockSpec((1, rows, LANES), lambda i, g, s: (i, s, NSA_G_COL // LANES)),
            z_spec(z_blk + GROUPS), z_spec(z_blk + 2 * GROUPS),
            whole, whole, whole,
            pl.BlockSpec((1, 1, N_SLC, rows), lambda i, g, s: (i, g, 0, s)),
            whole, whole, whole, whole,
        ],
        out_specs=pl.BlockSpec((1, rows, grp_w), lambda i, g, s: (i, s, g)),
        out_shape=jax.ShapeDtypeStruct((b, SEQ, NSA_WIDTH), BF16),
        scratch_shapes=[
            pltpu.VMEM((2, HEADS_PER_GROUP * tq, LANES), BF16),
            pltpu.VMEM((2, HEADS_PER_GROUP * tq, LANES), F32),
            pltpu.VMEM((4, HEADS_PER_GROUP * tq, LANES), F32),
            pltpu.VMEM((2, HEADS_PER_GROUP * tq, tk), F32),
            pltpu.VMEM((2, HEADS_PER_GROUP * tq, tk), BF16),
            pltpu.VMEM((2, HEADS_PER_GROUP * tq, LANES), F32),
            pltpu.VMEM((3, SEQ // tk, LANES, tk), BF16),
            pltpu.VMEM((4, SEQ, LANES), BF16),
            pltpu.VMEM((4, tq, grp_w), F32),
        ],
        compiler_params=_cparams(("parallel", "parallel", "arbitrary")),
        name="token_attention",
    )(proj, proj, proj, proj, proj, proj, proj, proj, a_tiles, lane_consts, gate_expand,
      sel, *sel_consts)


def _retention_kernel(q_ref, k_ref, v_ref, z_ref, cos_ref, sin_ref, inner_ref, xi_ref, zeta_ref,
                      gn_ref, y_ref, state_ref, *, decays):
    @pl.when(pl.program_id(1) == 0)
    def _():
        state_ref[...] = jnp.zeros(state_ref.shape, F32)

    cos = cos_ref[...]
    sin = sin_ref[...]
    half = RET_QK_DIM // 2

    def rot(x):
        x1, x2 = x[:, :half], x[:, half:]
        return jnp.concatenate([x1 * cos - x2 * sin, x1 * sin + x2 * cos], axis=1)

    for h in range(RET_HEADS):
        qs = slice(h * RET_QK_DIM, (h + 1) * RET_QK_DIM)
        vs = slice(h * RET_V_DIM, (h + 1) * RET_V_DIM)
        qr = rot(q_ref[0, :, qs].astype(F32))
        kr = rot(k_ref[0, :, qs].astype(F32)) * (RET_QK_DIM ** -0.5)
        qb = qr.astype(BF16)
        vh = v_ref[0, :, vs]
        attn = lax.dot_general(qb, kr.astype(BF16), _NT, preferred_element_type=F32) * inner_ref[h]
        st = state_ref[h]
        o = (jnp.dot(attn.astype(BF16), vh, preferred_element_type=F32)
             + jnp.dot(qb, st.astype(BF16), preferred_element_type=F32) * xi_ref[h])
        kz = (kr * zeta_ref[h]).astype(BF16)
        state_ref[h] = st * decays[h] + lax.dot_general(kz, vh, _TN, preferred_element_type=F32)
        mu = jnp.mean(o, axis=1, keepdims=True)
        d = o - mu
        var = jnp.mean(d * d, axis=1, keepdims=True)
        on = d * lax.rsqrt(var + GN_EPS) * gn_ref[h]
        y_ref[0, :, vs] = (on * _silu(z_ref[0, :, vs].astype(F32))).astype(y_ref.dtype)


def _retention(proj, gn_gain, tables):
    b = proj.shape[0]
    c = RET_CHUNK
    cos, sin, inner, xi, zeta, decays = tables
    v_blk = 2 * RET_QK_WIDTH // RET_WIDTH
    return pl.pallas_call(
        functools.partial(_retention_kernel, decays=decays),
        grid=(b, SEQ // c),
        in_specs=[
            pl.BlockSpec((1, c, RET_QK_WIDTH), lambda i, j: (i, j, 0)),
            pl.BlockSpec((1, c, RET_QK_WIDTH), lambda i, j: (i, j, 1)),
            pl.BlockSpec((1, c, RET_WIDTH), lambda i, j: (i, j, v_blk)),
            pl.BlockSpec((1, c, RET_WIDTH), lambda i, j: (i, j, v_blk + 1)),
            pl.BlockSpec((c, RET_QK_DIM // 2), lambda i, j: (j, 0)),
            pl.BlockSpec((c, RET_QK_DIM // 2), lambda i, j: (j, 0)),
            pl.BlockSpec((RET_HEADS, c, c), lambda i, j: (0, 0, 0)),
            pl.BlockSpec((RET_HEADS, c, 1), lambda i, j: (0, 0, 0)),
            pl.BlockSpec((RET_HEADS, c, 1), lambda i, j: (0, 0, 0)),
            pl.BlockSpec((RET_HEADS, 1, RET_V_DIM), lambda i, j: (0, 0, 0)),
        ],
        out_specs=pl.BlockSpec((1, c, RET_WIDTH), lambda i, j: (i, j, 0)),
        out_shape=jax.ShapeDtypeStruct((b, SEQ, RET_WIDTH), BF16),
        scratch_shapes=[pltpu.VMEM((RET_HEADS, RET_QK_DIM, RET_V_DIM), F32)],
        compiler_params=_cparams(("parallel", "arbitrary")),
        name="retention",
    )(proj, proj, proj, proj, cos, sin, inner, xi, zeta, gn_gain.reshape(RET_HEADS, 1, RET_V_DIM))


def _t5_bucket_np(dist):
    n = np.maximum(dist, 0)
    max_exact = REL_BUCKETS // 2
    nf = np.maximum(n, 1).astype(np.float64)
    large = max_exact + (np.log(nf / max_exact) / math.log(REL_MAX_DIST / max_exact)
                         * (REL_BUCKETS - max_exact)).astype(np.int64)
    large = np.minimum(large, REL_BUCKETS - 1)
    return np.where(n < max_exact, n, large).astype(np.int32)


def _skewed(vec, rows, stride, cols):
    p = vec.shape[-1]
    lead = vec.shape[:-1]
    flat = jnp.tile(vec, (1,) * len(lead) + (rows,))[..., :rows * (p - stride)]
    return flat.reshape(lead + (rows, p - stride))[..., :cols]


def _nsa_tables(table):
    tq, tk = ATT_TQ, ATT_TK
    assert tq == tk
    tab_t = table.T * LOG2E
    period = 2 * SEQ
    x = np.arange(period)
    ok = (x >= CMP_BLOCK - 1) & (x < SEQ)
    vec = jnp.take(tab_t, jnp.asarray(_t5_bucket_np(x - (CMP_BLOCK - 1))), axis=1)
    vec = jnp.where(jnp.asarray(ok)[None, :], vec, MASKED)
    bias_cmp = _skewed(vec, N_CMP_PAD, CMP_STRIDE, SEQ)
    rel_t = tab_t - tab_t[:, REL_BUCKETS - 1:]
    period = 2 * tq
    x = np.arange(period)
    x = np.where(x < tq, x, x - period)
    tiles = []
    for off in range(3):
        d = off * tk - x
        ok = (d >= 0) & (d < WIN_SIZE) if off == 2 else (d >= 0)
        vec = jnp.take(rel_t, jnp.asarray(_t5_bucket_np(d)), axis=1)
        vec = jnp.where(jnp.asarray(ok)[None, :], vec, MASKED)
        tiles.append(_skewed(vec, tq, 1, tk))
    a_tiles = jnp.stack(tiles + [jnp.zeros_like(tiles[0])])
    cs = np.arange(N_CMP_PAD)[None, :] * CMP_STRIDE
    jj = np.arange(N_SLC)[:, None]
    ovl = ((cs < (jj + 1) * SLC_BLOCK) & (cs + CMP_BLOCK > jj * SLC_BLOCK)
           & (np.arange(N_CMP_PAD)[None, :] < N_CMP))
    ovl = jnp.asarray(ovl.astype(np.float32), BF16)
    lane = np.arange(LANES).reshape(1, 1, 1, LANES)
    key = np.arange(SEQ).reshape(1, SEQ // tk, tk, 1)
    base = np.array([HEAD_DIM, 0]).reshape(2, 1, 1, 1)
    key_blocks = jnp.asarray((lane == base + key // SLC_BLOCK).astype(np.float32), BF16)
    col = np.arange(2 * LANES).reshape(1, -1)
    jb = np.arange(N_SLC).reshape(-1, 1)
    place = (col == HEAD_DIM + jb) | (col == LANES + jb)
    row = np.arange(2 * LANES).reshape(-1, 1)
    out = np.arange(LANES).reshape(1, -1)
    den_spread = ((row == LANES - 1) & (out < HEAD_DIM)) | ((row == LANES) & (out >= HEAD_DIM))
    sel_consts = (key_blocks, jnp.asarray(place.astype(np.float32), BF16),
                  jnp.asarray(place.any(axis=0, keepdims=True).astype(np.float32)),
                  jnp.asarray(den_spread.astype(np.float32), BF16))
    c = np.arange(LANES).reshape(1, 1, LANES, 1)
    col = np.arange(HEADS_PER_GROUP * HEAD_DIM).reshape(1, 1, 1, -1) // HEAD_DIM
    br = np.arange(3).reshape(3, 1, 1, 1)
    gg = np.arange(GROUPS).reshape(1, GROUPS, 1, 1)
    gate_expand = jnp.asarray((c == br * HEADS + HEADS_PER_GROUP * gg + col).astype(np.float32),
                              BF16)
    return bias_cmp, a_tiles, ovl, sel_consts, _lane_consts(max(tq, CMP_TQ)), gate_expand


def _retention_tables():
    c = RET_CHUNK
    log_g = jnp.log(1.0 - 2.0 ** (-5.0 - jnp.arange(RET_HEADS, dtype=F32)))
    i = jnp.arange(c, dtype=F32)
    diff = i[:, None] - i[None, :]
    inner = jnp.where(diff >= 0, jnp.exp(diff[None] * log_g[:, None, None]), 0.0)
    xi = jnp.exp((i + 1.0)[None, :] * log_g[:, None])[:, :, None]
    zeta = jnp.exp((c - 1.0 - i)[None, :] * log_g[:, None])[:, :, None]
    decays = tuple(float((1.0 - 2.0 ** (-5.0 - h)) ** c) for h in range(RET_HEADS))
    inv = 1.0 / (ROPE_BASE ** jnp.linspace(0.0, 1.0, RET_QK_DIM // 2, dtype=F32))
    ang = jnp.arange(SEQ, dtype=F32)[:, None] * inv[None, :]
    return jnp.cos(ang), jnp.sin(ang), inner, xi, zeta, decays


def _nsa_w_in_layout(w):
    d = w.shape[0]
    q = w[:, :NSA_WIDTH] * (HEAD_DIM ** -0.5 * LOG2E)
    kv0 = NSA_WIDTH
    g0 = kv0 + 6 * KV_WIDTH
    z0 = g0 + 3 * HEADS
    slab = lambda n: w[:, kv0 + n * KV_WIDTH: kv0 + (n + 1) * KV_WIDTH]
    cols = [q, w[:, z0:z0 + 3 * NSA_WIDTH]]
    for n in (2, 4, 3, 5):
        for g in range(GROUPS):
            part = slab(n)[:, g * HEAD_DIM:(g + 1) * HEAD_DIM]
            cols += [part, part]
    cols += [slab(0), slab(1), w[:, g0:z0]]
    used = NSA_G_COL + 3 * HEADS
    cols.append(jnp.zeros((d, NSA_PROJ_PAD - used), w.dtype))
    return jnp.concatenate(cols, axis=1).astype(BF16)


def _nsa_layer(h2d, b, pre_gain, post_gain, w_in, w_out, k_pos, k_w1, k_w2, v_pos, v_w1, v_w2, tabs):
    bias_cmp, a_tiles, ovl, sel_consts, lane_consts, gate_expand = tabs
    proj = _norm_proj(h2d, pre_gain, _nsa_w_in_layout(w_in), NSA_PROJ_TN)
    proj = proj.reshape(b, SEQ, NSA_PROJ_PAD)
    ckv = proj[:, :, NSA_CKV_COL:NSA_CKV_COL + 2 * KV_WIDTH]
    ckv = ckv.reshape(b, N_CMP_PAD, CMP_STRIDE, 2, GROUPS, HEAD_DIM).transpose(0, 3, 4, 1, 2, 5)
    ckv = ckv.reshape(b, 2, GROUPS, N_CMP_PAD, CMP_STRIDE * HEAD_DIM)
    pos = jnp.stack([k_pos, v_pos]).reshape(2, 2, CMP_STRIDE * HEAD_DIM)
    w1 = jnp.stack([k_w1, v_w1]).astype(BF16)
    w2 = jnp.stack([k_w2, v_w2])
    w2d = jnp.concatenate([w2, w2], axis=2).astype(BF16)
    ckv_c = _compress(ckv, pos, w1, w2d)
    y_cmp, sel = _cmp_select(proj, ckv_c, bias_cmp, ovl, lane_consts, gate_expand[0])
    y_tok = _token_attention(proj, a_tiles, lane_consts, gate_expand[1:], sel, sel_consts)
    parts = [y.reshape(b * SEQ, NSA_WIDTH) for y in (y_cmp, y_tok)]
    return _out_post(parts, w_out.astype(BF16), h2d, post_gain)


def _ret_layer(h2d, b, pre_gain, post_gain, w_in, w_out, gn_gain, tabs):
    proj = _norm_proj(h2d, pre_gain, w_in.astype(BF16), RET_PROJ_TN).reshape(b, SEQ, -1)
    y = _retention(proj, gn_gain, tabs)
    return _out_post([y.reshape(b * SEQ, RET_WIDTH)], w_out.astype(BF16), h2d, post_gain)


def kernel(x, pre_norm_gain, post_norm_gain, rel_bias_table, nsa_w_in, nsa_w_out, nsa_cmp_k_pos, nsa_cmp_k_w1, nsa_cmp_k_w2, nsa_cmp_v_pos, nsa_cmp_v_w1, nsa_cmp_v_w2, ret_w_in, ret_w_out, ret_gn_gain):
    b, s, d = x.shape
    assert s == SEQ and d == D_MODEL
    nsa_tabs = _nsa_tables(rel_bias_table)
    ret_tabs = _retention_tables()
    h = x.reshape(b * s, d)
    for layer in range(DEPTH):
        slot = layer // 2
        if layer % 2 == 0:
            h = _nsa_layer(h, b, pre_norm_gain[layer], post_norm_gain[layer], nsa_w_in[slot],
                           nsa_w_out[slot], nsa_cmp_k_pos[slot], nsa_cmp_k_w1[slot],
                           nsa_cmp_k_w2[slot], nsa_cmp_v_pos[slot], nsa_cmp_v_w1[slot],
                           nsa_cmp_v_w2[slot], nsa_tabs)
        else:
            h = _ret_layer(h, b, pre_norm_gain[layer], post_norm_gain[layer], ret_w_in[slot],
                           ret_w_out[slot], ret_gn_gain[slot], ret_tabs)
    return h.reshape(b, s, d)
```

```python
import functools
import math

import numpy as np
import jax
import jax.numpy as jnp
from jax import lax
from jax.experimental import pallas as pl
from jax.experimental.pallas import tpu as pltpu

F32 = jnp.float32
BF16 = jnp.bfloat16

D_MODEL = 1024
SEQ = 2048
DEPTH = 4
RMS_EPS = 1e-6
GN_EPS = 1e-6
MASKED = -1e30
LOG2E = math.log2(math.e)

HEADS = 16
HEAD_DIM = 64
GROUPS = 4
HEADS_PER_GROUP = HEADS // GROUPS
NSA_WIDTH = HEADS * HEAD_DIM
KV_WIDTH = GROUPS * HEAD_DIM
CMP_BLOCK = 32
CMP_STRIDE = 16
CMP_HIDDEN = 256
N_CMP = (SEQ - CMP_BLOCK) // CMP_STRIDE + 1
N_CMP_PAD = 128
SLC_BLOCK = 64
N_SLC = SEQ // SLC_BLOCK
SLC_TOPN = 16
WIN_SIZE = 512
FORCED_SCORE = 1e3
REL_BUCKETS = 32
REL_MAX_DIST = 128

RET_HEADS = 4
RET_QK_DIM = 256
RET_V_DIM = 512
RET_QK_WIDTH = RET_HEADS * RET_QK_DIM
RET_WIDTH = RET_HEADS * RET_V_DIM
ROPE_BASE = 10000.0

LANES = 128
VMEM_LIMIT_BYTES = 56 * 1024 * 1024

PROJ_TM = 1024
RET_PROJ_TN = 2048
NSA_PROJ_TN = 3456
POST_TM = 512
ATT_TQ = 256
ATT_TK = 256
ATT_TILES_PER_STEP = 8
CMP_TQ = 1024
RET_CHUNK = 256

NSA_Q_COL = 0
NSA_Z_COL = 1024
NSA_K_COL = 4096
NSA_V_COL = 5120
NSA_CKV_COL = 6144
NSA_G_COL = 6656
NSA_PROJ_PAD = 6912

_NT = (((1,), (1,)), ((), ()))
_TN = (((0,), (0,)), ((), ()))


def _cparams(sem):
    return pltpu.CompilerParams(dimension_semantics=sem, vmem_limit_bytes=VMEM_LIMIT_BYTES)


def _norm_proj_kernel(x_ref, g_ref, w_ref, o_ref, xn_ref):
    @pl.when(pl.program_id(1) == 0)
    def _():
        x = x_ref[...]
        ms = jnp.mean(x * x, axis=-1, keepdims=True)
        xn_ref[...] = (x * lax.rsqrt(ms + RMS_EPS) * g_ref[...]).astype(BF16)

    o_ref[...] = jnp.dot(xn_ref[...], w_ref[...], preferred_element_type=F32).astype(o_ref.dtype)


def _norm_proj(x2d, gain, w_bf16, tn):
    m, d = x2d.shape
    n = w_bf16.shape[1]
    assert n % tn == 0
    return pl.pallas_call(
        _norm_proj_kernel,
        grid=(m // PROJ_TM, n // tn),
        in_specs=[
            pl.BlockSpec((PROJ_TM, d), lambda i, j: (i, 0)),
            pl.BlockSpec((1, d), lambda i, j: (0, 0)),
            pl.BlockSpec((d, tn), lambda i, j: (0, j)),
        ],
        out_specs=pl.BlockSpec((PROJ_TM, tn), lambda i, j: (i, j)),
        out_shape=jax.ShapeDtypeStruct((m, n), BF16),
        scratch_shapes=[pltpu.VMEM((PROJ_TM, d), BF16)],
        compiler_params=_cparams(("parallel", "arbitrary")),
        name="norm_proj",
    )(x2d, gain.reshape(1, d), w_bf16)


def _out_post_kernel(*refs, n_parts):
    y_refs = refs[:n_parts]
    w_ref, h_ref, g_ref, o_ref = refs[n_parts:]
    y = y_refs[0][...].astype(F32)
    for r in y_refs[1:]:
        y = y + r[...].astype(F32)
    t = jnp.dot(y.astype(BF16), w_ref[...], preferred_element_type=F32)
    ms = jnp.mean(t * t, axis=-1, keepdims=True)
    o_ref[...] = h_ref[...] + t * lax.rsqrt(ms + RMS_EPS) * g_ref[...]


def _out_post(parts, w_bf16, h2d, gain):
    m, d = h2d.shape
    k = w_bf16.shape[0]
    n_parts = len(parts)
    return pl.pallas_call(
        functools.partial(_out_post_kernel, n_parts=n_parts),
        grid=(m // POST_TM,),
        in_specs=[pl.BlockSpec((POST_TM, k), lambda i: (i, 0)) for _ in parts] + [
            pl.BlockSpec((k, d), lambda i: (0, 0)),
            pl.BlockSpec((POST_TM, d), lambda i: (i, 0)),
            pl.BlockSpec((1, d), lambda i: (0, 0)),
        ],
        out_specs=pl.BlockSpec((POST_TM, d), lambda i: (i, 0)),
        out_shape=jax.ShapeDtypeStruct((m, d), F32),
        compiler_params=_cparams(("parallel",)),
        name="out_post",
    )(*parts, w_bf16, h2d, gain.reshape(1, d))


def _compress_kernel(x_ref, pos_ref, w1_ref, w2_ref, o_ref):
    half = CMP_STRIDE * HEAD_DIM
    for which in range(2):
        x = x_ref[0, which, 0].astype(F32)
        xa = (x + pos_ref[which, 0:1, :]).astype(BF16)
        xb = (x + pos_ref[which, 1:2, :]).astype(BF16)
        pa = jnp.dot(xa, w1_ref[which, :half, :], preferred_element_type=F32)
        pb = jnp.dot(xb, w1_ref[which, half:, :], preferred_element_type=F32)
        hid = pa + pltpu.roll(pb, N_CMP_PAD - 1, 0)
        hid = hid * jax.nn.sigmoid(hid)
        o_ref[0, which, 0] = jnp.dot(hid.astype(BF16), w2_ref[which],
                                     preferred_element_type=F32).astype(o_ref.dtype)


def _compress(ckv_rows, pos, w1, w2d):
    b = ckv_rows.shape[0]
    row_w = CMP_STRIDE * HEAD_DIM
    return pl.pallas_call(
        _compress_kernel,
        grid=(b, GROUPS),
        in_specs=[
            pl.BlockSpec((1, 2, 1, N_CMP_PAD, row_w), lambda i, g: (i, 0, g, 0, 0)),
            pl.BlockSpec((2, 2, row_w), lambda i, g: (0, 0, 0)),
            pl.BlockSpec((2, 2 * row_w, CMP_HIDDEN), lambda i, g: (0, 0, 0)),
            pl.BlockSpec((2, CMP_HIDDEN, 2 * HEAD_DIM), lambda i, g: (0, 0, 0)),
        ],
        out_specs=pl.BlockSpec((1, 2, 1, N_CMP_PAD, 2 * HEAD_DIM), lambda i, g: (i, 0, g, 0, 0)),
        out_shape=jax.ShapeDtypeStruct((b, 2, GROUPS, N_CMP_PAD, 2 * HEAD_DIM), BF16),
        compiler_params=_cparams(("parallel", "parallel")),
        name="compress",
    )(ckv_rows, pos, w1, w2d)


def _lane_consts(rows):
    lane = np.arange(LANES)
    lo = (lane < HEAD_DIM).astype(np.float32)
    hi = (lane >= HEAD_DIM).astype(np.float32)
    last = (lane == LANES - 1).astype(np.float32)
    first = (lane == 0).astype(np.float32)
    c = np.stack([lo, hi, 1.0 - last, last, 1.0 - first, first])
    return jnp.asarray(np.broadcast_to(c[:, None, :], (6, rows, LANES)), BF16)


def _bf16_terms(x, n):
    terms = []
    for _ in range(n - 1):
        t = x.astype(BF16)
        terms.append(t)
        x = x - t.astype(F32)
    terms.append(x.astype(BF16))
    return terms


def _gates(gate_logits, expands):
    terms = _bf16_terms(jax.nn.sigmoid(gate_logits.astype(F32)), 2)
    return [sum(jnp.dot(t, ex, preferred_element_type=F32) for t in terms) for ex in expands]


def _silu(x):
    return x * jax.nn.sigmoid(x)


def _cmp_select_kernel(q_ref, gate_ref, z_ref, ckv_ref, bias_ref, ovl_ref, lc_ref, gexp_ref,
                       y_ref, sel_ref):
    g = pl.program_id(0)
    qi = pl.program_id(1)
    tq = q_ref.shape[1]
    q = q_ref[0]
    kk = ckv_ref[0, 0, 0]
    vv = ckv_ref[0, 1, 0]
    lane = lax.broadcasted_iota(jnp.int32, (tq, LANES), 1)

    psum = jnp.zeros((N_CMP_PAD, tq), F32)
    pairs = []
    for a in range(2):
        qp = q[:, a * LANES:(a + 1) * LANES]
        outs = []
        for e in range(2):
            s = lax.dot_general(kk, qp * lc_ref[e, :tq], _NT, preferred_element_type=F32)
            bias = bias_ref[2 * a + e]
            s = s + bias
            m = jnp.max(s, axis=0, keepdims=True)
            p = jnp.exp2(s - m)
            p = jnp.where(bias > 0.5 * MASKED, p / jnp.sum(p, axis=0, keepdims=True), 0.0)
            psum = psum + p
            outs.append(lax.dot_general(p.astype(BF16), vv, _TN, preferred_element_type=F32))
        pairs.append(jnp.where(lane < HEAD_DIM, outs[0], outs[1]))
    y_ref[0] = (jnp.concatenate(pairs, axis=1) * _gates(gate_ref[0], [gexp_ref[g]])[0]
                * _silu(z_ref[0].astype(F32))).astype(y_ref.dtype)

    ovl = ovl_ref[...]
    imp = sum(jnp.dot(ovl, t, preferred_element_type=F32) for t in _bf16_terms(psum, 3))
    t = qi * tq + lax.broadcasted_iota(jnp.int32, (N_SLC, tq), 1)
    blk = lax.broadcasted_iota(jnp.int32, (N_SLC, tq), 0)
    cur = lax.shift_right_logical(t, int(math.log2(SLC_BLOCK)))
    forced = (blk == 0) | (blk == cur) | (blk == cur - 1)
    score = jnp.where(blk * SLC_BLOCK <= t, imp + jnp.where(forced, FORCED_SCORE, 0.0), MASKED)
    cnt = jnp.zeros((N_SLC, tq), jnp.int32)
    for jp in range(N_SLC):
        row = score[jp:jp + 1, :]
        beats = (row > score) | ((row == score) & (blk > jp))
        cnt = cnt + beats.astype(jnp.int32)
    sel_ref[0, 0] = jnp.where(cnt < SLC_TOPN, 1.0, 0.0).astype(sel_ref.dtype)


def _cmp_select(proj, ckv, bias_cmp, ovl, lane_consts, gate_expand):
    b = proj.shape[0]
    tq = CMP_TQ
    grp_w = HEADS_PER_GROUP * HEAD_DIM
    return pl.pallas_call(
        _cmp_select_kernel,
        grid=(GROUPS, SEQ // tq, b),
        in_specs=[
            pl.BlockSpec((1, tq, grp_w), lambda g, qi, i: (i, qi, NSA_Q_COL // grp_w + g)),
            pl.BlockSpec((1, tq, LANES), lambda g, qi, i: (i, qi, NSA_G_COL // LANES)),
            pl.BlockSpec((1, tq, grp_w), lambda g, qi, i: (i, qi, NSA_Z_COL // grp_w + g)),
            pl.BlockSpec((1, 2, 1, N_CMP_PAD, 2 * HEAD_DIM), lambda g, qi, i: (i, 0, g, 0, 0)),
            pl.BlockSpec((HEADS_PER_GROUP, N_CMP_PAD, tq), lambda g, qi, i: (g, 0, qi)),
            pl.BlockSpec((N_SLC, N_CMP_PAD), lambda g, qi, i: (0, 0)),
            pl.BlockSpec(memory_space=pltpu.VMEM),
            pl.BlockSpec(memory_space=pltpu.VMEM),
        ],
        out_specs=[
            pl.BlockSpec((1, tq, grp_w), lambda g, qi, i: (i, qi, g)),
            pl.BlockSpec((1, 1, N_SLC, tq), lambda g, qi, i: (i, g, 0, qi)),
        ],
        out_shape=[
            jax.ShapeDtypeStruct((b, SEQ, NSA_WIDTH), BF16),
            jax.ShapeDtypeStruct((b, GROUPS, N_SLC, SEQ), BF16),
        ],
        compiler_params=_cparams(("parallel", "parallel", "parallel")),
        name="cmp_select",
    )(proj, proj, proj, ckv, bias_cmp, ovl, lane_consts, gate_expand)


def _token_kernel(q_ref, ks_ref, vs_ref, kw_ref, vw_ref, gate_ref, zs_ref, zw_ref, a_ref, lc_ref,
                  gexp_ref, sel_ref, kblk_ref, place_ref, placed_ref, dspread_ref, y_ref,
                  qm_ref, m_ref, acc_ref, s_bufs, p_bufs, c_bufs, kx_ref, vx_ref, gz_ref):
    g = pl.program_id(1)
    pair_id = pl.program_id(2)
    tq = ATT_TQ
    tk = ATT_TK

    def stream_len(qi_):
        return jnp.minimum(qi_, WIN_SIZE // tk) + 1 + qi_ + 1

    @pl.when(pair_id == 0)
    def _():
        for c in range(SEQ // tk):
            rows = slice(c * tk, (c + 1) * tk)
            for branch, v_ref in enumerate((vs_ref, vw_ref)):
                vv = v_ref[0, rows, :]
                vx_ref[2 * branch, rows, :] = vv * lc_ref[2, :tk] + lc_ref[3, :tk]
                vx_ref[2 * branch + 1, rows, :] = vv * lc_ref[4, :tk] + lc_ref[5, :tk]
            kk = ks_ref[0, rows, :]
            for e in range(2):
                kx_ref[e, c] = (kk * lc_ref[e, :tk] + kblk_ref[e, c]).astype(F32).T.astype(BF16)
            kx_ref[2, c] = kw_ref[0, rows, :].astype(F32).T.astype(BF16)

    def tile(qi_, j):
        n_win = jnp.minimum(qi_, WIN_SIZE // tk) + 1
        win = (j < n_win).astype(jnp.int32)
        t = j - (1 - win) * n_win
        return win, jnp.clip(qi_ - t, 0, qi_), jnp.where(win == 1, t, jnp.where(t < 2, t, 3))

    def scores(qi, j, s_ref):
        win, ki, _ = tile(qi, j)
        for e in range(2):
            pair = slice(2 * e * tq, 2 * (e + 1) * tq)
            s_ref[pair, :] = jnp.dot(qm_ref[win, pair, :], kx_ref[jnp.where(win == 1, 2, e), ki],
                                     preferred_element_type=F32)

    def softmax(qi, j, s_ref, p_ref, c_ref):
        win, _, kind = tile(qi, j)
        for slot in range(HEADS_PER_GROUP):
            e, a = divmod(slot, 2)
            rows = slice(slot * tq, (slot + 1) * tq)
            s = s_ref[rows, :] + a_ref[kind, HEADS_PER_GROUP * g + 2 * a + e]
            m_prev = m_ref[win, rows, :]
            m_new = jnp.maximum(m_prev, jnp.max(s, axis=1, keepdims=True))
            c_ref[rows, :] = jnp.exp2(m_prev - m_new)
            x = s - jnp.concatenate([m_new] * (tk // LANES), axis=1)
            p_ref[rows, :] = jnp.exp2(x.astype(BF16))
            m_ref[win, rows, :] = m_new

    def values(qi_, par_, j, p_ref, c_ref):
        win, ki, _ = tile(qi_, j)
        rows = pl.ds(pl.multiple_of(ki * tk, tk), tk)
        for e in range(2):
            pair = slice(2 * e * tq, 2 * (e + 1) * tq)
            pv = jnp.dot(p_ref[pair, :], vx_ref[2 * win + e, rows, :], preferred_element_type=F32)
            acc = acc_ref.at[2 * par_ + win]
            acc[pair, :] = c_ref[pair, :] * acc[pair, :] + pv

    lane = lax.broadcasted_iota(jnp.int32, (tq, LANES), 1)

    def gated_output(par_, branch):
        acc = acc_ref.at[2 * par_ + branch]
        pairs = []
        for a in range(2):
            acc_e = acc[a * tq:(a + 1) * tq, :]
            acc_o = acc[(2 + a) * tq:(3 + a) * tq, :]
            both = jnp.concatenate([acc_e, acc_o], axis=1)
            den = sum(jnp.dot(t, dspread_ref[...], preferred_element_type=F32)
                      for t in _bf16_terms(both, 2))
            pairs.append(jnp.where(lane < HEAD_DIM, acc_e, acc_o) / den)
        return jnp.concatenate(pairs, axis=1) * gz_ref[2 * par_ + branch]

    bufs = [(s_bufs.at[i], p_bufs.at[i], c_bufs.at[i]) for i in range(2)]

    def step(qi, half, j, parity):
        cur, nxt = bufs[parity], bufs[1 - parity]
        values(qi, half, j - 1, nxt[1], nxt[2])
        softmax(qi, j, *cur)
        scores(qi, j + 1, nxt[0])

    def enter(qi, slot):
        rows = slice(slot * tq, (slot + 1) * tq)
        half = slot % 2
        q = q_ref[0, rows, :]
        qms = []
        for slot in range(HEADS_PER_GROUP):
            e, a = divmod(slot, 2)
            qms.append(q[:, a * LANES:(a + 1) * LANES] * lc_ref[e, :tq])
            qm_ref[1, slot * tq:(slot + 1) * tq, :] = qms[slot]
        flags = lax.dot_general(sel_ref[0, 0, :, rows], place_ref[...], _TN,
                                preferred_element_type=F32)
        fill = ((placed_ref[...] - flags) * MASKED).astype(BF16)
        for slot in range(HEADS_PER_GROUP):
            e = slot // 2
            qm_ref[0, slot * tq:(slot + 1) * tq, :] = (qms[slot]
                                                       + fill[:, e * LANES:(e + 1) * LANES])
        gates = _gates(gate_ref[0, rows, :], [gexp_ref[0, g], gexp_ref[1, g]])
        for branch, z_ref in enumerate((zs_ref, zw_ref)):
            gz_ref[2 * half + branch] = gates[branch] * _silu(z_ref[0, rows, :].astype(F32))
        m_ref[...] = jnp.full(m_ref.shape, -jnp.inf, F32)
        for branch in range(2):
            acc_ref[2 * half + branch] = jnp.zeros(acc_ref.shape[1:], F32)
        scores(qi, 0, bufs[0][0])

    def fill_pipeline(qi, half):
        softmax(qi, 0, *bufs[0])
        scores(qi, 1, bufs[1][0])
        step(qi, half, 1, 1)

    def run_stream(qi, half):
        def steps(first, count):
            for k in range(count):
                step(qi, half, first + k, k % 2)

        n_tiles = stream_len(qi)
        n_steps = n_tiles - 2
        n_quads = lax.shift_right_logical(n_steps, 2)

        def four_steps(i, carry):
            steps(4 * i + 2, 4)
            return carry
        lax.fori_loop(0, n_quads, four_steps, 0)

        @pl.when(lax.bitwise_and(n_steps, 2) == 2)
        def _():
            steps(4 * n_quads + 2, 2)

        @pl.when(lax.bitwise_and(n_steps, 1) == 1)
        def _():
            steps(n_tiles - 1, 1)

    def leave(qi, half):
        y_win = gated_output(half, 1)
        last_pos = stream_len(qi) - 1
        last = lax.bitwise_and(last_pos, 1)
        values(qi, half, last_pos, p_bufs.at[last], c_bufs.at[last])
        return y_win

    def write_output(slot, y_win):
        rows = slice(slot * tq, (slot + 1) * tq)
        y_ref[0, rows, :] = (y_win + gated_output(slot % 2, 0)).astype(y_ref.dtype)

    y_win = None
    for slot in range(ATT_TILES_PER_STEP):
        qi = ATT_TILES_PER_STEP * pair_id + slot
        enter(qi, slot)
        if slot > 0:
            write_output(slot - 1, y_win)
        fill_pipeline(qi, slot % 2)
        run_stream(qi, slot % 2)
        y_win = leave(qi, slot % 2)
    write_output(ATT_TILES_PER_STEP - 1, y_win)


def _token_attention(proj, a_tiles, lane_consts, gate_expand, sel, sel_consts):
    b = proj.shape[0]
    tq, tk = ATT_TQ, ATT_TK
    grp_w = HEADS_PER_GROUP * HEAD_DIM
    k_blk = NSA_K_COL // LANES
    v_blk = NSA_V_COL // LANES
    z_blk = NSA_Z_COL // grp_w
    rows = ATT_TILES_PER_STEP * tq
    kv_spec = lambda blk: pl.BlockSpec((1, SEQ, LANES), lambda i, g, s: (i, 0, blk + g))
    z_spec = lambda blk: pl.BlockSpec((1, rows, grp_w), lambda i, g, s: (i, s, blk + g))
    whole = pl.BlockSpec(memory_space=pltpu.VMEM)
    return pl.pallas_call(
        _token_kernel,
        grid=(b, GROUPS, SEQ // rows),
        in_specs=[
            pl.BlockSpec((1, rows, grp_w), lambda i, g, s: (i, s, NSA_Q_COL // grp_w + g)),
            kv_spec(k_blk), kv_spec(v_blk), kv_spec(k_blk + GROUPS), kv_spec(v_blk + GROUPS),
            pl.BlockSpec((1, rows, LANES), lambda i, g, s: (i, s, NSA_G_COL // LANES)),
            z_spec(z_blk + GROUPS), z_spec(z_blk + 2 * GROUPS),
            whole, whole, whole,
            pl.BlockSpec((1, 1, N_SLC, rows), lambda i, g, s: (i, g, 0, s)),
            whole, whole, whole, whole,
        ],
        out_specs=pl.BlockSpec((1, rows, grp_w), lambda i, g, s: (i, s, g)),
        out_shape=jax.ShapeDtypeStruct((b, SEQ, NSA_WIDTH), BF16),
        scratch_shapes=[
            pltpu.VMEM((2, HEADS_PER_GROUP * tq, LANES), BF16),
            pltpu.VMEM((2, HEADS_PER_GROUP * tq, LANES), F32),
            pltpu.VMEM((4, HEADS_PER_GROUP * tq, LANES), F32),
            pltpu.VMEM((2, HEADS_PER_GROUP * tq, tk), F32),
            pltpu.VMEM((2, HEADS_PER_GROUP * tq, tk), BF16),
            pltpu.VMEM((2, HEADS_PER_GROUP * tq, LANES), F32),
            pltpu.VMEM((3, SEQ // tk, LANES, tk), BF16),
            pltpu.VMEM((4, SEQ, LANES), BF16),
            pltpu.VMEM((4, tq, grp_w), F32),
        ],
        compiler_params=_cparams(("parallel", "parallel", "arbitrary")),
        name="token_attention",
    )(proj, proj, proj, proj, proj, proj, proj, proj, a_tiles, lane_consts, gate_expand,
      sel, *sel_consts)


def _retention_kernel(q_ref, k_ref, v_ref, z_ref, cos_ref, sin_ref, inner_ref, xi_ref, zeta_ref,
                      gn_ref, y_ref, state_ref, *, decays):
    @pl.when(pl.program_id(1) == 0)
    def _():
        state_ref[...] = jnp.zeros(state_ref.shape, F32)

    cos = cos_ref[...]
    sin = sin_ref[...]
    half = RET_QK_DIM // 2

    def rot(x):
        x1, x2 = x[:, :half], x[:, half:]
        return jnp.concatenate([x1 * cos - x2 * sin, x1 * sin + x2 * cos], axis=1)

    for h in range(RET_HEADS):
        qs = slice(h * RET_QK_DIM, (h + 1) * RET_QK_DIM)
        vs = slice(h * RET_V_DIM, (h + 1) * RET_V_DIM)
        qr = rot(q_ref[0, :, qs].astype(F32))
        kr = rot(k_ref[0, :, qs].astype(F32)) * (RET_QK_DIM ** -0.5)
        qb = qr.astype(BF16)
        vh = v_ref[0, :, vs]
        attn = lax.dot_general(qb, kr.astype(BF16), _NT, preferred_element_type=F32) * inner_ref[h]
        st = state_ref[h]
        o = (jnp.dot(attn.astype(BF16), vh, preferred_element_type=F32)
             + jnp.dot(qb, st.astype(BF16), preferred_element_type=F32) * xi_ref[h])
        kz = (kr * zeta_ref[h]).astype(BF16)
        state_ref[h] = st * decays[h] + lax.dot_general(kz, vh, _TN, preferred_element_type=F32)
        mu = jnp.mean(o, axis=1, keepdims=True)
        d = o - mu
        var = jnp.mean(d * d, axis=1, keepdims=True)
        on = d * lax.rsqrt(var + GN_EPS) * gn_ref[h]
        y_ref[0, :, vs] = (on * _silu(z_ref[0, :, vs].astype(F32))).astype(y_ref.dtype)


def _retention(proj, gn_gain, tables):
    b = proj.shape[0]
    c = RET_CHUNK
    cos, sin, inner, xi, zeta, decays = tables
    v_blk = 2 * RET_QK_WIDTH // RET_WIDTH
    return pl.pallas_call(
        functools.partial(_retention_kernel, decays=decays),
        grid=(b, SEQ // c),
        in_specs=[
            pl.BlockSpec((1, c, RET_QK_WIDTH), lambda i, j: (i, j, 0)),
            pl.BlockSpec((1, c, RET_QK_WIDTH), lambda i, j: (i, j, 1)),
            pl.BlockSpec((1, c, RET_WIDTH), lambda i, j: (i, j, v_blk)),
            pl.BlockSpec((1, c, RET_WIDTH), lambda i, j: (i, j, v_blk + 1)),
            pl.BlockSpec((c, RET_QK_DIM // 2), lambda i, j: (j, 0)),
            pl.BlockSpec((c, RET_QK_DIM // 2), lambda i, j: (j, 0)),
            pl.BlockSpec((RET_HEADS, c, c), lambda i, j: (0, 0, 0)),
            pl.BlockSpec((RET_HEADS, c, 1), lambda i, j: (0, 0, 0)),
            pl.BlockSpec((RET_HEADS, c, 1), lambda i, j: (0, 0, 0)),
            pl.BlockSpec((RET_HEADS, 1, RET_V_DIM), lambda i, j: (0, 0, 0)),
        ],
        out_specs=pl.BlockSpec((1, c, RET_WIDTH), lambda i, j: (i, j, 0)),
        out_shape=jax.ShapeDtypeStruct((b, SEQ, RET_WIDTH), BF16),
        scratch_shapes=[pltpu.VMEM((RET_HEADS, RET_QK_DIM, RET_V_DIM), F32)],
        compiler_params=_cparams(("parallel", "arbitrary")),
        name="retention",
    )(proj, proj, proj, proj, cos, sin, inner, xi, zeta, gn_gain.reshape(RET_HEADS, 1, RET_V_DIM))


def _t5_bucket_np(dist):
    n = np.maximum(dist, 0)
    max_exact = REL_BUCKETS // 2
    nf = np.maximum(n, 1).astype(np.float64)
    large = max_exact + (np.log(nf / max_exact) / math.log(REL_MAX_DIST / max_exact)
                         * (REL_BUCKETS - max_exact)).astype(np.int64)
    large = np.minimum(large, REL_BUCKETS - 1)
    return np.where(n < max_exact, n, large).astype(np.int32)


def _skewed(vec, rows, stride, cols):
    p = vec.shape[-1]
    lead = vec.shape[:-1]
    flat = jnp.tile(vec, (1,) * len(lead) + (rows,))[..., :rows * (p - stride)]
    return flat.reshape(lead + (rows, p - stride))[..., :cols]


def _nsa_tables(table):
    tq, tk = ATT_TQ, ATT_TK
    assert tq == tk
    tab_t = table.T * LOG2E
    period = 2 * SEQ
    x = np.arange(period)
    ok = (x >= CMP_BLOCK - 1) & (x < SEQ)
    vec = jnp.take(tab_t, jnp.asarray(_t5_bucket_np(x - (CMP_BLOCK - 1))), axis=1)
    vec = jnp.where(jnp.asarray(ok)[None, :], vec, MASKED)
    bias_cmp = _skewed(vec, N_CMP_PAD, CMP_STRIDE, SEQ)
    rel_t = tab_t - tab_t[:, REL_BUCKETS - 1:]
    period = 2 * tq
    x = np.arange(period)
    x = np.where(x < tq, x, x - period)
    tiles = []
    for off in range(3):
        d = off * tk - x
        ok = (d >= 0) & (d < WIN_SIZE) if off == 2 else (d >= 0)
        vec = jnp.take(rel_t, jnp.asarray(_t5_bucket_np(d)), axis=1)
        vec = jnp.where(jnp.asarray(ok)[None, :], vec, MASKED)
        tiles.append(_skewed(vec, tq, 1, tk))
    a_tiles = jnp.stack(tiles + [jnp.zeros_like(tiles[0])])
    cs = np.arange(N_CMP_PAD)[None, :] * CMP_STRIDE
    jj = np.arange(N_SLC)[:, None]
    ovl = ((cs < (jj + 1) * SLC_BLOCK) & (cs + CMP_BLOCK > jj * SLC_BLOCK)
           & (np.arange(N_CMP_PAD)[None, :] < N_CMP))
    ovl = jnp.asarray(ovl.astype(np.float32), BF16)
    lane = np.arange(LANES).reshape(1, 1, 1, LANES)
    key = np.arange(SEQ).reshape(1, SEQ // tk, tk, 1)
    base = np.array([HEAD_DIM, 0]).reshape(2, 1, 1, 1)
    key_blocks = jnp.asarray((lane == base + key // SLC_BLOCK).astype(np.float32), BF16)
    col = np.arange(2 * LANES).reshape(1, -1)
    jb = np.arange(N_SLC).reshape(-1, 1)
    place = (col == HEAD_DIM + jb) | (col == LANES + jb)
    row = np.arange(2 * LANES).reshape(-1, 1)
    out = np.arange(LANES).reshape(1, -1)
    den_spread = ((row == LANES - 1) & (out < HEAD_DIM)) | ((row == LANES) & (out >= HEAD_DIM))
    sel_consts = (key_blocks, jnp.asarray(place.astype(np.float32), BF16),
                  jnp.asarray(place.any(axis=0, keepdims=True).astype(np.float32)),
                  jnp.asarray(den_spread.astype(np.float32), BF16))
    c = np.arange(LANES).reshape(1, 1, LANES, 1)
    col = np.arange(HEADS_PER_GROUP * HEAD_DIM).reshape(1, 1, 1, -1) // HEAD_DIM
    br = np.arange(3).reshape(3, 1, 1, 1)
    gg = np.arange(GROUPS).reshape(1, GROUPS, 1, 1)
    gate_expand = jnp.asarray((c == br * HEADS + HEADS_PER_GROUP * gg + col).astype(np.float32),
                              BF16)
    return bias_cmp, a_tiles, ovl, sel_consts, _lane_consts(max(tq, CMP_TQ)), gate_expand


def _retention_tables():
    c = RET_CHUNK
    log_g = jnp.log(1.0 - 2.0 ** (-5.0 - jnp.arange(RET_HEADS, dtype=F32)))
    i = jnp.arange(c, dtype=F32)
    diff = i[:, None] - i[None, :]
    inner = jnp.where(diff >= 0, jnp.exp(diff[None] * log_g[:, None, None]), 0.0)
    xi = jnp.exp((i + 1.0)[None, :] * log_g[:, None])[:, :, None]
    zeta = jnp.exp((c - 1.0 - i)[None, :] * log_g[:, None])[:, :, None]
    decays = tuple(float((1.0 - 2.0 ** (-5.0 - h)) ** c) for h in range(RET_HEADS))
    inv = 1.0 / (ROPE_BASE ** jnp.linspace(0.0, 1.0, RET_QK_DIM // 2, dtype=F32))
    ang = jnp.arange(SEQ, dtype=F32)[:, None] * inv[None, :]
    return jnp.cos(ang), jnp.sin(ang), inner, xi, zeta, decays


def _nsa_w_in_layout(w):
    d = w.shape[0]
    q = w[:, :NSA_WIDTH] * (HEAD_DIM ** -0.5 * LOG2E)
    kv0 = NSA_WIDTH
    g0 = kv0 + 6 * KV_WIDTH
    z0 = g0 + 3 * HEADS
    slab = lambda n: w[:, kv0 + n * KV_WIDTH: kv0 + (n + 1) * KV_WIDTH]
    cols = [q, w[:, z0:z0 + 3 * NSA_WIDTH]]
    for n in (2, 4, 3, 5):
        for g in range(GROUPS):
            part = slab(n)[:, g * HEAD_DIM:(g + 1) * HEAD_DIM]
            cols += [part, part]
    cols += [slab(0), slab(1), w[:, g0:z0]]
    used = NSA_G_COL + 3 * HEADS
    cols.append(jnp.zeros((d, NSA_PROJ_PAD - used), w.dtype))
    return jnp.concatenate(cols, axis=1).astype(BF16)


def _nsa_layer(h2d, b, pre_gain, post_gain, w_in, w_out, k_pos, k_w1, k_w2, v_pos, v_w1, v_w2, tabs):
    bias_cmp, a_tiles, ovl, sel_consts, lane_consts, gate_expand = tabs
    proj = _norm_proj(h2d, pre_gain, _nsa_w_in_layout(w_in), NSA_PROJ_TN)
    proj = proj.reshape(b, SEQ, NSA_PROJ_PAD)
    ckv = proj[:, :, NSA_CKV_COL:NSA_CKV_COL + 2 * KV_WIDTH]
    ckv = ckv.reshape(b, N_CMP_PAD, CMP_STRIDE, 2, GROUPS, HEAD_DIM).transpose(0, 3, 4, 1, 2, 5)
    ckv = ckv.reshape(b, 2, GROUPS, N_CMP_PAD, CMP_STRIDE * HEAD_DIM)
    pos = jnp.stack([k_pos, v_pos]).reshape(2, 2, CMP_STRIDE * HEAD_DIM)
    w1 = jnp.stack([k_w1, v_w1]).astype(BF16)
    w2 = jnp.stack([k_w2, v_w2])
    w2d = jnp.concatenate([w2, w2], axis=2).astype(BF16)
    ckv_c = _compress(ckv, pos, w1, w2d)
    y_cmp, sel = _cmp_select(proj, ckv_c, bias_cmp, ovl, lane_consts, gate_expand[0])
    y_tok = _token_attention(proj, a_tiles, lane_consts, gate_expand[1:], sel, sel_consts)
    parts = [y.reshape(b * SEQ, NSA_WIDTH) for y in (y_cmp, y_tok)]
    return _out_post(parts, w_out.astype(BF16), h2d, post_gain)


def _ret_layer(h2d, b, pre_gain, post_gain, w_in, w_out, gn_gain, tabs):
    proj = _norm_proj(h2d, pre_gain, w_in.astype(BF16), RET_PROJ_TN).reshape(b, SEQ, -1)
    y = _retention(proj, gn_gain, tabs)
    return _out_post([y.reshape(b * SEQ, RET_WIDTH)], w_out.astype(BF16), h2d, post_gain)


def kernel(x, pre_norm_gain, post_norm_gain, rel_bias_table, nsa_w_in, nsa_w_out, nsa_cmp_k_pos, nsa_cmp_k_w1, nsa_cmp_k_w2, nsa_cmp_v_pos, nsa_cmp_v_w1, nsa_cmp_v_w2, ret_w_in, ret_w_out, ret_gn_gain):
    b, s, d = x.shape
    assert s == SEQ and d == D_MODEL
    nsa_tabs = _nsa_tables(rel_bias_table)
    ret_tabs = _retention_tables()
    h = x.reshape(b * s, d)
    for layer in range(DEPTH):
        slot = layer // 2
        if layer % 2 == 0:
            h = _nsa_layer(h, b, pre_norm_gain[layer], post_norm_gain[layer], nsa_w_in[slot],
                           nsa_w_out[slot], nsa_cmp_k_pos[slot], nsa_cmp_k_w1[slot],
                           nsa_cmp_k_w2[slot], nsa_cmp_v_pos[slot], nsa_cmp_v_w1[slot],
                           nsa_cmp_v_w2[slot], nsa_tabs)
        else:
            h = _ret_layer(h, b, pre_norm_gain[layer], post_norm_gain[layer], ret_w_in[slot],
                           ret_w_out[slot], ret_gn_gain[slot], ret_tabs)
    return h.reshape(b, s, d)
```

```python
import functools
import math

import numpy as np
import jax
import jax.numpy as jnp
from jax import lax
from jax.experimental import pallas as pl
from jax.experimental.pallas import tpu as pltpu

F32 = jnp.float32
BF16 = jnp.bfloat16

D_MODEL = 1024
SEQ = 2048
DEPTH = 4
RMS_EPS = 1e-6
GN_EPS = 1e-6
MASKED = -1e30
LOG2E = math.log2(math.e)

HEADS = 16
HEAD_DIM = 64
GROUPS = 4
HEADS_PER_GROUP = HEADS // GROUPS
NSA_WIDTH = HEADS * HEAD_DIM
KV_WIDTH = GROUPS * HEAD_DIM
CMP_BLOCK = 32
CMP_STRIDE = 16
CMP_HIDDEN = 256
N_CMP = (SEQ - CMP_BLOCK) // CMP_STRIDE + 1
N_CMP_PAD = 128
SLC_BLOCK = 64
N_SLC = SEQ // SLC_BLOCK
SLC_TOPN = 16
WIN_SIZE = 512
FORCED_SCORE = 1e3
REL_BUCKETS = 32
REL_MAX_DIST = 128

RET_HEADS = 4
RET_QK_DIM = 256
RET_V_DIM = 512
RET_QK_WIDTH = RET_HEADS * RET_QK_DIM
RET_WIDTH = RET_HEADS * RET_V_DIM
ROPE_BASE = 10000.0

LANES = 128
VMEM_LIMIT_BYTES = 56 * 1024 * 1024

PROJ_TM = 1024
RET_PROJ_TN = 2048
NSA_PROJ_TN = 3456
POST_TM = 512
ATT_TQ = 256
ATT_TK = 256
CMP_TQ = 1024
RET_CHUNK = 256

NSA_Q_COL = 0
NSA_Z_COL = 1024
NSA_K_COL = 4096
NSA_V_COL = 5120
NSA_CKV_COL = 6144
NSA_G_COL = 6656
NSA_PROJ_PAD = 6912

_NT = (((1,), (1,)), ((), ()))
_TN = (((0,), (0,)), ((), ()))


def _cparams(sem):
    return pltpu.CompilerParams(dimension_semantics=sem, vmem_limit_bytes=VMEM_LIMIT_BYTES)


def _norm_proj_kernel(x_ref, g_ref, w_ref, o_ref, xn_ref):
    @pl.when(pl.program_id(1) == 0)
    def _():
        x = x_ref[...]
        ms = jnp.mean(x * x, axis=-1, keepdims=True)
        xn_ref[...] = (x * lax.rsqrt(ms + RMS_EPS) * g_ref[...]).astype(BF16)

    o_ref[...] = jnp.dot(xn_ref[...], w_ref[...], preferred_element_type=F32).astype(o_ref.dtype)


def _norm_proj(x2d, gain, w_bf16, tn):
    m, d = x2d.shape
    n = w_bf16.shape[1]
    assert n % tn == 0
    return pl.pallas_call(
        _norm_proj_kernel,
        grid=(m // PROJ_TM, n // tn),
        in_specs=[
            pl.BlockSpec((PROJ_TM, d), lambda i, j: (i, 0)),
            pl.BlockSpec((1, d), lambda i, j: (0, 0)),
            pl.BlockSpec((d, tn), lambda i, j: (0, j)),
        ],
        out_specs=pl.BlockSpec((PROJ_TM, tn), lambda i, j: (i, j)),
        out_shape=jax.ShapeDtypeStruct((m, n), BF16),
        scratch_shapes=[pltpu.VMEM((PROJ_TM, d), BF16)],
        compiler_params=_cparams(("parallel", "arbitrary")),
        name="norm_proj",
    )(x2d, gain.reshape(1, d), w_bf16)


def _out_post_kernel(*refs, n_parts):
    y_refs = refs[:n_parts]
    w_ref, h_ref, g_ref, o_ref = refs[n_parts:]
    y = y_refs[0][...].astype(F32)
    for r in y_refs[1:]:
        y = y + r[...].astype(F32)
    t = jnp.dot(y.astype(BF16), w_ref[...], preferred_element_type=F32)
    ms = jnp.mean(t * t, axis=-1, keepdims=True)
    o_ref[...] = h_ref[...] + t * lax.rsqrt(ms + RMS_EPS) * g_ref[...]


def _out_post(parts, w_bf16, h2d, gain):
    m, d = h2d.shape
    k = w_bf16.shape[0]
    n_parts = len(parts)
    return pl.pallas_call(
        functools.partial(_out_post_kernel, n_parts=n_parts),
        grid=(m // POST_TM,),
        in_specs=[pl.BlockSpec((POST_TM, k), lambda i: (i, 0)) for _ in parts] + [
            pl.BlockSpec((k, d), lambda i: (0, 0)),
            pl.BlockSpec((POST_TM, d), lambda i: (i, 0)),
            pl.BlockSpec((1, d), lambda i: (0, 0)),
        ],
        out_specs=pl.BlockSpec((POST_TM, d), lambda i: (i, 0)),
        out_shape=jax.ShapeDtypeStruct((m, d), F32),
        compiler_params=_cparams(("parallel",)),
        name="out_post",
    )(*parts, w_bf16, h2d, gain.reshape(1, d))


def _compress_kernel(x_ref, pos_ref, w1_ref, w2_ref, o_ref):
    half = CMP_STRIDE * HEAD_DIM
    for which in range(2):
        x = x_ref[0, which, 0].astype(F32)
        xa = (x + pos_ref[which, 0:1, :]).astype(BF16)
        xb = (x + pos_ref[which, 1:2, :]).astype(BF16)
        pa = jnp.dot(xa, w1_ref[which, :half, :], preferred_element_type=F32)
        pb = jnp.dot(xb, w1_ref[which, half:, :], preferred_element_type=F32)
        hid = pa + pltpu.roll(pb, N_CMP_PAD - 1, 0)
        hid = hid * jax.nn.sigmoid(hid)
        o_ref[0, which, 0] = jnp.dot(hid.astype(BF16), w2_ref[which],
                                     preferred_element_type=F32).astype(o_ref.dtype)


def _compress(ckv_rows, pos, w1, w2d):
    b = ckv_rows.shape[0]
    row_w = CMP_STRIDE * HEAD_DIM
    return pl.pallas_call(
        _compress_kernel,
        grid=(b, GROUPS),
        in_specs=[
            pl.BlockSpec((1, 2, 1, N_CMP_PAD, row_w), lambda i, g: (i, 0, g, 0, 0)),
            pl.BlockSpec((2, 2, row_w), lambda i, g: (0, 0, 0)),
            pl.BlockSpec((2, 2 * row_w, CMP_HIDDEN), lambda i, g: (0, 0, 0)),
            pl.BlockSpec((2, CMP_HIDDEN, 2 * HEAD_DIM), lambda i, g: (0, 0, 0)),
        ],
        out_specs=pl.BlockSpec((1, 2, 1, N_CMP_PAD, 2 * HEAD_DIM), lambda i, g: (i, 0, g, 0, 0)),
        out_shape=jax.ShapeDtypeStruct((b, 2, GROUPS, N_CMP_PAD, 2 * HEAD_DIM), BF16),
        compiler_params=_cparams(("parallel", "parallel")),
        name="compress",
    )(ckv_rows, pos, w1, w2d)


def _lane_consts(rows):
    lane = np.arange(LANES)
    lo = (lane < HEAD_DIM).astype(np.float32)
    hi = (lane >= HEAD_DIM).astype(np.float32)
    last = (lane == LANES - 1).astype(np.float32)
    first = (lane == 0).astype(np.float32)
    c = np.stack([lo, hi, 1.0 - last, last, 1.0 - first, first])
    return jnp.asarray(np.broadcast_to(c[:, None, :], (6, rows, LANES)), BF16)


def _bf16_terms(x, n):
    terms = []
    for _ in range(n - 1):
        t = x.astype(BF16)
        terms.append(t)
        x = x - t.astype(F32)
    terms.append(x.astype(BF16))
    return terms


def _gates(gate_logits, expands):
    terms = _bf16_terms(jax.nn.sigmoid(gate_logits.astype(F32)), 2)
    return [sum(jnp.dot(t, ex, preferred_element_type=F32) for t in terms) for ex in expands]


def _silu(x):
    return x * jax.nn.sigmoid(x)


def _cmp_select_kernel(q_ref, gate_ref, z_ref, ckv_ref, bias_ref, ovl_ref, lc_ref, gexp_ref,
                       y_ref, sel_ref):
    g = pl.program_id(0)
    qi = pl.program_id(1)
    tq = q_ref.shape[1]
    q = q_ref[0]
    kk = ckv_ref[0, 0, 0]
    vv = ckv_ref[0, 1, 0]
    lane = lax.broadcasted_iota(jnp.int32, (tq, LANES), 1)

    psum = jnp.zeros((N_CMP_PAD, tq), F32)
    pairs = []
    for a in range(2):
        qp = q[:, a * LANES:(a + 1) * LANES]
        outs = []
        for e in range(2):
            s = lax.dot_general(kk, qp * lc_ref[e, :tq], _NT, preferred_element_type=F32)
            bias = bias_ref[2 * a + e]
            s = s + bias
            m = jnp.max(s, axis=0, keepdims=True)
            p = jnp.exp2(s - m)
            p = jnp.where(bias > 0.5 * MASKED, p / jnp.sum(p, axis=0, keepdims=True), 0.0)
            psum = psum + p
            outs.append(lax.dot_general(p.astype(BF16), vv, _TN, preferred_element_type=F32))
        pairs.append(jnp.where(lane < HEAD_DIM, outs[0], outs[1]))
    y_ref[0] = (jnp.concatenate(pairs, axis=1) * _gates(gate_ref[0], [gexp_ref[g]])[0]
                * _silu(z_ref[0].astype(F32))).astype(y_ref.dtype)

    ovl = ovl_ref[...]
    imp = sum(jnp.dot(ovl, t, preferred_element_type=F32) for t in _bf16_terms(psum, 3))
    t = qi * tq + lax.broadcasted_iota(jnp.int32, (N_SLC, tq), 1)
    blk = lax.broadcasted_iota(jnp.int32, (N_SLC, tq), 0)
    cur = lax.shift_right_logical(t, int(math.log2(SLC_BLOCK)))
    forced = (blk == 0) | (blk == cur) | (blk == cur - 1)
    score = jnp.where(blk * SLC_BLOCK <= t, imp + jnp.where(forced, FORCED_SCORE, 0.0), MASKED)
    cnt = jnp.zeros((N_SLC, tq), jnp.int32)
    for jp in range(N_SLC):
        row = score[jp:jp + 1, :]
        beats = (row > score) | ((row == score) & (blk > jp))
        cnt = cnt + beats.astype(jnp.int32)
    sel_ref[0, 0] = jnp.where(cnt < SLC_TOPN, 1.0, 0.0).astype(sel_ref.dtype)


def _cmp_select(proj, ckv, bias_cmp, ovl, lane_consts, gate_expand):
    b = proj.shape[0]
    tq = CMP_TQ
    grp_w = HEADS_PER_GROUP * HEAD_DIM
    return pl.pallas_call(
        _cmp_select_kernel,
        grid=(GROUPS, SEQ // tq, b),
        in_specs=[
            pl.BlockSpec((1, tq, grp_w), lambda g, qi, i: (i, qi, NSA_Q_COL // grp_w + g)),
            pl.BlockSpec((1, tq, LANES), lambda g, qi, i: (i, qi, NSA_G_COL // LANES)),
            pl.BlockSpec((1, tq, grp_w), lambda g, qi, i: (i, qi, NSA_Z_COL // grp_w + g)),
            pl.BlockSpec((1, 2, 1, N_CMP_PAD, 2 * HEAD_DIM), lambda g, qi, i: (i, 0, g, 0, 0)),
            pl.BlockSpec((HEADS_PER_GROUP, N_CMP_PAD, tq), lambda g, qi, i: (g, 0, qi)),
            pl.BlockSpec((N_SLC, N_CMP_PAD), lambda g, qi, i: (0, 0)),
            pl.BlockSpec(memory_space=pltpu.VMEM),
            pl.BlockSpec(memory_space=pltpu.VMEM),
        ],
        out_specs=[
            pl.BlockSpec((1, tq, grp_w), lambda g, qi, i: (i, qi, g)),
            pl.BlockSpec((1, 1, N_SLC, tq), lambda g, qi, i: (i, g, 0, qi)),
        ],
        out_shape=[
            jax.ShapeDtypeStruct((b, SEQ, NSA_WIDTH), BF16),
            jax.ShapeDtypeStruct((b, GROUPS, N_SLC, SEQ), BF16),
        ],
        compiler_params=_cparams(("parallel", "parallel", "parallel")),
        name="cmp_select",
    )(proj, proj, proj, ckv, bias_cmp, ovl, lane_consts, gate_expand)


def _token_kernel(q_ref, ks_ref, vs_ref, kw_ref, vw_ref, gate_ref, zs_ref, zw_ref, a_ref, lc_ref,
                  gexp_ref, sel_ref, kblk_ref, place_ref, placed_ref, dspread_ref, y_ref,
                  qm_ref, m_ref, acc_ref, s_bufs, p_bufs, c_bufs, kx_ref, vx_ref, gz_ref):
    g = pl.program_id(1)
    tq = ATT_TQ
    tk = ATT_TK

    def stream_len(qi):
        return min(qi, WIN_SIZE // tk) + 1 + qi + 1

    for c in range(SEQ // tk):
        rows = slice(c * tk, (c + 1) * tk)
        for branch, v_ref in enumerate((vs_ref, vw_ref)):
            vv = v_ref[0, rows, :]
            vx_ref[2 * branch, rows, :] = vv * lc_ref[2, :tk] + lc_ref[3, :tk]
            vx_ref[2 * branch + 1, rows, :] = vv * lc_ref[4, :tk] + lc_ref[5, :tk]
        kk = ks_ref[0, rows, :]
        for e in range(2):
            kx_ref[e, c] = (kk * lc_ref[e, :tk] + kblk_ref[e, c]).astype(F32).T.astype(BF16)
        kx_ref[2, c] = kw_ref[0, rows, :].astype(F32).T.astype(BF16)

    def tile(qi, j):
        n_win = min(qi, WIN_SIZE // tk) + 1
        win = int(j < n_win)
        t = j - (1 - win) * n_win
        return win, qi - t, t if (win or t < 2) else None

    def scores(qi, j, s_ref):
        if j >= stream_len(qi):
            return
        win, ki, _ = tile(qi, j)
        for e in range(2):
            pair = slice(2 * e * tq, 2 * (e + 1) * tq)
            s_ref[pair, :] = jnp.dot(qm_ref[win, pair, :], kx_ref[2 if win else e, ki],
                                     preferred_element_type=F32)

    def softmax(qi, j, s_ref, p_ref, c_ref):
        win, _, kind = tile(qi, j)
        for slot in range(HEADS_PER_GROUP):
            e, a = divmod(slot, 2)
            rows = slice(slot * tq, (slot + 1) * tq)
            s = s_ref[rows, :]
            if kind is not None:
                s = s + a_ref[kind, HEADS_PER_GROUP * g + 2 * a + e]
            m_prev = m_ref[win, rows, :]
            m_new = jnp.maximum(m_prev, jnp.max(s, axis=1, keepdims=True))
            c_ref[rows, :] = jnp.exp2(m_prev - m_new)
            x = s - jnp.concatenate([m_new] * (tk // LANES), axis=1)
            p_ref[rows, :] = jnp.exp2(x.astype(BF16))
            m_ref[win, rows, :] = m_new

    def values(qi_, par_, j, p_ref, c_ref):
        win, ki, _ = tile(qi_, j)
        rows = slice(ki * tk, (ki + 1) * tk)
        for e in range(2):
            pair = slice(2 * e * tq, 2 * (e + 1) * tq)
            pv = jnp.dot(p_ref[pair, :], vx_ref[2 * win + e, rows, :], preferred_element_type=F32)
            acc = acc_ref.at[2 * par_ + win]
            acc[pair, :] = c_ref[pair, :] * acc[pair, :] + pv

    lane = lax.broadcasted_iota(jnp.int32, (tq, LANES), 1)

    def gated_output(par_, branch):
        acc = acc_ref.at[2 * par_ + branch]
        pairs = []
        for a in range(2):
            acc_e = acc[a * tq:(a + 1) * tq, :]
            acc_o = acc[(2 + a) * tq:(3 + a) * tq, :]
            both = jnp.concatenate([acc_e, acc_o], axis=1)
            den = sum(jnp.dot(t, dspread_ref[...], preferred_element_type=F32)
                      for t in _bf16_terms(both, 2))
            pairs.append(jnp.where(lane < HEAD_DIM, acc_e, acc_o) / den)
        return jnp.concatenate(pairs, axis=1) * gz_ref[2 * par_ + branch]

    bufs = [(s_bufs.at[i], p_bufs.at[i], c_bufs.at[i]) for i in range(2)]

    def step(qi, half, j, parity):
        cur, nxt = bufs[parity], bufs[1 - parity]
        values(qi, half, j - 1, nxt[1], nxt[2])
        softmax(qi, j, *cur)
        scores(qi, j + 1, nxt[0])

    def enter(qi, slot):
        rows = slice(slot * tq, (slot + 1) * tq)
        half = slot % 2
        q = q_ref[0, rows, :]
        qms = []
        for hs in range(HEADS_PER_GROUP):
            e, a = divmod(hs, 2)
            qms.append(q[:, a * LANES:(a + 1) * LANES] * lc_ref[e, :tq])
            qm_ref[1, hs * tq:(hs + 1) * tq, :] = qms[hs]
        flags = lax.dot_general(sel_ref[0, 0, :, rows], place_ref[...], _TN,
                                preferred_element_type=F32)
        fill = ((placed_ref[...] - flags) * MASKED).astype(BF16)
        for hs in range(HEADS_PER_GROUP):
            e = hs // 2
            qm_ref[0, hs * tq:(hs + 1) * tq, :] = qms[hs] + fill[:, e * LANES:(e + 1) * LANES]
        gates = _gates(gate_ref[0, rows, :], [gexp_ref[0, g], gexp_ref[1, g]])
        for branch, z_ref in enumerate((zs_ref, zw_ref)):
            gz_ref[2 * half + branch] = gates[branch] * _silu(z_ref[0, rows, :].astype(F32))
        m_ref[...] = jnp.full(m_ref.shape, -jnp.inf, F32)
        for branch in range(2):
            acc_ref[2 * half + branch] = jnp.zeros(acc_ref.shape[1:], F32)
        scores(qi, 0, bufs[0][0])

    def fill_pipeline(qi, half):
        softmax(qi, 0, *bufs[0])
        scores(qi, 1, bufs[1][0])
        step(qi, half, 1, 1)

    def run_stream(qi, half):
        @pl.when(g >= 0)
        def _():
            for j in range(2, stream_len(qi)):
                step(qi, half, j, j % 2)

    def leave(qi, half):
        y_win = gated_output(half, 1)
        last_pos = stream_len(qi) - 1
        values(qi, half, last_pos, bufs[last_pos % 2][1], bufs[last_pos % 2][2])
        return y_win

    def write_output(slot, y_win):
        rows = slice(slot * tq, (slot + 1) * tq)
        y_ref[0, rows, :] = (y_win + gated_output(slot % 2, 0)).astype(y_ref.dtype)

    y_win = None
    for qi in range(SEQ // tq):
        enter(qi, qi)
        if qi > 0:
            write_output(qi - 1, y_win)
        fill_pipeline(qi, qi % 2)
        run_stream(qi, qi % 2)
        y_win = leave(qi, qi % 2)
    write_output(SEQ // tq - 1, y_win)


def _token_attention(proj, a_tiles, lane_consts, gate_expand, sel, sel_consts):
    b = proj.shape[0]
    tq, tk = ATT_TQ, ATT_TK
    grp_w = HEADS_PER_GROUP * HEAD_DIM
    k_blk = NSA_K_COL // LANES
    v_blk = NSA_V_COL // LANES
    z_blk = NSA_Z_COL // grp_w
    kv_spec = lambda blk: pl.BlockSpec((1, SEQ, LANES), lambda i, g: (i, 0, blk + g))
    z_spec = lambda blk: pl.BlockSpec((1, SEQ, grp_w), lambda i, g: (i, 0, blk + g))
    whole = pl.BlockSpec(memory_space=pltpu.VMEM)
    return pl.pallas_call(
        _token_kernel,
        grid=(b, GROUPS),
        in_specs=[
            pl.BlockSpec((1, SEQ, grp_w), lambda i, g: (i, 0, NSA_Q_COL // grp_w + g)),
            kv_spec(k_blk), kv_spec(v_blk), kv_spec(k_blk + GROUPS), kv_spec(v_blk + GROUPS),
            pl.BlockSpec((1, SEQ, LANES), lambda i, g: (i, 0, NSA_G_COL // LANES)),
            z_spec(z_blk + GROUPS), z_spec(z_blk + 2 * GROUPS),
            whole, whole, whole,
            pl.BlockSpec((1, 1, N_SLC, SEQ), lambda i, g: (i, g, 0, 0)),
            whole, whole, whole, whole,
        ],
        out_specs=pl.BlockSpec((1, SEQ, grp_w), lambda i, g: (i, 0, g)),
        out_shape=jax.ShapeDtypeStruct((b, SEQ, NSA_WIDTH), BF16),
        scratch_shapes=[
            pltpu.VMEM((2, HEADS_PER_GROUP * tq, LANES), BF16),
            pltpu.VMEM((2, HEADS_PER_GROUP * tq, LANES), F32),
            pltpu.VMEM((4, HEADS_PER_GROUP * tq, LANES), F32),
            pltpu.VMEM((2, HEADS_PER_GROUP * tq, tk), F32),
            pltpu.VMEM((2, HEADS_PER_GROUP * tq, tk), BF16),
            pltpu.VMEM((2, HEADS_PER_GROUP * tq, LANES), F32),
            pltpu.VMEM((3, SEQ // tk, LANES, tk), BF16),
            pltpu.VMEM((4, SEQ, LANES), BF16),
            pltpu.VMEM((4, tq, grp_w), F32),
        ],
        compiler_params=_cparams(("parallel", "parallel")),
        name="token_attention",
    )(proj, proj, proj, proj, proj, proj, proj, proj, a_tiles, lane_consts, gate_expand,
      sel, *sel_consts)


def _retention_kernel(q_ref, k_ref, v_ref, z_ref, cos_ref, sin_ref, inner_ref, xi_ref, zeta_ref,
                      gn_ref, y_ref, state_ref, *, decays):
    @pl.when(pl.program_id(1) == 0)
    def _():
        state_ref[...] = jnp.zeros(state_ref.shape, F32)

    cos = cos_ref[...]
    sin = sin_ref[...]
    half = RET_QK_DIM // 2

    def rot(x):
        x1, x2 = x[:, :half], x[:, half:]
        return jnp.concatenate([x1 * cos - x2 * sin, x1 * sin + x2 * cos], axis=1)

    for h in range(RET_HEADS):
        qs = slice(h * RET_QK_DIM, (h + 1) * RET_QK_DIM)
        vs = slice(h * RET_V_DIM, (h + 1) * RET_V_DIM)
        qr = rot(q_ref[0, :, qs].astype(F32))
        kr = rot(k_ref[0, :, qs].astype(F32)) * (RET_QK_DIM ** -0.5)
        qb = qr.astype(BF16)
        vh = v_ref[0, :, vs]
        attn = lax.dot_general(qb, kr.astype(BF16), _NT, preferred_element_type=F32) * inner_ref[h]
        st = state_ref[h]
        o = (jnp.dot(attn.astype(BF16), vh, preferred_element_type=F32)
             + jnp.dot(qb, st.astype(BF16), preferred_element_type=F32) * xi_ref[h])
        kz = (kr * zeta_ref[h]).astype(BF16)
        state_ref[h] = st * decays[h] + lax.dot_general(kz, vh, _TN, preferred_element_type=F32)
        mu = jnp.mean(o, axis=1, keepdims=True)
        d = o - mu
        var = jnp.mean(d * d, axis=1, keepdims=True)
        on = d * lax.rsqrt(var + GN_EPS) * gn_ref[h]
        y_ref[0, :, vs] = (on * _silu(z_ref[0, :, vs].astype(F32))).astype(y_ref.dtype)


def _retention(proj, gn_gain, tables):
    b = proj.shape[0]
    c = RET_CHUNK
    cos, sin, inner, xi, zeta, decays = tables
    v_blk = 2 * RET_QK_WIDTH // RET_WIDTH
    return pl.pallas_call(
        functools.partial(_retention_kernel, decays=decays),
        grid=(b, SEQ // c),
        in_specs=[
            pl.BlockSpec((1, c, RET_QK_WIDTH), lambda i, j: (i, j, 0)),
            pl.BlockSpec((1, c, RET_QK_WIDTH), lambda i, j: (i, j, 1)),
            pl.BlockSpec((1, c, RET_WIDTH), lambda i, j: (i, j, v_blk)),
            pl.BlockSpec((1, c, RET_WIDTH), lambda i, j: (i, j, v_blk + 1)),
            pl.BlockSpec((c, RET_QK_DIM // 2), lambda i, j: (j, 0)),
            pl.BlockSpec((c, RET_QK_DIM // 2), lambda i, j: (j, 0)),
            pl.BlockSpec((RET_HEADS, c, c), lambda i, j: (0, 0, 0)),
            pl.BlockSpec((RET_HEADS, c, 1), lambda i, j: (0, 0, 0)),
            pl.BlockSpec((RET_HEADS, c, 1), lambda i, j: (0, 0, 0)),
            pl.BlockSpec((RET_HEADS, 1, RET_V_DIM), lambda i, j: (0, 0, 0)),
        ],
        out_specs=pl.BlockSpec((1, c, RET_WIDTH), lambda i, j: (i, j, 0)),
        out_shape=jax.ShapeDtypeStruct((b, SEQ, RET_WIDTH), BF16),
        scratch_shapes=[pltpu.VMEM((RET_HEADS, RET_QK_DIM, RET_V_DIM), F32)],
        compiler_params=_cparams(("parallel", "arbitrary")),
        name="retention",
    )(proj, proj, proj, proj, cos, sin, inner, xi, zeta, gn_gain.reshape(RET_HEADS, 1, RET_V_DIM))


def _t5_bucket_np(dist):
    n = np.maximum(dist, 0)
    max_exact = REL_BUCKETS // 2
    nf = np.maximum(n, 1).astype(np.float64)
    large = max_exact + (np.log(nf / max_exact) / math.log(REL_MAX_DIST / max_exact)
                         * (REL_BUCKETS - max_exact)).astype(np.int64)
    large = np.minimum(large, REL_BUCKETS - 1)
    return np.where(n < max_exact, n, large).astype(np.int32)


def _skewed(vec, rows, stride, cols):
    p = vec.shape[-1]
    lead = vec.shape[:-1]
    flat = jnp.tile(vec, (1,) * len(lead) + (rows,))[..., :rows * (p - stride)]
    return flat.reshape(lead + (rows, p - stride))[..., :cols]


def _nsa_tables(table):
    tq, tk = ATT_TQ, ATT_TK
    assert tq == tk
    tab_t = table.T * LOG2E
    period = 2 * SEQ
    x = np.arange(period)
    ok = (x >= CMP_BLOCK - 1) & (x < SEQ)
    vec = jnp.take(tab_t, jnp.asarray(_t5_bucket_np(x - (CMP_BLOCK - 1))), axis=1)
    vec = jnp.where(jnp.asarray(ok)[None, :], vec, MASKED)
    bias_cmp = _skewed(vec, N_CMP_PAD, CMP_STRIDE, SEQ)
    rel_t = tab_t - tab_t[:, REL_BUCKETS - 1:]
    period = 2 * tq
    x = np.arange(period)
    x = np.where(x < tq, x, x - period)
    tiles = []
    for off in range(3):
        d = off * tk - x
        ok = (d >= 0) & (d < WIN_SIZE) if off == 2 else (d >= 0)
        vec = jnp.take(rel_t, jnp.asarray(_t5_bucket_np(d)), axis=1)
        vec = jnp.where(jnp.asarray(ok)[None, :], vec, MASKED)
        tiles.append(_skewed(vec, tq, 1, tk))
    a_tiles = jnp.stack(tiles + [jnp.zeros_like(tiles[0])])
    cs = np.arange(N_CMP_PAD)[None, :] * CMP_STRIDE
    jj = np.arange(N_SLC)[:, None]
    ovl = ((cs < (jj + 1) * SLC_BLOCK) & (cs + CMP_BLOCK > jj * SLC_BLOCK)
           & (np.arange(N_CMP_PAD)[None, :] < N_CMP))
    ovl = jnp.asarray(ovl.astype(np.float32), BF16)
    lane = np.arange(LANES).reshape(1, 1, 1, LANES)
    key = np.arange(SEQ).reshape(1, SEQ // tk, tk, 1)
    base = np.array([HEAD_DIM, 0]).reshape(2, 1, 1, 1)
    key_blocks = jnp.asarray((lane == base + key // SLC_BLOCK).astype(np.float32), BF16)
    col = np.arange(2 * LANES).reshape(1, -1)
    jb = np.arange(N_SLC).reshape(-1, 1)
    place = (col == HEAD_DIM + jb) | (col == LANES + jb)
    row = np.arange(2 * LANES).reshape(-1, 1)
    out = np.arange(LANES).reshape(1, -1)
    den_spread = ((row == LANES - 1) & (out < HEAD_DIM)) | ((row == LANES) & (out >= HEAD_DIM))
    sel_consts = (key_blocks, jnp.asarray(place.astype(np.float32), BF16),
                  jnp.asarray(place.any(axis=0, keepdims=True).astype(np.float32)),
                  jnp.asarray(den_spread.astype(np.float32), BF16))
    c = np.arange(LANES).reshape(1, 1, LANES, 1)
    col = np.arange(HEADS_PER_GROUP * HEAD_DIM).reshape(1, 1, 1, -1) // HEAD_DIM
    br = np.arange(3).reshape(3, 1, 1, 1)
    gg = np.arange(GROUPS).reshape(1, GROUPS, 1, 1)
    gate_expand = jnp.asarray((c == br * HEADS + HEADS_PER_GROUP * gg + col).astype(np.float32),
                              BF16)
    return bias_cmp, a_tiles, ovl, sel_consts, _lane_consts(max(tq, CMP_TQ)), gate_expand


def _retention_tables():
    c = RET_CHUNK
    log_g = jnp.log(1.0 - 2.0 ** (-5.0 - jnp.arange(RET_HEADS, dtype=F32)))
    i = jnp.arange(c, dtype=F32)
    diff = i[:, None] - i[None, :]
    inner = jnp.where(diff >= 0, jnp.exp(diff[None] * log_g[:, None, None]), 0.0)
    xi = jnp.exp((i + 1.0)[None, :] * log_g[:, None])[:, :, None]
    zeta = jnp.exp((c - 1.0 - i)[None, :] * log_g[:, None])[:, :, None]
    decays = tuple(float((1.0 - 2.0 ** (-5.0 - h)) ** c) for h in range(RET_HEADS))
    inv = 1.0 / (ROPE_BASE ** jnp.linspace(0.0, 1.0, RET_QK_DIM // 2, dtype=F32))
    ang = jnp.arange(SEQ, dtype=F32)[:, None] * inv[None, :]
    return jnp.cos(ang), jnp.sin(ang), inner, xi, zeta, decays


def _nsa_w_in_layout(w):
    d = w.shape[0]
    q = w[:, :NSA_WIDTH] * (HEAD_DIM ** -0.5 * LOG2E)
    kv0 = NSA_WIDTH
    g0 = kv0 + 6 * KV_WIDTH
    z0 = g0 + 3 * HEADS
    slab = lambda n: w[:, kv0 + n * KV_WIDTH: kv0 + (n + 1) * KV_WIDTH]
    cols = [q, w[:, z0:z0 + 3 * NSA_WIDTH]]
    for n in (2, 4, 3, 5):
        for g in range(GROUPS):
            part = slab(n)[:, g * HEAD_DIM:(g + 1) * HEAD_DIM]
            cols += [part, part]
    cols += [slab(0), slab(1), w[:, g0:z0]]
    used = NSA_G_COL + 3 * HEADS
    cols.append(jnp.zeros((d, NSA_PROJ_PAD - used), w.dtype))
    return jnp.concatenate(cols, axis=1).astype(BF16)


def _nsa_layer(h2d, b, pre_gain, post_gain, w_in, w_out, k_pos, k_w1, k_w2, v_pos, v_w1, v_w2, tabs):
    bias_cmp, a_tiles, ovl, sel_consts, lane_consts, gate_expand = tabs
    proj = _norm_proj(h2d, pre_gain, _nsa_w_in_layout(w_in), NSA_PROJ_TN)
    proj = proj.reshape(b, SEQ, NSA_PROJ_PAD)
    ckv = proj[:, :, NSA_CKV_COL:NSA_CKV_COL + 2 * KV_WIDTH]
    ckv = ckv.reshape(b, N_CMP_PAD, CMP_STRIDE, 2, GROUPS, HEAD_DIM).transpose(0, 3, 4, 1, 2, 5)
    ckv = ckv.reshape(b, 2, GROUPS, N_CMP_PAD, CMP_STRIDE * HEAD_DIM)
    pos = jnp.stack([k_pos, v_pos]).reshape(2, 2, CMP_STRIDE * HEAD_DIM)
    w1 = jnp.stack([k_w1, v_w1]).astype(BF16)
    w2 = jnp.stack([k_w2, v_w2])
    w2d = jnp.concatenate([w2, w2], axis=2).astype(BF16)
    ckv_c = _compress(ckv, pos, w1, w2d)
    y_cmp, sel = _cmp_select(proj, ckv_c, bias_cmp, ovl, lane_consts, gate_expand[0])
    y_tok = _token_attention(proj, a_tiles, lane_consts, gate_expand[1:], sel, sel_consts)
    parts = [y.reshape(b * SEQ, NSA_WIDTH) for y in (y_cmp, y_tok)]
    return _out_post(parts, w_out.astype(BF16), h2d, post_gain)


def _ret_layer(h2d, b, pre_gain, post_gain, w_in, w_out, gn_gain, tabs):
    proj = _norm_proj(h2d, pre_gain, w_in.astype(BF16), RET_PROJ_TN).reshape(b, SEQ, -1)
    y = _retention(proj, gn_gain, tabs)
    return _out_post([y.reshape(b * SEQ, RET_WIDTH)], w_out.astype(BF16), h2d, post_gain)


def kernel(x, pre_norm_gain, post_norm_gain, rel_bias_table, nsa_w_in, nsa_w_out, nsa_cmp_k_pos, nsa_cmp_k_w1, nsa_cmp_k_w2, nsa_cmp_v_pos, nsa_cmp_v_w1, nsa_cmp_v_w2, ret_w_in, ret_w_out, ret_gn_gain):
    b, s, d = x.shape
    assert s == SEQ and d == D_MODEL
    nsa_tabs = _nsa_tables(rel_bias_table)
    ret_tabs = _retention_tables()
    h = x.reshape(b * s, d)
    for layer in range(DEPTH):
        slot = layer // 2
        if layer % 2 == 0:
            h = _nsa_layer(h, b, pre_norm_gain[layer], post_norm_gain[layer], nsa_w_in[slot],
                           nsa_w_out[slot], nsa_cmp_k_pos[slot], nsa_cmp_k_w1[slot],
                           nsa_cmp_k_w2[slot], nsa_cmp_v_pos[slot], nsa_cmp_v_w1[slot],
                           nsa_cmp_v_w2[slot], nsa_tabs)
        else:
            h = _ret_layer(h, b, pre_norm_gain[layer], post_norm_gain[layer], ret_w_in[slot],
                           ret_w_out[slot], ret_gn_gain[slot], ret_tabs)
    return h.reshape(b, s, d)
```

```python
import functools
import math

import numpy as np
import jax
import jax.numpy as jnp
from jax import lax
from jax.experimental import pallas as pl
from jax.experimental.pallas import tpu as pltpu

F32 = jnp.float32
BF16 = jnp.bfloat16

D_MODEL = 1024
SEQ = 2048
DEPTH = 4
RMS_EPS = 1e-6
GN_EPS = 1e-6
MASKED = -1e30
LOG2E = math.log2(math.e)

HEADS = 16
HEAD_DIM = 64
GROUPS = 4
HEADS_PER_GROUP = HEADS // GROUPS
NSA_WIDTH = HEADS * HEAD_DIM
KV_WIDTH = GROUPS * HEAD_DIM
CMP_BLOCK = 32
CMP_STRIDE = 16
CMP_HIDDEN = 256
N_CMP = (SEQ - CMP_BLOCK) // CMP_STRIDE + 1
N_CMP_PAD = 128
SLC_BLOCK = 64
N_SLC = SEQ // SLC_BLOCK
SLC_TOPN = 16
WIN_SIZE = 512
FORCED_SCORE = 1e3
REL_BUCKETS = 32
REL_MAX_DIST = 128

RET_HEADS = 4
RET_QK_DIM = 256
RET_V_DIM = 512
RET_QK_WIDTH = RET_HEADS * RET_QK_DIM
RET_WIDTH = RET_HEADS * RET_V_DIM
ROPE_BASE = 10000.0

LANES = 128
VMEM_LIMIT_BYTES = 56 * 1024 * 1024

PROJ_TM = 1024
RET_PROJ_TN = 2048
NSA_PROJ_TN = 3456
POST_TM = 512
ATT_TQ = 256
ATT_TK = 256
CMP_TQ = 1024
RET_CHUNK = 256

NSA_Q_COL = 0
NSA_Z_COL = 1024
NSA_K_COL = 4096
NSA_V_COL = 5120
NSA_CKV_COL = 6144
NSA_G_COL = 6656
NSA_PROJ_PAD = 6912

_NT = (((1,), (1,)), ((), ()))
_TN = (((0,), (0,)), ((), ()))


def _cparams(sem):
    return pltpu.CompilerParams(dimension_semantics=sem, vmem_limit_bytes=VMEM_LIMIT_BYTES)


def _norm_proj_kernel(x_ref, g_ref, w_ref, o_ref, xn_ref):
    @pl.when(pl.program_id(1) == 0)
    def _():
        x = x_ref[...]
        ms = jnp.mean(x * x, axis=-1, keepdims=True)
        xn_ref[...] = (x * lax.rsqrt(ms + RMS_EPS) * g_ref[...]).astype(BF16)

    o_ref[...] = jnp.dot(xn_ref[...], w_ref[...], preferred_element_type=F32).astype(o_ref.dtype)


def _norm_proj(x2d, gain, w_bf16, tn):
    m, d = x2d.shape
    n = w_bf16.shape[1]
    assert n % tn == 0
    return pl.pallas_call(
        _norm_proj_kernel,
        grid=(m // PROJ_TM, n // tn),
        in_specs=[
            pl.BlockSpec((PROJ_TM, d), lambda i, j: (i, 0)),
            pl.BlockSpec((1, d), lambda i, j: (0, 0)),
            pl.BlockSpec((d, tn), lambda i, j: (0, j)),
        ],
        out_specs=pl.BlockSpec((PROJ_TM, tn), lambda i, j: (i, j)),
        out_shape=jax.ShapeDtypeStruct((m, n), BF16),
        scratch_shapes=[pltpu.VMEM((PROJ_TM, d), BF16)],
        compiler_params=_cparams(("parallel", "arbitrary")),
        name="norm_proj",
    )(x2d, gain.reshape(1, d), w_bf16)


def _out_post_kernel(*refs, n_parts):
    y_refs = refs[:n_parts]
    w_ref, h_ref, g_ref, o_ref = refs[n_parts:]
    y = y_refs[0][...].astype(F32)
    for r in y_refs[1:]:
        y = y + r[...].astype(F32)
    t = jnp.dot(y.astype(BF16), w_ref[...], preferred_element_type=F32)
    ms = jnp.mean(t * t, axis=-1, keepdims=True)
    o_ref[...] = h_ref[...] + t * lax.rsqrt(ms + RMS_EPS) * g_ref[...]


def _out_post(parts, w_bf16, h2d, gain):
    m, d = h2d.shape
    k = w_bf16.shape[0]
    n_parts = len(parts)
    return pl.pallas_call(
        functools.partial(_out_post_kernel, n_parts=n_parts),
        grid=(m // POST_TM,),
        in_specs=[pl.BlockSpec((POST_TM, k), lambda i: (i, 0)) for _ in parts] + [
            pl.BlockSpec((k, d), lambda i: (0, 0)),
            pl.BlockSpec((POST_TM, d), lambda i: (i, 0)),
            pl.BlockSpec((1, d), lambda i: (0, 0)),
        ],
        out_specs=pl.BlockSpec((POST_TM, d), lambda i: (i, 0)),
        out_shape=jax.ShapeDtypeStruct((m, d), F32),
        compiler_params=_cparams(("parallel",)),
        name="out_post",
    )(*parts, w_bf16, h2d, gain.reshape(1, d))


def _compress_kernel(x_ref, pos_ref, w1_ref, w2_ref, o_ref):
    half = CMP_STRIDE * HEAD_DIM
    for which in range(2):
        x = x_ref[0, which, 0].astype(F32)
        xa = (x + pos_ref[which, 0:1, :]).astype(BF16)
        xb = (x + pos_ref[which, 1:2, :]).astype(BF16)
        pa = jnp.dot(xa, w1_ref[which, :half, :], preferred_element_type=F32)
        pb = jnp.dot(xb, w1_ref[which, half:, :], preferred_element_type=F32)
        hid = pa + pltpu.roll(pb, N_CMP_PAD - 1, 0)
        hid = hid * jax.nn.sigmoid(hid)
        o_ref[0, which, 0] = jnp.dot(hid.astype(BF16), w2_ref[which],
                                     preferred_element_type=F32).astype(o_ref.dtype)


def _compress(ckv_rows, pos, w1, w2d):
    b = ckv_rows.shape[0]
    row_w = CMP_STRIDE * HEAD_DIM
    return pl.pallas_call(
        _compress_kernel,
        grid=(b, GROUPS),
        in_specs=[
            pl.BlockSpec((1, 2, 1, N_CMP_PAD, row_w), lambda i, g: (i, 0, g, 0, 0)),
            pl.BlockSpec((2, 2, row_w), lambda i, g: (0, 0, 0)),
            pl.BlockSpec((2, 2 * row_w, CMP_HIDDEN), lambda i, g: (0, 0, 0)),
            pl.BlockSpec((2, CMP_HIDDEN, 2 * HEAD_DIM), lambda i, g: (0, 0, 0)),
        ],
        out_specs=pl.BlockSpec((1, 2, 1, N_CMP_PAD, 2 * HEAD_DIM), lambda i, g: (i, 0, g, 0, 0)),
        out_shape=jax.ShapeDtypeStruct((b, 2, GROUPS, N_CMP_PAD, 2 * HEAD_DIM), BF16),
        compiler_params=_cparams(("parallel", "parallel")),
        name="compress",
    )(ckv_rows, pos, w1, w2d)


def _lane_consts(rows):
    lane = np.arange(LANES)
    lo = (lane < HEAD_DIM).astype(np.float32)
    hi = (lane >= HEAD_DIM).astype(np.float32)
    last = (lane == LANES - 1).astype(np.float32)
    first = (lane == 0).astype(np.float32)
    c = np.stack([lo, hi, 1.0 - last, last, 1.0 - first, first])
    return jnp.asarray(np.broadcast_to(c[:, None, :], (6, rows, LANES)), BF16)


def _bf16_terms(x, n):
    terms = []
    for _ in range(n - 1):
        t = x.astype(BF16)
        terms.append(t)
        x = x - t.astype(F32)
    terms.append(x.astype(BF16))
    return terms


def _gates(gate_logits, expands):
    terms = _bf16_terms(jax.nn.sigmoid(gate_logits.astype(F32)), 2)
    return [sum(jnp.dot(t, ex, preferred_element_type=F32) for t in terms) for ex in expands]


def _silu(x):
    return x * jax.nn.sigmoid(x)


def _cmp_select_kernel(q_ref, gate_ref, z_ref, ckv_ref, bias_ref, ovl_ref, lc_ref, gexp_ref,
                       y_ref, sel_ref):
    g = pl.program_id(0)
    qi = pl.program_id(1)
    tq = q_ref.shape[1]
    q = q_ref[0]
    kk = ckv_ref[0, 0, 0]
    vv = ckv_ref[0, 1, 0]
    lane = lax.broadcasted_iota(jnp.int32, (tq, LANES), 1)

    psum = jnp.zeros((N_CMP_PAD, tq), F32)
    pairs = []
    for a in range(2):
        qp = q[:, a * LANES:(a + 1) * LANES]
        outs = []
        for e in range(2):
            s = lax.dot_general(kk, qp * lc_ref[e, :tq], _NT, preferred_element_type=F32)
            bias = bias_ref[2 * a + e]
            s = s + bias
            m = jnp.max(s, axis=0, keepdims=True)
            p = jnp.exp2(s - m)
            p = jnp.where(bias > 0.5 * MASKED, p / jnp.sum(p, axis=0, keepdims=True), 0.0)
            psum = psum + p
            outs.append(lax.dot_general(p.astype(BF16), vv, _TN, preferred_element_type=F32))
        pairs.append(jnp.where(lane < HEAD_DIM, outs[0], outs[1]))
    y_ref[0] = (jnp.concatenate(pairs, axis=1) * _gates(gate_ref[0], [gexp_ref[g]])[0]
                * _silu(z_ref[0].astype(F32))).astype(y_ref.dtype)

    ovl = ovl_ref[...]
    imp = sum(jnp.dot(ovl, t, preferred_element_type=F32) for t in _bf16_terms(psum, 3))
    t = qi * tq + lax.broadcasted_iota(jnp.int32, (N_SLC, tq), 1)
    blk = lax.broadcasted_iota(jnp.int32, (N_SLC, tq), 0)
    cur = lax.shift_right_logical(t, int(math.log2(SLC_BLOCK)))
    forced = (blk == 0) | (blk == cur) | (blk == cur - 1)
    score = jnp.where(blk * SLC_BLOCK <= t, imp + jnp.where(forced, FORCED_SCORE, 0.0), MASKED)
    cnt = jnp.zeros((N_SLC, tq), jnp.int32)
    for jp in range(N_SLC):
        row = score[jp:jp + 1, :]
        beats = (row > score) | ((row == score) & (blk > jp))
        cnt = cnt + beats.astype(jnp.int32)
    sel_ref[0, 0] = jnp.where(cnt < SLC_TOPN, 1.0, 0.0).astype(sel_ref.dtype)


def _cmp_select(proj, ckv, bias_cmp, ovl, lane_consts, gate_expand):
    b = proj.shape[0]
    tq = CMP_TQ
    grp_w = HEADS_PER_GROUP * HEAD_DIM
    return pl.pallas_call(
        _cmp_select_kernel,
        grid=(GROUPS, SEQ // tq, b),
        in_specs=[
            pl.BlockSpec((1, tq, grp_w), lambda g, qi, i: (i, qi, NSA_Q_COL // grp_w + g)),
            pl.BlockSpec((1, tq, LANES), lambda g, qi, i: (i, qi, NSA_G_COL // LANES)),
            pl.BlockSpec((1, tq, grp_w), lambda g, qi, i: (i, qi, NSA_Z_COL // grp_w + g)),
            pl.BlockSpec((1, 2, 1, N_CMP_PAD, 2 * HEAD_DIM), lambda g, qi, i: (i, 0, g, 0, 0)),
            pl.BlockSpec((HEADS_PER_GROUP, N_CMP_PAD, tq), lambda g, qi, i: (g, 0, qi)),
            pl.BlockSpec((N_SLC, N_CMP_PAD), lambda g, qi, i: (0, 0)),
            pl.BlockSpec(memory_space=pltpu.VMEM),
            pl.BlockSpec(memory_space=pltpu.VMEM),
        ],
        out_specs=[
            pl.BlockSpec((1, tq, grp_w), lambda g, qi, i: (i, qi, g)),
            pl.BlockSpec((1, 1, N_SLC, tq), lambda g, qi, i: (i, g, 0, qi)),
        ],
        out_shape=[
            jax.ShapeDtypeStruct((b, SEQ, NSA_WIDTH), BF16),
            jax.ShapeDtypeStruct((b, GROUPS, N_SLC, SEQ), BF16),
        ],
        compiler_params=_cparams(("parallel", "parallel", "parallel")),
        name="cmp_select",
    )(proj, proj, proj, ckv, bias_cmp, ovl, lane_consts, gate_expand)


def _token_kernel(q_ref, ks_ref, vs_ref, kw_ref, vw_ref, gate_ref, zs_ref, zw_ref, a_ref, lc_ref,
                  gexp_ref, sel_ref, kblk_ref, place_ref, placed_ref, dspread_ref, y_ref,
                  qm_ref, m_ref, acc_ref, s_bufs, p_bufs, c_bufs, kx_ref, vx_ref, gz_ref):
    g = pl.program_id(1)
    tq = ATT_TQ
    tk = ATT_TK

    def stream_len(qi):
        return min(qi, WIN_SIZE // tk) + 1 + qi + 1

    for c in range(SEQ // tk):
        rows = slice(c * tk, (c + 1) * tk)
        for branch, v_ref in enumerate((vs_ref, vw_ref)):
            vv = v_ref[0, rows, :]
            vx_ref[2 * branch, rows, :] = vv * lc_ref[2, :tk] + lc_ref[3, :tk]
            vx_ref[2 * branch + 1, rows, :] = vv * lc_ref[4, :tk] + lc_ref[5, :tk]
        kk = ks_ref[0, rows, :]
        for e in range(2):
            kx_ref[e, c] = (kk * lc_ref[e, :tk] + kblk_ref[e, c]).astype(F32).T.astype(BF16)
        kx_ref[2, c] = kw_ref[0, rows, :].astype(F32).T.astype(BF16)

    def tile(qi, j):
        n_win = min(qi, WIN_SIZE // tk) + 1
        win = int(j < n_win)
        t = j - (1 - win) * n_win
        return win, qi - t, t if (win or t < 2) else None

    def scores(qi, j, s_ref):
        if j >= stream_len(qi):
            return
        win, ki, _ = tile(qi, j)
        for e in range(2):
            pair = slice(2 * e * tq, 2 * (e + 1) * tq)
            s_ref[pair, :] = jnp.dot(qm_ref[win, pair, :], kx_ref[2 if win else e, ki],
                                     preferred_element_type=F32)

    def softmax(qi, j, s_ref, p_ref, c_ref):
        win, _, kind = tile(qi, j)
        for slot in range(HEADS_PER_GROUP):
            e, a = divmod(slot, 2)
            rows = slice(slot * tq, (slot + 1) * tq)
            s = s_ref[rows, :]
            if kind is not None:
                s = s + a_ref[kind, HEADS_PER_GROUP * g + 2 * a + e]
            m_prev = m_ref[win, rows, :]
            m_new = jnp.maximum(m_prev, jnp.max(s, axis=1, keepdims=True))
            c_ref[rows, :] = jnp.exp2(m_prev - m_new)
            x = s - jnp.concatenate([m_new] * (tk // LANES), axis=1)
            p_ref[rows, :] = jnp.exp2(x.astype(BF16))
            m_ref[win, rows, :] = m_new

    def values(qi_, par_, j, p_ref, c_ref):
        win, ki, _ = tile(qi_, j)
        rows = slice(ki * tk, (ki + 1) * tk)
        for e in range(2):
            pair = slice(2 * e * tq, 2 * (e + 1) * tq)
            pv = jnp.dot(p_ref[pair, :], vx_ref[2 * win + e, rows, :], preferred_element_type=F32)
            acc = acc_ref.at[2 * par_ + win]
            acc[pair, :] = c_ref[pair, :] * acc[pair, :] + pv

    lane = lax.broadcasted_iota(jnp.int32, (tq, LANES), 1)

    def gated_output(par_, branch):
        acc = acc_ref.at[2 * par_ + branch]
        pairs = []
        for a in range(2):
            acc_e = acc[a * tq:(a + 1) * tq, :]
            acc_o = acc[(2 + a) * tq:(3 + a) * tq, :]
            both = jnp.concatenate([acc_e, acc_o], axis=1)
            den = sum(jnp.dot(t, dspread_ref[...], preferred_element_type=F32)
                      for t in _bf16_terms(both, 2))
            pairs.append(jnp.where(lane < HEAD_DIM, acc_e, acc_o) / den)
        return jnp.concatenate(pairs, axis=1) * gz_ref[2 * par_ + branch]

    bufs = [(s_bufs.at[i], p_bufs.at[i], c_bufs.at[i]) for i in range(2)]

    def step(qi, half, j, parity):
        cur, nxt = bufs[parity], bufs[1 - parity]
        values(qi, half, j - 1, nxt[1], nxt[2])
        softmax(qi, j, *cur)
        scores(qi, j + 1, nxt[0])

    def enter(qi, slot):
        rows = slice(slot * tq, (slot + 1) * tq)
        half = slot % 2
        q = q_ref[0, rows, :]
        qms = []
        for hs in range(HEADS_PER_GROUP):
            e, a = divmod(hs, 2)
            qms.append(q[:, a * LANES:(a + 1) * LANES] * lc_ref[e, :tq])
            qm_ref[1, hs * tq:(hs + 1) * tq, :] = qms[hs]
        flags = lax.dot_general(sel_ref[0, 0, :, rows], place_ref[...], _TN,
                                preferred_element_type=F32)
        fill = ((placed_ref[...] - flags) * MASKED).astype(BF16)
        for hs in range(HEADS_PER_GROUP):
            e = hs // 2
            qm_ref[0, hs * tq:(hs + 1) * tq, :] = qms[hs] + fill[:, e * LANES:(e + 1) * LANES]
        gates = _gates(gate_ref[0, rows, :], [gexp_ref[0, g], gexp_ref[1, g]])
        for branch, z_ref in enumerate((zs_ref, zw_ref)):
            gz_ref[2 * half + branch] = gates[branch] * _silu(z_ref[0, rows, :].astype(F32))
        m_ref[...] = jnp.full(m_ref.shape, -jnp.inf, F32)
        for branch in range(2):
            acc_ref[2 * half + branch] = jnp.zeros(acc_ref.shape[1:], F32)
        scores(qi, 0, bufs[0][0])

    def fill_pipeline(qi, half):
        softmax(qi, 0, *bufs[0])
        scores(qi, 1, bufs[1][0])
        step(qi, half, 1, 1)

    def run_stream(qi, half):
        for j in range(2, stream_len(qi)):
            step(qi, half, j, j % 2)

    def leave(qi, half):
        y_win = gated_output(half, 1)
        last_pos = stream_len(qi) - 1
        values(qi, half, last_pos, bufs[last_pos % 2][1], bufs[last_pos % 2][2])
        return y_win

    def write_output(slot, y_win):
        rows = slice(slot * tq, (slot + 1) * tq)
        y_ref[0, rows, :] = (y_win + gated_output(slot % 2, 0)).astype(y_ref.dtype)

    y_win = None
    for qi in range(SEQ // tq):
        enter(qi, qi)
        if qi > 0:
            write_output(qi - 1, y_win)
        fill_pipeline(qi, qi % 2)
        run_stream(qi, qi % 2)
        y_win = leave(qi, qi % 2)
    write_output(SEQ // tq - 1, y_win)


def _token_attention(proj, a_tiles, lane_consts, gate_expand, sel, sel_consts):
    b = proj.shape[0]
    tq, tk = ATT_TQ, ATT_TK
    grp_w = HEADS_PER_GROUP * HEAD_DIM
    k_blk = NSA_K_COL // LANES
    v_blk = NSA_V_COL // LANES
    z_blk = NSA_Z_COL // grp_w
    kv_spec = lambda blk: pl.BlockSpec((1, SEQ, LANES), lambda i, g: (i, 0, blk + g))
    z_spec = lambda blk: pl.BlockSpec((1, SEQ, grp_w), lambda i, g: (i, 0, blk + g))
    whole = pl.BlockSpec(memory_space=pltpu.VMEM)
    return pl.pallas_call(
        _token_kernel,
        grid=(b, GROUPS),
        in_specs=[
            pl.BlockSpec((1, SEQ, grp_w), lambda i, g: (i, 0, NSA_Q_COL // grp_w + g)),
            kv_spec(k_blk), kv_spec(v_blk), kv_spec(k_blk + GROUPS), kv_spec(v_blk + GROUPS),
            pl.BlockSpec((1, SEQ, LANES), lambda i, g: (i, 0, NSA_G_COL // LANES)),
            z_spec(z_blk + GROUPS), z_spec(z_blk + 2 * GROUPS),
            whole, whole, whole,
            pl.BlockSpec((1, 1, N_SLC, SEQ), lambda i, g: (i, g, 0, 0)),
            whole, whole, whole, whole,
        ],
        out_specs=pl.BlockSpec((1, SEQ, grp_w), lambda i, g: (i, 0, g)),
        out_shape=jax.ShapeDtypeStruct((b, SEQ, NSA_WIDTH), BF16),
        scratch_shapes=[
            pltpu.VMEM((2, HEADS_PER_GROUP * tq, LANES), BF16),
            pltpu.VMEM((2, HEADS_PER_GROUP * tq, LANES), F32),
            pltpu.VMEM((4, HEADS_PER_GROUP * tq, LANES), F32),
            pltpu.VMEM((2, HEADS_PER_GROUP * tq, tk), F32),
            pltpu.VMEM((2, HEADS_PER_GROUP * tq, tk), BF16),
            pltpu.VMEM((2, HEADS_PER_GROUP * tq, LANES), F32),
            pltpu.VMEM((3, SEQ // tk, LANES, tk), BF16),
            pltpu.VMEM((4, SEQ, LANES), BF16),
            pltpu.VMEM((4, tq, grp_w), F32),
        ],
        compiler_params=_cparams(("parallel", "parallel")),
        name="token_attention",
    )(proj, proj, proj, proj, proj, proj, proj, proj, a_tiles, lane_consts, gate_expand,
      sel, *sel_consts)


def _retention_kernel(q_ref, k_ref, v_ref, z_ref, cos_ref, sin_ref, inner_ref, xi_ref, zeta_ref,
                      gn_ref, y_ref, state_ref, *, decays):
    @pl.when(pl.program_id(1) == 0)
    def _():
        state_ref[...] = jnp.zeros(state_ref.shape, F32)

    cos = cos_ref[...]
    sin = sin_ref[...]
    half = RET_QK_DIM // 2

    def rot(x):
        x1, x2 = x[:, :half], x[:, half:]
        return jnp.concatenate([x1 * cos - x2 * sin, x1 * sin + x2 * cos], axis=1)

    for h in range(RET_HEADS):
        qs = slice(h * RET_QK_DIM, (h + 1) * RET_QK_DIM)
        vs = slice(h * RET_V_DIM, (h + 1) * RET_V_DIM)
        qr = rot(q_ref[0, :, qs].astype(F32))
        kr = rot(k_ref[0, :, qs].astype(F32)) * (RET_QK_DIM ** -0.5)
        qb = qr.astype(BF16)
        vh = v_ref[0, :, vs]
        attn = lax.dot_general(qb, kr.astype(BF16), _NT, preferred_element_type=F32) * inner_ref[h]
        st = state_ref[h]
        o = (jnp.dot(attn.astype(BF16), vh, preferred_element_type=F32)
             + jnp.dot(qb, st.astype(BF16), preferred_element_type=F32) * xi_ref[h])
        kz = (kr * zeta_ref[h]).astype(BF16)
        state_ref[h] = st * decays[h] + lax.dot_general(kz, vh, _TN, preferred_element_type=F32)
        mu = jnp.mean(o, axis=1, keepdims=True)
        d = o - mu
        var = jnp.mean(d * d, axis=1, keepdims=True)
        on = d * lax.rsqrt(var + GN_EPS) * gn_ref[h]
        y_ref[0, :, vs] = (on * _silu(z_ref[0, :, vs].astype(F32))).astype(y_ref.dtype)


def _retention(proj, gn_gain, tables):
    b = proj.shape[0]
    c = RET_CHUNK
    cos, sin, inner, xi, zeta, decays = tables
    v_blk = 2 * RET_QK_WIDTH // RET_WIDTH
    return pl.pallas_call(
        functools.partial(_retention_kernel, decays=decays),
        grid=(b, SEQ // c),
        in_specs=[
            pl.BlockSpec((1, c, RET_QK_WIDTH), lambda i, j: (i, j, 0)),
            pl.BlockSpec((1, c, RET_QK_WIDTH), lambda i, j: (i, j, 1)),
            pl.BlockSpec((1, c, RET_WIDTH), lambda i, j: (i, j, v_blk)),
            pl.BlockSpec((1, c, RET_WIDTH), lambda i, j: (i, j, v_blk + 1)),
            pl.BlockSpec((c, RET_QK_DIM // 2), lambda i, j: (j, 0)),
            pl.BlockSpec((c, RET_QK_DIM // 2), lambda i, j: (j, 0)),
            pl.BlockSpec((RET_HEADS, c, c), lambda i, j: (0, 0, 0)),
            pl.BlockSpec((RET_HEADS, c, 1), lambda i, j: (0, 0, 0)),
            pl.BlockSpec((RET_HEADS, c, 1), lambda i, j: (0, 0, 0)),
            pl.BlockSpec((RET_HEADS, 1, RET_V_DIM), lambda i, j: (0, 0, 0)),
        ],
        out_specs=pl.BlockSpec((1, c, RET_WIDTH), lambda i, j: (i, j, 0)),
        out_shape=jax.ShapeDtypeStruct((b, SEQ, RET_WIDTH), BF16),
        scratch_shapes=[pltpu.VMEM((RET_HEADS, RET_QK_DIM, RET_V_DIM), F32)],
        compiler_params=_cparams(("parallel", "arbitrary")),
        name="retention",
    )(proj, proj, proj, proj, cos, sin, inner, xi, zeta, gn_gain.reshape(RET_HEADS, 1, RET_V_DIM))


def _t5_bucket_np(dist):
    n = np.maximum(dist, 0)
    max_exact = REL_BUCKETS // 2
    nf = np.maximum(n, 1).astype(np.float64)
    large = max_exact + (np.log(nf / max_exact) / math.log(REL_MAX_DIST / max_exact)
                         * (REL_BUCKETS - max_exact)).astype(np.int64)
    large = np.minimum(large, REL_BUCKETS - 1)
    return np.where(n < max_exact, n, large).astype(np.int32)


def _skewed(vec, rows, stride, cols):
    p = vec.shape[-1]
    lead = vec.shape[:-1]
    flat = jnp.tile(vec, (1,) * len(lead) + (rows,))[..., :rows * (p - stride)]
    return flat.reshape(lead + (rows, p - stride))[..., :cols]


def _nsa_tables(table):
    tq, tk = ATT_TQ, ATT_TK
    assert tq == tk
    tab_t = table.T * LOG2E
    period = 2 * SEQ
    x = np.arange(period)
    ok = (x >= CMP_BLOCK - 1) & (x < SEQ)
    vec = jnp.take(tab_t, jnp.asarray(_t5_bucket_np(x - (CMP_BLOCK - 1))), axis=1)
    vec = jnp.where(jnp.asarray(ok)[None, :], vec, MASKED)
    bias_cmp = _skewed(vec, N_CMP_PAD, CMP_STRIDE, SEQ)
    rel_t = tab_t - tab_t[:, REL_BUCKETS - 1:]
    period = 2 * tq
    x = np.arange(period)
    x = np.where(x < tq, x, x - period)
    tiles = []
    for off in range(3):
        d = off * tk - x
        ok = (d >= 0) & (d < WIN_SIZE) if off == 2 else (d >= 0)
        vec = jnp.take(rel_t, jnp.asarray(_t5_bucket_np(d)), axis=1)
        vec = jnp.where(jnp.asarray(ok)[None, :], vec, MASKED)
        tiles.append(_skewed(vec, tq, 1, tk))
    a_tiles = jnp.stack(tiles + [jnp.zeros_like(tiles[0])])
    cs = np.arange(N_CMP_PAD)[None, :] * CMP_STRIDE
    jj = np.arange(N_SLC)[:, None]
    ovl = ((cs < (jj + 1) * SLC_BLOCK) & (cs + CMP_BLOCK > jj * SLC_BLOCK)
           & (np.arange(N_CMP_PAD)[None, :] < N_CMP))
    ovl = jnp.asarray(ovl.astype(np.float32), BF16)
    lane = np.arange(LANES).reshape(1, 1, 1, LANES)
    key = np.arange(SEQ).reshape(1, SEQ // tk, tk, 1)
    base = np.array([HEAD_DIM, 0]).reshape(2, 1, 1, 1)
    key_blocks = jnp.asarray((lane == base + key // SLC_BLOCK).astype(np.float32), BF16)
    col = np.arange(2 * LANES).reshape(1, -1)
    jb = np.arange(N_SLC).reshape(-1, 1)
    place = (col == HEAD_DIM + jb) | (col == LANES + jb)
    row = np.arange(2 * LANES).reshape(-1, 1)
    out = np.arange(LANES).reshape(1, -1)
    den_spread = ((row == LANES - 1) & (out < HEAD_DIM)) | ((row == LANES) & (out >= HEAD_DIM))
    sel_consts = (key_blocks, jnp.asarray(place.astype(np.float32), BF16),
                  jnp.asarray(place.any(axis=0, keepdims=True).astype(np.float32)),
                  jnp.asarray(den_spread.astype(np.float32), BF16))
    c = np.arange(LANES).reshape(1, 1, LANES, 1)
    col = np.arange(HEADS_PER_GROUP * HEAD_DIM).reshape(1, 1, 1, -1) // HEAD_DIM
    br = np.arange(3).reshape(3, 1, 1, 1)
    gg = np.arange(GROUPS).reshape(1, GROUPS, 1, 1)
    gate_expand = jnp.asarray((c == br * HEADS + HEADS_PER_GROUP * gg + col).astype(np.float32),
                              BF16)
    return bias_cmp, a_tiles, ovl, sel_consts, _lane_consts(max(tq, CMP_TQ)), gate_expand


def _retention_tables():
    c = RET_CHUNK
    log_g = jnp.log(1.0 - 2.0 ** (-5.0 - jnp.arange(RET_HEADS, dtype=F32)))
    i = jnp.arange(c, dtype=F32)
    diff = i[:, None] - i[None, :]
    inner = jnp.where(diff >= 0, jnp.exp(diff[None] * log_g[:, None, None]), 0.0)
    xi = jnp.exp((i + 1.0)[None, :] * log_g[:, None])[:, :, None]
    zeta = jnp.exp((c - 1.0 - i)[None, :] * log_g[:, None])[:, :, None]
    decays = tuple(float((1.0 - 2.0 ** (-5.0 - h)) ** c) for h in range(RET_HEADS))
    inv = 1.0 / (ROPE_BASE ** jnp.linspace(0.0, 1.0, RET_QK_DIM // 2, dtype=F32))
    ang = jnp.arange(SEQ, dtype=F32)[:, None] * inv[None, :]
    return jnp.cos(ang), jnp.sin(ang), inner, xi, zeta, decays


def _nsa_w_in_layout(w):
    d = w.shape[0]
    q = w[:, :NSA_WIDTH] * (HEAD_DIM ** -0.5 * LOG2E)
    kv0 = NSA_WIDTH
    g0 = kv0 + 6 * KV_WIDTH
    z0 = g0 + 3 * HEADS
    slab = lambda n: w[:, kv0 + n * KV_WIDTH: kv0 + (n + 1) * KV_WIDTH]
    cols = [q, w[:, z0:z0 + 3 * NSA_WIDTH]]
    for n in (2, 4, 3, 5):
        for g in range(GROUPS):
            part = slab(n)[:, g * HEAD_DIM:(g + 1) * HEAD_DIM]
            cols += [part, part]
    cols += [slab(0), slab(1), w[:, g0:z0]]
    used = NSA_G_COL + 3 * HEADS
    cols.append(jnp.zeros((d, NSA_PROJ_PAD - used), w.dtype))
    return jnp.concatenate(cols, axis=1).astype(BF16)


def _nsa_layer(h2d, b, pre_gain, post_gain, w_in, w_out, k_pos, k_w1, k_w2, v_pos, v_w1, v_w2, tabs):
    bias_cmp, a_tiles, ovl, sel_consts, lane_consts, gate_expand = tabs
    proj = _norm_proj(h2d, pre_gain, _nsa_w_in_layout(w_in), NSA_PROJ_TN)
    proj = proj.reshape(b, SEQ, NSA_PROJ_PAD)
    ckv = proj[:, :, NSA_CKV_COL:NSA_CKV_COL + 2 * KV_WIDTH]
    ckv = ckv.reshape(b, N_CMP_PAD, CMP_STRIDE, 2, GROUPS, HEAD_DIM).transpose(0, 3, 4, 1, 2, 5)
    ckv = ckv.reshape(b, 2, GROUPS, N_CMP_PAD, CMP_STRIDE * HEAD_DIM)
    pos = jnp.stack([k_pos, v_pos]).reshape(2, 2, CMP_STRIDE * HEAD_DIM)
    w1 = jnp.stack([k_w1, v_w1]).astype(BF16)
    w2 = jnp.stack([k_w2, v_w2])
    w2d = jnp.concatenate([w2, w2], axis=2).astype(BF16)
    ckv_c = _compress(ckv, pos, w1, w2d)
    y_cmp, sel = _cmp_select(proj, ckv_c, bias_cmp, ovl, lane_consts, gate_expand[0])
    y_tok = _token_attention(proj, a_tiles, lane_consts, gate_expand[1:], sel, sel_consts)
    parts = [y.reshape(b * SEQ, NSA_WIDTH) for y in (y_cmp, y_tok)]
    return _out_post(parts, w_out.astype(BF16), h2d, post_gain)


def _ret_layer(h2d, b, pre_gain, post_gain, w_in, w_out, gn_gain, tabs):
    proj = _norm_proj(h2d, pre_gain, w_in.astype(BF16), RET_PROJ_TN).reshape(b, SEQ, -1)
    y = _retention(proj, gn_gain, tabs)
    return _out_post([y.reshape(b * SEQ, RET_WIDTH)], w_out.astype(BF16), h2d, post_gain)


def kernel(x, pre_norm_gain, post_norm_gain, rel_bias_table, nsa_w_in, nsa_w_out, nsa_cmp_k_pos, nsa_cmp_k_w1, nsa_cmp_k_w2, nsa_cmp_v_pos, nsa_cmp_v_w1, nsa_cmp_v_w2, ret_w_in, ret_w_out, ret_gn_gain):
    b, s, d = x.shape
    assert s == SEQ and d == D_MODEL
    nsa_tabs = _nsa_tables(rel_bias_table)
    ret_tabs = _retention_tables()
    h = x.reshape(b * s, d)
    for layer in range(DEPTH):
        slot = layer // 2
        if layer % 2 == 0:
            h = _nsa_layer(h, b, pre_norm_gain[layer], post_norm_gain[layer], nsa_w_in[slot],
                           nsa_w_out[slot], nsa_cmp_k_pos[slot], nsa_cmp_k_w1[slot],
                           nsa_cmp_k_w2[slot], nsa_cmp_v_pos[slot], nsa_cmp_v_w1[slot],
                           nsa_cmp_v_w2[slot], nsa_tabs)
        else:
            h = _ret_layer(h, b, pre_norm_gain[layer], post_norm_gain[layer], ret_w_in[slot],
                           ret_w_out[slot], ret_gn_gain[slot], ret_tabs)
    return h.reshape(b, s, d)
```

```python
import functools
import math

import numpy as np
import jax
import jax.numpy as jnp
from jax import lax
from jax.experimental import pallas as pl
from jax.experimental.pallas import tpu as pltpu

F32 = jnp.float32
BF16 = jnp.bfloat16

D_MODEL = 1024
SEQ = 2048
DEPTH = 4
RMS_EPS = 1e-6
GN_EPS = 1e-6
MASKED = -1e30
LOG2E = math.log2(math.e)

HEADS = 16
HEAD_DIM = 64
GROUPS = 4
HEADS_PER_GROUP = HEADS // GROUPS
NSA_WIDTH = HEADS * HEAD_DIM
KV_WIDTH = GROUPS * HEAD_DIM
CMP_BLOCK = 32
CMP_STRIDE = 16
CMP_HIDDEN = 256
N_CMP = (SEQ - CMP_BLOCK) // CMP_STRIDE + 1
N_CMP_PAD = 128
SLC_BLOCK = 64
N_SLC = SEQ // SLC_BLOCK
SLC_TOPN = 16
WIN_SIZE = 512
FORCED_SCORE = 1e3
REL_BUCKETS = 32
REL_MAX_DIST = 128

RET_HEADS = 4
RET_QK_DIM = 256
RET_V_DIM = 512
RET_QK_WIDTH = RET_HEADS * RET_QK_DIM
RET_WIDTH = RET_HEADS * RET_V_DIM
ROPE_BASE = 10000.0

LANES = 128
VMEM_LIMIT_BYTES = 56 * 1024 * 1024

PROJ_TM = 1024
RET_PROJ_TN = 2048
NSA_PROJ_TN = 3456
POST_TM = 512
ATT_TQ = 256
ATT_TK = 256
CMP_TQ = 1024
RET_CHUNK = 256

NSA_Q_COL = 0
NSA_Z_COL = 1024
NSA_K_COL = 4096
NSA_V_COL = 5120
NSA_CKV_COL = 6144
NSA_G_COL = 6656
NSA_PROJ_PAD = 6912

_NT = (((1,), (1,)), ((), ()))
_TN = (((0,), (0,)), ((), ()))


def _cparams(sem):
    return pltpu.CompilerParams(dimension_semantics=sem, vmem_limit_bytes=VMEM_LIMIT_BYTES)


def _norm_proj_kernel(x_ref, g_ref, w_ref, o_ref, xn_ref):
    @pl.when(pl.program_id(1) == 0)
    def _():
        x = x_ref[...]
        ms = jnp.mean(x * x, axis=-1, keepdims=True)
        xn_ref[...] = (x * lax.rsqrt(ms + RMS_EPS) * g_ref[...]).astype(BF16)

    o_ref[...] = jnp.dot(xn_ref[...], w_ref[...], preferred_element_type=F32).astype(o_ref.dtype)


def _norm_proj(x2d, gain, w_bf16, tn):
    m, d = x2d.shape
    n = w_bf16.shape[1]
    assert n % tn == 0
    return pl.pallas_call(
        _norm_proj_kernel,
        grid=(m // PROJ_TM, n // tn),
        in_specs=[
            pl.BlockSpec((PROJ_TM, d), lambda i, j: (i, 0)),
            pl.BlockSpec((1, d), lambda i, j: (0, 0)),
            pl.BlockSpec((d, tn), lambda i, j: (0, j)),
        ],
        out_specs=pl.BlockSpec((PROJ_TM, tn), lambda i, j: (i, j)),
        out_shape=jax.ShapeDtypeStruct((m, n), BF16),
        scratch_shapes=[pltpu.VMEM((PROJ_TM, d), BF16)],
        compiler_params=_cparams(("parallel", "arbitrary")),
        name="norm_proj",
    )(x2d, gain.reshape(1, d), w_bf16)


def _out_post_kernel(*refs, n_parts):
    y_refs = refs[:n_parts]
    w_ref, h_ref, g_ref, o_ref = refs[n_parts:]
    y = y_refs[0][...].astype(F32)
    for r in y_refs[1:]:
        y = y + r[...].astype(F32)
    t = jnp.dot(y.astype(BF16), w_ref[...], preferred_element_type=F32)
    ms = jnp.mean(t * t, axis=-1, keepdims=True)
    o_ref[...] = h_ref[...] + t * lax.rsqrt(ms + RMS_EPS) * g_ref[...]


def _out_post(parts, w_bf16, h2d, gain):
    m, d = h2d.shape
    k = w_bf16.shape[0]
    n_parts = len(parts)
    return pl.pallas_call(
        functools.partial(_out_post_kernel, n_parts=n_parts),
        grid=(m // POST_TM,),
        in_specs=[pl.BlockSpec((POST_TM, k), lambda i: (i, 0)) for _ in parts] + [
            pl.BlockSpec((k, d), lambda i: (0, 0)),
            pl.BlockSpec((POST_TM, d), lambda i: (i, 0)),
            pl.BlockSpec((1, d), lambda i: (0, 0)),
        ],
        out_specs=pl.BlockSpec((POST_TM, d), lambda i: (i, 0)),
        out_shape=jax.ShapeDtypeStruct((m, d), F32),
        compiler_params=_cparams(("parallel",)),
        name="out_post",
    )(*parts, w_bf16, h2d, gain.reshape(1, d))


def _compress_kernel(x_ref, pos_ref, w1_ref, w2_ref, o_ref):
    half = CMP_STRIDE * HEAD_DIM
    for which in range(2):
        x = x_ref[0, which, 0].astype(F32)
        xa = (x + pos_ref[which, 0:1, :]).astype(BF16)
        xb = (x + pos_ref[which, 1:2, :]).astype(BF16)
        pa = jnp.dot(xa, w1_ref[which, :half, :], preferred_element_type=F32)
        pb = jnp.dot(xb, w1_ref[which, half:, :], preferred_element_type=F32)
        hid = pa + pltpu.roll(pb, N_CMP_PAD - 1, 0)
        hid = hid * jax.nn.sigmoid(hid)
        o_ref[0, which, 0] = jnp.dot(hid.astype(BF16), w2_ref[which],
                                     preferred_element_type=F32).astype(o_ref.dtype)


def _compress(ckv_rows, pos, w1, w2d):
    b = ckv_rows.shape[0]
    row_w = CMP_STRIDE * HEAD_DIM
    return pl.pallas_call(
        _compress_kernel,
        grid=(b, GROUPS),
        in_specs=[
            pl.BlockSpec((1, 2, 1, N_CMP_PAD, row_w), lambda i, g: (i, 0, g, 0, 0)),
            pl.BlockSpec((2, 2, row_w), lambda i, g: (0, 0, 0)),
            pl.BlockSpec((2, 2 * row_w, CMP_HIDDEN), lambda i, g: (0, 0, 0)),
            pl.BlockSpec((2, CMP_HIDDEN, 2 * HEAD_DIM), lambda i, g: (0, 0, 0)),
        ],
        out_specs=pl.BlockSpec((1, 2, 1, N_CMP_PAD, 2 * HEAD_DIM), lambda i, g: (i, 0, g, 0, 0)),
        out_shape=jax.ShapeDtypeStruct((b, 2, GROUPS, N_CMP_PAD, 2 * HEAD_DIM), BF16),
        compiler_params=_cparams(("parallel", "parallel")),
        name="compress",
    )(ckv_rows, pos, w1, w2d)


def _lane_consts(rows):
    lane = np.arange(LANES)
    lo = (lane < HEAD_DIM).astype(np.float32)
    hi = (lane >= HEAD_DIM).astype(np.float32)
    last = (lane == LANES - 1).astype(np.float32)
    first = (lane == 0).astype(np.float32)
    c = np.stack([lo, hi, 1.0 - last, last, 1.0 - first, first])
    return jnp.asarray(np.broadcast_to(c[:, None, :], (6, rows, LANES)), BF16)


def _bf16_terms(x, n):
    terms = []
    for _ in range(n - 1):
        t = x.astype(BF16)
        terms.append(t)
        x = x - t.astype(F32)
    terms.append(x.astype(BF16))
    return terms


def _gates(gate_logits, expands):
    terms = _bf16_terms(jax.nn.sigmoid(gate_logits.astype(F32)), 2)
    return [sum(jnp.dot(t, ex, preferred_element_type=F32) for t in terms) for ex in expands]


def _silu(x):
    return x * jax.nn.sigmoid(x)


def _cmp_select_kernel(q_ref, gate_ref, z_ref, ckv_ref, bias_ref, ovl_ref, lc_ref, gexp_ref,
                       y_ref, sel_ref):
    g = pl.program_id(0)
    qi = pl.program_id(1)
    tq = q_ref.shape[1]
    q = q_ref[0]
    kk = ckv_ref[0, 0, 0]
    vv = ckv_ref[0, 1, 0]
    lane = lax.broadcasted_iota(jnp.int32, (tq, LANES), 1)

    psum = jnp.zeros((N_CMP_PAD, tq), F32)
    pairs = []
    for a in range(2):
        qp = q[:, a * LANES:(a + 1) * LANES]
        outs = []
        for e in range(2):
            s = lax.dot_general(kk, qp * lc_ref[e, :tq], _NT, preferred_element_type=F32)
            bias = bias_ref[2 * a + e]
            s = s + bias
            m = jnp.max(s, axis=0, keepdims=True)
            p = jnp.exp2(s - m)
            p = jnp.where(bias > 0.5 * MASKED, p / jnp.sum(p, axis=0, keepdims=True), 0.0)
            psum = psum + p
            outs.append(lax.dot_general(p.astype(BF16), vv, _TN, preferred_element_type=F32))
        pairs.append(jnp.where(lane < HEAD_DIM, outs[0], outs[1]))
    y_ref[0] = (jnp.concatenate(pairs, axis=1) * _gates(gate_ref[0], [gexp_ref[g]])[0]
                * _silu(z_ref[0].astype(F32))).astype(y_ref.dtype)

    ovl = ovl_ref[...]
    imp = sum(jnp.dot(ovl, t, preferred_element_type=F32) for t in _bf16_terms(psum, 3))
    t = qi * tq + lax.broadcasted_iota(jnp.int32, (N_SLC, tq), 1)
    blk = lax.broadcasted_iota(jnp.int32, (N_SLC, tq), 0)
    cur = lax.shift_right_logical(t, int(math.log2(SLC_BLOCK)))
    forced = (blk == 0) | (blk == cur) | (blk == cur - 1)
    score = jnp.where(blk * SLC_BLOCK <= t, imp + jnp.where(forced, FORCED_SCORE, 0.0), MASKED)
    cnt = jnp.zeros((N_SLC, tq), jnp.int32)
    for jp in range(N_SLC):
        row = score[jp:jp + 1, :]
        beats = (row > score) | ((row == score) & (blk > jp))
        cnt = cnt + beats.astype(jnp.int32)
    sel_ref[0, 0] = jnp.where(cnt < SLC_TOPN, 1.0, 0.0).astype(sel_ref.dtype)


def _cmp_select(proj, ckv, bias_cmp, ovl, lane_consts, gate_expand):
    b = proj.shape[0]
    tq = CMP_TQ
    grp_w = HEADS_PER_GROUP * HEAD_DIM
    return pl.pallas_call(
        _cmp_select_kernel,
        grid=(GROUPS, SEQ // tq, b),
        in_specs=[
            pl.BlockSpec((1, tq, grp_w), lambda g, qi, i: (i, qi, NSA_Q_COL // grp_w + g)),
            pl.BlockSpec((1, tq, LANES), lambda g, qi, i: (i, qi, NSA_G_COL // LANES)),
            pl.BlockSpec((1, tq, grp_w), lambda g, qi, i: (i, qi, NSA_Z_COL // grp_w + g)),
            pl.BlockSpec((1, 2, 1, N_CMP_PAD, 2 * HEAD_DIM), lambda g, qi, i: (i, 0, g, 0, 0)),
            pl.BlockSpec((HEADS_PER_GROUP, N_CMP_PAD, tq), lambda g, qi, i: (g, 0, qi)),
            pl.BlockSpec((N_SLC, N_CMP_PAD), lambda g, qi, i: (0, 0)),
            pl.BlockSpec(memory_space=pltpu.VMEM),
            pl.BlockSpec(memory_space=pltpu.VMEM),
        ],
        out_specs=[
            pl.BlockSpec((1, tq, grp_w), lambda g, qi, i: (i, qi, g)),
            pl.BlockSpec((1, 1, N_SLC, tq), lambda g, qi, i: (i, g, 0, qi)),
        ],
        out_shape=[
            jax.ShapeDtypeStruct((b, SEQ, NSA_WIDTH), BF16),
            jax.ShapeDtypeStruct((b, GROUPS, N_SLC, SEQ), BF16),
        ],
        compiler_params=_cparams(("parallel", "parallel", "parallel")),
        name="cmp_select",
    )(proj, proj, proj, ckv, bias_cmp, ovl, lane_consts, gate_expand)


def _token_kernel(q_ref, ks_ref, vs_ref, kw_ref, vw_ref, gate_ref, zs_ref, zw_ref, a_ref, lc_ref,
                  gexp_ref, sel_ref, kblk_ref, place_ref, placed_ref, dspread_ref, y_ref,
                  qm_ref, m_ref, acc_ref, s_bufs, p_bufs, c_bufs, kx_ref, vx_ref, gz_ref):
    g = pl.program_id(1)
    tq = ATT_TQ
    tk = ATT_TK

    def stream_len(qi):
        return min(qi, WIN_SIZE // tk) + 1 + qi + 1

    for c in range(SEQ // tk):
        rows = slice(c * tk, (c + 1) * tk)
        for branch, v_ref in enumerate((vs_ref, vw_ref)):
            vv = v_ref[0, rows, :]
            vx_ref[2 * branch, rows, :] = vv * lc_ref[2, :tk] + lc_ref[3, :tk]
            vx_ref[2 * branch + 1, rows, :] = vv * lc_ref[4, :tk] + lc_ref[5, :tk]
        kk = ks_ref[0, rows, :]
        for e in range(2):
            kx_ref[e, c] = (kk * lc_ref[e, :tk] + kblk_ref[e, c]).astype(F32).T.astype(BF16)
        kx_ref[2, c] = kw_ref[0, rows, :].astype(F32).T.astype(BF16)

    def tile(qi, j):
        n_win = min(qi, WIN_SIZE // tk) + 1
        win = int(j < n_win)
        t = j - (1 - win) * n_win
        return win, qi - t, t if (win or t < 2) else None

    def scores(qi, j, s_ref):
        if j >= stream_len(qi):
            return
        win, ki, _ = tile(qi, j)
        for e in range(2):
            pair = slice(2 * e * tq, 2 * (e + 1) * tq)
            s_ref[pair, :] = jnp.dot(qm_ref[win, pair, :], kx_ref[2 if win else e, ki],
                                     preferred_element_type=F32)

    def softmax(qi, j, s_ref, p_ref, c_ref):
        win, _, kind = tile(qi, j)
        for slot in range(HEADS_PER_GROUP):
            e, a = divmod(slot, 2)
            rows = slice(slot * tq, (slot + 1) * tq)
            s = s_ref[rows, :]
            if kind is not None:
                s = s + a_ref[kind, HEADS_PER_GROUP * g + 2 * a + e]
            m_prev = m_ref[win, rows, :]
            m_new = jnp.maximum(m_prev, jnp.max(s, axis=1, keepdims=True))
            c_ref[rows, :] = jnp.exp2(m_prev - m_new)
            x = s - jnp.concatenate([m_new] * (tk // LANES), axis=1)
            p_ref[rows, :] = jnp.exp2(x.astype(BF16))
            m_ref[win, rows, :] = m_new

    def values(qi_, par_, j, p_ref, c_ref):
        win, ki, _ = tile(qi_, j)
        rows = slice(ki * tk, (ki + 1) * tk)
        for e in range(2):
            pair = slice(2 * e * tq, 2 * (e + 1) * tq)
            pv = jnp.dot(p_ref[pair, :], vx_ref[2 * win + e, rows, :], preferred_element_type=F32)
            acc = acc_ref.at[2 * par_ + win]
            acc[pair, :] = c_ref[pair, :] * acc[pair, :] + pv

    lane = lax.broadcasted_iota(jnp.int32, (tq, LANES), 1)

    def gated_output(par_, branch):
        acc = acc_ref.at[2 * par_ + branch]
        pairs = []
        for a in range(2):
            acc_e = acc[a * tq:(a + 1) * tq, :]
            acc_o = acc[(2 + a) * tq:(3 + a) * tq, :]
            both = jnp.concatenate([acc_e, acc_o], axis=1)
            den = sum(jnp.dot(t, dspread_ref[...], preferred_element_type=F32)
                      for t in _bf16_terms(both, 2))
            pairs.append(jnp.where(lane < HEAD_DIM, acc_e, acc_o) / den)
        return jnp.concatenate(pairs, axis=1) * gz_ref[2 * par_ + branch]

    bufs = [(s_bufs.at[i], p_bufs.at[i], c_bufs.at[i]) for i in range(2)]

    def step(qi, half, j, parity):
        cur, nxt = bufs[parity], bufs[1 - parity]
        values(qi, half, j - 1, nxt[1], nxt[2])
        softmax(qi, j, *cur)
        scores(qi, j + 1, nxt[0])

    def enter(qi, slot):
        rows = slice(slot * tq, (slot + 1) * tq)
        half = slot % 2
        q = q_ref[0, rows, :]
        qms = []
        for hs in range(HEADS_PER_GROUP):
            e, a = divmod(hs, 2)
            qms.append(q[:, a * LANES:(a + 1) * LANES] * lc_ref[e, :tq])
            qm_ref[1, hs * tq:(hs + 1) * tq, :] = qms[hs]
        flags = lax.dot_general(sel_ref[0, 0, :, rows], place_ref[...], _TN,
                                preferred_element_type=F32)
        fill = ((placed_ref[...] - flags) * MASKED).astype(BF16)
        for hs in range(HEADS_PER_GROUP):
            e = hs // 2
            qm_ref[0, hs * tq:(hs + 1) * tq, :] = qms[hs] + fill[:, e * LANES:(e + 1) * LANES]
        gates = _gates(gate_ref[0, rows, :], [gexp_ref[0, g], gexp_ref[1, g]])
        for branch, z_ref in enumerate((zs_ref, zw_ref)):
            gz_ref[2 * half + branch] = gates[branch] * _silu(z_ref[0, rows, :].astype(F32))
        m_ref[...] = jnp.full(m_ref.shape, -jnp.inf, F32)
        for branch in range(2):
            acc_ref[2 * half + branch] = jnp.zeros(acc_ref.shape[1:], F32)
        scores(qi, 0, bufs[0][0])

    def fill_pipeline(qi, half):
        softmax(qi, 0, *bufs[0])
        scores(qi, 1, bufs[1][0])
        step(qi, half, 1, 1)

    def run_stream(qi, half):
        for j in range(2, stream_len(qi)):
            step(qi, half, j, j % 2)

    def leave(qi, half):
        y_win = gated_output(half, 1)
        last_pos = stream_len(qi) - 1
        values(qi, half, last_pos, bufs[last_pos % 2][1], bufs[last_pos % 2][2])
        return y_win

    def write_output(slot, y_win):
        rows = slice(slot * tq, (slot + 1) * tq)
        y_ref[0, rows, :] = (y_win + gated_output(slot % 2, 0)).astype(y_ref.dtype)

    y_win = None
    for qi in range(SEQ // tq):
        enter(qi, qi)
        if qi > 0:
            write_output(qi - 1, y_win)
        fill_pipeline(qi, qi % 2)
        run_stream(qi, qi % 2)
        y_win = leave(qi, qi % 2)
    write_output(SEQ // tq - 1, y_win)


def _token_attention(proj, a_tiles, lane_consts, gate_expand, sel, sel_consts):
    b = proj.shape[0]
    tq, tk = ATT_TQ, ATT_TK
    grp_w = HEADS_PER_GROUP * HEAD_DIM
    k_blk = NSA_K_COL // LANES
    v_blk = NSA_V_COL // LANES
    z_blk = NSA_Z_COL // grp_w
    kv_spec = lambda blk: pl.BlockSpec((1, SEQ, LANES), lambda i, g: (i, 0, blk + g))
    z_spec = lambda blk: pl.BlockSpec((1, SEQ, grp_w), lambda i, g: (i, 0, blk + g))
    whole = pl.BlockSpec(memory_space=pltpu.VMEM)
    return pl.pallas_call(
        _token_kernel,
        grid=(b, GROUPS),
        in_specs=[
            pl.BlockSpec((1, SEQ, grp_w), lambda i, g: (i, 0, NSA_Q_COL // grp_w + g)),
            kv_spec(k_blk), kv_spec(v_blk), kv_spec(k_blk + GROUPS), kv_spec(v_blk + GROUPS),
            pl.BlockSpec((1, SEQ, LANES), lambda i, g: (i, 0, NSA_G_COL // LANES)),
            z_spec(z_blk + GROUPS), z_spec(z_blk + 2 * GROUPS),
            whole, whole, whole,
            pl.BlockSpec((1, 1, N_SLC, SEQ), lambda i, g: (i, g, 0, 0)),
            whole, whole, whole, whole,
        ],
        out_specs=pl.BlockSpec((1, SEQ, grp_w), lambda i, g: (i, 0, g)),
        out_shape=jax.ShapeDtypeStruct((b, SEQ, NSA_WIDTH), BF16),
        scratch_shapes=[
            pltpu.VMEM((2, HEADS_PER_GROUP * tq, LANES), BF16),
            pltpu.VMEM((2, HEADS_PER_GROUP * tq, LANES), F32),
            pltpu.VMEM((4, HEADS_PER_GROUP * tq, LANES), F32),
            pltpu.VMEM((2, HEADS_PER_GROUP * tq, tk), F32),
            pltpu.VMEM((2, HEADS_PER_GROUP * tq, tk), BF16),
            pltpu.VMEM((2, HEADS_PER_GROUP * tq, LANES), F32),
            pltpu.VMEM((3, SEQ // tk, LANES, tk), BF16),
            pltpu.VMEM((4, SEQ, LANES), BF16),
            pltpu.VMEM((4, tq, grp_w), F32),
        ],
        compiler_params=_cparams(("parallel", "parallel")),
        name="token_attention",
    )(proj, proj, proj, proj, proj, proj, proj, proj, a_tiles, lane_consts, gate_expand,
      sel, *sel_consts)


def _retention_kernel(q_ref, k_ref, v_ref, z_ref, cos_ref, sin_ref, inner_ref, xi_ref, zeta_ref,
                      gn_ref, y_ref, state_ref, *, decays):
    @pl.when(pl.program_id(1) == 0)
    def _():
        state_ref[...] = jnp.zeros(state_ref.shape, F32)

    cos = cos_ref[...]
    sin = sin_ref[...]
    half = RET_QK_DIM // 2

    def rot(x):
        x1, x2 = x[:, :half], x[:, half:]
        return jnp.concatenate([x1 * cos - x2 * sin, x1 * sin + x2 * cos], axis=1)

    for h in range(RET_HEADS):
        qs = slice(h * RET_QK_DIM, (h + 1) * RET_QK_DIM)
        vs = slice(h * RET_V_DIM, (h + 1) * RET_V_DIM)
        qr = rot(q_ref[0, :, qs].astype(F32))
        kr = rot(k_ref[0, :, qs].astype(F32)) * (RET_QK_DIM ** -0.5)
        qb = qr.astype(BF16)
        vh = v_ref[0, :, vs]
        attn = lax.dot_general(qb, kr.astype(BF16), _NT, preferred_element_type=F32) * inner_ref[h]
        st = state_ref[h]
        o = (jnp.dot(attn.astype(BF16), vh, preferred_element_type=F32)
             + jnp.dot(qb, st.astype(BF16), preferred_element_type=F32) * xi_ref[h])
        kz = (kr * zeta_ref[h]).astype(BF16)
        state_ref[h] = st * decays[h] + lax.dot_general(kz, vh, _TN, preferred_element_type=F32)
        mu = jnp.mean(o, axis=1, keepdims=True)
        d = o - mu
        var = jnp.mean(d * d, axis=1, keepdims=True)
        on = d * lax.rsqrt(var + GN_EPS) * gn_ref[h]
        y_ref[0, :, vs] = (on * _silu(z_ref[0, :, vs].astype(F32))).astype(y_ref.dtype)


def _retention(proj, gn_gain, tables):
    b = proj.shape[0]
    c = RET_CHUNK
    cos, sin, inner, xi, zeta, decays = tables
    v_blk = 2 * RET_QK_WIDTH // RET_WIDTH
    return pl.pallas_call(
        functools.partial(_retention_kernel, decays=decays),
        grid=(b, SEQ // c),
        in_specs=[
            pl.BlockSpec((1, c, RET_QK_WIDTH), lambda i, j: (i, j, 0)),
            pl.BlockSpec((1, c, RET_QK_WIDTH), lambda i, j: (i, j, 1)),
            pl.BlockSpec((1, c, RET_WIDTH), lambda i, j: (i, j, v_blk)),
            pl.BlockSpec((1, c, RET_WIDTH), lambda i, j: (i, j, v_blk + 1)),
            pl.BlockSpec((c, RET_QK_DIM // 2), lambda i, j: (j, 0)),
            pl.BlockSpec((c, RET_QK_DIM // 2), lambda i, j: (j, 0)),
            pl.BlockSpec((RET_HEADS, c, c), lambda i, j: (0, 0, 0)),
            pl.BlockSpec((RET_HEADS, c, 1), lambda i, j: (0, 0, 0)),
            pl.BlockSpec((RET_HEADS, c, 1), lambda i, j: (0, 0, 0)),
            pl.BlockSpec((RET_HEADS, 1, RET_V_DIM), lambda i, j: (0, 0, 0)),
        ],
        out_specs=pl.BlockSpec((1, c, RET_WIDTH), lambda i, j: (i, j, 0)),
        out_shape=jax.ShapeDtypeStruct((b, SEQ, RET_WIDTH), BF16),
        scratch_shapes=[pltpu.VMEM((RET_HEADS, RET_QK_DIM, RET_V_DIM), F32)],
        compiler_params=_cparams(("parallel", "arbitrary")),
        name="retention",
    )(proj, proj, proj, proj, cos, sin, inner, xi, zeta, gn_gain.reshape(RET_HEADS, 1, RET_V_DIM))


def _t5_bucket_np(dist):
    n = np.maximum(dist, 0)
    max_exact = REL_BUCKETS // 2
    nf = np.maximum(n, 1).astype(np.float64)
    large = max_exact + (np.log(nf / max_exact) / math.log(REL_MAX_DIST / max_exact)
                         * (REL_BUCKETS - max_exact)).astype(np.int64)
    large = np.minimum(large, REL_BUCKETS - 1)
    return np.where(n < max_exact, n, large).astype(np.int32)


def _skewed(vec, rows, stride, cols):
    p = vec.shape[-1]
    lead = vec.shape[:-1]
    flat = jnp.tile(vec, (1,) * len(lead) + (rows,))[..., :rows * (p - stride)]
    return flat.reshape(lead + (rows, p - stride))[..., :cols]


def _nsa_tables(table):
    tq, tk = ATT_TQ, ATT_TK
    assert tq == tk
    tab_t = table.T * LOG2E
    period = 2 * SEQ
    x = np.arange(period)
    ok = (x >= CMP_BLOCK - 1) & (x < SEQ)
    vec = jnp.take(tab_t, jnp.asarray(_t5_bucket_np(x - (CMP_BLOCK - 1))), axis=1)
    vec = jnp.where(jnp.asarray(ok)[None, :], vec, MASKED)
    vec2 = jnp.concatenate([vec, vec], axis=1)
    starts = [(-CMP_STRIDE * c) % period for c in range(N_CMP_PAD)]
    bias_cmp = jnp.stack([vec2[:, s0:s0 + SEQ] for s0 in starts], axis=1)
    rel_t = tab_t - tab_t[:, REL_BUCKETS - 1:]
    period = 2 * tq
    x = np.arange(period)
    x = np.where(x < tq, x, x - period)
    tiles = []
    for off in range(3):
        d = off * tk - x
        ok = (d >= 0) & (d < WIN_SIZE) if off == 2 else (d >= 0)
        vec = jnp.take(rel_t, jnp.asarray(_t5_bucket_np(d)), axis=1)
        vec = jnp.where(jnp.asarray(ok)[None, :], vec, MASKED)
        tiles.append(_skewed(vec, tq, 1, tk))
    a_tiles = jnp.stack(tiles + [jnp.zeros_like(tiles[0])])
    cs = np.arange(N_CMP_PAD)[None, :] * CMP_STRIDE
    jj = np.arange(N_SLC)[:, None]
    ovl = ((cs < (jj + 1) * SLC_BLOCK) & (cs + CMP_BLOCK > jj * SLC_BLOCK)
           & (np.arange(N_CMP_PAD)[None, :] < N_CMP))
    ovl = jnp.asarray(ovl.astype(np.float32), BF16)
    lane = np.arange(LANES).reshape(1, 1, 1, LANES)
    key = np.arange(SEQ).reshape(1, SEQ // tk, tk, 1)
    base = np.array([HEAD_DIM, 0]).reshape(2, 1, 1, 1)
    key_blocks = jnp.asarray((lane == base + key // SLC_BLOCK).astype(np.float32), BF16)
    col = np.arange(2 * LANES).reshape(1, -1)
    jb = np.arange(N_SLC).reshape(-1, 1)
    place = (col == HEAD_DIM + jb) | (col == LANES + jb)
    row = np.arange(2 * LANES).reshape(-1, 1)
    out = np.arange(LANES).reshape(1, -1)
    den_spread = ((row == LANES - 1) & (out < HEAD_DIM)) | ((row == LANES) & (out >= HEAD_DIM))
    sel_consts = (key_blocks, jnp.asarray(place.astype(np.float32), BF16),
                  jnp.asarray(place.any(axis=0, keepdims=True).astype(np.float32)),
                  jnp.asarray(den_spread.astype(np.float32), BF16))
    c = np.arange(LANES).reshape(1, 1, LANES, 1)
    col = np.arange(HEADS_PER_GROUP * HEAD_DIM).reshape(1, 1, 1, -1) // HEAD_DIM
    br = np.arange(3).reshape(3, 1, 1, 1)
    gg = np.arange(GROUPS).reshape(1, GROUPS, 1, 1)
    gate_expand = jnp.asarray((c == br * HEADS + HEADS_PER_GROUP * gg + col).astype(np.float32),
                              BF16)
    return bias_cmp, a_tiles, ovl, sel_consts, _lane_consts(max(tq, CMP_TQ)), gate_expand


def _retention_tables():
    c = RET_CHUNK
    log_g = jnp.log(1.0 - 2.0 ** (-5.0 - jnp.arange(RET_HEADS, dtype=F32)))
    i = jnp.arange(c, dtype=F32)
    diff = i[:, None] - i[None, :]
    inner = jnp.where(diff >= 0, jnp.exp(diff[None] * log_g[:, None, None]), 0.0)
    xi = jnp.exp((i + 1.0)[None, :] * log_g[:, None])[:, :, None]
    zeta = jnp.exp((c - 1.0 - i)[None, :] * log_g[:, None])[:, :, None]
    decays = tuple(float((1.0 - 2.0 ** (-5.0 - h)) ** c) for h in range(RET_HEADS))
    inv = 1.0 / (ROPE_BASE ** jnp.linspace(0.0, 1.0, RET_QK_DIM // 2, dtype=F32))
    ang = jnp.arange(SEQ, dtype=F32)[:, None] * inv[None, :]
    return jnp.cos(ang), jnp.sin(ang), inner, xi, zeta, decays


def _nsa_w_in_layout(w):
    d = w.shape[0]
    q = (w[:, :NSA_WIDTH] * (HEAD_DIM ** -0.5 * LOG2E)).astype(BF16)
    w = w.astype(BF16)
    kv0 = NSA_WIDTH
    g0 = kv0 + 6 * KV_WIDTH
    z0 = g0 + 3 * HEADS
    slab = lambda n: w[:, kv0 + n * KV_WIDTH: kv0 + (n + 1) * KV_WIDTH]
    cols = [q, w[:, z0:z0 + 3 * NSA_WIDTH]]
    for n in (2, 4, 3, 5):
        for g in range(GROUPS):
            part = slab(n)[:, g * HEAD_DIM:(g + 1) * HEAD_DIM]
            cols += [part, part]
    cols += [slab(0), slab(1), w[:, g0:z0]]
    used = NSA_G_COL + 3 * HEADS
    cols.append(jnp.zeros((d, NSA_PROJ_PAD - used), w.dtype))
    return jnp.concatenate(cols, axis=1).astype(BF16)


def _nsa_layer(h2d, b, pre_gain, post_gain, w_in, w_out, k_pos, k_w1, k_w2, v_pos, v_w1, v_w2, tabs):
    bias_cmp, a_tiles, ovl, sel_consts, lane_consts, gate_expand = tabs
    proj = _norm_proj(h2d, pre_gain, _nsa_w_in_layout(w_in), NSA_PROJ_TN)
    proj = proj.reshape(b, SEQ, NSA_PROJ_PAD)
    ckv = proj[:, :, NSA_CKV_COL:NSA_CKV_COL + 2 * KV_WIDTH]
    ckv = ckv.reshape(b, N_CMP_PAD, CMP_STRIDE, 2, GROUPS, HEAD_DIM).transpose(0, 3, 4, 1, 2, 5)
    ckv = ckv.reshape(b, 2, GROUPS, N_CMP_PAD, CMP_STRIDE * HEAD_DIM)
    pos = jnp.stack([k_pos, v_pos]).reshape(2, 2, CMP_STRIDE * HEAD_DIM)
    w1 = jnp.stack([k_w1, v_w1]).astype(BF16)
    w2 = jnp.stack([k_w2, v_w2])
    w2d = jnp.concatenate([w2, w2], axis=2).astype(BF16)
    ckv_c = _compress(ckv, pos, w1, w2d)
    y_cmp, sel = _cmp_select(proj, ckv_c, bias_cmp, ovl, lane_consts, gate_expand[0])
    y_tok = _token_attention(proj, a_tiles, lane_consts, gate_expand[1:], sel, sel_consts)
    parts = [y.reshape(b * SEQ, NSA_WIDTH) for y in (y_cmp, y_tok)]
    return _out_post(parts, w_out.astype(BF16), h2d, post_gain)


def _ret_layer(h2d, b, pre_gain, post_gain, w_in, w_out, gn_gain, tabs):
    proj = _norm_proj(h2d, pre_gain, w_in.astype(BF16), RET_PROJ_TN).reshape(b, SEQ, -1)
    y = _retention(proj, gn_gain, tabs)
    return _out_post([y.reshape(b * SEQ, RET_WIDTH)], w_out.astype(BF16), h2d, post_gain)


def kernel(x, pre_norm_gain, post_norm_gain, rel_bias_table, nsa_w_in, nsa_w_out, nsa_cmp_k_pos, nsa_cmp_k_w1, nsa_cmp_k_w2, nsa_cmp_v_pos, nsa_cmp_v_w1, nsa_cmp_v_w2, ret_w_in, ret_w_out, ret_gn_gain):
    b, s, d = x.shape
    assert s == SEQ and d == D_MODEL
    nsa_tabs = _nsa_tables(rel_bias_table)
    ret_tabs = _retention_tables()
    h = x.reshape(b * s, d)
    for layer in range(DEPTH):
        slot = layer // 2
        if layer % 2 == 0:
            h = _nsa_layer(h, b, pre_norm_gain[layer], post_norm_gain[layer], nsa_w_in[slot],
                           nsa_w_out[slot], nsa_cmp_k_pos[slot], nsa_cmp_k_w1[slot],
                           nsa_cmp_k_w2[slot], nsa_cmp_v_pos[slot], nsa_cmp_v_w1[slot],
                           nsa_cmp_v_w2[slot], nsa_tabs)
        else:
            h = _ret_layer(h, b, pre_norm_gain[layer], post_norm_gain[layer], ret_w_in[slot],
                           ret_w_out[slot], ret_gn_gain[slot], ret_tabs)
    return h.reshape(b, s, d)
```

```python
import functools
import math

import numpy as np
import jax
import jax.numpy as jnp
from jax import lax
from jax.experimental import pallas as pl
from jax.experimental.pallas import tpu as pltpu

F32 = jnp.float32
BF16 = jnp.bfloat16

D_MODEL = 1024
SEQ = 2048
DEPTH = 4
RMS_EPS = 1e-6
GN_EPS = 1e-6
MASKED = -1e30
LOG2E = math.log2(math.e)

HEADS = 16
HEAD_DIM = 64
GROUPS = 4
HEADS_PER_GROUP = HEADS // GROUPS
NSA_WIDTH = HEADS * HEAD_DIM
KV_WIDTH = GROUPS * HEAD_DIM
CMP_BLOCK = 32
CMP_STRIDE = 16
CMP_HIDDEN = 256
N_CMP = (SEQ - CMP_BLOCK) // CMP_STRIDE + 1
N_CMP_PAD = 128
SLC_BLOCK = 64
N_SLC = SEQ // SLC_BLOCK
SLC_TOPN = 16
WIN_SIZE = 512
FORCED_SCORE = 1e3
REL_BUCKETS = 32
REL_MAX_DIST = 128

RET_HEADS = 4
RET_QK_DIM = 256
RET_V_DIM = 512
RET_QK_WIDTH = RET_HEADS * RET_QK_DIM
RET_WIDTH = RET_HEADS * RET_V_DIM
ROPE_BASE = 10000.0

LANES = 128
VMEM_LIMIT_BYTES = 56 * 1024 * 1024

PROJ_TM = 1024
RET_PROJ_TN = 2048
NSA_PROJ_TN = 3456
POST_TM = 512
ATT_TQ = 256
ATT_TK = 256
CMP_TQ = 1024
RET_CHUNK = 256

NSA_Q_COL = 0
NSA_Z_COL = 1024
NSA_K_COL = 4096
NSA_V_COL = 5120
NSA_CKV_COL = 6144
NSA_G_COL = 6656
NSA_PROJ_PAD = 6912

_NT = (((1,), (1,)), ((), ()))
_TN = (((0,), (0,)), ((), ()))


def _cparams(sem):
    return pltpu.CompilerParams(dimension_semantics=sem, vmem_limit_bytes=VMEM_LIMIT_BYTES)


def _norm_proj_kernel(x_ref, g_ref, w_ref, o_ref, xn_ref):
    @pl.when(pl.program_id(1) == 0)
    def _():
        x = x_ref[...]
        ms = jnp.mean(x * x, axis=-1, keepdims=True)
        xn_ref[...] = (x * lax.rsqrt(ms + RMS_EPS) * g_ref[...]).astype(BF16)

    o_ref[...] = jnp.dot(xn_ref[...], w_ref[...], preferred_element_type=F32).astype(o_ref.dtype)


def _norm_proj(x2d, gain, w_bf16, tn):
    m, d = x2d.shape
    n = w_bf16.shape[1]
    assert n % tn == 0
    return pl.pallas_call(
        _norm_proj_kernel,
        grid=(m // PROJ_TM, n // tn),
        in_specs=[
            pl.BlockSpec((PROJ_TM, d), lambda i, j: (i, 0)),
            pl.BlockSpec((1, d), lambda i, j: (0, 0)),
            pl.BlockSpec((d, tn), lambda i, j: (0, j)),
        ],
        out_specs=pl.BlockSpec((PROJ_TM, tn), lambda i, j: (i, j)),
        out_shape=jax.ShapeDtypeStruct((m, n), BF16),
        scratch_shapes=[pltpu.VMEM((PROJ_TM, d), BF16)],
        compiler_params=_cparams(("parallel", "arbitrary")),
        name="norm_proj",
    )(x2d, gain.reshape(1, d), w_bf16)


def _out_post_kernel(*refs, n_parts):
    y_refs = refs[:n_parts]
    w_ref, h_ref, g_ref, o_ref = refs[n_parts:]
    y = y_refs[0][...].astype(F32)
    for r in y_refs[1:]:
        y = y + r[...].astype(F32)
    t = jnp.dot(y.astype(BF16), w_ref[...], preferred_element_type=F32)
    ms = jnp.mean(t * t, axis=-1, keepdims=True)
    o_ref[...] = h_ref[...] + t * lax.rsqrt(ms + RMS_EPS) * g_ref[...]


def _out_post(parts, w_bf16, h2d, gain):
    m, d = h2d.shape
    k = w_bf16.shape[0]
    n_parts = len(parts)
    return pl.pallas_call(
        functools.partial(_out_post_kernel, n_parts=n_parts),
        grid=(m // POST_TM,),
        in_specs=[pl.BlockSpec((POST_TM, k), lambda i: (i, 0)) for _ in parts] + [
            pl.BlockSpec((k, d), lambda i: (0, 0)),
            pl.BlockSpec((POST_TM, d), lambda i: (i, 0)),
            pl.BlockSpec((1, d), lambda i: (0, 0)),
        ],
        out_specs=pl.BlockSpec((POST_TM, d), lambda i: (i, 0)),
        out_shape=jax.ShapeDtypeStruct((m, d), F32),
        compiler_params=_cparams(("parallel",)),
        name="out_post",
    )(*parts, w_bf16, h2d, gain.reshape(1, d))


def _compress_kernel(x_ref, pos_ref, w1_ref, w2_ref, o_ref):
    half = CMP_STRIDE * HEAD_DIM
    for which in range(2):
        x = x_ref[0, which, 0].astype(F32)
        xa = (x + pos_ref[which, 0:1, :]).astype(BF16)
        xb = (x + pos_ref[which, 1:2, :]).astype(BF16)
        pa = jnp.dot(xa, w1_ref[which, :half, :], preferred_element_type=F32)
        pb = jnp.dot(xb, w1_ref[which, half:, :], preferred_element_type=F32)
        hid = pa + pltpu.roll(pb, N_CMP_PAD - 1, 0)
        hid = hid * jax.nn.sigmoid(hid)
        o_ref[0, which, 0] = jnp.dot(hid.astype(BF16), w2_ref[which],
                                     preferred_element_type=F32).astype(o_ref.dtype)


def _compress(ckv_rows, pos, w1, w2d):
    b = ckv_rows.shape[0]
    row_w = CMP_STRIDE * HEAD_DIM
    return pl.pallas_call(
        _compress_kernel,
        grid=(b, GROUPS),
        in_specs=[
            pl.BlockSpec((1, 2, 1, N_CMP_PAD, row_w), lambda i, g: (i, 0, g, 0, 0)),
            pl.BlockSpec((2, 2, row_w), lambda i, g: (0, 0, 0)),
            pl.BlockSpec((2, 2 * row_w, CMP_HIDDEN), lambda i, g: (0, 0, 0)),
            pl.BlockSpec((2, CMP_HIDDEN, 2 * HEAD_DIM), lambda i, g: (0, 0, 0)),
        ],
        out_specs=pl.BlockSpec((1, 2, 1, N_CMP_PAD, 2 * HEAD_DIM), lambda i, g: (i, 0, g, 0, 0)),
        out_shape=jax.ShapeDtypeStruct((b, 2, GROUPS, N_CMP_PAD, 2 * HEAD_DIM), BF16),
        compiler_params=_cparams(("parallel", "parallel")),
        name="compress",
    )(ckv_rows, pos, w1, w2d)


def _lane_consts(rows):
    lane = np.arange(LANES)
    lo = (lane < HEAD_DIM).astype(np.float32)
    hi = (lane >= HEAD_DIM).astype(np.float32)
    last = (lane == LANES - 1).astype(np.float32)
    first = (lane == 0).astype(np.float32)
    c = np.stack([lo, hi, 1.0 - last, last, 1.0 - first, first])
    return jnp.asarray(np.broadcast_to(c[:, None, :], (6, rows, LANES)), BF16)


def _bf16_terms(x, n):
    terms = []
    for _ in range(n - 1):
        t = x.astype(BF16)
        terms.append(t)
        x = x - t.astype(F32)
    terms.append(x.astype(BF16))
    return terms


def _gates(gate_logits, expands):
    terms = _bf16_terms(jax.nn.sigmoid(gate_logits.astype(F32)), 2)
    return [sum(jnp.dot(t, ex, preferred_element_type=F32) for t in terms) for ex in expands]


def _silu(x):
    return x * jax.nn.sigmoid(x)


def _cmp_select_kernel(q_ref, gate_ref, z_ref, ckv_ref, bias_ref, ovl_ref, lc_ref, gexp_ref,
                       y_ref, sel_ref):
    g = pl.program_id(0)
    qi = pl.program_id(1)
    tq = q_ref.shape[1]
    q = q_ref[0]
    kk = ckv_ref[0, 0, 0]
    vv = ckv_ref[0, 1, 0]
    lane = lax.broadcasted_iota(jnp.int32, (tq, LANES), 1)

    psum = jnp.zeros((N_CMP_PAD, tq), F32)
    pairs = []
    for a in range(2):
        qp = q[:, a * LANES:(a + 1) * LANES]
        outs = []
        for e in range(2):
            s = lax.dot_general(kk, qp * lc_ref[e, :tq], _NT, preferred_element_type=F32)
            bias = bias_ref[2 * a + e]
            s = s + bias
            m = jnp.max(s, axis=0, keepdims=True)
            p = jnp.exp2(s - m)
            p = jnp.where(bias > 0.5 * MASKED, p / jnp.sum(p, axis=0, keepdims=True), 0.0)
            psum = psum + p
            outs.append(lax.dot_general(p.astype(BF16), vv, _TN, preferred_element_type=F32))
        pairs.append(jnp.where(lane < HEAD_DIM, outs[0], outs[1]))
    y_ref[0] = (jnp.concatenate(pairs, axis=1) * _gates(gate_ref[0], [gexp_ref[g]])[0]
                * _silu(z_ref[0].astype(F32))).astype(y_ref.dtype)

    ovl = ovl_ref[...]
    imp = sum(jnp.dot(ovl, t, preferred_element_type=F32) for t in _bf16_terms(psum, 3))
    t = qi * tq + lax.broadcasted_iota(jnp.int32, (N_SLC, tq), 1)
    blk = lax.broadcasted_iota(jnp.int32, (N_SLC, tq), 0)
    cur = lax.shift_right_logical(t, int(math.log2(SLC_BLOCK)))
    forced = (blk == 0) | (blk == cur) | (blk == cur - 1)
    score = jnp.where(blk * SLC_BLOCK <= t, imp + jnp.where(forced, FORCED_SCORE, 0.0), MASKED)
    cnt = jnp.zeros((N_SLC, tq), jnp.int32)
    for jp in range(N_SLC):
        row = score[jp:jp + 1, :]
        beats = (row > score) | ((row == score) & (blk > jp))
        cnt = cnt + beats.astype(jnp.int32)
    sel_ref[0, 0] = jnp.where(cnt < SLC_TOPN, 1.0, 0.0).astype(sel_ref.dtype)


def _cmp_select(proj, ckv, bias_cmp, ovl, lane_consts, gate_expand):
    b = proj.shape[0]
    tq = CMP_TQ
    grp_w = HEADS_PER_GROUP * HEAD_DIM
    return pl.pallas_call(
        _cmp_select_kernel,
        grid=(GROUPS, SEQ // tq, b),
        in_specs=[
            pl.BlockSpec((1, tq, grp_w), lambda g, qi, i: (i, qi, NSA_Q_COL // grp_w + g)),
            pl.BlockSpec((1, tq, LANES), lambda g, qi, i: (i, qi, NSA_G_COL // LANES)),
            pl.BlockSpec((1, tq, grp_w), lambda g, qi, i: (i, qi, NSA_Z_COL // grp_w + g)),
            pl.BlockSpec((1, 2, 1, N_CMP_PAD, 2 * HEAD_DIM), lambda g, qi, i: (i, 0, g, 0, 0)),
            pl.BlockSpec((HEADS_PER_GROUP, N_CMP_PAD, tq), lambda g, qi, i: (g, 0, qi)),
            pl.BlockSpec((N_SLC, N_CMP_PAD), lambda g, qi, i: (0, 0)),
            pl.BlockSpec(memory_space=pltpu.VMEM),
            pl.BlockSpec(memory_space=pltpu.VMEM),
        ],
        out_specs=[
            pl.BlockSpec((1, tq, grp_w), lambda g, qi, i: (i, qi, g)),
            pl.BlockSpec((1, 1, N_SLC, tq), lambda g, qi, i: (i, g, 0, qi)),
        ],
        out_shape=[
            jax.ShapeDtypeStruct((b, SEQ, NSA_WIDTH), BF16),
            jax.ShapeDtypeStruct((b, GROUPS, N_SLC, SEQ), BF16),
        ],
        compiler_params=_cparams(("parallel", "parallel", "parallel")),
        name="cmp_select",
    )(proj, proj, proj, ckv, bias_cmp, ovl, lane_consts, gate_expand)


def _token_kernel(q_ref, ks_ref, vs_ref, kw_ref, vw_ref, gate_ref, zs_ref, zw_ref, a_ref, lc_ref,
                  gexp_ref, sel_ref, kblk_ref, place_ref, placed_ref, dspread_ref, y_ref,
                  qm_ref, m_ref, acc_ref, s_bufs, p_bufs, c_bufs, kx_ref, vx_ref, gz_ref):
    g = pl.program_id(1)
    tq = ATT_TQ
    tk = ATT_TK

    def stream_len(qi):
        return min(qi, WIN_SIZE // tk) + 1 + qi + 1

    for c in range(SEQ // tk):
        rows = slice(c * tk, (c + 1) * tk)
        for branch, v_ref in enumerate((vs_ref, vw_ref)):
            vv = v_ref[0, rows, :]
            vx_ref[2 * branch, rows, :] = vv * lc_ref[2, :tk] + lc_ref[3, :tk]
            vx_ref[2 * branch + 1, rows, :] = vv * lc_ref[4, :tk] + lc_ref[5, :tk]
        kk = ks_ref[0, rows, :]
        for e in range(2):
            kx_ref[e, c] = (kk * lc_ref[e, :tk] + kblk_ref[e, c]).astype(F32).T.astype(BF16)
        kx_ref[2, c] = kw_ref[0, rows, :].astype(F32).T.astype(BF16)

    def tile(qi, j):
        n_win = min(qi, WIN_SIZE // tk) + 1
        win = int(j < n_win)
        t = j - (1 - win) * n_win
        return win, qi - t, t if (win or t < 2) else None

    def scores(qi, j, s_ref):
        if j >= stream_len(qi):
            return
        win, ki, _ = tile(qi, j)
        for e in range(2):
            pair = slice(2 * e * tq, 2 * (e + 1) * tq)
            s_ref[pair, :] = jnp.dot(qm_ref[win, pair, :], kx_ref[2 if win else e, ki],
                                     preferred_element_type=F32)

    def softmax(qi, j, s_ref, p_ref, c_ref):
        win, _, kind = tile(qi, j)
        for slot in range(HEADS_PER_GROUP):
            e, a = divmod(slot, 2)
            rows = slice(slot * tq, (slot + 1) * tq)
            s = s_ref[rows, :]
            if kind is not None:
                s = s + a_ref[kind, HEADS_PER_GROUP * g + 2 * a + e]
            m_prev = m_ref[win, rows, :]
            m_new = jnp.maximum(m_prev, jnp.max(s, axis=1, keepdims=True))
            c_ref[rows, :] = jnp.exp2(m_prev - m_new)
            x = s - jnp.concatenate([m_new] * (tk // LANES), axis=1)
            p_ref[rows, :] = jnp.exp2(x.astype(BF16))
            m_ref[win, rows, :] = m_new

    def values(qi_, par_, j, p_ref, c_ref):
        win, ki, _ = tile(qi_, j)
        rows = slice(ki * tk, (ki + 1) * tk)
        for e in range(2):
            pair = slice(2 * e * tq, 2 * (e + 1) * tq)
            pv = jnp.dot(p_ref[pair, :], vx_ref[2 * win + e, rows, :], preferred_element_type=F32)
            acc = acc_ref.at[2 * par_ + win]
            acc[pair, :] = c_ref[pair, :] * acc[pair, :] + pv

    lane = lax.broadcasted_iota(jnp.int32, (tq, LANES), 1)

    def gated_output(par_, branch):
        acc = acc_ref.at[2 * par_ + branch]
        pairs = []
        for a in range(2):
            acc_e = acc[a * tq:(a + 1) * tq, :]
            acc_o = acc[(2 + a) * tq:(3 + a) * tq, :]
            both = jnp.concatenate([acc_e, acc_o], axis=1)
            den = sum(jnp.dot(t, dspread_ref[...], preferred_element_type=F32)
                      for t in _bf16_terms(both, 2))
            pairs.append(jnp.where(lane < HEAD_DIM, acc_e, acc_o) / den)
        return jnp.concatenate(pairs, axis=1) * gz_ref[2 * par_ + branch]

    bufs = [(s_bufs.at[i], p_bufs.at[i], c_bufs.at[i]) for i in range(2)]

    def step(qi, half, j, parity):
        cur, nxt = bufs[parity], bufs[1 - parity]
        values(qi, half, j - 1, nxt[1], nxt[2])
        softmax(qi, j, *cur)
        scores(qi, j + 1, nxt[0])

    def enter(qi, slot):
        rows = slice(slot * tq, (slot + 1) * tq)
        half = slot % 2
        q = q_ref[0, rows, :]
        qms = []
        for hs in range(HEADS_PER_GROUP):
            e, a = divmod(hs, 2)
            qms.append(q[:, a * LANES:(a + 1) * LANES] * lc_ref[e, :tq])
            qm_ref[1, hs * tq:(hs + 1) * tq, :] = qms[hs]
        flags = lax.dot_general(sel_ref[0, 0, :, rows], place_ref[...], _TN,
                                preferred_element_type=F32)
        fill = ((placed_ref[...] - flags) * MASKED).astype(BF16)
        for hs in range(HEADS_PER_GROUP):
            e = hs // 2
            qm_ref[0, hs * tq:(hs + 1) * tq, :] = qms[hs] + fill[:, e * LANES:(e + 1) * LANES]
        gates = _gates(gate_ref[0, rows, :], [gexp_ref[0, g], gexp_ref[1, g]])
        for branch, z_ref in enumerate((zs_ref, zw_ref)):
            gz_ref[2 * half + branch] = gates[branch] * _silu(z_ref[0, rows, :].astype(F32))
        for branch in range(2):
            acc_ref[2 * half + branch] = jnp.zeros(acc_ref.shape[1:], F32)

    def first_scores(qi):
        m_ref[...] = jnp.full(m_ref.shape, -jnp.inf, F32)
        scores(qi, 0, bufs[0][0])
        scores(qi, 1, bufs[1][0])

    def write_output(slot, y_win):
        rows = slice(slot * tq, (slot + 1) * tq)
        y_ref[0, rows, :] = (y_win + gated_output(slot % 2, 0)).astype(y_ref.dtype)

    n_q = SEQ // tq
    enter(0, 0)
    first_scores(0)
    softmax(0, 0, *bufs[0])
    for qi in range(n_q):
        half = qi % 2
        n = stream_len(qi)
        for j in range(1, n - 1):
            step(qi, half, j, j % 2)
        if qi + 1 < n_q:
            enter(qi + 1, qi + 1)
        last = (n - 1) % 2
        values(qi, half, n - 2, bufs[1 - last][1], bufs[1 - last][2])
        softmax(qi, n - 1, *bufs[last])
        if qi + 1 < n_q:
            first_scores(qi + 1)
        y_win = gated_output(half, 1)
        values(qi, half, n - 1, bufs[last][1], bufs[last][2])
        if qi + 1 < n_q:
            softmax(qi + 1, 0, *bufs[0])
        write_output(qi, y_win)


def _token_attention(proj, a_tiles, lane_consts, gate_expand, sel, sel_consts):
    b = proj.shape[0]
    tq, tk = ATT_TQ, ATT_TK
    grp_w = HEADS_PER_GROUP * HEAD_DIM
    k_blk = NSA_K_COL // LANES
    v_blk = NSA_V_COL // LANES
    z_blk = NSA_Z_COL // grp_w
    kv_spec = lambda blk: pl.BlockSpec((1, SEQ, LANES), lambda i, g: (i, 0, blk + g))
    z_spec = lambda blk: pl.BlockSpec((1, SEQ, grp_w), lambda i, g: (i, 0, blk + g))
    whole = pl.BlockSpec(memory_space=pltpu.VMEM)
    return pl.pallas_call(
        _token_kernel,
        grid=(b, GROUPS),
        in_specs=[
            pl.BlockSpec((1, SEQ, grp_w), lambda i, g: (i, 0, NSA_Q_COL // grp_w + g)),
            kv_spec(k_blk), kv_spec(v_blk), kv_spec(k_blk + GROUPS), kv_spec(v_blk + GROUPS),
            pl.BlockSpec((1, SEQ, LANES), lambda i, g: (i, 0, NSA_G_COL // LANES)),
            z_spec(z_blk + GROUPS), z_spec(z_blk + 2 * GROUPS),
            whole, whole, whole,
            pl.BlockSpec((1, 1, N_SLC, SEQ), lambda i, g: (i, g, 0, 0)),
            whole, whole, whole, whole,
        ],
        out_specs=pl.BlockSpec((1, SEQ, grp_w), lambda i, g: (i, 0, g)),
        out_shape=jax.ShapeDtypeStruct((b, SEQ, NSA_WIDTH), BF16),
        scratch_shapes=[
            pltpu.VMEM((2, HEADS_PER_GROUP * tq, LANES), BF16),
            pltpu.VMEM((2, HEADS_PER_GROUP * tq, LANES), F32),
            pltpu.VMEM((4, HEADS_PER_GROUP * tq, LANES), F32),
            pltpu.VMEM((2, HEADS_PER_GROUP * tq, tk), F32),
            pltpu.VMEM((2, HEADS_PER_GROUP * tq, tk), BF16),
            pltpu.VMEM((2, HEADS_PER_GROUP * tq, LANES), F32),
            pltpu.VMEM((3, SEQ // tk, LANES, tk), BF16),
            pltpu.VMEM((4, SEQ, LANES), BF16),
            pltpu.VMEM((4, tq, grp_w), F32),
        ],
        compiler_params=_cparams(("parallel", "parallel")),
        name="token_attention",
    )(proj, proj, proj, proj, proj, proj, proj, proj, a_tiles, lane_consts, gate_expand,
      sel, *sel_consts)


def _retention_kernel(q_ref, k_ref, v_ref, z_ref, cos_ref, sin_ref, inner_ref, xi_ref, zeta_ref,
                      gn_ref, y_ref, state_ref, *, decays):
    @pl.when(pl.program_id(1) == 0)
    def _():
        state_ref[...] = jnp.zeros(state_ref.shape, F32)

    cos = cos_ref[...]
    sin = sin_ref[...]
    half = RET_QK_DIM // 2

    def rot(x):
        x1, x2 = x[:, :half], x[:, half:]
        return jnp.concatenate([x1 * cos - x2 * sin, x1 * sin + x2 * cos], axis=1)

    for h in range(RET_HEADS):
        qs = slice(h * RET_QK_DIM, (h + 1) * RET_QK_DIM)
        vs = slice(h * RET_V_DIM, (h + 1) * RET_V_DIM)
        qr = rot(q_ref[0, :, qs].astype(F32))
        kr = rot(k_ref[0, :, qs].astype(F32)) * (RET_QK_DIM ** -0.5)
        qb = qr.astype(BF16)
        vh = v_ref[0, :, vs]
        attn = lax.dot_general(qb, kr.astype(BF16), _NT, preferred_element_type=F32) * inner_ref[h]
        st = state_ref[h]
        o = (jnp.dot(attn.astype(BF16), vh, preferred_element_type=F32)
             + jnp.dot(qb, st.astype(BF16), preferred_element_type=F32) * xi_ref[h])
        kz = (kr * zeta_ref[h]).astype(BF16)
        state_ref[h] = st * decays[h] + lax.dot_general(kz, vh, _TN, preferred_element_type=F32)
        mu = jnp.mean(o, axis=1, keepdims=True)
        d = o - mu
        var = jnp.mean(d * d, axis=1, keepdims=True)
        on = d * lax.rsqrt(var + GN_EPS) * gn_ref[h]
        y_ref[0, :, vs] = (on * _silu(z_ref[0, :, vs].astype(F32))).astype(y_ref.dtype)


def _retention(proj, gn_gain, tables):
    b = proj.shape[0]
    c = RET_CHUNK
    cos, sin, inner, xi, zeta, decays = tables
    v_blk = 2 * RET_QK_WIDTH // RET_WIDTH
    return pl.pallas_call(
        functools.partial(_retention_kernel, decays=decays),
        grid=(b, SEQ // c),
        in_specs=[
            pl.BlockSpec((1, c, RET_QK_WIDTH), lambda i, j: (i, j, 0)),
            pl.BlockSpec((1, c, RET_QK_WIDTH), lambda i, j: (i, j, 1)),
            pl.BlockSpec((1, c, RET_WIDTH), lambda i, j: (i, j, v_blk)),
            pl.BlockSpec((1, c, RET_WIDTH), lambda i, j: (i, j, v_blk + 1)),
            pl.BlockSpec((c, RET_QK_DIM // 2), lambda i, j: (j, 0)),
            pl.BlockSpec((c, RET_QK_DIM // 2), lambda i, j: (j, 0)),
            pl.BlockSpec((RET_HEADS, c, c), lambda i, j: (0, 0, 0)),
            pl.BlockSpec((RET_HEADS, c, 1), lambda i, j: (0, 0, 0)),
            pl.BlockSpec((RET_HEADS, c, 1), lambda i, j: (0, 0, 0)),
            pl.BlockSpec((RET_HEADS, 1, RET_V_DIM), lambda i, j: (0, 0, 0)),
        ],
        out_specs=pl.BlockSpec((1, c, RET_WIDTH), lambda i, j: (i, j, 0)),
        out_shape=jax.ShapeDtypeStruct((b, SEQ, RET_WIDTH), BF16),
        scratch_shapes=[pltpu.VMEM((RET_HEADS, RET_QK_DIM, RET_V_DIM), F32)],
        compiler_params=_cparams(("parallel", "arbitrary")),
        name="retention",
    )(proj, proj, proj, proj, cos, sin, inner, xi, zeta, gn_gain.reshape(RET_HEADS, 1, RET_V_DIM))


def _t5_bucket_np(dist):
    n = np.maximum(dist, 0)
    max_exact = REL_BUCKETS // 2
    nf = np.maximum(n, 1).astype(np.float64)
    large = max_exact + (np.log(nf / max_exact) / math.log(REL_MAX_DIST / max_exact)
                         * (REL_BUCKETS - max_exact)).astype(np.int64)
    large = np.minimum(large, REL_BUCKETS - 1)
    return np.where(n < max_exact, n, large).astype(np.int32)


def _skewed(vec, rows, stride, cols):
    p = vec.shape[-1]
    lead = vec.shape[:-1]
    flat = jnp.tile(vec, (1,) * len(lead) + (rows,))[..., :rows * (p - stride)]
    return flat.reshape(lead + (rows, p - stride))[..., :cols]


def _nsa_tables(table):
    tq, tk = ATT_TQ, ATT_TK
    assert tq == tk
    tab_t = table.T * LOG2E
    period = 2 * SEQ
    x = np.arange(period)
    ok = (x >= CMP_BLOCK - 1) & (x < SEQ)
    vec = jnp.take(tab_t, jnp.asarray(_t5_bucket_np(x - (CMP_BLOCK - 1))), axis=1)
    vec = jnp.where(jnp.asarray(ok)[None, :], vec, MASKED)
    bias_cmp = _skewed(vec, N_CMP_PAD, CMP_STRIDE, SEQ)
    rel_t = tab_t - tab_t[:, REL_BUCKETS - 1:]
    period = 2 * tq
    x = np.arange(period)
    x = np.where(x < tq, x, x - period)
    tiles = []
    for off in range(3):
        d = off * tk - x
        ok = (d >= 0) & (d < WIN_SIZE) if off == 2 else (d >= 0)
        vec = jnp.take(rel_t, jnp.asarray(_t5_bucket_np(d)), axis=1)
        vec = jnp.where(jnp.asarray(ok)[None, :], vec, MASKED)
        tiles.append(_skewed(vec, tq, 1, tk))
    a_tiles = jnp.stack(tiles + [jnp.zeros_like(tiles[0])])
    cs = np.arange(N_CMP_PAD)[None, :] * CMP_STRIDE
    jj = np.arange(N_SLC)[:, None]
    ovl = ((cs < (jj + 1) * SLC_BLOCK) & (cs + CMP_BLOCK > jj * SLC_BLOCK)
           & (np.arange(N_CMP_PAD)[None, :] < N_CMP))
    ovl = jnp.asarray(ovl.astype(np.float32), BF16)
    lane = np.arange(LANES).reshape(1, 1, 1, LANES)
    key = np.arange(SEQ).reshape(1, SEQ // tk, tk, 1)
    base = np.array([HEAD_DIM, 0]).reshape(2, 1, 1, 1)
    key_blocks = jnp.asarray((lane == base + key // SLC_BLOCK).astype(np.float32), BF16)
    col = np.arange(2 * LANES).reshape(1, -1)
    jb = np.arange(N_SLC).reshape(-1, 1)
    place = (col == HEAD_DIM + jb) | (col == LANES + jb)
    row = np.arange(2 * LANES).reshape(-1, 1)
    out = np.arange(LANES).reshape(1, -1)
    den_spread = ((row == LANES - 1) & (out < HEAD_DIM)) | ((row == LANES) & (out >= HEAD_DIM))
    sel_consts = (key_blocks, jnp.asarray(place.astype(np.float32), BF16),
                  jnp.asarray(place.any(axis=0, keepdims=True).astype(np.float32)),
                  jnp.asarray(den_spread.astype(np.float32), BF16))
    c = np.arange(LANES).reshape(1, 1, LANES, 1)
    col = np.arange(HEADS_PER_GROUP * HEAD_DIM).reshape(1, 1, 1, -1) // HEAD_DIM
    br = np.arange(3).reshape(3, 1, 1, 1)
    gg = np.arange(GROUPS).reshape(1, GROUPS, 1, 1)
    gate_expand = jnp.asarray((c == br * HEADS + HEADS_PER_GROUP * gg + col).astype(np.float32),
                              BF16)
    return bias_cmp, a_tiles, ovl, sel_consts, _lane_consts(max(tq, CMP_TQ)), gate_expand


def _retention_tables():
    c = RET_CHUNK
    log_g = jnp.log(1.0 - 2.0 ** (-5.0 - jnp.arange(RET_HEADS, dtype=F32)))
    i = jnp.arange(c, dtype=F32)
    diff = i[:, None] - i[None, :]
    inner = jnp.where(diff >= 0, jnp.exp(diff[None] * log_g[:, None, None]), 0.0)
    xi = jnp.exp((i + 1.0)[None, :] * log_g[:, None])[:, :, None]
    zeta = jnp.exp((c - 1.0 - i)[None, :] * log_g[:, None])[:, :, None]
    decays = tuple(float((1.0 - 2.0 ** (-5.0 - h)) ** c) for h in range(RET_HEADS))
    inv = 1.0 / (ROPE_BASE ** jnp.linspace(0.0, 1.0, RET_QK_DIM // 2, dtype=F32))
    ang = jnp.arange(SEQ, dtype=F32)[:, None] * inv[None, :]
    return jnp.cos(ang), jnp.sin(ang), inner, xi, zeta, decays


def _nsa_w_in_layout(w):
    d = w.shape[0]
    q = w[:, :NSA_WIDTH] * (HEAD_DIM ** -0.5 * LOG2E)
    kv0 = NSA_WIDTH
    g0 = kv0 + 6 * KV_WIDTH
    z0 = g0 + 3 * HEADS
    slab = lambda n: w[:, kv0 + n * KV_WIDTH: kv0 + (n + 1) * KV_WIDTH]
    cols = [q, w[:, z0:z0 + 3 * NSA_WIDTH]]
    for n in (2, 4, 3, 5):
        for g in range(GROUPS):
            part = slab(n)[:, g * HEAD_DIM:(g + 1) * HEAD_DIM]
            cols += [part, part]
    cols += [slab(0), slab(1), w[:, g0:z0]]
    used = NSA_G_COL + 3 * HEADS
    cols.append(jnp.zeros((d, NSA_PROJ_PAD - used), w.dtype))
    return jnp.concatenate(cols, axis=1).astype(BF16)


def _nsa_layer(h2d, b, pre_gain, post_gain, w_in, w_out, k_pos, k_w1, k_w2, v_pos, v_w1, v_w2, tabs):
    bias_cmp, a_tiles, ovl, sel_consts, lane_consts, gate_expand = tabs
    proj = _norm_proj(h2d, pre_gain, _nsa_w_in_layout(w_in), NSA_PROJ_TN)
    proj = proj.reshape(b, SEQ, NSA_PROJ_PAD)
    ckv = proj[:, :, NSA_CKV_COL:NSA_CKV_COL + 2 * KV_WIDTH]
    ckv = ckv.reshape(b, N_CMP_PAD, CMP_STRIDE, 2, GROUPS, HEAD_DIM).transpose(0, 3, 4, 1, 2, 5)
    ckv = ckv.reshape(b, 2, GROUPS, N_CMP_PAD, CMP_STRIDE * HEAD_DIM)
    pos = jnp.stack([k_pos, v_pos]).reshape(2, 2, CMP_STRIDE * HEAD_DIM)
    w1 = jnp.stack([k_w1, v_w1]).astype(BF16)
    w2 = jnp.stack([k_w2, v_w2])
    w2d = jnp.concatenate([w2, w2], axis=2).astype(BF16)
    ckv_c = _compress(ckv, pos, w1, w2d)
    y_cmp, sel = _cmp_select(proj, ckv_c, bias_cmp, ovl, lane_consts, gate_expand[0])
    y_tok = _token_attention(proj, a_tiles, lane_consts, gate_expand[1:], sel, sel_consts)
    parts = [y.reshape(b * SEQ, NSA_WIDTH) for y in (y_cmp, y_tok)]
    return _out_post(parts, w_out.astype(BF16), h2d, post_gain)


def _ret_layer(h2d, b, pre_gain, post_gain, w_in, w_out, gn_gain, tabs):
    proj = _norm_proj(h2d, pre_gain, w_in.astype(BF16), RET_PROJ_TN).reshape(b, SEQ, -1)
    y = _retention(proj, gn_gain, tabs)
    return _out_post([y.reshape(b * SEQ, RET_WIDTH)], w_out.astype(BF16), h2d, post_gain)


def kernel(x, pre_norm_gain, post_norm_gain, rel_bias_table, nsa_w_in, nsa_w_out, nsa_cmp_k_pos, nsa_cmp_k_w1, nsa_cmp_k_w2, nsa_cmp_v_pos, nsa_cmp_v_w1, nsa_cmp_v_w2, ret_w_in, ret_w_out, ret_gn_gain):
    b, s, d = x.shape
    assert s == SEQ and d == D_MODEL
    nsa_tabs = _nsa_tables(rel_bias_table)
    ret_tabs = _retention_tables()
    h = x.reshape(b * s, d)
    for layer in range(DEPTH):
        slot = layer // 2
        if layer % 2 == 0:
            h = _nsa_layer(h, b, pre_norm_gain[layer], post_norm_gain[layer], nsa_w_in[slot],
                           nsa_w_out[slot], nsa_cmp_k_pos[slot], nsa_cmp_k_w1[slot],
                           nsa_cmp_k_w2[slot], nsa_cmp_v_pos[slot], nsa_cmp_v_w1[slot],
                           nsa_cmp_v_w2[slot], nsa_tabs)
        else:
            h = _ret_layer(h, b, pre_norm_gain[layer], post_norm_gain[layer], ret_w_in[slot],
                           ret_w_out[slot], ret_gn_gain[slot], ret_tabs)
    return h.reshape(b, s, d)
```

```python
import functools
import math

import numpy as np
import jax
import jax.numpy as jnp
from jax import lax
from jax.experimental import pallas as pl
from jax.experimental.pallas import tpu as pltpu

F32 = jnp.float32
BF16 = jnp.bfloat16

D_MODEL = 1024
SEQ = 2048
DEPTH = 4
RMS_EPS = 1e-6
GN_EPS = 1e-6
MASKED = -1e30
LOG2E = math.log2(math.e)

HEADS = 16
HEAD_DIM = 64
GROUPS = 4
HEADS_PER_GROUP = HEADS // GROUPS
NSA_WIDTH = HEADS * HEAD_DIM
KV_WIDTH = GROUPS * HEAD_DIM
CMP_BLOCK = 32
CMP_STRIDE = 16
CMP_HIDDEN = 256
N_CMP = (SEQ - CMP_BLOCK) // CMP_STRIDE + 1
N_CMP_PAD = 128
SLC_BLOCK = 64
N_SLC = SEQ // SLC_BLOCK
SLC_TOPN = 16
WIN_SIZE = 512
FORCED_SCORE = 1e3
REL_BUCKETS = 32
REL_MAX_DIST = 128

RET_HEADS = 4
RET_QK_DIM = 256
RET_V_DIM = 512
RET_QK_WIDTH = RET_HEADS * RET_QK_DIM
RET_WIDTH = RET_HEADS * RET_V_DIM
ROPE_BASE = 10000.0

LANES = 128
VMEM_LIMIT_BYTES = 56 * 1024 * 1024

PROJ_TM = 1024
RET_PROJ_TN = 2048
NSA_PROJ_TN = 3456
POST_TM = 512
ATT_TQ = 256
ATT_TK = 256
CMP_TQ = 1024
RET_CHUNK = 256
RET_CHUNKS_PER_STEP = 4

NSA_Q_COL = 0
NSA_Z_COL = 1024
NSA_K_COL = 4096
NSA_V_COL = 5120
NSA_CKV_COL = 6144
NSA_G_COL = 6656
NSA_PROJ_PAD = 6912

_NT = (((1,), (1,)), ((), ()))
_TN = (((0,), (0,)), ((), ()))


def _cparams(sem):
    return pltpu.CompilerParams(dimension_semantics=sem, vmem_limit_bytes=VMEM_LIMIT_BYTES)


def _norm_proj_kernel(x_ref, g_ref, w_ref, o_ref, xn_ref):
    @pl.when(pl.program_id(1) == 0)
    def _():
        x = x_ref[...]
        ms = jnp.mean(x * x, axis=-1, keepdims=True)
        xn_ref[...] = (x * lax.rsqrt(ms + RMS_EPS) * g_ref[...]).astype(BF16)

    o_ref[...] = jnp.dot(xn_ref[...], w_ref[...], preferred_element_type=F32).astype(o_ref.dtype)


def _norm_proj(x2d, gain, w_bf16, tn):
    m, d = x2d.shape
    n = w_bf16.shape[1]
    assert n % tn == 0
    return pl.pallas_call(
        _norm_proj_kernel,
        grid=(m // PROJ_TM, n // tn),
        in_specs=[
            pl.BlockSpec((PROJ_TM, d), lambda i, j: (i, 0)),
            pl.BlockSpec((1, d), lambda i, j: (0, 0)),
            pl.BlockSpec((d, tn), lambda i, j: (0, j)),
        ],
        out_specs=pl.BlockSpec((PROJ_TM, tn), lambda i, j: (i, j)),
        out_shape=jax.ShapeDtypeStruct((m, n), BF16),
        scratch_shapes=[pltpu.VMEM((PROJ_TM, d), BF16)],
        compiler_params=_cparams(("parallel", "arbitrary")),
        name="norm_proj",
    )(x2d, gain.reshape(1, d), w_bf16)


def _out_post_kernel(*refs, n_parts):
    y_refs = refs[:n_parts]
    w_ref, h_ref, g_ref, o_ref = refs[n_parts:]
    y = y_refs[0][...].astype(F32)
    for r in y_refs[1:]:
        y = y + r[...].astype(F32)
    t = jnp.dot(y.astype(BF16), w_ref[...], preferred_element_type=F32)
    ms = jnp.mean(t * t, axis=-1, keepdims=True)
    o_ref[...] = h_ref[...] + t * lax.rsqrt(ms + RMS_EPS) * g_ref[...]


def _out_post(parts, w_bf16, h2d, gain):
    m, d = h2d.shape
    k = w_bf16.shape[0]
    n_parts = len(parts)
    return pl.pallas_call(
        functools.partial(_out_post_kernel, n_parts=n_parts),
        grid=(m // POST_TM,),
        in_specs=[pl.BlockSpec((POST_TM, k), lambda i: (i, 0)) for _ in parts] + [
            pl.BlockSpec((k, d), lambda i: (0, 0)),
            pl.BlockSpec((POST_TM, d), lambda i: (i, 0)),
            pl.BlockSpec((1, d), lambda i: (0, 0)),
        ],
        out_specs=pl.BlockSpec((POST_TM, d), lambda i: (i, 0)),
        out_shape=jax.ShapeDtypeStruct((m, d), F32),
        compiler_params=_cparams(("parallel",)),
        name="out_post",
    )(*parts, w_bf16, h2d, gain.reshape(1, d))


def _compress_kernel(x_ref, pos_ref, w1_ref, w2_ref, o_ref):
    half = CMP_STRIDE * HEAD_DIM
    for which in range(2):
        x = x_ref[0, which, 0].astype(F32)
        xa = (x + pos_ref[which, 0:1, :]).astype(BF16)
        xb = (x + pos_ref[which, 1:2, :]).astype(BF16)
        pa = jnp.dot(xa, w1_ref[which, :half, :], preferred_element_type=F32)
        pb = jnp.dot(xb, w1_ref[which, half:, :], preferred_element_type=F32)
        hid = pa + pltpu.roll(pb, N_CMP_PAD - 1, 0)
        hid = hid * jax.nn.sigmoid(hid)
        o_ref[0, which, 0] = jnp.dot(hid.astype(BF16), w2_ref[which],
                                     preferred_element_type=F32).astype(o_ref.dtype)


def _compress(ckv_rows, pos, w1, w2d):
    b = ckv_rows.shape[0]
    row_w = CMP_STRIDE * HEAD_DIM
    return pl.pallas_call(
        _compress_kernel,
        grid=(b, GROUPS),
        in_specs=[
            pl.BlockSpec((1, 2, 1, N_CMP_PAD, row_w), lambda i, g: (i, 0, g, 0, 0)),
            pl.BlockSpec((2, 2, row_w), lambda i, g: (0, 0, 0)),
            pl.BlockSpec((2, 2 * row_w, CMP_HIDDEN), lambda i, g: (0, 0, 0)),
            pl.BlockSpec((2, CMP_HIDDEN, 2 * HEAD_DIM), lambda i, g: (0, 0, 0)),
        ],
        out_specs=pl.BlockSpec((1, 2, 1, N_CMP_PAD, 2 * HEAD_DIM), lambda i, g: (i, 0, g, 0, 0)),
        out_shape=jax.ShapeDtypeStruct((b, 2, GROUPS, N_CMP_PAD, 2 * HEAD_DIM), BF16),
        compiler_params=_cparams(("parallel", "parallel")),
        name="compress",
    )(ckv_rows, pos, w1, w2d)


def _lane_consts(rows):
    lane = np.arange(LANES)
    lo = (lane < HEAD_DIM).astype(np.float32)
    hi = (lane >= HEAD_DIM).astype(np.float32)
    last = (lane == LANES - 1).astype(np.float32)
    first = (lane == 0).astype(np.float32)
    c = np.stack([lo, hi, 1.0 - last, last, 1.0 - first, first])
    return jnp.asarray(np.broadcast_to(c[:, None, :], (6, rows, LANES)), BF16)


def _bf16_terms(x, n):
    terms = []
    for _ in range(n - 1):
        t = x.astype(BF16)
        terms.append(t)
        x = x - t.astype(F32)
    terms.append(x.astype(BF16))
    return terms


def _gates(gate_logits, expands):
    terms = _bf16_terms(jax.nn.sigmoid(gate_logits.astype(F32)), 2)
    return [sum(jnp.dot(t, ex, preferred_element_type=F32) for t in terms) for ex in expands]


def _silu(x):
    return x * jax.nn.sigmoid(x)


def _cmp_select_kernel(q_ref, gate_ref, z_ref, ckv_ref, bias_ref, ovl_ref, lc_ref, gexp_ref,
                       y_ref, sel_ref):
    g = pl.program_id(0)
    qi = pl.program_id(1)
    tq = q_ref.shape[1]
    q = q_ref[0]
    kk = ckv_ref[0, 0, 0]
    vv = ckv_ref[0, 1, 0]
    lane = lax.broadcasted_iota(jnp.int32, (tq, LANES), 1)

    psum = jnp.zeros((N_CMP_PAD, tq), F32)
    pairs = []
    for a in range(2):
        qp = q[:, a * LANES:(a + 1) * LANES]
        outs = []
        for e in range(2):
            s = lax.dot_general(kk, qp * lc_ref[e, :tq], _NT, preferred_element_type=F32)
            bias = bias_ref[2 * a + e]
            s = s + bias
            m = jnp.max(s, axis=0, keepdims=True)
            p = jnp.exp2(s - m)
            p = jnp.where(bias > 0.5 * MASKED, p / jnp.sum(p, axis=0, keepdims=True), 0.0)
            psum = psum + p
            outs.append(lax.dot_general(p.astype(BF16), vv, _TN, preferred_element_type=F32))
        pairs.append(jnp.where(lane < HEAD_DIM, outs[0], outs[1]))
    y_ref[0] = (jnp.concatenate(pairs, axis=1) * _gates(gate_ref[0], [gexp_ref[g]])[0]
                * _silu(z_ref[0].astype(F32))).astype(y_ref.dtype)

    ovl = ovl_ref[...]
    imp = sum(jnp.dot(ovl, t, preferred_element_type=F32) for t in _bf16_terms(psum, 3))
    t = qi * tq + lax.broadcasted_iota(jnp.int32, (N_SLC, tq), 1)
    blk = lax.broadcasted_iota(jnp.int32, (N_SLC, tq), 0)
    cur = lax.shift_right_logical(t, int(math.log2(SLC_BLOCK)))
    forced = (blk == 0) | (blk == cur) | (blk == cur - 1)
    score = jnp.where(blk * SLC_BLOCK <= t, imp + jnp.where(forced, FORCED_SCORE, 0.0), MASKED)
    cnt = jnp.zeros((N_SLC, tq), jnp.int32)
    for jp in range(N_SLC):
        row = score[jp:jp + 1, :]
        beats = (row > score) | ((row == score) & (blk > jp))
        cnt = cnt + beats.astype(jnp.int32)
    sel_ref[0, 0] = jnp.where(cnt < SLC_TOPN, 1.0, 0.0).astype(sel_ref.dtype)


def _cmp_select(proj, ckv, bias_cmp, ovl, lane_consts, gate_expand):
    b = proj.shape[0]
    tq = CMP_TQ
    grp_w = HEADS_PER_GROUP * HEAD_DIM
    return pl.pallas_call(
        _cmp_select_kernel,
        grid=(GROUPS, SEQ // tq, b),
        in_specs=[
            pl.BlockSpec((1, tq, grp_w), lambda g, qi, i: (i, qi, NSA_Q_COL // grp_w + g)),
            pl.BlockSpec((1, tq, LANES), lambda g, qi, i: (i, qi, NSA_G_COL // LANES)),
            pl.BlockSpec((1, tq, grp_w), lambda g, qi, i: (i, qi, NSA_Z_COL // grp_w + g)),
            pl.BlockSpec((1, 2, 1, N_CMP_PAD, 2 * HEAD_DIM), lambda g, qi, i: (i, 0, g, 0, 0)),
            pl.BlockSpec((HEADS_PER_GROUP, N_CMP_PAD, tq), lambda g, qi, i: (g, 0, qi)),
            pl.BlockSpec((N_SLC, N_CMP_PAD), lambda g, qi, i: (0, 0)),
            pl.BlockSpec(memory_space=pltpu.VMEM),
            pl.BlockSpec(memory_space=pltpu.VMEM),
        ],
        out_specs=[
            pl.BlockSpec((1, tq, grp_w), lambda g, qi, i: (i, qi, g)),
            pl.BlockSpec((1, 1, N_SLC, tq), lambda g, qi, i: (i, g, 0, qi)),
        ],
        out_shape=[
            jax.ShapeDtypeStruct((b, SEQ, NSA_WIDTH), BF16),
            jax.ShapeDtypeStruct((b, GROUPS, N_SLC, SEQ), BF16),
        ],
        compiler_params=_cparams(("parallel", "parallel", "parallel")),
        name="cmp_select",
    )(proj, proj, proj, ckv, bias_cmp, ovl, lane_consts, gate_expand)


def _token_kernel(q_ref, ks_ref, vs_ref, kw_ref, vw_ref, gate_ref, zs_ref, zw_ref, a_ref, lc_ref,
                  gexp_ref, sel_ref, kblk_ref, place_ref, placed_ref, dspread_ref, y_ref,
                  qm_ref, m_ref, acc_ref, s_bufs, p_bufs, c_bufs, kx_ref, vx_ref, gz_ref):
    g = pl.program_id(1)
    tq = ATT_TQ
    tk = ATT_TK

    def stream_len(qi):
        return min(qi, WIN_SIZE // tk) + 1 + qi + 1

    for c in range(SEQ // tk):
        rows = slice(c * tk, (c + 1) * tk)
        for branch, v_ref in enumerate((vs_ref, vw_ref)):
            vv = v_ref[0, rows, :]
            vx_ref[2 * branch, rows, :] = vv * lc_ref[2, :tk] + lc_ref[3, :tk]
            vx_ref[2 * branch + 1, rows, :] = vv * lc_ref[4, :tk] + lc_ref[5, :tk]
        kk = ks_ref[0, rows, :]
        for e in range(2):
            kx_ref[e, c] = (kk * lc_ref[e, :tk] + kblk_ref[e, c]).astype(F32).T.astype(BF16)
        kx_ref[2, c] = kw_ref[0, rows, :].astype(F32).T.astype(BF16)

    def tile(qi, j):
        n_win = min(qi, WIN_SIZE // tk) + 1
        win = int(j < n_win)
        t = j - (1 - win) * n_win
        return win, qi - t, t if (win or t < 2) else None

    def scores(qi, j, s_ref):
        if j >= stream_len(qi):
            return
        win, ki, _ = tile(qi, j)
        for e in range(2):
            pair = slice(2 * e * tq, 2 * (e + 1) * tq)
            s_ref[pair, :] = jnp.dot(qm_ref[win, pair, :], kx_ref[2 if win else e, ki],
                                     preferred_element_type=F32)

    def softmax(qi, j, s_ref, p_ref, c_ref):
        win, _, kind = tile(qi, j)
        for slot in range(HEADS_PER_GROUP):
            e, a = divmod(slot, 2)
            rows = slice(slot * tq, (slot + 1) * tq)
            s = s_ref[rows, :]
            if kind is not None:
                s = s + a_ref[kind, HEADS_PER_GROUP * g + 2 * a + e]
            m_prev = m_ref[win, rows, :]
            m_new = jnp.maximum(m_prev, jnp.max(s, axis=1, keepdims=True))
            c_ref[rows, :] = jnp.exp2(m_prev - m_new)
            x = s - jnp.concatenate([m_new] * (tk // LANES), axis=1)
            p_ref[rows, :] = jnp.exp2(x.astype(BF16))
            m_ref[win, rows, :] = m_new

    def values(qi_, par_, j, p_ref, c_ref):
        win, ki, _ = tile(qi_, j)
        rows = slice(ki * tk, (ki + 1) * tk)
        for e in range(2):
            pair = slice(2 * e * tq, 2 * (e + 1) * tq)
            pv = jnp.dot(p_ref[pair, :], vx_ref[2 * win + e, rows, :], preferred_element_type=F32)
            acc = acc_ref.at[2 * par_ + win]
            acc[pair, :] = c_ref[pair, :] * acc[pair, :] + pv

    lane = lax.broadcasted_iota(jnp.int32, (tq, LANES), 1)

    def gated_output(par_, branch):
        acc = acc_ref.at[2 * par_ + branch]
        pairs = []
        for a in range(2):
            acc_e = acc[a * tq:(a + 1) * tq, :]
            acc_o = acc[(2 + a) * tq:(3 + a) * tq, :]
            both = jnp.concatenate([acc_e, acc_o], axis=1)
            den = sum(jnp.dot(t, dspread_ref[...], preferred_element_type=F32)
                      for t in _bf16_terms(both, 2))
            pairs.append(jnp.where(lane < HEAD_DIM, acc_e, acc_o) / den)
        return jnp.concatenate(pairs, axis=1) * gz_ref[2 * par_ + branch]

    bufs = [(s_bufs.at[i], p_bufs.at[i], c_bufs.at[i]) for i in range(2)]

    def step(qi, half, j, parity):
        cur, nxt = bufs[parity], bufs[1 - parity]
        values(qi, half, j - 1, nxt[1], nxt[2])
        softmax(qi, j, *cur)
        scores(qi, j + 1, nxt[0])

    def enter(qi, slot):
        rows = slice(slot * tq, (slot + 1) * tq)
        half = slot % 2
        q = q_ref[0, rows, :]
        qms = []
        for hs in range(HEADS_PER_GROUP):
            e, a = divmod(hs, 2)
            qms.append(q[:, a * LANES:(a + 1) * LANES] * lc_ref[e, :tq])
            qm_ref[1, hs * tq:(hs + 1) * tq, :] = qms[hs]
        flags = lax.dot_general(sel_ref[0, 0, :, rows], place_ref[...], _TN,
                                preferred_element_type=F32)
        fill = ((placed_ref[...] - flags) * MASKED).astype(BF16)
        for hs in range(HEADS_PER_GROUP):
            e = hs // 2
            qm_ref[0, hs * tq:(hs + 1) * tq, :] = qms[hs] + fill[:, e * LANES:(e + 1) * LANES]
        gates = _gates(gate_ref[0, rows, :], [gexp_ref[0, g], gexp_ref[1, g]])
        for branch, z_ref in enumerate((zs_ref, zw_ref)):
            gz_ref[2 * half + branch] = gates[branch] * _silu(z_ref[0, rows, :].astype(F32))
        m_ref[...] = jnp.full(m_ref.shape, -jnp.inf, F32)
        for branch in range(2):
            acc_ref[2 * half + branch] = jnp.zeros(acc_ref.shape[1:], F32)
        scores(qi, 0, bufs[0][0])

    def fill_pipeline(qi, half):
        softmax(qi, 0, *bufs[0])
        scores(qi, 1, bufs[1][0])
        step(qi, half, 1, 1)

    def run_stream(qi, half):
        for j in range(2, stream_len(qi)):
            step(qi, half, j, j % 2)

    def leave(qi, half):
        y_win = gated_output(half, 1)
        last_pos = stream_len(qi) - 1
        values(qi, half, last_pos, bufs[last_pos % 2][1], bufs[last_pos % 2][2])
        return y_win

    def write_output(slot, y_win):
        rows = slice(slot * tq, (slot + 1) * tq)
        y_ref[0, rows, :] = (y_win + gated_output(slot % 2, 0)).astype(y_ref.dtype)

    y_win = None
    for qi in range(SEQ // tq):
        enter(qi, qi)
        if qi > 0:
            write_output(qi - 1, y_win)
        fill_pipeline(qi, qi % 2)
        run_stream(qi, qi % 2)
        y_win = leave(qi, qi % 2)
    write_output(SEQ // tq - 1, y_win)


def _token_attention(proj, a_tiles, lane_consts, gate_expand, sel, sel_consts):
    b = proj.shape[0]
    tq, tk = ATT_TQ, ATT_TK
    grp_w = HEADS_PER_GROUP * HEAD_DIM
    k_blk = NSA_K_COL // LANES
    v_blk = NSA_V_COL // LANES
    z_blk = NSA_Z_COL // grp_w
    kv_spec = lambda blk: pl.BlockSpec((1, SEQ, LANES), lambda i, g: (i, 0, blk + g))
    z_spec = lambda blk: pl.BlockSpec((1, SEQ, grp_w), lambda i, g: (i, 0, blk + g))
    whole = pl.BlockSpec(memory_space=pltpu.VMEM)
    return pl.pallas_call(
        _token_kernel,
        grid=(b, GROUPS),
        in_specs=[
            pl.BlockSpec((1, SEQ, grp_w), lambda i, g: (i, 0, NSA_Q_COL // grp_w + g)),
            kv_spec(k_blk), kv_spec(v_blk), kv_spec(k_blk + GROUPS), kv_spec(v_blk + GROUPS),
            pl.BlockSpec((1, SEQ, LANES), lambda i, g: (i, 0, NSA_G_COL // LANES)),
            z_spec(z_blk + GROUPS), z_spec(z_blk + 2 * GROUPS),
            whole, whole, whole,
            pl.BlockSpec((1, 1, N_SLC, SEQ), lambda i, g: (i, g, 0, 0)),
            whole, whole, whole, whole,
        ],
        out_specs=pl.BlockSpec((1, SEQ, grp_w), lambda i, g: (i, 0, g)),
        out_shape=jax.ShapeDtypeStruct((b, SEQ, NSA_WIDTH), BF16),
        scratch_shapes=[
            pltpu.VMEM((2, HEADS_PER_GROUP * tq, LANES), BF16),
            pltpu.VMEM((2, HEADS_PER_GROUP * tq, LANES), F32),
            pltpu.VMEM((4, HEADS_PER_GROUP * tq, LANES), F32),
            pltpu.VMEM((2, HEADS_PER_GROUP * tq, tk), F32),
            pltpu.VMEM((2, HEADS_PER_GROUP * tq, tk), BF16),
            pltpu.VMEM((2, HEADS_PER_GROUP * tq, LANES), F32),
            pltpu.VMEM((3, SEQ // tk, LANES, tk), BF16),
            pltpu.VMEM((4, SEQ, LANES), BF16),
            pltpu.VMEM((4, tq, grp_w), F32),
        ],
        compiler_params=_cparams(("parallel", "parallel")),
        name="token_attention",
    )(proj, proj, proj, proj, proj, proj, proj, proj, a_tiles, lane_consts, gate_expand,
      sel, *sel_consts)


def _retention_kernel(q_ref, k_ref, v_ref, z_ref, cos_ref, sin_ref, inner_ref, xi_ref, zeta_ref,
                      gn_ref, y_ref, state_ref, *, decays):
    @pl.when(pl.program_id(1) == 0)
    def _():
        state_ref[...] = jnp.zeros(state_ref.shape, F32)

    half = RET_QK_DIM // 2
    c = RET_CHUNK

    for sub in range(RET_CHUNKS_PER_STEP):
        rows = slice(sub * c, (sub + 1) * c)
        cos = cos_ref[rows, :]
        sin = sin_ref[rows, :]

        def rot(x):
            x1, x2 = x[:, :half], x[:, half:]
            return jnp.concatenate([x1 * cos - x2 * sin, x1 * sin + x2 * cos], axis=1)

        for h in range(RET_HEADS):
            qs = slice(h * RET_QK_DIM, (h + 1) * RET_QK_DIM)
            vs = slice(h * RET_V_DIM, (h + 1) * RET_V_DIM)
            qr = rot(q_ref[0, rows, qs].astype(F32))
            kr = rot(k_ref[0, rows, qs].astype(F32)) * (RET_QK_DIM ** -0.5)
            qb = qr.astype(BF16)
            vh = v_ref[0, rows, vs]
            attn = (lax.dot_general(qb, kr.astype(BF16), _NT, preferred_element_type=F32)
                    * inner_ref[h])
            st = state_ref[h]
            o = (jnp.dot(attn.astype(BF16), vh, preferred_element_type=F32)
                 + jnp.dot(qb, st.astype(BF16), preferred_element_type=F32) * xi_ref[h])
            kz = (kr * zeta_ref[h]).astype(BF16)
            state_ref[h] = (st * decays[h]
                            + lax.dot_general(kz, vh, _TN, preferred_element_type=F32))
            mu = jnp.mean(o, axis=1, keepdims=True)
            d = o - mu
            var = jnp.mean(d * d, axis=1, keepdims=True)
            on = d * lax.rsqrt(var + GN_EPS) * gn_ref[h]
            y_ref[0, rows, vs] = (on * _silu(z_ref[0, rows, vs].astype(F32))).astype(y_ref.dtype)


def _retention(proj, gn_gain, tables):
    b = proj.shape[0]
    c = RET_CHUNK
    rows = RET_CHUNKS_PER_STEP * c
    cos, sin, inner, xi, zeta, decays = tables
    v_blk = 2 * RET_QK_WIDTH // RET_WIDTH
    return pl.pallas_call(
        functools.partial(_retention_kernel, decays=decays),
        grid=(b, SEQ // rows),
        in_specs=[
            pl.BlockSpec((1, rows, RET_QK_WIDTH), lambda i, j: (i, j, 0)),
            pl.BlockSpec((1, rows, RET_QK_WIDTH), lambda i, j: (i, j, 1)),
            pl.BlockSpec((1, rows, RET_WIDTH), lambda i, j: (i, j, v_blk)),
            pl.BlockSpec((1, rows, RET_WIDTH), lambda i, j: (i, j, v_blk + 1)),
            pl.BlockSpec((rows, RET_QK_DIM // 2), lambda i, j: (j, 0)),
            pl.BlockSpec((rows, RET_QK_DIM // 2), lambda i, j: (j, 0)),
            pl.BlockSpec((RET_HEADS, c, c), lambda i, j: (0, 0, 0)),
            pl.BlockSpec((RET_HEADS, c, 1), lambda i, j: (0, 0, 0)),
            pl.BlockSpec((RET_HEADS, c, 1), lambda i, j: (0, 0, 0)),
            pl.BlockSpec((RET_HEADS, 1, RET_V_DIM), lambda i, j: (0, 0, 0)),
        ],
        out_specs=pl.BlockSpec((1, rows, RET_WIDTH), lambda i, j: (i, j, 0)),
        out_shape=jax.ShapeDtypeStruct((b, SEQ, RET_WIDTH), BF16),
        scratch_shapes=[pltpu.VMEM((RET_HEADS, RET_QK_DIM, RET_V_DIM), F32)],
        compiler_params=_cparams(("parallel", "arbitrary")),
        name="retention",
    )(proj, proj, proj, proj, cos, sin, inner, xi, zeta, gn_gain.reshape(RET_HEADS, 1, RET_V_DIM))


def _t5_bucket_np(dist):
    n = np.maximum(dist, 0)
    max_exact = REL_BUCKETS // 2
    nf = np.maximum(n, 1).astype(np.float64)
    large = max_exact + (np.log(nf / max_exact) / math.log(REL_MAX_DIST / max_exact)
                         * (REL_BUCKETS - max_exact)).astype(np.int64)
    large = np.minimum(large, REL_BUCKETS - 1)
    return np.where(n < max_exact, n, large).astype(np.int32)


def _skewed(vec, rows, stride, cols):
    p = vec.shape[-1]
    lead = vec.shape[:-1]
    flat = jnp.tile(vec, (1,) * len(lead) + (rows,))[..., :rows * (p - stride)]
    return flat.reshape(lead + (rows, p - stride))[..., :cols]


def _nsa_tables(table):
    tq, tk = ATT_TQ, ATT_TK
    assert tq == tk
    tab_t = table.T * LOG2E
    period = 2 * SEQ
    x = np.arange(period)
    ok = (x >= CMP_BLOCK - 1) & (x < SEQ)
    vec = jnp.take(tab_t, jnp.asarray(_t5_bucket_np(x - (CMP_BLOCK - 1))), axis=1)
    vec = jnp.where(jnp.asarray(ok)[None, :], vec, MASKED)
    bias_cmp = _skewed(vec, N_CMP_PAD, CMP_STRIDE, SEQ)
    rel_t = tab_t - tab_t[:, REL_BUCKETS - 1:]
    period = 2 * tq
    x = np.arange(period)
    x = np.where(x < tq, x, x - period)
    tiles = []
    for off in range(3):
        d = off * tk - x
        ok = (d >= 0) & (d < WIN_SIZE) if off == 2 else (d >= 0)
        vec = jnp.take(rel_t, jnp.asarray(_t5_bucket_np(d)), axis=1)
        vec = jnp.where(jnp.asarray(ok)[None, :], vec, MASKED)
        tiles.append(_skewed(vec, tq, 1, tk))
    a_tiles = jnp.stack(tiles + [jnp.zeros_like(tiles[0])])
    cs = np.arange(N_CMP_PAD)[None, :] * CMP_STRIDE
    jj = np.arange(N_SLC)[:, None]
    ovl = ((cs < (jj + 1) * SLC_BLOCK) & (cs + CMP_BLOCK > jj * SLC_BLOCK)
           & (np.arange(N_CMP_PAD)[None, :] < N_CMP))
    ovl = jnp.asarray(ovl.astype(np.float32), BF16)
    lane = np.arange(LANES).reshape(1, 1, 1, LANES)
    key = np.arange(SEQ).reshape(1, SEQ // tk, tk, 1)
    base = np.array([HEAD_DIM, 0]).reshape(2, 1, 1, 1)
    key_blocks = jnp.asarray((lane == base + key // SLC_BLOCK).astype(np.float32), BF16)
    col = np.arange(2 * LANES).reshape(1, -1)
    jb = np.arange(N_SLC).reshape(-1, 1)
    place = (col == HEAD_DIM + jb) | (col == LANES + jb)
    row = np.arange(2 * LANES).reshape(-1, 1)
    out = np.arange(LANES).reshape(1, -1)
    den_spread = ((row == LANES - 1) & (out < HEAD_DIM)) | ((row == LANES) & (out >= HEAD_DIM))
    sel_consts = (key_blocks, jnp.asarray(place.astype(np.float32), BF16),
                  jnp.asarray(place.any(axis=0, keepdims=True).astype(np.float32)),
                  jnp.asarray(den_spread.astype(np.float32), BF16))
    c = np.arange(LANES).reshape(1, 1, LANES, 1)
    col = np.arange(HEADS_PER_GROUP * HEAD_DIM).reshape(1, 1, 1, -1) // HEAD_DIM
    br = np.arange(3).reshape(3, 1, 1, 1)
    gg = np.arange(GROUPS).reshape(1, GROUPS, 1, 1)
    gate_expand = jnp.asarray((c == br * HEADS + HEADS_PER_GROUP * gg + col).astype(np.float32),
                              BF16)
    return bias_cmp, a_tiles, ovl, sel_consts, _lane_consts(max(tq, CMP_TQ)), gate_expand


def _retention_tables():
    c = RET_CHUNK
    log_g = jnp.log(1.0 - 2.0 ** (-5.0 - jnp.arange(RET_HEADS, dtype=F32)))
    i = jnp.arange(c, dtype=F32)
    diff = i[:, None] - i[None, :]
    inner = jnp.where(diff >= 0, jnp.exp(diff[None] * log_g[:, None, None]), 0.0)
    xi = jnp.exp((i + 1.0)[None, :] * log_g[:, None])[:, :, None]
    zeta = jnp.exp((c - 1.0 - i)[None, :] * log_g[:, None])[:, :, None]
    decays = tuple(float((1.0 - 2.0 ** (-5.0 - h)) ** c) for h in range(RET_HEADS))
    inv = 1.0 / (ROPE_BASE ** jnp.linspace(0.0, 1.0, RET_QK_DIM // 2, dtype=F32))
    ang = jnp.arange(SEQ, dtype=F32)[:, None] * inv[None, :]
    return jnp.cos(ang), jnp.sin(ang), inner, xi, zeta, decays


def _nsa_w_in_layout(w):
    d = w.shape[0]
    q = w[:, :NSA_WIDTH] * (HEAD_DIM ** -0.5 * LOG2E)
    kv0 = NSA_WIDTH
    g0 = kv0 + 6 * KV_WIDTH
    z0 = g0 + 3 * HEADS
    slab = lambda n: w[:, kv0 + n * KV_WIDTH: kv0 + (n + 1) * KV_WIDTH]
    cols = [q, w[:, z0:z0 + 3 * NSA_WIDTH]]
    for n in (2, 4, 3, 5):
        for g in range(GROUPS):
            part = slab(n)[:, g * HEAD_DIM:(g + 1) * HEAD_DIM]
            cols += [part, part]
    cols += [slab(0), slab(1), w[:, g0:z0]]
    used = NSA_G_COL + 3 * HEADS
    cols.append(jnp.zeros((d, NSA_PROJ_PAD - used), w.dtype))
    return jnp.concatenate(cols, axis=1).astype(BF16)


def _nsa_layer(h2d, b, pre_gain, post_gain, w_in, w_out, k_pos, k_w1, k_w2, v_pos, v_w1, v_w2, tabs):
    bias_cmp, a_tiles, ovl, sel_consts, lane_consts, gate_expand = tabs
    proj = _norm_proj(h2d, pre_gain, _nsa_w_in_layout(w_in), NSA_PROJ_TN)
    proj = proj.reshape(b, SEQ, NSA_PROJ_PAD)
    ckv = proj[:, :, NSA_CKV_COL:NSA_CKV_COL + 2 * KV_WIDTH]
    ckv = ckv.reshape(b, N_CMP_PAD, CMP_STRIDE, 2, GROUPS, HEAD_DIM).transpose(0, 3, 4, 1, 2, 5)
    ckv = ckv.reshape(b, 2, GROUPS, N_CMP_PAD, CMP_STRIDE * HEAD_DIM)
    pos = jnp.stack([k_pos, v_pos]).reshape(2, 2, CMP_STRIDE * HEAD_DIM)
    w1 = jnp.stack([k_w1, v_w1]).astype(BF16)
    w2 = jnp.stack([k_w2, v_w2])
    w2d = jnp.concatenate([w2, w2], axis=2).astype(BF16)
    ckv_c = _compress(ckv, pos, w1, w2d)
    y_cmp, sel = _cmp_select(proj, ckv_c, bias_cmp, ovl, lane_consts, gate_expand[0])
    y_tok = _token_attention(proj, a_tiles, lane_consts, gate_expand[1:], sel, sel_consts)
    parts = [y.reshape(b * SEQ, NSA_WIDTH) for y in (y_cmp, y_tok)]
    return _out_post(parts, w_out.astype(BF16), h2d, post_gain)


def _ret_layer(h2d, b, pre_gain, post_gain, w_in, w_out, gn_gain, tabs):
    proj = _norm_proj(h2d, pre_gain, w_in.astype(BF16), RET_PROJ_TN).reshape(b, SEQ, -1)
    y = _retention(proj, gn_gain, tabs)
    return _out_post([y.reshape(b * SEQ, RET_WIDTH)], w_out.astype(BF16), h2d, post_gain)


def kernel(x, pre_norm_gain, post_norm_gain, rel_bias_table, nsa_w_in, nsa_w_out, nsa_cmp_k_pos, nsa_cmp_k_w1, nsa_cmp_k_w2, nsa_cmp_v_pos, nsa_cmp_v_w1, nsa_cmp_v_w2, ret_w_in, ret_w_out, ret_gn_gain):
    b, s, d = x.shape
    assert s == SEQ and d == D_MODEL
    nsa_tabs = _nsa_tables(rel_bias_table)
    ret_tabs = _retention_tables()
    h = x.reshape(b * s, d)
    for layer in range(DEPTH):
        slot = layer // 2
        if layer % 2 == 0:
            h = _nsa_layer(h, b, pre_norm_gain[layer], post_norm_gain[layer], nsa_w_in[slot],
                           nsa_w_out[slot], nsa_cmp_k_pos[slot], nsa_cmp_k_w1[slot],
                           nsa_cmp_k_w2[slot], nsa_cmp_v_pos[slot], nsa_cmp_v_w1[slot],
                           nsa_cmp_v_w2[slot], nsa_tabs)
        else:
            h = _ret_layer(h, b, pre_norm_gain[layer], post_norm_gain[layer], ret_w_in[slot],
                           ret_w_out[slot], ret_gn_gain[slot], ret_tabs)
    return h.reshape(b, s, d)
```

```python
import functools
import math

import numpy as np
import jax
import jax.numpy as jnp
from jax import lax
from jax.experimental import pallas as pl
from jax.experimental.pallas import tpu as pltpu

F32 = jnp.float32
BF16 = jnp.bfloat16

D_MODEL = 1024
SEQ = 2048
DEPTH = 4
RMS_EPS = 1e-6
GN_EPS = 1e-6
MASKED = -1e30
LOG2E = math.log2(math.e)

HEADS = 16
HEAD_DIM = 64
GROUPS = 4
HEADS_PER_GROUP = HEADS // GROUPS
NSA_WIDTH = HEADS * HEAD_DIM
KV_WIDTH = GROUPS * HEAD_DIM
CMP_BLOCK = 32
CMP_STRIDE = 16
CMP_HIDDEN = 256
N_CMP = (SEQ - CMP_BLOCK) // CMP_STRIDE + 1
N_CMP_PAD = 128
SLC_BLOCK = 64
N_SLC = SEQ // SLC_BLOCK
SLC_TOPN = 16
WIN_SIZE = 512
FORCED_SCORE = 1e3
REL_BUCKETS = 32
REL_MAX_DIST = 128

RET_HEADS = 4
RET_QK_DIM = 256
RET_V_DIM = 512
RET_QK_WIDTH = RET_HEADS * RET_QK_DIM
RET_WIDTH = RET_HEADS * RET_V_DIM
ROPE_BASE = 10000.0

LANES = 128
VMEM_LIMIT_BYTES = 56 * 1024 * 1024

PROJ_TM = 1024
RET_PROJ_TN = 2048
NSA_PROJ_TN = 3456
POST_TM = 512
ATT_TQ = 256
ATT_TK = 256
CMP_TQ = 1024
RET_CHUNK = 256

NSA_Q_COL = 0
NSA_Z_COL = 1024
NSA_K_COL = 4096
NSA_V_COL = 5120
NSA_CKV_COL = 6144
NSA_G_COL = 6656
NSA_PROJ_PAD = 6912

_NT = (((1,), (1,)), ((), ()))
_TN = (((0,), (0,)), ((), ()))


def _cparams(sem):
    return pltpu.CompilerParams(dimension_semantics=sem, vmem_limit_bytes=VMEM_LIMIT_BYTES)


def _norm_proj_kernel(x_ref, g_ref, w_ref, o_ref, xn_ref):
    @pl.when(pl.program_id(1) == 0)
    def _():
        x = x_ref[...]
        ms = jnp.mean(x * x, axis=-1, keepdims=True)
        xn_ref[...] = (x * lax.rsqrt(ms + RMS_EPS) * g_ref[...]).astype(BF16)

    o_ref[...] = jnp.dot(xn_ref[...], w_ref[...], preferred_element_type=F32).astype(o_ref.dtype)


def _norm_proj(x2d, gain, w_bf16, tn):
    m, d = x2d.shape
    n = w_bf16.shape[1]
    assert n % tn == 0
    return pl.pallas_call(
        _norm_proj_kernel,
        grid=(m // PROJ_TM, n // tn),
        in_specs=[
            pl.BlockSpec((PROJ_TM, d), lambda i, j: (i, 0)),
            pl.BlockSpec((1, d), lambda i, j: (0, 0)),
            pl.BlockSpec((d, tn), lambda i, j: (0, j)),
        ],
        out_specs=pl.BlockSpec((PROJ_TM, tn), lambda i, j: (i, j)),
        out_shape=jax.ShapeDtypeStruct((m, n), BF16),
        scratch_shapes=[pltpu.VMEM((PROJ_TM, d), BF16)],
        compiler_params=_cparams(("parallel", "arbitrary")),
        name="norm_proj",
    )(x2d, gain.reshape(1, d), w_bf16)


def _out_post_kernel(*refs, n_parts):
    y_refs = refs[:n_parts]
    w_ref, h_ref, g_ref, o_ref = refs[n_parts:]
    y = y_refs[0][...].astype(F32)
    for r in y_refs[1:]:
        y = y + r[...].astype(F32)
    t = jnp.dot(y.astype(BF16), w_ref[...], preferred_element_type=F32)
    ms = jnp.mean(t * t, axis=-1, keepdims=True)
    o_ref[...] = h_ref[...] + t * lax.rsqrt(ms + RMS_EPS) * g_ref[...]


def _out_post(parts, w_bf16, h2d, gain):
    m, d = h2d.shape
    k = w_bf16.shape[0]
    n_parts = len(parts)
    return pl.pallas_call(
        functools.partial(_out_post_kernel, n_parts=n_parts),
        grid=(m // POST_TM,),
        in_specs=[pl.BlockSpec((POST_TM, k), lambda i: (i, 0)) for _ in parts] + [
            pl.BlockSpec((k, d), lambda i: (0, 0)),
            pl.BlockSpec((POST_TM, d), lambda i: (i, 0)),
            pl.BlockSpec((1, d), lambda i: (0, 0)),
        ],
        out_specs=pl.BlockSpec((POST_TM, d), lambda i: (i, 0)),
        out_shape=jax.ShapeDtypeStruct((m, d), F32),
        compiler_params=_cparams(("parallel",)),
        name="out_post",
    )(*parts, w_bf16, h2d, gain.reshape(1, d))


def _compress_kernel(x_ref, pos_ref, w1_ref, w2_ref, o_ref):
    half = CMP_STRIDE * HEAD_DIM
    for which in range(2):
        x = x_ref[0, which, 0].astype(F32)
        xa = (x + pos_ref[which, 0:1, :]).astype(BF16)
        xb = (x + pos_ref[which, 1:2, :]).astype(BF16)
        pa = jnp.dot(xa, w1_ref[which, :half, :], preferred_element_type=F32)
        pb = jnp.dot(xb, w1_ref[which, half:, :], preferred_element_type=F32)
        hid = pa + pltpu.roll(pb, N_CMP_PAD - 1, 0)
        hid = hid * jax.nn.sigmoid(hid)
        o_ref[0, which, 0] = jnp.dot(hid.astype(BF16), w2_ref[which],
                                     preferred_element_type=F32).astype(o_ref.dtype)


def _compress(ckv_rows, pos, w1, w2d):
    b = ckv_rows.shape[0]
    row_w = CMP_STRIDE * HEAD_DIM
    return pl.pallas_call(
        _compress_kernel,
        grid=(b, GROUPS),
        in_specs=[
            pl.BlockSpec((1, 2, 1, N_CMP_PAD, row_w), lambda i, g: (i, 0, g, 0, 0)),
            pl.BlockSpec((2, 2, row_w), lambda i, g: (0, 0, 0)),
            pl.BlockSpec((2, 2 * row_w, CMP_HIDDEN), lambda i, g: (0, 0, 0)),
            pl.BlockSpec((2, CMP_HIDDEN, 2 * HEAD_DIM), lambda i, g: (0, 0, 0)),
        ],
        out_specs=pl.BlockSpec((1, 2, 1, N_CMP_PAD, 2 * HEAD_DIM), lambda i, g: (i, 0, g, 0, 0)),
        out_shape=jax.ShapeDtypeStruct((b, 2, GROUPS, N_CMP_PAD, 2 * HEAD_DIM), BF16),
        compiler_params=_cparams(("parallel", "parallel")),
        name="compress",
    )(ckv_rows, pos, w1, w2d)


def _lane_consts(rows):
    lane = np.arange(LANES)
    lo = (lane < HEAD_DIM).astype(np.float32)
    hi = (lane >= HEAD_DIM).astype(np.float32)
    last = (lane == LANES - 1).astype(np.float32)
    first = (lane == 0).astype(np.float32)
    c = np.stack([lo, hi, 1.0 - last, last, 1.0 - first, first])
    return jnp.asarray(np.broadcast_to(c[:, None, :], (6, rows, LANES)), BF16)


def _bf16_terms(x, n):
    terms = []
    for _ in range(n - 1):
        t = x.astype(BF16)
        terms.append(t)
        x = x - t.astype(F32)
    terms.append(x.astype(BF16))
    return terms


def _gates(gate_logits, expands):
    terms = _bf16_terms(jax.nn.sigmoid(gate_logits.astype(F32)), 2)
    return [sum(jnp.dot(t, ex, preferred_element_type=F32) for t in terms) for ex in expands]


def _silu(x):
    return x * jax.nn.sigmoid(x)


def _cmp_select_kernel(q_ref, gate_ref, z_ref, ckv_ref, bias_ref, ovl_ref, lc_ref, gexp_ref,
                       y_ref, sel_ref):
    g = pl.program_id(0)
    qi = pl.program_id(1)
    tq = q_ref.shape[1]
    q = q_ref[0]
    kk = ckv_ref[0, 0, 0]
    vv = ckv_ref[0, 1, 0]
    lane = lax.broadcasted_iota(jnp.int32, (tq, LANES), 1)

    psum = jnp.zeros((N_CMP_PAD, tq), F32)
    pairs = []
    for a in range(2):
        qp = q[:, a * LANES:(a + 1) * LANES]
        outs = []
        for e in range(2):
            s = lax.dot_general(kk, qp * lc_ref[e, :tq], _NT, preferred_element_type=F32)
            bias = bias_ref[2 * a + e]
            s = s + bias
            m = jnp.max(s, axis=0, keepdims=True)
            p = jnp.exp2(s - m)
            p = jnp.where(bias > 0.5 * MASKED, p / jnp.sum(p, axis=0, keepdims=True), 0.0)
            psum = psum + p
            outs.append(lax.dot_general(p.astype(BF16), vv, _TN, preferred_element_type=F32))
        pairs.append(jnp.where(lane < HEAD_DIM, outs[0], outs[1]))
    y_ref[0] = (jnp.concatenate(pairs, axis=1) * _gates(gate_ref[0], [gexp_ref[g]])[0]
                * _silu(z_ref[0].astype(F32))).astype(y_ref.dtype)

    ovl = ovl_ref[...]
    imp = sum(jnp.dot(ovl, t, preferred_element_type=F32) for t in _bf16_terms(psum, 3))
    t = qi * tq + lax.broadcasted_iota(jnp.int32, (N_SLC, tq), 1)
    blk = lax.broadcasted_iota(jnp.int32, (N_SLC, tq), 0)
    cur = lax.shift_right_logical(t, int(math.log2(SLC_BLOCK)))
    forced = (blk == 0) | (blk == cur) | (blk == cur - 1)
    score = jnp.where(blk * SLC_BLOCK <= t, imp + jnp.where(forced, FORCED_SCORE, 0.0), MASKED)
    cnt = jnp.zeros((N_SLC, tq), jnp.int32)
    for jp in range(N_SLC):
        row = score[jp:jp + 1, :]
        beats = (row > score) | ((row == score) & (blk > jp))
        cnt = cnt + beats.astype(jnp.int32)
    sel_ref[0, 0] = jnp.where(cnt < SLC_TOPN, 1.0, 0.0).astype(sel_ref.dtype)


def _cmp_select(proj, ckv, bias_cmp, ovl, lane_consts, gate_expand):
    b = proj.shape[0]
    tq = CMP_TQ
    grp_w = HEADS_PER_GROUP * HEAD_DIM
    return pl.pallas_call(
        _cmp_select_kernel,
        grid=(GROUPS, SEQ // tq, b),
        in_specs=[
            pl.BlockSpec((1, tq, grp_w), lambda g, qi, i: (i, qi, NSA_Q_COL // grp_w + g)),
            pl.BlockSpec((1, tq, LANES), lambda g, qi, i: (i, qi, NSA_G_COL // LANES)),
            pl.BlockSpec((1, tq, grp_w), lambda g, qi, i: (i, qi, NSA_Z_COL // grp_w + g)),
            pl.BlockSpec((1, 2, 1, N_CMP_PAD, 2 * HEAD_DIM), lambda g, qi, i: (i, 0, g, 0, 0)),
            pl.BlockSpec((HEADS_PER_GROUP, N_CMP_PAD, tq), lambda g, qi, i: (g, 0, qi)),
            pl.BlockSpec((N_SLC, N_CMP_PAD), lambda g, qi, i: (0, 0)),
            pl.BlockSpec(memory_space=pltpu.VMEM),
            pl.BlockSpec(memory_space=pltpu.VMEM),
        ],
        out_specs=[
            pl.BlockSpec((1, tq, grp_w), lambda g, qi, i: (i, qi, g)),
            pl.BlockSpec((1, 1, N_SLC, tq), lambda g, qi, i: (i, g, 0, qi)),
        ],
        out_shape=[
            jax.ShapeDtypeStruct((b, SEQ, NSA_WIDTH), BF16),
            jax.ShapeDtypeStruct((b, GROUPS, N_SLC, SEQ), BF16),
        ],
        compiler_params=_cparams(("parallel", "parallel", "parallel")),
        name="cmp_select",
    )(proj, proj, proj, ckv, bias_cmp, ovl, lane_consts, gate_expand)


def _token_kernel(q_ref, ks_ref, vs_ref, kw_ref, vw_ref, gate_ref, zs_ref, zw_ref, a_ref, lc_ref,
                  gexp_ref, sel_ref, kblk_ref, place_ref, placed_ref, dspread_ref, y_ref,
                  qm_ref, m_ref, acc_ref, s_bufs, p_bufs, c_bufs, kx_ref, vx_ref, gz_ref):
    g = pl.program_id(1)
    tq = ATT_TQ
    tk = ATT_TK

    def stream_len(qi):
        return min(qi, WIN_SIZE // tk) + 1 + qi + 1

    for c in range(SEQ // tk):
        rows = slice(c * tk, (c + 1) * tk)
        for branch, v_ref in enumerate((vs_ref, vw_ref)):
            vv = v_ref[0, rows, :]
            vx_ref[2 * branch, rows, :] = vv * lc_ref[2, :tk] + lc_ref[3, :tk]
            vx_ref[2 * branch + 1, rows, :] = vv * lc_ref[4, :tk] + lc_ref[5, :tk]
        kk = ks_ref[0, rows, :]
        for e in range(2):
            kx_ref[e, c] = (kk * lc_ref[e, :tk] + kblk_ref[e, c]).astype(F32).T.astype(BF16)
        kx_ref[2, c] = kw_ref[0, rows, :].astype(F32).T.astype(BF16)

    def tile(qi, j):
        n_win = min(qi, WIN_SIZE // tk) + 1
        win = int(j < n_win)
        t = j - (1 - win) * n_win
        return win, qi - t, t if (win or t < 2) else None

    def scores(qi, j, s_ref):
        if j >= stream_len(qi):
            return
        win, ki, _ = tile(qi, j)
        for e in range(2):
            pair = slice(2 * e * tq, 2 * (e + 1) * tq)
            s_ref[pair, :] = jnp.dot(qm_ref[win, pair, :], kx_ref[2 if win else e, ki],
                                     preferred_element_type=F32)

    def softmax(qi, j, s_ref, p_ref, c_ref):
        win, _, kind = tile(qi, j)
        for slot in range(HEADS_PER_GROUP):
            e, a = divmod(slot, 2)
            rows = slice(slot * tq, (slot + 1) * tq)
            s = s_ref[rows, :]
            if kind is not None:
                s = s + a_ref[kind, HEADS_PER_GROUP * g + 2 * a + e]
            m_prev = m_ref[win, rows, :]
            m_new = jnp.maximum(m_prev, jnp.max(s, axis=1, keepdims=True))
            c_ref[rows, :] = jnp.exp2(m_prev - m_new)
            x = s - jnp.concatenate([m_new] * (tk // LANES), axis=1)
            p_ref[rows, :] = jnp.exp2(x.astype(BF16))
            m_ref[win, rows, :] = m_new

    def values(qi_, par_, j, p_ref, c_ref):
        win, ki, _ = tile(qi_, j)
        rows = slice(ki * tk, (ki + 1) * tk)
        for e in range(2):
            pair = slice(2 * e * tq, 2 * (e + 1) * tq)
            pv = jnp.dot(p_ref[pair, :], vx_ref[2 * win + e, rows, :], preferred_element_type=F32)
            acc = acc_ref.at[2 * par_ + win]
            acc[pair, :] = c_ref[pair, :] * acc[pair, :] + pv

    lane = lax.broadcasted_iota(jnp.int32, (tq, LANES), 1)

    def gated_output(par_, branch):
        acc = acc_ref.at[2 * par_ + branch]
        pairs = []
        for a in range(2):
            acc_e = acc[a * tq:(a + 1) * tq, :]
            acc_o = acc[(2 + a) * tq:(3 + a) * tq, :]
            both = jnp.concatenate([acc_e, acc_o], axis=1)
            den = sum(jnp.dot(t, dspread_ref[...], preferred_element_type=F32)
                      for t in _bf16_terms(both, 2))
            pairs.append(jnp.where(lane < HEAD_DIM, acc_e, acc_o) / den)
        return jnp.concatenate(pairs, axis=1) * gz_ref[2 * par_ + branch]

    bufs = [(s_bufs.at[i], p_bufs.at[i], c_bufs.at[i]) for i in range(2)]

    def step(qi, half, j, parity):
        cur, nxt = bufs[parity], bufs[1 - parity]
        values(qi, half, j - 1, nxt[1], nxt[2])
        softmax(qi, j, *cur)
        scores(qi, j + 1, nxt[0])

    def enter(qi, slot):
        rows = slice(slot * tq, (slot + 1) * tq)
        half = slot % 2
        q = q_ref[0, rows, :]
        qms = []
        for hs in range(HEADS_PER_GROUP):
            e, a = divmod(hs, 2)
            qms.append(q[:, a * LANES:(a + 1) * LANES] * lc_ref[e, :tq])
            qm_ref[1, hs * tq:(hs + 1) * tq, :] = qms[hs]
        flags = lax.dot_general(sel_ref[0, 0, :, rows], place_ref[...], _TN,
                                preferred_element_type=F32)
        fill = ((placed_ref[...] - flags) * MASKED).astype(BF16)
        for hs in range(HEADS_PER_GROUP):
            e = hs // 2
            qm_ref[0, hs * tq:(hs + 1) * tq, :] = qms[hs] + fill[:, e * LANES:(e + 1) * LANES]
        gates = _gates(gate_ref[0, rows, :], [gexp_ref[0, g], gexp_ref[1, g]])
        for branch, z_ref in enumerate((zs_ref, zw_ref)):
            gz_ref[2 * half + branch] = gates[branch] * _silu(z_ref[0, rows, :].astype(F32))
        m_ref[...] = jnp.full(m_ref.shape, -jnp.inf, F32)
        for branch in range(2):
            acc_ref[2 * half + branch] = jnp.zeros(acc_ref.shape[1:], F32)
        scores(qi, 0, bufs[0][0])

    def fill_pipeline(qi, half):
        softmax(qi, 0, *bufs[0])
        scores(qi, 1, bufs[1][0])
        step(qi, half, 1, 1)

    def run_stream(qi, half):
        for j in range(2, stream_len(qi)):
            step(qi, half, j, j % 2)

    def leave(qi, half):
        y_win = gated_output(half, 1)
        last_pos = stream_len(qi) - 1
        values(qi, half, last_pos, bufs[last_pos % 2][1], bufs[last_pos % 2][2])
        return y_win

    def write_output(slot, y_win):
        rows = slice(slot * tq, (slot + 1) * tq)
        y_ref[0, rows, :] = (y_win + gated_output(slot % 2, 0)).astype(y_ref.dtype)

    y_win = None
    for qi in range(SEQ // tq):
        enter(qi, qi)
        if qi > 0:
            write_output(qi - 1, y_win)
        fill_pipeline(qi, qi % 2)
        run_stream(qi, qi % 2)
        y_win = leave(qi, qi % 2)
    write_output(SEQ // tq - 1, y_win)


def _token_attention(proj, a_tiles, lane_consts, gate_expand, sel, sel_consts):
    b = proj.shape[0]
    tq, tk = ATT_TQ, ATT_TK
    grp_w = HEADS_PER_GROUP * HEAD_DIM
    k_blk = NSA_K_COL // LANES
    v_blk = NSA_V_COL // LANES
    z_blk = NSA_Z_COL // grp_w
    kv_spec = lambda blk: pl.BlockSpec((1, SEQ, LANES), lambda i, g: (i, 0, blk + g))
    z_spec = lambda blk: pl.BlockSpec((1, SEQ, grp_w), lambda i, g: (i, 0, blk + g))
    whole = pl.BlockSpec(memory_space=pltpu.VMEM)
    return pl.pallas_call(
        _token_kernel,
        grid=(b, GROUPS),
        in_specs=[
            pl.BlockSpec((1, SEQ, grp_w), lambda i, g: (i, 0, NSA_Q_COL // grp_w + g)),
            kv_spec(k_blk), kv_spec(v_blk), kv_spec(k_blk + GROUPS), kv_spec(v_blk + GROUPS),
            pl.BlockSpec((1, SEQ, LANES), lambda i, g: (i, 0, NSA_G_COL // LANES)),
            z_spec(z_blk + GROUPS), z_spec(z_blk + 2 * GROUPS),
            whole, whole, whole,
            pl.BlockSpec((1, 1, N_SLC, SEQ), lambda i, g: (i, g, 0, 0)),
            whole, whole, whole, whole,
        ],
        out_specs=pl.BlockSpec((1, SEQ, grp_w), lambda i, g: (i, 0, g)),
        out_shape=jax.ShapeDtypeStruct((b, SEQ, NSA_WIDTH), BF16),
        scratch_shapes=[
            pltpu.VMEM((2, HEADS_PER_GROUP * tq, LANES), BF16),
            pltpu.VMEM((2, HEADS_PER_GROUP * tq, LANES), F32),
            pltpu.VMEM((4, HEADS_PER_GROUP * tq, LANES), F32),
            pltpu.VMEM((2, HEADS_PER_GROUP * tq, tk), F32),
            pltpu.VMEM((2, HEADS_PER_GROUP * tq, tk), BF16),
            pltpu.VMEM((2, HEADS_PER_GROUP * tq, LANES), F32),
            pltpu.VMEM((3, SEQ // tk, LANES, tk), BF16),
            pltpu.VMEM((4, SEQ, LANES), BF16),
            pltpu.VMEM((4, tq, grp_w), F32),
        ],
        compiler_params=_cparams(("parallel", "parallel")),
        name="token_attention",
    )(proj, proj, proj, proj, proj, proj, proj, proj, a_tiles, lane_consts, gate_expand,
      sel, *sel_consts)


def _retention_kernel(q_ref, k_ref, v_ref, z_ref, cos_ref, sin_ref, inner_ref, xi_ref, zeta_ref,
                      gn_ref, y_ref, state_ref, *, decays):
    @pl.when(pl.program_id(1) == 0)
    def _():
        state_ref[...] = jnp.zeros(state_ref.shape, F32)

    cos = cos_ref[...]
    sin = sin_ref[...]
    half = RET_QK_DIM // 2

    def rot(x):
        x1, x2 = x[:, :half], x[:, half:]
        return jnp.concatenate([x1 * cos - x2 * sin, x1 * sin + x2 * cos], axis=1)

    for h in range(RET_HEADS):
        qs = slice(h * RET_QK_DIM, (h + 1) * RET_QK_DIM)
        vs = slice(h * RET_V_DIM, (h + 1) * RET_V_DIM)
        qr = rot(q_ref[0, :, qs].astype(F32))
        kr = rot(k_ref[0, :, qs].astype(F32)) * (RET_QK_DIM ** -0.5)
        qb = qr.astype(BF16)
        vh = v_ref[0, :, vs]
        attn = lax.dot_general(qb, kr.astype(BF16), _NT, preferred_element_type=F32) * inner_ref[h]
        st = state_ref[h]
        o = (jnp.dot(attn.astype(BF16), vh, preferred_element_type=F32)
             + jnp.dot(qb, st.astype(BF16), preferred_element_type=F32) * xi_ref[h])
        kz = (kr * zeta_ref[h]).astype(BF16)
        state_ref[h] = st * decays[h] + lax.dot_general(kz, vh, _TN, preferred_element_type=F32)
        mu = jnp.mean(o, axis=1, keepdims=True)
        d = o - mu
        var = jnp.mean(d * d, axis=1, keepdims=True)
        on = d * lax.rsqrt(var + GN_EPS) * gn_ref[h]
        y_ref[0, :, vs] = (on * _silu(z_ref[0, :, vs].astype(F32))).astype(y_ref.dtype)


def _retention(proj, gn_gain, tables):
    b = proj.shape[0]
    c = RET_CHUNK
    cos, sin, inner, xi, zeta, decays = tables
    v_blk = 2 * RET_QK_WIDTH // RET_WIDTH
    return pl.pallas_call(
        functools.partial(_retention_kernel, decays=decays),
        grid=(b, SEQ // c),
        in_specs=[
            pl.BlockSpec((1, c, RET_QK_WIDTH), lambda i, j: (i, j, 0)),
            pl.BlockSpec((1, c, RET_QK_WIDTH), lambda i, j: (i, j, 1)),
            pl.BlockSpec((1, c, RET_WIDTH), lambda i, j: (i, j, v_blk)),
            pl.BlockSpec((1, c, RET_WIDTH), lambda i, j: (i, j, v_blk + 1)),
            pl.BlockSpec((c, RET_QK_DIM // 2), lambda i, j: (j, 0)),
            pl.BlockSpec((c, RET_QK_DIM // 2), lambda i, j: (j, 0)),
            pl.BlockSpec((RET_HEADS, c, c), lambda i, j: (0, 0, 0)),
            pl.BlockSpec((RET_HEADS, c, 1), lambda i, j: (0, 0, 0)),
            pl.BlockSpec((RET_HEADS, c, 1), lambda i, j: (0, 0, 0)),
            pl.BlockSpec((RET_HEADS, 1, RET_V_DIM), lambda i, j: (0, 0, 0)),
        ],
        out_specs=pl.BlockSpec((1, c, RET_WIDTH), lambda i, j: (i, j, 0)),
        out_shape=jax.ShapeDtypeStruct((b, SEQ, RET_WIDTH), BF16),
        scratch_shapes=[pltpu.VMEM((RET_HEADS, RET_QK_DIM, RET_V_DIM), F32)],
        compiler_params=_cparams(("parallel", "arbitrary")),
        name="retention",
    )(proj, proj, proj, proj, cos, sin, inner, xi, zeta, gn_gain.reshape(RET_HEADS, 1, RET_V_DIM))


def _t5_bucket_np(dist):
    n = np.maximum(dist, 0)
    max_exact = REL_BUCKETS // 2
    nf = np.maximum(n, 1).astype(np.float64)
    large = max_exact + (np.log(nf / max_exact) / math.log(REL_MAX_DIST / max_exact)
                         * (REL_BUCKETS - max_exact)).astype(np.int64)
    large = np.minimum(large, REL_BUCKETS - 1)
    return np.where(n < max_exact, n, large).astype(np.int32)


def _skewed(vec, rows, stride, cols):
    p = vec.shape[-1]
    lead = vec.shape[:-1]
    flat = jnp.tile(vec, (1,) * len(lead) + (rows,))[..., :rows * (p - stride)]
    return flat.reshape(lead + (rows, p - stride))[..., :cols]


def _nsa_tables(table):
    tq, tk = ATT_TQ, ATT_TK
    assert tq == tk
    tab_t = table.T * LOG2E
    period = 2 * SEQ
    x = np.arange(period)
    ok = (x >= CMP_BLOCK - 1) & (x < SEQ)
    vec = jnp.take(tab_t, jnp.asarray(_t5_bucket_np(x - (CMP_BLOCK - 1))), axis=1)
    vec = jnp.where(jnp.asarray(ok)[None, :], vec, MASKED)
    bias_cmp = _skewed(vec, N_CMP_PAD, CMP_STRIDE, SEQ)
    rel_t = tab_t - tab_t[:, REL_BUCKETS - 1:]
    period = 2 * tq
    x = np.arange(period)
    x = np.where(x < tq, x, x - period)
    tiles = []
    for off in range(3):
        d = off * tk - x
        ok = (d >= 0) & (d < WIN_SIZE) if off == 2 else (d >= 0)
        vec = jnp.take(rel_t, jnp.asarray(_t5_bucket_np(d)), axis=1)
        vec = jnp.where(jnp.asarray(ok)[None, :], vec, MASKED)
        tiles.append(_skewed(vec, tq, 1, tk))
    a_tiles = jnp.stack(tiles + [jnp.zeros_like(tiles[0])])
    cs = np.arange(N_CMP_PAD)[None, :] * CMP_STRIDE
    jj = np.arange(N_SLC)[:, None]
    ovl = ((cs < (jj + 1) * SLC_BLOCK) & (cs + CMP_BLOCK > jj * SLC_BLOCK)
           & (np.arange(N_CMP_PAD)[None, :] < N_CMP))
    ovl = jnp.asarray(ovl.astype(np.float32), BF16)
    lane = np.arange(LANES).reshape(1, 1, 1, LANES)
    key = np.arange(SEQ).reshape(1, SEQ // tk, tk, 1)
    base = np.array([HEAD_DIM, 0]).reshape(2, 1, 1, 1)
    key_blocks = jnp.asarray((lane == base + key // SLC_BLOCK).astype(np.float32), BF16)
    col = np.arange(2 * LANES).reshape(1, -1)
    jb = np.arange(N_SLC).reshape(-1, 1)
    place = (col == HEAD_DIM + jb) | (col == LANES + jb)
    row = np.arange(2 * LANES).reshape(-1, 1)
    out = np.arange(LANES).reshape(1, -1)
    den_spread = ((row == LANES - 1) & (out < HEAD_DIM)) | ((row == LANES) & (out >= HEAD_DIM))
    sel_consts = (key_blocks, jnp.asarray(place.astype(np.float32), BF16),
                  jnp.asarray(place.any(axis=0, keepdims=True).astype(np.float32)),
                  jnp.asarray(den_spread.astype(np.float32), BF16))
    c = np.arange(LANES).reshape(1, 1, LANES, 1)
    col = np.arange(HEADS_PER_GROUP * HEAD_DIM).reshape(1, 1, 1, -1) // HEAD_DIM
    br = np.arange(3).reshape(3, 1, 1, 1)
    gg = np.arange(GROUPS).reshape(1, GROUPS, 1, 1)
    gate_expand = jnp.asarray((c == br * HEADS + HEADS_PER_GROUP * gg + col).astype(np.float32),
                              BF16)
    return bias_cmp, a_tiles, ovl, sel_consts, _lane_consts(max(tq, CMP_TQ)), gate_expand


def _retention_tables():
    c = RET_CHUNK
    log_g = jnp.log(1.0 - 2.0 ** (-5.0 - jnp.arange(RET_HEADS, dtype=F32)))
    i = jnp.arange(c, dtype=F32)
    diff = i[:, None] - i[None, :]
    inner = jnp.where(diff >= 0, jnp.exp(diff[None] * log_g[:, None, None]), 0.0)
    xi = jnp.exp((i + 1.0)[None, :] * log_g[:, None])[:, :, None]
    zeta = jnp.exp((c - 1.0 - i)[None, :] * log_g[:, None])[:, :, None]
    decays = tuple(float((1.0 - 2.0 ** (-5.0 - h)) ** c) for h in range(RET_HEADS))
    inv = 1.0 / (ROPE_BASE ** jnp.linspace(0.0, 1.0, RET_QK_DIM // 2, dtype=F32))
    ang = jnp.arange(SEQ, dtype=F32)[:, None] * inv[None, :]
    return jnp.cos(ang), jnp.sin(ang), inner, xi, zeta, decays


def _nsa_w_in_layout(w):
    d = w.shape[0]
    q = w[:, :NSA_WIDTH] * (HEAD_DIM ** -0.5 * LOG2E)
    kv0 = NSA_WIDTH
    g0 = kv0 + 6 * KV_WIDTH
    z0 = g0 + 3 * HEADS
    kv = w[:, kv0:g0].reshape(d, 6, GROUPS, 1, HEAD_DIM)
    dup = jnp.stack([kv[:, n] for n in (2, 4, 3, 5)], axis=1)
    dup = jnp.broadcast_to(dup, (d, 4, GROUPS, 2, HEAD_DIM)).reshape(d, 8 * KV_WIDTH)
    used = NSA_G_COL + 3 * HEADS
    cols = [q, w[:, z0:z0 + 3 * NSA_WIDTH], dup, w[:, kv0:kv0 + 2 * KV_WIDTH], w[:, g0:z0],
            jnp.zeros((d, NSA_PROJ_PAD - used), w.dtype)]
    return jnp.concatenate(cols, axis=1).astype(BF16)


def _nsa_layer(h2d, b, pre_gain, post_gain, w_in, w_out, k_pos, k_w1, k_w2, v_pos, v_w1, v_w2, tabs):
    bias_cmp, a_tiles, ovl, sel_consts, lane_consts, gate_expand = tabs
    proj = _norm_proj(h2d, pre_gain, _nsa_w_in_layout(w_in), NSA_PROJ_TN)
    proj = proj.reshape(b, SEQ, NSA_PROJ_PAD)
    ckv = proj[:, :, NSA_CKV_COL:NSA_CKV_COL + 2 * KV_WIDTH]
    ckv = ckv.reshape(b, N_CMP_PAD, CMP_STRIDE, 2, GROUPS, HEAD_DIM).transpose(0, 3, 4, 1, 2, 5)
    ckv = ckv.reshape(b, 2, GROUPS, N_CMP_PAD, CMP_STRIDE * HEAD_DIM)
    pos = jnp.stack([k_pos, v_pos]).reshape(2, 2, CMP_STRIDE * HEAD_DIM)
    w1 = jnp.stack([k_w1, v_w1]).astype(BF16)
    w2 = jnp.stack([k_w2, v_w2])
    w2d = jnp.concatenate([w2, w2], axis=2).astype(BF16)
    ckv_c = _compress(ckv, pos, w1, w2d)
    y_cmp, sel = _cmp_select(proj, ckv_c, bias_cmp, ovl, lane_consts, gate_expand[0])
    y_tok = _token_attention(proj, a_tiles, lane_consts, gate_expand[1:], sel, sel_consts)
    parts = [y.reshape(b * SEQ, NSA_WIDTH) for y in (y_cmp, y_tok)]
    return _out_post(parts, w_out.astype(BF16), h2d, post_gain)


def _ret_layer(h2d, b, pre_gain, post_gain, w_in, w_out, gn_gain, tabs):
    proj = _norm_proj(h2d, pre_gain, w_in.astype(BF16), RET_PROJ_TN).reshape(b, SEQ, -1)
    y = _retention(proj, gn_gain, tabs)
    return _out_post([y.reshape(b * SEQ, RET_WIDTH)], w_out.astype(BF16), h2d, post_gain)


def kernel(x, pre_norm_gain, post_norm_gain, rel_bias_table, nsa_w_in, nsa_w_out, nsa_cmp_k_pos, nsa_cmp_k_w1, nsa_cmp_k_w2, nsa_cmp_v_pos, nsa_cmp_v_w1, nsa_cmp_v_w2, ret_w_in, ret_w_out, ret_gn_gain):
    b, s, d = x.shape
    assert s == SEQ and d == D_MODEL
    nsa_tabs = _nsa_tables(rel_bias_table)
    ret_tabs = _retention_tables()
    h = x.reshape(b * s, d)
    for layer in range(DEPTH):
        slot = layer // 2
        if layer % 2 == 0:
            h = _nsa_layer(h, b, pre_norm_gain[layer], post_norm_gain[layer], nsa_w_in[slot],
                           nsa_w_out[slot], nsa_cmp_k_pos[slot], nsa_cmp_k_w1[slot],
                           nsa_cmp_k_w2[slot], nsa_cmp_v_pos[slot], nsa_cmp_v_w1[slot],
                           nsa_cmp_v_w2[slot], nsa_tabs)
        else:
            h = _ret_layer(h, b, pre_norm_gain[layer], post_norm_gain[layer], ret_w_in[slot],
                           ret_w_out[slot], ret_gn_gain[slot], ret_tabs)
    return h.reshape(b, s, d)
```

```python
import functools
import math

import numpy as np
import jax
import jax.numpy as jnp
from jax import lax
from jax.experimental import pallas as pl
from jax.experimental.pallas import tpu as pltpu

F32 = jnp.float32
BF16 = jnp.bfloat16

D_MODEL = 1024
SEQ = 2048
DEPTH = 4
RMS_EPS = 1e-6
GN_EPS = 1e-6
MASKED = -1e30
LOG2E = math.log2(math.e)

HEADS = 16
HEAD_DIM = 64
GROUPS = 4
HEADS_PER_GROUP = HEADS // GROUPS
NSA_WIDTH = HEADS * HEAD_DIM
KV_WIDTH = GROUPS * HEAD_DIM
CMP_BLOCK = 32
CMP_STRIDE = 16
CMP_HIDDEN = 256
N_CMP = (SEQ - CMP_BLOCK) // CMP_STRIDE + 1
N_CMP_PAD = 128
SLC_BLOCK = 64
N_SLC = SEQ // SLC_BLOCK
SLC_TOPN = 16
WIN_SIZE = 512
FORCED_SCORE = 1e3
REL_BUCKETS = 32
REL_MAX_DIST = 128

RET_HEADS = 4
RET_QK_DIM = 256
RET_V_DIM = 512
RET_QK_WIDTH = RET_HEADS * RET_QK_DIM
RET_WIDTH = RET_HEADS * RET_V_DIM
ROPE_BASE = 10000.0

LANES = 128
VMEM_LIMIT_BYTES = 56 * 1024 * 1024

PROJ_TM = 1024
RET_PROJ_TN = 2048
NSA_PROJ_TN = 3456
POST_TM = 512
ATT_TQ = 256
ATT_TK = 256
CMP_TQ = 2048
RET_CHUNK = 256

NSA_Q_COL = 0
NSA_Z_COL = 1024
NSA_K_COL = 4096
NSA_V_COL = 5120
NSA_CKV_COL = 6144
NSA_G_COL = 6656
NSA_PROJ_PAD = 6912

_NT = (((1,), (1,)), ((), ()))
_TN = (((0,), (0,)), ((), ()))


def _cparams(sem):
    return pltpu.CompilerParams(dimension_semantics=sem, vmem_limit_bytes=VMEM_LIMIT_BYTES)


def _norm_proj_kernel(x_ref, g_ref, w_ref, o_ref, xn_ref):
    @pl.when(pl.program_id(1) == 0)
    def _():
        x = x_ref[...]
        ms = jnp.mean(x * x, axis=-1, keepdims=True)
        xn_ref[...] = (x * lax.rsqrt(ms + RMS_EPS) * g_ref[...]).astype(BF16)

    o_ref[...] = jnp.dot(xn_ref[...], w_ref[...], preferred_element_type=F32).astype(o_ref.dtype)


def _norm_proj(x2d, gain, w_bf16, tn):
    m, d = x2d.shape
    n = w_bf16.shape[1]
    assert n % tn == 0
    return pl.pallas_call(
        _norm_proj_kernel,
        grid=(m // PROJ_TM, n // tn),
        in_specs=[
            pl.BlockSpec((PROJ_TM, d), lambda i, j: (i, 0)),
            pl.BlockSpec((1, d), lambda i, j: (0, 0)),
            pl.BlockSpec((d, tn), lambda i, j: (0, j)),
        ],
        out_specs=pl.BlockSpec((PROJ_TM, tn), lambda i, j: (i, j)),
        out_shape=jax.ShapeDtypeStruct((m, n), BF16),
        scratch_shapes=[pltpu.VMEM((PROJ_TM, d), BF16)],
        compiler_params=_cparams(("parallel", "arbitrary")),
        name="norm_proj",
    )(x2d, gain.reshape(1, d), w_bf16)


def _out_post_kernel(*refs, n_parts):
    y_refs = refs[:n_parts]
    w_ref, h_ref, g_ref, o_ref = refs[n_parts:]
    y = y_refs[0][...].astype(F32)
    for r in y_refs[1:]:
        y = y + r[...].astype(F32)
    t = jnp.dot(y.astype(BF16), w_ref[...], preferred_element_type=F32)
    ms = jnp.mean(t * t, axis=-1, keepdims=True)
    o_ref[...] = h_ref[...] + t * lax.rsqrt(ms + RMS_EPS) * g_ref[...]


def _out_post(parts, w_bf16, h2d, gain):
    m, d = h2d.shape
    k = w_bf16.shape[0]
    n_parts = len(parts)
    return pl.pallas_call(
        functools.partial(_out_post_kernel, n_parts=n_parts),
        grid=(m // POST_TM,),
        in_specs=[pl.BlockSpec((POST_TM, k), lambda i: (i, 0)) for _ in parts] + [
            pl.BlockSpec((k, d), lambda i: (0, 0)),
            pl.BlockSpec((POST_TM, d), lambda i: (i, 0)),
            pl.BlockSpec((1, d), lambda i: (0, 0)),
        ],
        out_specs=pl.BlockSpec((POST_TM, d), lambda i: (i, 0)),
        out_shape=jax.ShapeDtypeStruct((m, d), F32),
        compiler_params=_cparams(("parallel",)),
        name="out_post",
    )(*parts, w_bf16, h2d, gain.reshape(1, d))


def _compress_kernel(x_ref, pos_ref, w1_ref, w2_ref, o_ref):
    half = CMP_STRIDE * HEAD_DIM
    for which in range(2):
        x = x_ref[0, which, 0].astype(F32)
        xa = (x + pos_ref[which, 0:1, :]).astype(BF16)
        xb = (x + pos_ref[which, 1:2, :]).astype(BF16)
        pa = jnp.dot(xa, w1_ref[which, :half, :], preferred_element_type=F32)
        pb = jnp.dot(xb, w1_ref[which, half:, :], preferred_element_type=F32)
        hid = pa + pltpu.roll(pb, N_CMP_PAD - 1, 0)
        hid = hid * jax.nn.sigmoid(hid)
        o_ref[0, which, 0] = jnp.dot(hid.astype(BF16), w2_ref[which],
                                     preferred_element_type=F32).astype(o_ref.dtype)


def _compress(ckv_rows, pos, w1, w2d):
    b = ckv_rows.shape[0]
    row_w = CMP_STRIDE * HEAD_DIM
    return pl.pallas_call(
        _compress_kernel,
        grid=(b, GROUPS),
        in_specs=[
            pl.BlockSpec((1, 2, 1, N_CMP_PAD, row_w), lambda i, g: (i, 0, g, 0, 0)),
            pl.BlockSpec((2, 2, row_w), lambda i, g: (0, 0, 0)),
            pl.BlockSpec((2, 2 * row_w, CMP_HIDDEN), lambda i, g: (0, 0, 0)),
            pl.BlockSpec((2, CMP_HIDDEN, 2 * HEAD_DIM), lambda i, g: (0, 0, 0)),
        ],
        out_specs=pl.BlockSpec((1, 2, 1, N_CMP_PAD, 2 * HEAD_DIM), lambda i, g: (i, 0, g, 0, 0)),
        out_shape=jax.ShapeDtypeStruct((b, 2, GROUPS, N_CMP_PAD, 2 * HEAD_DIM), BF16),
        compiler_params=_cparams(("parallel", "parallel")),
        name="compress",
    )(ckv_rows, pos, w1, w2d)


def _lane_consts(rows):
    lane = np.arange(LANES)
    lo = (lane < HEAD_DIM).astype(np.float32)
    hi = (lane >= HEAD_DIM).astype(np.float32)
    last = (lane == LANES - 1).astype(np.float32)
    first = (lane == 0).astype(np.float32)
    c = np.stack([lo, hi, 1.0 - last, last, 1.0 - first, first])
    return jnp.asarray(np.broadcast_to(c[:, None, :], (6, rows, LANES)), BF16)


def _bf16_terms(x, n):
    terms = []
    for _ in range(n - 1):
        t = x.astype(BF16)
        terms.append(t)
        x = x - t.astype(F32)
    terms.append(x.astype(BF16))
    return terms


def _gates(gate_logits, expands):
    terms = _bf16_terms(jax.nn.sigmoid(gate_logits.astype(F32)), 2)
    return [sum(jnp.dot(t, ex, preferred_element_type=F32) for t in terms) for ex in expands]


def _silu(x):
    return x * jax.nn.sigmoid(x)


def _cmp_select_kernel(q_ref, gate_ref, z_ref, ckv_ref, bias_ref, ovl_ref, lc_ref, gexp_ref,
                       y_ref, sel_ref):
    g = pl.program_id(0)
    qi = pl.program_id(1)
    tq = q_ref.shape[1]
    q = q_ref[0]
    kk = ckv_ref[0, 0, 0]
    vv = ckv_ref[0, 1, 0]
    lane = lax.broadcasted_iota(jnp.int32, (tq, LANES), 1)

    psum = jnp.zeros((N_CMP_PAD, tq), F32)
    pairs = []
    for a in range(2):
        qp = q[:, a * LANES:(a + 1) * LANES]
        outs = []
        for e in range(2):
            s = lax.dot_general(kk, qp * lc_ref[e, :tq], _NT, preferred_element_type=F32)
            bias = bias_ref[2 * a + e]
            s = s + bias
            m = jnp.max(s, axis=0, keepdims=True)
            p = jnp.exp2(s - m)
            p = jnp.where(bias > 0.5 * MASKED, p / jnp.sum(p, axis=0, keepdims=True), 0.0)
            psum = psum + p
            outs.append(lax.dot_general(p.astype(BF16), vv, _TN, preferred_element_type=F32))
        pairs.append(jnp.where(lane < HEAD_DIM, outs[0], outs[1]))
    y_ref[0] = (jnp.concatenate(pairs, axis=1) * _gates(gate_ref[0], [gexp_ref[g]])[0]
                * _silu(z_ref[0].astype(F32))).astype(y_ref.dtype)

    ovl = ovl_ref[...]
    imp = sum(jnp.dot(ovl, t, preferred_element_type=F32) for t in _bf16_terms(psum, 3))
    t = qi * tq + lax.broadcasted_iota(jnp.int32, (N_SLC, tq), 1)
    blk = lax.broadcasted_iota(jnp.int32, (N_SLC, tq), 0)
    cur = lax.shift_right_logical(t, int(math.log2(SLC_BLOCK)))
    forced = (blk == 0) | (blk == cur) | (blk == cur - 1)
    score = jnp.where(blk * SLC_BLOCK <= t, imp + jnp.where(forced, FORCED_SCORE, 0.0), MASKED)
    cnt = jnp.zeros((N_SLC, tq), jnp.int32)
    for jp in range(N_SLC):
        row = score[jp:jp + 1, :]
        beats = (row > score) | ((row == score) & (blk > jp))
        cnt = cnt + beats.astype(jnp.int32)
    sel_ref[0, 0] = jnp.where(cnt < SLC_TOPN, 1.0, 0.0).astype(sel_ref.dtype)


def _cmp_select(proj, ckv, bias_cmp, ovl, lane_consts, gate_expand):
    b = proj.shape[0]
    tq = CMP_TQ
    grp_w = HEADS_PER_GROUP * HEAD_DIM
    return pl.pallas_call(
        _cmp_select_kernel,
        grid=(GROUPS, SEQ // tq, b),
        in_specs=[
            pl.BlockSpec((1, tq, grp_w), lambda g, qi, i: (i, qi, NSA_Q_COL // grp_w + g)),
            pl.BlockSpec((1, tq, LANES), lambda g, qi, i: (i, qi, NSA_G_COL // LANES)),
            pl.BlockSpec((1, tq, grp_w), lambda g, qi, i: (i, qi, NSA_Z_COL // grp_w + g)),
            pl.BlockSpec((1, 2, 1, N_CMP_PAD, 2 * HEAD_DIM), lambda g, qi, i: (i, 0, g, 0, 0)),
            pl.BlockSpec((HEADS_PER_GROUP, N_CMP_PAD, tq), lambda g, qi, i: (g, 0, qi)),
            pl.BlockSpec((N_SLC, N_CMP_PAD), lambda g, qi, i: (0, 0)),
            pl.BlockSpec(memory_space=pltpu.VMEM),
            pl.BlockSpec(memory_space=pltpu.VMEM),
        ],
        out_specs=[
            pl.BlockSpec((1, tq, grp_w), lambda g, qi, i: (i, qi, g)),
            pl.BlockSpec((1, 1, N_SLC, tq), lambda g, qi, i: (i, g, 0, qi)),
        ],
        out_shape=[
            jax.ShapeDtypeStruct((b, SEQ, NSA_WIDTH), BF16),
            jax.ShapeDtypeStruct((b, GROUPS, N_SLC, SEQ), BF16),
        ],
        compiler_params=_cparams(("parallel", "parallel", "parallel")),
        name="cmp_select",
    )(proj, proj, proj, ckv, bias_cmp, ovl, lane_consts, gate_expand)


def _token_kernel(q_ref, ks_ref, vs_ref, kw_ref, vw_ref, gate_ref, zs_ref, zw_ref, a_ref, lc_ref,
                  gexp_ref, sel_ref, kblk_ref, place_ref, placed_ref, dspread_ref, y_ref,
                  qm_ref, m_ref, acc_ref, s_bufs, p_bufs, c_bufs, kx_ref, vx_ref, gz_ref):
    g = pl.program_id(1)
    tq = ATT_TQ
    tk = ATT_TK

    def stream_len(qi):
        return min(qi, WIN_SIZE // tk) + 1 + qi + 1

    for c in range(SEQ // tk):
        rows = slice(c * tk, (c + 1) * tk)
        for branch, v_ref in enumerate((vs_ref, vw_ref)):
            vv = v_ref[0, rows, :]
            vx_ref[2 * branch, rows, :] = vv * lc_ref[2, :tk] + lc_ref[3, :tk]
            vx_ref[2 * branch + 1, rows, :] = vv * lc_ref[4, :tk] + lc_ref[5, :tk]
        kk = ks_ref[0, rows, :]
        for e in range(2):
            kx_ref[e, c] = (kk * lc_ref[e, :tk] + kblk_ref[e, c]).astype(F32).T.astype(BF16)
        kx_ref[2, c] = kw_ref[0, rows, :].astype(F32).T.astype(BF16)

    def tile(qi, j):
        n_win = min(qi, WIN_SIZE // tk) + 1
        win = int(j < n_win)
        t = j - (1 - win) * n_win
        return win, qi - t, t if (win or t < 2) else None

    def scores(qi, j, s_ref):
        if j >= stream_len(qi):
            return
        win, ki, _ = tile(qi, j)
        for e in range(2):
            pair = slice(2 * e * tq, 2 * (e + 1) * tq)
            s_ref[pair, :] = jnp.dot(qm_ref[win, pair, :], kx_ref[2 if win else e, ki],
                                     preferred_element_type=F32)

    def softmax(qi, j, s_ref, p_ref, c_ref):
        win, _, kind = tile(qi, j)
        for slot in range(HEADS_PER_GROUP):
            e, a = divmod(slot, 2)
            rows = slice(slot * tq, (slot + 1) * tq)
            s = s_ref[rows, :]
            if kind is not None:
                s = s + a_ref[kind, HEADS_PER_GROUP * g + 2 * a + e]
            m_prev = m_ref[win, rows, :]
            m_new = jnp.maximum(m_prev, jnp.max(s, axis=1, keepdims=True))
            c_ref[rows, :] = jnp.exp2(m_prev - m_new)
            x = s - jnp.concatenate([m_new] * (tk // LANES), axis=1)
            p_ref[rows, :] = jnp.exp2(x.astype(BF16))
            m_ref[win, rows, :] = m_new

    def values(qi_, par_, j, p_ref, c_ref):
        win, ki, _ = tile(qi_, j)
        rows = slice(ki * tk, (ki + 1) * tk)
        for e in range(2):
            pair = slice(2 * e * tq, 2 * (e + 1) * tq)
            pv = jnp.dot(p_ref[pair, :], vx_ref[2 * win + e, rows, :], preferred_element_type=F32)
            acc = acc_ref.at[2 * par_ + win]
            acc[pair, :] = c_ref[pair, :] * acc[pair, :] + pv

    lane = lax.broadcasted_iota(jnp.int32, (tq, LANES), 1)

    def gated_output(par_, branch):
        acc = acc_ref.at[2 * par_ + branch]
        pairs = []
        for a in range(2):
            acc_e = acc[a * tq:(a + 1) * tq, :]
            acc_o = acc[(2 + a) * tq:(3 + a) * tq, :]
            both = jnp.concatenate([acc_e, acc_o], axis=1)
            den = sum(jnp.dot(t, dspread_ref[...], preferred_element_type=F32)
                      for t in _bf16_terms(both, 2))
            pairs.append(jnp.where(lane < HEAD_DIM, acc_e, acc_o) / den)
        return jnp.concatenate(pairs, axis=1) * gz_ref[2 * par_ + branch]

    bufs = [(s_bufs.at[i], p_bufs.at[i], c_bufs.at[i]) for i in range(2)]

    def step(qi, half, j, parity):
        cur, nxt = bufs[parity], bufs[1 - parity]
        values(qi, half, j - 1, nxt[1], nxt[2])
        softmax(qi, j, *cur)
        scores(qi, j + 1, nxt[0])

    def enter(qi, slot):
        rows = slice(slot * tq, (slot + 1) * tq)
        half = slot % 2
        q = q_ref[0, rows, :]
        qms = []
        for hs in range(HEADS_PER_GROUP):
            e, a = divmod(hs, 2)
            qms.append(q[:, a * LANES:(a + 1) * LANES] * lc_ref[e, :tq])
            qm_ref[1, hs * tq:(hs + 1) * tq, :] = qms[hs]
        flags = lax.dot_general(sel_ref[0, 0, :, rows], place_ref[...], _TN,
                                preferred_element_type=F32)
        fill = ((placed_ref[...] - flags) * MASKED).astype(BF16)
        for hs in range(HEADS_PER_GROUP):
            e = hs // 2
            qm_ref[0, hs * tq:(hs + 1) * tq, :] = qms[hs] + fill[:, e * LANES:(e + 1) * LANES]
        gates = _gates(gate_ref[0, rows, :], [gexp_ref[0, g], gexp_ref[1, g]])
        for branch, z_ref in enumerate((zs_ref, zw_ref)):
            gz_ref[2 * half + branch] = gates[branch] * _silu(z_ref[0, rows, :].astype(F32))
        m_ref[...] = jnp.full(m_ref.shape, -jnp.inf, F32)
        for branch in range(2):
            acc_ref[2 * half + branch] = jnp.zeros(acc_ref.shape[1:], F32)
        scores(qi, 0, bufs[0][0])

    def fill_pipeline(qi, half):
        softmax(qi, 0, *bufs[0])
        scores(qi, 1, bufs[1][0])
        step(qi, half, 1, 1)

    def run_stream(qi, half):
        for j in range(2, stream_len(qi)):
            step(qi, half, j, j % 2)

    def leave(qi, half):
        y_win = gated_output(half, 1)
        last_pos = stream_len(qi) - 1
        values(qi, half, last_pos, bufs[last_pos % 2][1], bufs[last_pos % 2][2])
        return y_win

    def write_output(slot, y_win):
        rows = slice(slot * tq, (slot + 1) * tq)
        y_ref[0, rows, :] = (y_win + gated_output(slot % 2, 0)).astype(y_ref.dtype)

    y_win = None
    for qi in range(SEQ // tq):
        enter(qi, qi)
        if qi > 0:
            write_output(qi - 1, y_win)
        fill_pipeline(qi, qi % 2)
        run_stream(qi, qi % 2)
        y_win = leave(qi, qi % 2)
    write_output(SEQ // tq - 1, y_win)


def _token_attention(proj, a_tiles, lane_consts, gate_expand, sel, sel_consts):
    b = proj.shape[0]
    tq, tk = ATT_TQ, ATT_TK
    grp_w = HEADS_PER_GROUP * HEAD_DIM
    k_blk = NSA_K_COL // LANES
    v_blk = NSA_V_COL // LANES
    z_blk = NSA_Z_COL // grp_w
    kv_spec = lambda blk: pl.BlockSpec((1, SEQ, LANES), lambda i, g: (i, 0, blk + g))
    z_spec = lambda blk: pl.BlockSpec((1, SEQ, grp_w), lambda i, g: (i, 0, blk + g))
    whole = pl.BlockSpec(memory_space=pltpu.VMEM)
    return pl.pallas_call(
        _token_kernel,
        grid=(b, GROUPS),
        in_specs=[
            pl.BlockSpec((1, SEQ, grp_w), lambda i, g: (i, 0, NSA_Q_COL // grp_w + g)),
            kv_spec(k_blk), kv_spec(v_blk), kv_spec(k_blk + GROUPS), kv_spec(v_blk + GROUPS),
            pl.BlockSpec((1, SEQ, LANES), lambda i, g: (i, 0, NSA_G_COL // LANES)),
            z_spec(z_blk + GROUPS), z_spec(z_blk + 2 * GROUPS),
            whole, whole, whole,
            pl.BlockSpec((1, 1, N_SLC, SEQ), lambda i, g: (i, g, 0, 0)),
            whole, whole, whole, whole,
        ],
        out_specs=pl.BlockSpec((1, SEQ, grp_w), lambda i, g: (i, 0, g)),
        out_shape=jax.ShapeDtypeStruct((b, SEQ, NSA_WIDTH), BF16),
        scratch_shapes=[
            pltpu.VMEM((2, HEADS_PER_GROUP * tq, LANES), BF16),
            pltpu.VMEM((2, HEADS_PER_GROUP * tq, LANES), F32),
            pltpu.VMEM((4, HEADS_PER_GROUP * tq, LANES), F32),
            pltpu.VMEM((2, HEADS_PER_GROUP * tq, tk), F32),
            pltpu.VMEM((2, HEADS_PER_GROUP * tq, tk), BF16),
            pltpu.VMEM((2, HEADS_PER_GROUP * tq, LANES), F32),
            pltpu.VMEM((3, SEQ // tk, LANES, tk), BF16),
            pltpu.VMEM((4, SEQ, LANES), BF16),
            pltpu.VMEM((4, tq, grp_w), F32),
        ],
        compiler_params=_cparams(("parallel", "parallel")),
        name="token_attention",
    )(proj, proj, proj, proj, proj, proj, proj, proj, a_tiles, lane_consts, gate_expand,
      sel, *sel_consts)


def _retention_kernel(q_ref, k_ref, v_ref, z_ref, cos_ref, sin_ref, inner_ref, xi_ref, zeta_ref,
                      gn_ref, y_ref, state_ref, *, decays):
    @pl.when(pl.program_id(1) == 0)
    def _():
        state_ref[...] = jnp.zeros(state_ref.shape, F32)

    cos = cos_ref[...]
    sin = sin_ref[...]
    half = RET_QK_DIM // 2

    def rot(x):
        x1, x2 = x[:, :half], x[:, half:]
        return jnp.concatenate([x1 * cos - x2 * sin, x1 * sin + x2 * cos], axis=1)

    for h in range(RET_HEADS):
        qs = slice(h * RET_QK_DIM, (h + 1) * RET_QK_DIM)
        vs = slice(h * RET_V_DIM, (h + 1) * RET_V_DIM)
        qr = rot(q_ref[0, :, qs].astype(F32))
        kr = rot(k_ref[0, :, qs].astype(F32)) * (RET_QK_DIM ** -0.5)
        qb = qr.astype(BF16)
        vh = v_ref[0, :, vs]
        attn = lax.dot_general(qb, kr.astype(BF16), _NT, preferred_element_type=F32) * inner_ref[h]
        st = state_ref[h]
        o = (jnp.dot(attn.astype(BF16), vh, preferred_element_type=F32)
             + jnp.dot(qb, st.astype(BF16), preferred_element_type=F32) * xi_ref[h])
        kz = (kr * zeta_ref[h]).astype(BF16)
        state_ref[h] = st * decays[h] + lax.dot_general(kz, vh, _TN, preferred_element_type=F32)
        mu = jnp.mean(o, axis=1, keepdims=True)
        d = o - mu
        var = jnp.mean(d * d, axis=1, keepdims=True)
        on = d * lax.rsqrt(var + GN_EPS) * gn_ref[h]
        y_ref[0, :, vs] = (on * _silu(z_ref[0, :, vs].astype(F32))).astype(y_ref.dtype)


def _retention(proj, gn_gain, tables):
    b = proj.shape[0]
    c = RET_CHUNK
    cos, sin, inner, xi, zeta, decays = tables
    v_blk = 2 * RET_QK_WIDTH // RET_WIDTH
    return pl.pallas_call(
        functools.partial(_retention_kernel, decays=decays),
        grid=(b, SEQ // c),
        in_specs=[
            pl.BlockSpec((1, c, RET_QK_WIDTH), lambda i, j: (i, j, 0)),
            pl.BlockSpec((1, c, RET_QK_WIDTH), lambda i, j: (i, j, 1)),
            pl.BlockSpec((1, c, RET_WIDTH), lambda i, j: (i, j, v_blk)),
            pl.BlockSpec((1, c, RET_WIDTH), lambda i, j: (i, j, v_blk + 1)),
            pl.BlockSpec((c, RET_QK_DIM // 2), lambda i, j: (j, 0)),
            pl.BlockSpec((c, RET_QK_DIM // 2), lambda i, j: (j, 0)),
            pl.BlockSpec((RET_HEADS, c, c), lambda i, j: (0, 0, 0)),
            pl.BlockSpec((RET_HEADS, c, 1), lambda i, j: (0, 0, 0)),
            pl.BlockSpec((RET_HEADS, c, 1), lambda i, j: (0, 0, 0)),
            pl.BlockSpec((RET_HEADS, 1, RET_V_DIM), lambda i, j: (0, 0, 0)),
        ],
        out_specs=pl.BlockSpec((1, c, RET_WIDTH), lambda i, j: (i, j, 0)),
        out_shape=jax.ShapeDtypeStruct((b, SEQ, RET_WIDTH), BF16),
        scratch_shapes=[pltpu.VMEM((RET_HEADS, RET_QK_DIM, RET_V_DIM), F32)],
        compiler_params=_cparams(("parallel", "arbitrary")),
        name="retention",
    )(proj, proj, proj, proj, cos, sin, inner, xi, zeta, gn_gain.reshape(RET_HEADS, 1, RET_V_DIM))


def _t5_bucket_np(dist):
    n = np.maximum(dist, 0)
    max_exact = REL_BUCKETS // 2
    nf = np.maximum(n, 1).astype(np.float64)
    large = max_exact + (np.log(nf / max_exact) / math.log(REL_MAX_DIST / max_exact)
                         * (REL_BUCKETS - max_exact)).astype(np.int64)
    large = np.minimum(large, REL_BUCKETS - 1)
    return np.where(n < max_exact, n, large).astype(np.int32)


def _skewed(vec, rows, stride, cols):
    p = vec.shape[-1]
    lead = vec.shape[:-1]
    flat = jnp.tile(vec, (1,) * len(lead) + (rows,))[..., :rows * (p - stride)]
    return flat.reshape(lead + (rows, p - stride))[..., :cols]


def _nsa_tables(table):
    tq, tk = ATT_TQ, ATT_TK
    assert tq == tk
    tab_t = table.T * LOG2E
    period = 2 * SEQ
    x = np.arange(period)
    ok = (x >= CMP_BLOCK - 1) & (x < SEQ)
    vec = jnp.take(tab_t, jnp.asarray(_t5_bucket_np(x - (CMP_BLOCK - 1))), axis=1)
    vec = jnp.where(jnp.asarray(ok)[None, :], vec, MASKED)
    bias_cmp = _skewed(vec, N_CMP_PAD, CMP_STRIDE, SEQ)
    rel_t = tab_t - tab_t[:, REL_BUCKETS - 1:]
    period = 2 * tq
    x = np.arange(period)
    x = np.where(x < tq, x, x - period)
    tiles = []
    for off in range(3):
        d = off * tk - x
        ok = (d >= 0) & (d < WIN_SIZE) if off == 2 else (d >= 0)
        vec = jnp.take(rel_t, jnp.asarray(_t5_bucket_np(d)), axis=1)
        vec = jnp.where(jnp.asarray(ok)[None, :], vec, MASKED)
        tiles.append(_skewed(vec, tq, 1, tk))
    a_tiles = jnp.stack(tiles + [jnp.zeros_like(tiles[0])])
    cs = np.arange(N_CMP_PAD)[None, :] * CMP_STRIDE
    jj = np.arange(N_SLC)[:, None]
    ovl = ((cs < (jj + 1) * SLC_BLOCK) & (cs + CMP_BLOCK > jj * SLC_BLOCK)
           & (np.arange(N_CMP_PAD)[None, :] < N_CMP))
    ovl = jnp.asarray(ovl.astype(np.float32), BF16)
    lane = np.arange(LANES).reshape(1, 1, 1, LANES)
    key = np.arange(SEQ).reshape(1, SEQ // tk, tk, 1)
    base = np.array([HEAD_DIM, 0]).reshape(2, 1, 1, 1)
    key_blocks = jnp.asarray((lane == base + key // SLC_BLOCK).astype(np.float32), BF16)
    col = np.arange(2 * LANES).reshape(1, -1)
    jb = np.arange(N_SLC).reshape(-1, 1)
    place = (col == HEAD_DIM + jb) | (col == LANES + jb)
    row = np.arange(2 * LANES).reshape(-1, 1)
    out = np.arange(LANES).reshape(1, -1)
    den_spread = ((row == LANES - 1) & (out < HEAD_DIM)) | ((row == LANES) & (out >= HEAD_DIM))
    sel_consts = (key_blocks, jnp.asarray(place.astype(np.float32), BF16),
                  jnp.asarray(place.any(axis=0, keepdims=True).astype(np.float32)),
                  jnp.asarray(den_spread.astype(np.float32), BF16))
    c = np.arange(LANES).reshape(1, 1, LANES, 1)
    col = np.arange(HEADS_PER_GROUP * HEAD_DIM).reshape(1, 1, 1, -1) // HEAD_DIM
    br = np.arange(3).reshape(3, 1, 1, 1)
    gg = np.arange(GROUPS).reshape(1, GROUPS, 1, 1)
    gate_expand = jnp.asarray((c == br * HEADS + HEADS_PER_GROUP * gg + col).astype(np.float32),
                              BF16)
    return bias_cmp, a_tiles, ovl, sel_consts, _lane_consts(max(tq, CMP_TQ)), gate_expand


def _retention_tables():
    c = RET_CHUNK
    log_g = jnp.log(1.0 - 2.0 ** (-5.0 - jnp.arange(RET_HEADS, dtype=F32)))
    i = jnp.arange(c, dtype=F32)
    diff = i[:, None] - i[None, :]
    inner = jnp.where(diff >= 0, jnp.exp(diff[None] * log_g[:, None, None]), 0.0)
    xi = jnp.exp((i + 1.0)[None, :] * log_g[:, None])[:, :, None]
    zeta = jnp.exp((c - 1.0 - i)[None, :] * log_g[:, None])[:, :, None]
    decays = tuple(float((1.0 - 2.0 ** (-5.0 - h)) ** c) for h in range(RET_HEADS))
    inv = 1.0 / (ROPE_BASE ** jnp.linspace(0.0, 1.0, RET_QK_DIM // 2, dtype=F32))
    ang = jnp.arange(SEQ, dtype=F32)[:, None] * inv[None, :]
    return jnp.cos(ang), jnp.sin(ang), inner, xi, zeta, decays


def _nsa_w_in_layout(w):
    d = w.shape[0]
    q = w[:, :NSA_WIDTH] * (HEAD_DIM ** -0.5 * LOG2E)
    kv0 = NSA_WIDTH
    g0 = kv0 + 6 * KV_WIDTH
    z0 = g0 + 3 * HEADS
    slab = lambda n: w[:, kv0 + n * KV_WIDTH: kv0 + (n + 1) * KV_WIDTH]
    cols = [q, w[:, z0:z0 + 3 * NSA_WIDTH]]
    for n in (2, 4, 3, 5):
        for g in range(GROUPS):
            part = slab(n)[:, g * HEAD_DIM:(g + 1) * HEAD_DIM]
            cols += [part, part]
    cols += [slab(0), slab(1), w[:, g0:z0]]
    used = NSA_G_COL + 3 * HEADS
    cols.append(jnp.zeros((d, NSA_PROJ_PAD - used), w.dtype))
    return jnp.concatenate(cols, axis=1).astype(BF16)


def _nsa_layer(h2d, b, pre_gain, post_gain, w_in, w_out, k_pos, k_w1, k_w2, v_pos, v_w1, v_w2, tabs):
    bias_cmp, a_tiles, ovl, sel_consts, lane_consts, gate_expand = tabs
    proj = _norm_proj(h2d, pre_gain, _nsa_w_in_layout(w_in), NSA_PROJ_TN)
    proj = proj.reshape(b, SEQ, NSA_PROJ_PAD)
    ckv = proj[:, :, NSA_CKV_COL:NSA_CKV_COL + 2 * KV_WIDTH]
    ckv = ckv.reshape(b, N_CMP_PAD, CMP_STRIDE, 2, GROUPS, HEAD_DIM).transpose(0, 3, 4, 1, 2, 5)
    ckv = ckv.reshape(b, 2, GROUPS, N_CMP_PAD, CMP_STRIDE * HEAD_DIM)
    pos = jnp.stack([k_pos, v_pos]).reshape(2, 2, CMP_STRIDE * HEAD_DIM)
    w1 = jnp.stack([k_w1, v_w1]).astype(BF16)
    w2 = jnp.stack([k_w2, v_w2])
    w2d = jnp.concatenate([w2, w2], axis=2).astype(BF16)
    ckv_c = _compress(ckv, pos, w1, w2d)
    y_cmp, sel = _cmp_select(proj, ckv_c, bias_cmp, ovl, lane_consts, gate_expand[0])
    y_tok = _token_attention(proj, a_tiles, lane_consts, gate_expand[1:], sel, sel_consts)
    parts = [y.reshape(b * SEQ, NSA_WIDTH) for y in (y_cmp, y_tok)]
    return _out_post(parts, w_out.astype(BF16), h2d, post_gain)


def _ret_layer(h2d, b, pre_gain, post_gain, w_in, w_out, gn_gain, tabs):
    proj = _norm_proj(h2d, pre_gain, w_in.astype(BF16), RET_PROJ_TN).reshape(b, SEQ, -1)
    y = _retention(proj, gn_gain, tabs)
    return _out_post([y.reshape(b * SEQ, RET_WIDTH)], w_out.astype(BF16), h2d, post_gain)


def kernel(x, pre_norm_gain, post_norm_gain, rel_bias_table, nsa_w_in, nsa_w_out, nsa_cmp_k_pos, nsa_cmp_k_w1, nsa_cmp_k_w2, nsa_cmp_v_pos, nsa_cmp_v_w1, nsa_cmp_v_w2, ret_w_in, ret_w_out, ret_gn_gain):
    b, s, d = x.shape
    assert s == SEQ and d == D_MODEL
    nsa_tabs = _nsa_tables(rel_bias_table)
    ret_tabs = _retention_tables()
    h = x.reshape(b * s, d)
    for layer in range(DEPTH):
        slot = layer // 2
        if layer % 2 == 0:
            h = _nsa_layer(h, b, pre_norm_gain[layer], post_norm_gain[layer], nsa_w_in[slot],
                           nsa_w_out[slot], nsa_cmp_k_pos[slot], nsa_cmp_k_w1[slot],
                           nsa_cmp_k_w2[slot], nsa_cmp_v_pos[slot], nsa_cmp_v_w1[slot],
                           nsa_cmp_v_w2[slot], nsa_tabs)
        else:
            h = _ret_layer(h, b, pre_norm_gain[layer], post_norm_gain[layer], ret_w_in[slot],
                           ret_w_out[slot], ret_gn_gain[slot], ret_tabs)
    return h.reshape(b, s, d)
```

```python
import functools
import math

import numpy as np
import jax
import jax.numpy as jnp
from jax import lax
from jax.experimental import pallas as pl
from jax.experimental.pallas import tpu as pltpu

F32 = jnp.float32
BF16 = jnp.bfloat16

D_MODEL = 1024
SEQ = 2048
DEPTH = 4
RMS_EPS = 1e-6
GN_EPS = 1e-6
MASKED = -1e30
LOG2E = math.log2(math.e)

HEADS = 16
HEAD_DIM = 64
GROUPS = 4
HEADS_PER_GROUP = HEADS // GROUPS
NSA_WIDTH = HEADS * HEAD_DIM
KV_WIDTH = GROUPS * HEAD_DIM
CMP_BLOCK = 32
CMP_STRIDE = 16
CMP_HIDDEN = 256
N_CMP = (SEQ - CMP_BLOCK) // CMP_STRIDE + 1
N_CMP_PAD = 128
SLC_BLOCK = 64
N_SLC = SEQ // SLC_BLOCK
SLC_TOPN = 16
WIN_SIZE = 512
FORCED_SCORE = 1e3
REL_BUCKETS = 32
REL_MAX_DIST = 128

RET_HEADS = 4
RET_QK_DIM = 256
RET_V_DIM = 512
RET_QK_WIDTH = RET_HEADS * RET_QK_DIM
RET_WIDTH = RET_HEADS * RET_V_DIM
ROPE_BASE = 10000.0

LANES = 128
VMEM_LIMIT_BYTES = 56 * 1024 * 1024

PROJ_TM = 2048
RET_PROJ_TN = 2048
NSA_PROJ_TN = 2304
POST_TM = 512
ATT_TQ = 256
ATT_TK = 256
CMP_TQ = 2048
RET_CHUNK = 256

NSA_Q_COL = 0
NSA_Z_COL = 1024
NSA_K_COL = 4096
NSA_V_COL = 5120
NSA_CKV_COL = 6144
NSA_G_COL = 6656
NSA_PROJ_PAD = 6912

_NT = (((1,), (1,)), ((), ()))
_TN = (((0,), (0,)), ((), ()))


def _cparams(sem):
    return pltpu.CompilerParams(dimension_semantics=sem, vmem_limit_bytes=VMEM_LIMIT_BYTES)


def _norm_proj_kernel(x_ref, g_ref, w_ref, o_ref, xn_ref):
    @pl.when(pl.program_id(1) == 0)
    def _():
        x = x_ref[...]
        ms = jnp.mean(x * x, axis=-1, keepdims=True)
        xn_ref[...] = (x * lax.rsqrt(ms + RMS_EPS) * g_ref[...]).astype(BF16)

    o_ref[...] = jnp.dot(xn_ref[...], w_ref[...], preferred_element_type=F32).astype(o_ref.dtype)


def _norm_proj(x2d, gain, w_bf16, tn):
    m, d = x2d.shape
    n = w_bf16.shape[1]
    assert n % tn == 0
    return pl.pallas_call(
        _norm_proj_kernel,
        grid=(m // PROJ_TM, n // tn),
        in_specs=[
            pl.BlockSpec((PROJ_TM, d), lambda i, j: (i, 0)),
            pl.BlockSpec((1, d), lambda i, j: (0, 0)),
            pl.BlockSpec((d, tn), lambda i, j: (0, j)),
        ],
        out_specs=pl.BlockSpec((PROJ_TM, tn), lambda i, j: (i, j)),
        out_shape=jax.ShapeDtypeStruct((m, n), BF16),
        scratch_shapes=[pltpu.VMEM((PROJ_TM, d), BF16)],
        compiler_params=_cparams(("parallel", "arbitrary")),
        name="norm_proj",
    )(x2d, gain.reshape(1, d), w_bf16)


def _out_post_kernel(*refs, n_parts):
    y_refs = refs[:n_parts]
    w_ref, h_ref, g_ref, o_ref = refs[n_parts:]
    y = y_refs[0][...].astype(F32)
    for r in y_refs[1:]:
        y = y + r[...].astype(F32)
    t = jnp.dot(y.astype(BF16), w_ref[...], preferred_element_type=F32)
    ms = jnp.mean(t * t, axis=-1, keepdims=True)
    o_ref[...] = h_ref[...] + t * lax.rsqrt(ms + RMS_EPS) * g_ref[...]


def _out_post(parts, w_bf16, h2d, gain):
    m, d = h2d.shape
    k = w_bf16.shape[0]
    n_parts = len(parts)
    return pl.pallas_call(
        functools.partial(_out_post_kernel, n_parts=n_parts),
        grid=(m // POST_TM,),
        in_specs=[pl.BlockSpec((POST_TM, k), lambda i: (i, 0)) for _ in parts] + [
            pl.BlockSpec((k, d), lambda i: (0, 0)),
            pl.BlockSpec((POST_TM, d), lambda i: (i, 0)),
            pl.BlockSpec((1, d), lambda i: (0, 0)),
        ],
        out_specs=pl.BlockSpec((POST_TM, d), lambda i: (i, 0)),
        out_shape=jax.ShapeDtypeStruct((m, d), F32),
        compiler_params=_cparams(("parallel",)),
        name="out_post",
    )(*parts, w_bf16, h2d, gain.reshape(1, d))


def _compress_kernel(x_ref, pos_ref, w1_ref, w2_ref, o_ref):
    half = CMP_STRIDE * HEAD_DIM
    for which in range(2):
        x = x_ref[0, which, 0].astype(F32)
        xa = (x + pos_ref[which, 0:1, :]).astype(BF16)
        xb = (x + pos_ref[which, 1:2, :]).astype(BF16)
        pa = jnp.dot(xa, w1_ref[which, :half, :], preferred_element_type=F32)
        pb = jnp.dot(xb, w1_ref[which, half:, :], preferred_element_type=F32)
        hid = pa + pltpu.roll(pb, N_CMP_PAD - 1, 0)
        hid = hid * jax.nn.sigmoid(hid)
        o_ref[0, which, 0] = jnp.dot(hid.astype(BF16), w2_ref[which],
                                     preferred_element_type=F32).astype(o_ref.dtype)


def _compress(ckv_rows, pos, w1, w2d):
    b = ckv_rows.shape[0]
    row_w = CMP_STRIDE * HEAD_DIM
    return pl.pallas_call(
        _compress_kernel,
        grid=(b, GROUPS),
        in_specs=[
            pl.BlockSpec((1, 2, 1, N_CMP_PAD, row_w), lambda i, g: (i, 0, g, 0, 0)),
            pl.BlockSpec((2, 2, row_w), lambda i, g: (0, 0, 0)),
            pl.BlockSpec((2, 2 * row_w, CMP_HIDDEN), lambda i, g: (0, 0, 0)),
            pl.BlockSpec((2, CMP_HIDDEN, 2 * HEAD_DIM), lambda i, g: (0, 0, 0)),
        ],
        out_specs=pl.BlockSpec((1, 2, 1, N_CMP_PAD, 2 * HEAD_DIM), lambda i, g: (i, 0, g, 0, 0)),
        out_shape=jax.ShapeDtypeStruct((b, 2, GROUPS, N_CMP_PAD, 2 * HEAD_DIM), BF16),
        compiler_params=_cparams(("parallel", "parallel")),
        name="compress",
    )(ckv_rows, pos, w1, w2d)


def _lane_consts(rows):
    lane = np.arange(LANES)
    lo = (lane < HEAD_DIM).astype(np.float32)
    hi = (lane >= HEAD_DIM).astype(np.float32)
    last = (lane == LANES - 1).astype(np.float32)
    first = (lane == 0).astype(np.float32)
    c = np.stack([lo, hi, 1.0 - last, last, 1.0 - first, first])
    return jnp.asarray(np.broadcast_to(c[:, None, :], (6, rows, LANES)), BF16)


def _bf16_terms(x, n):
    terms = []
    for _ in range(n - 1):
        t = x.astype(BF16)
        terms.append(t)
        x = x - t.astype(F32)
    terms.append(x.astype(BF16))
    return terms


def _gates(gate_logits, expands):
    terms = _bf16_terms(jax.nn.sigmoid(gate_logits.astype(F32)), 2)
    return [sum(jnp.dot(t, ex, preferred_element_type=F32) for t in terms) for ex in expands]


def _silu(x):
    return x * jax.nn.sigmoid(x)


def _cmp_select_kernel(q_ref, gate_ref, z_ref, ckv_ref, bias_ref, ovl_ref, lc_ref, gexp_ref,
                       y_ref, sel_ref):
    g = pl.program_id(0)
    qi = pl.program_id(1)
    tq = q_ref.shape[1]
    q = q_ref[0]
    kk = ckv_ref[0, 0, 0]
    vv = ckv_ref[0, 1, 0]
    lane = lax.broadcasted_iota(jnp.int32, (tq, LANES), 1)

    psum = jnp.zeros((N_CMP_PAD, tq), F32)
    pairs = []
    for a in range(2):
        qp = q[:, a * LANES:(a + 1) * LANES]
        outs = []
        for e in range(2):
            s = lax.dot_general(kk, qp * lc_ref[e, :tq], _NT, preferred_element_type=F32)
            bias = bias_ref[2 * a + e]
            s = s + bias
            m = jnp.max(s, axis=0, keepdims=True)
            p = jnp.exp2(s - m)
            p = jnp.where(bias > 0.5 * MASKED, p / jnp.sum(p, axis=0, keepdims=True), 0.0)
            psum = psum + p
            outs.append(lax.dot_general(p.astype(BF16), vv, _TN, preferred_element_type=F32))
        pairs.append(jnp.where(lane < HEAD_DIM, outs[0], outs[1]))
    y_ref[0] = (jnp.concatenate(pairs, axis=1) * _gates(gate_ref[0], [gexp_ref[g]])[0]
                * _silu(z_ref[0].astype(F32))).astype(y_ref.dtype)

    ovl = ovl_ref[...]
    imp = sum(jnp.dot(ovl, t, preferred_element_type=F32) for t in _bf16_terms(psum, 3))
    t = qi * tq + lax.broadcasted_iota(jnp.int32, (N_SLC, tq), 1)
    blk = lax.broadcasted_iota(jnp.int32, (N_SLC, tq), 0)
    cur = lax.shift_right_logical(t, int(math.log2(SLC_BLOCK)))
    forced = (blk == 0) | (blk == cur) | (blk == cur - 1)
    score = jnp.where(blk * SLC_BLOCK <= t, imp + jnp.where(forced, FORCED_SCORE, 0.0), MASKED)
    cnt = jnp.zeros((N_SLC, tq), jnp.int32)
    for jp in range(N_SLC):
        row = score[jp:jp + 1, :]
        beats = (row > score) | ((row == score) & (blk > jp))
        cnt = cnt + beats.astype(jnp.int32)
    sel_ref[0, 0] = jnp.where(cnt < SLC_TOPN, 1.0, 0.0).astype(sel_ref.dtype)


def _cmp_select(proj, ckv, bias_cmp, ovl, lane_consts, gate_expand):
    b = proj.shape[0]
    tq = CMP_TQ
    grp_w = HEADS_PER_GROUP * HEAD_DIM
    return pl.pallas_call(
        _cmp_select_kernel,
        grid=(GROUPS, SEQ // tq, b),
        in_specs=[
            pl.BlockSpec((1, tq, grp_w), lambda g, qi, i: (i, qi, NSA_Q_COL // grp_w + g)),
            pl.BlockSpec((1, tq, LANES), lambda g, qi, i: (i, qi, NSA_G_COL // LANES)),
            pl.BlockSpec((1, tq, grp_w), lambda g, qi, i: (i, qi, NSA_Z_COL // grp_w + g)),
            pl.BlockSpec((1, 2, 1, N_CMP_PAD, 2 * HEAD_DIM), lambda g, qi, i: (i, 0, g, 0, 0)),
            pl.BlockSpec((HEADS_PER_GROUP, N_CMP_PAD, tq), lambda g, qi, i: (g, 0, qi)),
            pl.BlockSpec((N_SLC, N_CMP_PAD), lambda g, qi, i: (0, 0)),
            pl.BlockSpec(memory_space=pltpu.VMEM),
            pl.BlockSpec(memory_space=pltpu.VMEM),
        ],
        out_specs=[
            pl.BlockSpec((1, tq, grp_w), lambda g, qi, i: (i, qi, g)),
            pl.BlockSpec((1, 1, N_SLC, tq), lambda g, qi, i: (i, g, 0, qi)),
        ],
        out_shape=[
            jax.ShapeDtypeStruct((b, SEQ, NSA_WIDTH), BF16),
            jax.ShapeDtypeStruct((b, GROUPS, N_SLC, SEQ), BF16),
        ],
        compiler_params=_cparams(("parallel", "parallel", "parallel")),
        name="cmp_select",
    )(proj, proj, proj, ckv, bias_cmp, ovl, lane_consts, gate_expand)


def _token_kernel(q_ref, ks_ref, vs_ref, kw_ref, vw_ref, gate_ref, zs_ref, zw_ref, a_ref, lc_ref,
                  gexp_ref, sel_ref, kblk_ref, place_ref, placed_ref, dspread_ref, y_ref,
                  qm_ref, m_ref, acc_ref, s_bufs, p_bufs, c_bufs, kx_ref, vx_ref, gz_ref):
    g = pl.program_id(1)
    tq = ATT_TQ
    tk = ATT_TK

    def stream_len(qi):
        return min(qi, WIN_SIZE // tk) + 1 + qi + 1

    for c in range(SEQ // tk):
        rows = slice(c * tk, (c + 1) * tk)
        for branch, v_ref in enumerate((vs_ref, vw_ref)):
            vv = v_ref[0, rows, :]
            vx_ref[2 * branch, rows, :] = vv * lc_ref[2, :tk] + lc_ref[3, :tk]
            vx_ref[2 * branch + 1, rows, :] = vv * lc_ref[4, :tk] + lc_ref[5, :tk]
        kk = ks_ref[0, rows, :]
        for e in range(2):
            kx_ref[e, c] = (kk * lc_ref[e, :tk] + kblk_ref[e, c]).astype(F32).T.astype(BF16)
        kx_ref[2, c] = kw_ref[0, rows, :].astype(F32).T.astype(BF16)

    def tile(qi, j):
        n_win = min(qi, WIN_SIZE // tk) + 1
        win = int(j < n_win)
        t = j - (1 - win) * n_win
        return win, qi - t, t if (win or t < 2) else None

    def scores(qi, j, s_ref):
        if j >= stream_len(qi):
            return
        win, ki, _ = tile(qi, j)
        for e in range(2):
            pair = slice(2 * e * tq, 2 * (e + 1) * tq)
            s_ref[pair, :] = jnp.dot(qm_ref[win, pair, :], kx_ref[2 if win else e, ki],
                                     preferred_element_type=F32)

    def softmax(qi, j, s_ref, p_ref, c_ref):
        win, _, kind = tile(qi, j)
        for slot in range(HEADS_PER_GROUP):
            e, a = divmod(slot, 2)
            rows = slice(slot * tq, (slot + 1) * tq)
            s = s_ref[rows, :]
            if kind is not None:
                s = s + a_ref[kind, HEADS_PER_GROUP * g + 2 * a + e]
            m_prev = m_ref[win, rows, :]
            m_new = jnp.maximum(m_prev, jnp.max(s, axis=1, keepdims=True))
            c_ref[rows, :] = jnp.exp2(m_prev - m_new)
            x = s - jnp.concatenate([m_new] * (tk // LANES), axis=1)
            p_ref[rows, :] = jnp.exp2(x.astype(BF16))
            m_ref[win, rows, :] = m_new

    def values(qi_, par_, j, p_ref, c_ref):
        win, ki, _ = tile(qi_, j)
        rows = slice(ki * tk, (ki + 1) * tk)
        for e in range(2):
            pair = slice(2 * e * tq, 2 * (e + 1) * tq)
            pv = jnp.dot(p_ref[pair, :], vx_ref[2 * win + e, rows, :], preferred_element_type=F32)
            acc = acc_ref.at[2 * par_ + win]
            acc[pair, :] = c_ref[pair, :] * acc[pair, :] + pv

    lane = lax.broadcasted_iota(jnp.int32, (tq, LANES), 1)

    def gated_output(par_, branch):
        acc = acc_ref.at[2 * par_ + branch]
        pairs = []
        for a in range(2):
            acc_e = acc[a * tq:(a + 1) * tq, :]
            acc_o = acc[(2 + a) * tq:(3 + a) * tq, :]
            both = jnp.concatenate([acc_e, acc_o], axis=1)
            den = sum(jnp.dot(t, dspread_ref[...], preferred_element_type=F32)
                      for t in _bf16_terms(both, 2))
            pairs.append(jnp.where(lane < HEAD_DIM, acc_e, acc_o) / den)
        return jnp.concatenate(pairs, axis=1) * gz_ref[2 * par_ + branch]

    bufs = [(s_bufs.at[i], p_bufs.at[i], c_bufs.at[i]) for i in range(2)]

    def step(qi, half, j, parity):
        cur, nxt = bufs[parity], bufs[1 - parity]
        values(qi, half, j - 1, nxt[1], nxt[2])
        softmax(qi, j, *cur)
        scores(qi, j + 1, nxt[0])

    def enter(qi, slot):
        rows = slice(slot * tq, (slot + 1) * tq)
        half = slot % 2
        q = q_ref[0, rows, :]
        qms = []
        for hs in range(HEADS_PER_GROUP):
            e, a = divmod(hs, 2)
            qms.append(q[:, a * LANES:(a + 1) * LANES] * lc_ref[e, :tq])
            qm_ref[1, hs * tq:(hs + 1) * tq, :] = qms[hs]
        flags = lax.dot_general(sel_ref[0, 0, :, rows], place_ref[...], _TN,
                                preferred_element_type=F32)
        fill = ((placed_ref[...] - flags) * MASKED).astype(BF16)
        for hs in range(HEADS_PER_GROUP):
            e = hs // 2
            qm_ref[0, hs * tq:(hs + 1) * tq, :] = qms[hs] + fill[:, e * LANES:(e + 1) * LANES]
        gates = _gates(gate_ref[0, rows, :], [gexp_ref[0, g], gexp_ref[1, g]])
        for branch, z_ref in enumerate((zs_ref, zw_ref)):
            gz_ref[2 * half + branch] = gates[branch] * _silu(z_ref[0, rows, :].astype(F32))
        m_ref[...] = jnp.full(m_ref.shape, -jnp.inf, F32)
        for branch in range(2):
            acc_ref[2 * half + branch] = jnp.zeros(acc_ref.shape[1:], F32)
        scores(qi, 0, bufs[0][0])

    def fill_pipeline(qi, half):
        softmax(qi, 0, *bufs[0])
        scores(qi, 1, bufs[1][0])
        step(qi, half, 1, 1)

    def run_stream(qi, half):
        for j in range(2, stream_len(qi)):
            step(qi, half, j, j % 2)

    def leave(qi, half):
        y_win = gated_output(half, 1)
        last_pos = stream_len(qi) - 1
        values(qi, half, last_pos, bufs[last_pos % 2][1], bufs[last_pos % 2][2])
        return y_win

    def write_output(slot, y_win):
        rows = slice(slot * tq, (slot + 1) * tq)
        y_ref[0, rows, :] = (y_win + gated_output(slot % 2, 0)).astype(y_ref.dtype)

    y_win = None
    for qi in range(SEQ // tq):
        enter(qi, qi)
        if qi > 0:
            write_output(qi - 1, y_win)
        fill_pipeline(qi, qi % 2)
        run_stream(qi, qi % 2)
        y_win = leave(qi, qi % 2)
    write_output(SEQ // tq - 1, y_win)


def _token_attention(proj, a_tiles, lane_consts, gate_expand, sel, sel_consts):
    b = proj.shape[0]
    tq, tk = ATT_TQ, ATT_TK
    grp_w = HEADS_PER_GROUP * HEAD_DIM
    k_blk = NSA_K_COL // LANES
    v_blk = NSA_V_COL // LANES
    z_blk = NSA_Z_COL // grp_w
    kv_spec = lambda blk: pl.BlockSpec((1, SEQ, LANES), lambda i, g: (i, 0, blk + g))
    z_spec = lambda blk: pl.BlockSpec((1, SEQ, grp_w), lambda i, g: (i, 0, blk + g))
    whole = pl.BlockSpec(memory_space=pltpu.VMEM)
    return pl.pallas_call(
        _token_kernel,
        grid=(b, GROUPS),
        in_specs=[
            pl.BlockSpec((1, SEQ, grp_w), lambda i, g: (i, 0, NSA_Q_COL // grp_w + g)),
            kv_spec(k_blk), kv_spec(v_blk), kv_spec(k_blk + GROUPS), kv_spec(v_blk + GROUPS),
            pl.BlockSpec((1, SEQ, LANES), lambda i, g: (i, 0, NSA_G_COL // LANES)),
            z_spec(z_blk + GROUPS), z_spec(z_blk + 2 * GROUPS),
            whole, whole, whole,
            pl.BlockSpec((1, 1, N_SLC, SEQ), lambda i, g: (i, g, 0, 0)),
            whole, whole, whole, whole,
        ],
        out_specs=pl.BlockSpec((1, SEQ, grp_w), lambda i, g: (i, 0, g)),
        out_shape=jax.ShapeDtypeStruct((b, SEQ, NSA_WIDTH), BF16),
        scratch_shapes=[
            pltpu.VMEM((2, HEADS_PER_GROUP * tq, LANES), BF16),
            pltpu.VMEM((2, HEADS_PER_GROUP * tq, LANES), F32),
            pltpu.VMEM((4, HEADS_PER_GROUP * tq, LANES), F32),
            pltpu.VMEM((2, HEADS_PER_GROUP * tq, tk), F32),
            pltpu.VMEM((2, HEADS_PER_GROUP * tq, tk), BF16),
            pltpu.VMEM((2, HEADS_PER_GROUP * tq, LANES), F32),
            pltpu.VMEM((3, SEQ // tk, LANES, tk), BF16),
            pltpu.VMEM((4, SEQ, LANES), BF16),
            pltpu.VMEM((4, tq, grp_w), F32),
        ],
        compiler_params=_cparams(("parallel", "parallel")),
        name="token_attention",
    )(proj, proj, proj, proj, proj, proj, proj, proj, a_tiles, lane_consts, gate_expand,
      sel, *sel_consts)


def _retention_kernel(q_ref, k_ref, v_ref, z_ref, cos_ref, sin_ref, inner_ref, xi_ref, zeta_ref,
                      gn_ref, y_ref, state_ref, *, decays):
    @pl.when(pl.program_id(1) == 0)
    def _():
        state_ref[...] = jnp.zeros(state_ref.shape, F32)

    cos = cos_ref[...]
    sin = sin_ref[...]
    half = RET_QK_DIM // 2

    def rot(x):
        x1, x2 = x[:, :half], x[:, half:]
        return jnp.concatenate([x1 * cos - x2 * sin, x1 * sin + x2 * cos], axis=1)

    for h in range(RET_HEADS):
        qs = slice(h * RET_QK_DIM, (h + 1) * RET_QK_DIM)
        vs = slice(h * RET_V_DIM, (h + 1) * RET_V_DIM)
        qr = rot(q_ref[0, :, qs].astype(F32))
        kr = rot(k_ref[0, :, qs].astype(F32)) * (RET_QK_DIM ** -0.5)
        qb = qr.astype(BF16)
        vh = v_ref[0, :, vs]
        attn = lax.dot_general(qb, kr.astype(BF16), _NT, preferred_element_type=F32) * inner_ref[h]
        st = state_ref[h]
        o = (jnp.dot(attn.astype(BF16), vh, preferred_element_type=F32)
             + jnp.dot(qb, st.astype(BF16), preferred_element_type=F32) * xi_ref[h])
        kz = (kr * zeta_ref[h]).astype(BF16)
        state_ref[h] = st * decays[h] + lax.dot_general(kz, vh, _TN, preferred_element_type=F32)
        mu = jnp.mean(o, axis=1, keepdims=True)
        d = o - mu
        var = jnp.mean(d * d, axis=1, keepdims=True)
        on = d * lax.rsqrt(var + GN_EPS) * gn_ref[h]
        y_ref[0, :, vs] = (on * _silu(z_ref[0, :, vs].astype(F32))).astype(y_ref.dtype)


def _retention(proj, gn_gain, tables):
    b = proj.shape[0]
    c = RET_CHUNK
    cos, sin, inner, xi, zeta, decays = tables
    v_blk = 2 * RET_QK_WIDTH // RET_WIDTH
    return pl.pallas_call(
        functools.partial(_retention_kernel, decays=decays),
        grid=(b, SEQ // c),
        in_specs=[
            pl.BlockSpec((1, c, RET_QK_WIDTH), lambda i, j: (i, j, 0)),
            pl.BlockSpec((1, c, RET_QK_WIDTH), lambda i, j: (i, j, 1)),
            pl.BlockSpec((1, c, RET_WIDTH), lambda i, j: (i, j, v_blk)),
            pl.BlockSpec((1, c, RET_WIDTH), lambda i, j: (i, j, v_blk + 1)),
            pl.BlockSpec((c, RET_QK_DIM // 2), lambda i, j: (j, 0)),
            pl.BlockSpec((c, RET_QK_DIM // 2), lambda i, j: (j, 0)),
            pl.BlockSpec((RET_HEADS, c, c), lambda i, j: (0, 0, 0)),
            pl.BlockSpec((RET_HEADS, c, 1), lambda i, j: (0, 0, 0)),
            pl.BlockSpec((RET_HEADS, c, 1), lambda i, j: (0, 0, 0)),
            pl.BlockSpec((RET_HEADS, 1, RET_V_DIM), lambda i, j: (0, 0, 0)),
        ],
        out_specs=pl.BlockSpec((1, c, RET_WIDTH), lambda i, j: (i, j, 0)),
        out_shape=jax.ShapeDtypeStruct((b, SEQ, RET_WIDTH), BF16),
        scratch_shapes=[pltpu.VMEM((RET_HEADS, RET_QK_DIM, RET_V_DIM), F32)],
        compiler_params=_cparams(("parallel", "arbitrary")),
        name="retention",
    )(proj, proj, proj, proj, cos, sin, inner, xi, zeta, gn_gain.reshape(RET_HEADS, 1, RET_V_DIM))


def _t5_bucket_np(dist):
    n = np.maximum(dist, 0)
    max_exact = REL_BUCKETS // 2
    nf = np.maximum(n, 1).astype(np.float64)
    large = max_exact + (np.log(nf / max_exact) / math.log(REL_MAX_DIST / max_exact)
                         * (REL_BUCKETS - max_exact)).astype(np.int64)
    large = np.minimum(large, REL_BUCKETS - 1)
    return np.where(n < max_exact, n, large).astype(np.int32)


def _skewed(vec, rows, stride, cols):
    p = vec.shape[-1]
    lead = vec.shape[:-1]
    flat = jnp.tile(vec, (1,) * len(lead) + (rows,))[..., :rows * (p - stride)]
    return flat.reshape(lead + (rows, p - stride))[..., :cols]


def _nsa_tables(table):
    tq, tk = ATT_TQ, ATT_TK
    assert tq == tk
    tab_t = table.T * LOG2E
    period = 2 * SEQ
    x = np.arange(period)
    ok = (x >= CMP_BLOCK - 1) & (x < SEQ)
    vec = jnp.take(tab_t, jnp.asarray(_t5_bucket_np(x - (CMP_BLOCK - 1))), axis=1)
    vec = jnp.where(jnp.asarray(ok)[None, :], vec, MASKED)
    bias_cmp = _skewed(vec, N_CMP_PAD, CMP_STRIDE, SEQ)
    rel_t = tab_t - tab_t[:, REL_BUCKETS - 1:]
    period = 2 * tq
    x = np.arange(period)
    x = np.where(x < tq, x, x - period)
    tiles = []
    for off in range(3):
        d = off * tk - x
        ok = (d >= 0) & (d < WIN_SIZE) if off == 2 else (d >= 0)
        vec = jnp.take(rel_t, jnp.asarray(_t5_bucket_np(d)), axis=1)
        vec = jnp.where(jnp.asarray(ok)[None, :], vec, MASKED)
        tiles.append(_skewed(vec, tq, 1, tk))
    a_tiles = jnp.stack(tiles + [jnp.zeros_like(tiles[0])])
    cs = np.arange(N_CMP_PAD)[None, :] * CMP_STRIDE
    jj = np.arange(N_SLC)[:, None]
    ovl = ((cs < (jj + 1) * SLC_BLOCK) & (cs + CMP_BLOCK > jj * SLC_BLOCK)
           & (np.arange(N_CMP_PAD)[None, :] < N_CMP))
    ovl = jnp.asarray(ovl.astype(np.float32), BF16)
    lane = np.arange(LANES).reshape(1, 1, 1, LANES)
    key = np.arange(SEQ).reshape(1, SEQ // tk, tk, 1)
    base = np.array([HEAD_DIM, 0]).reshape(2, 1, 1, 1)
    key_blocks = jnp.asarray((lane == base + key // SLC_BLOCK).astype(np.float32), BF16)
    col = np.arange(2 * LANES).reshape(1, -1)
    jb = np.arange(N_SLC).reshape(-1, 1)
    place = (col == HEAD_DIM + jb) | (col == LANES + jb)
    row = np.arange(2 * LANES).reshape(-1, 1)
    out = np.arange(LANES).reshape(1, -1)
    den_spread = ((row == LANES - 1) & (out < HEAD_DIM)) | ((row == LANES) & (out >= HEAD_DIM))
    sel_consts = (key_blocks, jnp.asarray(place.astype(np.float32), BF16),
                  jnp.asarray(place.any(axis=0, keepdims=True).astype(np.float32)),
                  jnp.asarray(den_spread.astype(np.float32), BF16))
    c = np.arange(LANES).reshape(1, 1, LANES, 1)
    col = np.arange(HEADS_PER_GROUP * HEAD_DIM).reshape(1, 1, 1, -1) // HEAD_DIM
    br = np.arange(3).reshape(3, 1, 1, 1)
    gg = np.arange(GROUPS).reshape(1, GROUPS, 1, 1)
    gate_expand = jnp.asarray((c == br * HEADS + HEADS_PER_GROUP * gg + col).astype(np.float32),
                              BF16)
    return bias_cmp, a_tiles, ovl, sel_consts, _lane_consts(max(tq, CMP_TQ)), gate_expand


def _retention_tables():
    c = RET_CHUNK
    log_g = jnp.log(1.0 - 2.0 ** (-5.0 - jnp.arange(RET_HEADS, dtype=F32)))
    i = jnp.arange(c, dtype=F32)
    diff = i[:, None] - i[None, :]
    inner = jnp.where(diff >= 0, jnp.exp(diff[None] * log_g[:, None, None]), 0.0)
    xi = jnp.exp((i + 1.0)[None, :] * log_g[:, None])[:, :, None]
    zeta = jnp.exp((c - 1.0 - i)[None, :] * log_g[:, None])[:, :, None]
    decays = tuple(float((1.0 - 2.0 ** (-5.0 - h)) ** c) for h in range(RET_HEADS))
    inv = 1.0 / (ROPE_BASE ** jnp.linspace(0.0, 1.0, RET_QK_DIM // 2, dtype=F32))
    ang = jnp.arange(SEQ, dtype=F32)[:, None] * inv[None, :]
    return jnp.cos(ang), jnp.sin(ang), inner, xi, zeta, decays


def _nsa_w_in_layout(w):
    d = w.shape[0]
    q = w[:, :NSA_WIDTH] * (HEAD_DIM ** -0.5 * LOG2E)
    kv0 = NSA_WIDTH
    g0 = kv0 + 6 * KV_WIDTH
    z0 = g0 + 3 * HEADS
    slab = lambda n: w[:, kv0 + n * KV_WIDTH: kv0 + (n + 1) * KV_WIDTH]
    cols = [q, w[:, z0:z0 + 3 * NSA_WIDTH]]
    for n in (2, 4, 3, 5):
        for g in range(GROUPS):
            part = slab(n)[:, g * HEAD_DIM:(g + 1) * HEAD_DIM]
            cols += [part, part]
    cols += [slab(0), slab(1), w[:, g0:z0]]
    used = NSA_G_COL + 3 * HEADS
    cols.append(jnp.zeros((d, NSA_PROJ_PAD - used), w.dtype))
    return jnp.concatenate(cols, axis=1).astype(BF16)


def _nsa_layer(h2d, b, pre_gain, post_gain, w_in, w_out, k_pos, k_w1, k_w2, v_pos, v_w1, v_w2, tabs):
    bias_cmp, a_tiles, ovl, sel_consts, lane_consts, gate_expand = tabs
    proj = _norm_proj(h2d, pre_gain, _nsa_w_in_layout(w_in), NSA_PROJ_TN)
    proj = proj.reshape(b, SEQ, NSA_PROJ_PAD)
    ckv = proj[:, :, NSA_CKV_COL:NSA_CKV_COL + 2 * KV_WIDTH]
    ckv = ckv.reshape(b, N_CMP_PAD, CMP_STRIDE, 2, GROUPS, HEAD_DIM).transpose(0, 3, 4, 1, 2, 5)
    ckv = ckv.reshape(b, 2, GROUPS, N_CMP_PAD, CMP_STRIDE * HEAD_DIM)
    pos = jnp.stack([k_pos, v_pos]).reshape(2, 2, CMP_STRIDE * HEAD_DIM)
    w1 = jnp.stack([k_w1, v_w1]).astype(BF16)
    w2 = jnp.stack([k_w2, v_w2])
    w2d = jnp.concatenate([w2, w2], axis=2).astype(BF16)
    ckv_c = _compress(ckv, pos, w1, w2d)
    y_cmp, sel = _cmp_select(proj, ckv_c, bias_cmp, ovl, lane_consts, gate_expand[0])
    y_tok = _token_attention(proj, a_tiles, lane_consts, gate_expand[1:], sel, sel_consts)
    parts = [y.reshape(b * SEQ, NSA_WIDTH) for y in (y_cmp, y_tok)]
    return _out_post(parts, w_out.astype(BF16), h2d, post_gain)


def _ret_layer(h2d, b, pre_gain, post_gain, w_in, w_out, gn_gain, tabs):
    proj = _norm_proj(h2d, pre_gain, w_in.astype(BF16), RET_PROJ_TN).reshape(b, SEQ, -1)
    y = _retention(proj, gn_gain, tabs)
    return _out_post([y.reshape(b * SEQ, RET_WIDTH)], w_out.astype(BF16), h2d, post_gain)


def kernel(x, pre_norm_gain, post_norm_gain, rel_bias_table, nsa_w_in, nsa_w_out, nsa_cmp_k_pos, nsa_cmp_k_w1, nsa_cmp_k_w2, nsa_cmp_v_pos, nsa_cmp_v_w1, nsa_cmp_v_w2, ret_w_in, ret_w_out, ret_gn_gain):
    b, s, d = x.shape
    assert s == SEQ and d == D_MODEL
    nsa_tabs = _nsa_tables(rel_bias_table)
    ret_tabs = _retention_tables()
    h = x.reshape(b * s, d)
    for layer in range(DEPTH):
        slot = layer // 2
        if layer % 2 == 0:
            h = _nsa_layer(h, b, pre_norm_gain[layer], post_norm_gain[layer], nsa_w_in[slot],
                           nsa_w_out[slot], nsa_cmp_k_pos[slot], nsa_cmp_k_w1[slot],
                           nsa_cmp_k_w2[slot], nsa_cmp_v_pos[slot], nsa_cmp_v_w1[slot],
                           nsa_cmp_v_w2[slot], nsa_tabs)
        else:
            h = _ret_layer(h, b, pre_norm_gain[layer], post_norm_gain[layer], ret_w_in[slot],
                           ret_w_out[slot], ret_gn_gain[slot], ret_tabs)
    return h.reshape(b, s, d)
```
